```python
import jax, jax.numpy as jnp
from jax import lax
import numpy as np

D_MODEL = 2048
BATCH = 32
SEQ = 256
DEPTH = 1
DEC_BATCH = 4
DEC_SEQ = 1024
PAST_LEN = 256

GRID_W = 64
N_HEADS = 16
QK_NOPE_DIM = 128
QK_ROPE_DIM = 64
QK_DIM = QK_NOPE_DIM + QK_ROPE_DIM
V_DIM = 128
KV_RANK = 512
POOL_WINDOWS = (2, 4, 8, 16)
POOL_WIDTH = 1024
POOL_GROUP = POOL_WIDTH // len(POOL_WINDOWS)
N_EXPERTS = 32
TOP_K = 4
D_FF = 2048
SWIGLU_LIMIT = 7.0
SWIGLU_ALPHA = 1.702
ROPE_BASE = 10000.0
RMS_EPS = 1e-6
Q_BLOCK = 128
MOE_BLOCK = 128
ATTN_SCALE = QK_DIM ** -0.5
N_IN = N_HEADS * QK_DIM + KV_RANK + QK_ROPE_DIM + POOL_WIDTH + 2 * D_MODEL

kernel_name = "hybrid_mla_pool_moe_diffusion_step"


def rms_norm(x, g):
    xf = x.astype(jnp.float32)
    y = xf * lax.rsqrt(jnp.mean(xf * xf, axis=-1, keepdims=True) + RMS_EPS)
    return (y * g.astype(jnp.float32)).astype(x.dtype)


def adaln_params(cond, w_ada, b_ada):
    m = jax.nn.silu(cond) @ w_ada + b_ada
    return jnp.split(m, 6, axis=-1)


def modulate(x, g, shift, scale):
    return rms_norm(x, g) * (1 + scale) + shift


def split_in(proj):
    B, S, _ = proj.shape
    i0 = N_HEADS * QK_DIM
    i1 = i0 + KV_RANK
    i2 = i1 + QK_ROPE_DIM
    i3 = i2 + POOL_WIDTH
    q = proj[..., :i0].reshape(B, S, N_HEADS, QK_DIM)
    return q, proj[..., i0:i1], proj[..., i1:i2], proj[..., i2:i3], proj[..., i3:]


def expand_kv(ckv_n, krope, w_ukv, k_norm_g):
    B, L, _ = ckv_n.shape
    kv = (ckv_n @ w_ukv).reshape(B, L, N_HEADS, QK_NOPE_DIM + V_DIM)
    k_nope, v = kv[..., :QK_NOPE_DIM], kv[..., QK_NOPE_DIM:]
    k_rope = jnp.broadcast_to(krope[:, :, None, :], (B, L, N_HEADS, QK_ROPE_DIM))
    k = jnp.concatenate([k_nope, k_rope], axis=-1)
    return rms_norm(k, k_norm_g), v


def axial_rope(x, row, col):
    nf = QK_ROPE_DIM // 4
    half = QK_ROPE_DIM // 2
    inv = ROPE_BASE ** (-jnp.arange(nf, dtype=jnp.float32) / nf)

    def rot(xh, pos):
        ang = pos[:, None] * inv[None, :]
        cos = jnp.cos(ang)[None, :, None, :].astype(x.dtype)
        sin = jnp.sin(ang)[None, :, None, :].astype(x.dtype)
        x1, x2 = xh[..., :nf], xh[..., nf:]
        return jnp.concatenate([x1 * cos - x2 * sin, x1 * sin + x2 * cos], axis=-1)

    return jnp.concatenate([rot(x[..., :half], row), rot(x[..., half:], col)], axis=-1)


def rope_on_tail(t, row, col):
    return jnp.concatenate([t[..., :QK_NOPE_DIM], axial_rope(t[..., QK_NOPE_DIM:], row, col)], axis=-1)


def block_attention(q, k, v):
    B, Sq, H, Dq = q.shape
    nq = Sq // Q_BLOCK
    qb = jnp.moveaxis(q.reshape(B, nq, Q_BLOCK, H, Dq), 1, 0)

    def one(qblk):
        s = jnp.einsum('bqhd,bkhd->bhqk', qblk, k, preferred_element_type=jnp.float32) * ATTN_SCALE
        p = jax.nn.softmax(s, axis=-1).astype(v.dtype)
        return jnp.einsum('bhqk,bkhd->bqhd', p, v)

    o = lax.map(one, qb)
    return jnp.moveaxis(o, 0, 1).reshape(B, Sq, H * V_DIM)


def multiscale_pool(u, w_pool, pool_scale):
    B, S, _ = u.shape
    G = len(POOL_WINDOWS)
    uf = u.reshape(B, S, G, POOL_GROUP).astype(jnp.float32)
    cs = jnp.concatenate([jnp.zeros((B, 1, G, POOL_GROUP), jnp.float32), jnp.cumsum(uf, axis=1)], axis=1)
    t = jnp.arange(S)
    means = []
    for g, w in enumerate(POOL_WINDOWS):
        lo = jnp.clip(t - w // 2, 0, S)
        hi = jnp.clip(t + w - w // 2, 0, S)
        cnt = (hi - lo).astype(jnp.float32)
        means.append((cs[:, hi, g] - cs[:, lo, g]) / cnt[None, :, None])
    pooled = jnp.stack(means, axis=2)
    mixed = (pooled - uf).astype(u.dtype)
    out = jnp.einsum('bsgc,gcd->bsgd', mixed, w_pool)
    return out.reshape(B, S, POOL_WIDTH) * pool_scale


def merge_branches(attn_o, pool_o, gates, b_branch_gate, w_o_attn, w_o_pool, w_out):
    ga, gp = jnp.split(gates + b_branch_gate, 2, axis=-1)
    merged = jax.nn.sigmoid(ga) * (attn_o @ w_o_attn) + jax.nn.sigmoid(gp) * (pool_o @ w_o_pool)
    return merged @ w_out


def moe_ffn(h, router_w, router_b, w_g, b_g, w_u, b_u, w_d, b_d):
    B, S, D = h.shape
    T = B * S
    x2d = h.reshape(T, D)
    logits = (x2d @ router_w + router_b).astype(jnp.float32)
    top_logit, top_idx = lax.top_k(logits, TOP_K)
    top_w = jax.nn.softmax(top_logit, axis=-1)
    n_assign = T * TOP_K
    flat_e = top_idx.reshape(-1)
    flat_tok = jnp.repeat(jnp.arange(T, dtype=jnp.int32), TOP_K)
    order = jnp.argsort(flat_e)
    sorted_e = flat_e[order]
    sorted_tok = flat_tok[order]
    sorted_w = top_w.reshape(-1)[order]
    counts = jnp.bincount(flat_e, length=N_EXPERTS)
    padded = (counts + MOE_BLOCK - 1) // MOE_BLOCK * MOE_BLOCK
    pad_end = jnp.cumsum(padded)
    pad_start = pad_end - padded
    start = jnp.cumsum(counts) - counts
    dest = pad_start[sorted_e] + jnp.arange(n_assign, dtype=jnp.int32) - start[sorted_e]
    n_blocks = -(-(n_assign + N_EXPERTS * (MOE_BLOCK - 1)) // MOE_BLOCK)
    cap = n_blocks * MOE_BLOCK
    buf_tok = jnp.full((cap,), T, jnp.int32).at[dest].set(sorted_tok)
    block_e = jnp.minimum(jnp.searchsorted(pad_end, jnp.arange(n_blocks, dtype=pad_end.dtype) * MOE_BLOCK, side='right'), N_EXPERTS - 1)
    x_pad = jnp.concatenate([x2d, jnp.zeros((1, D), x2d.dtype)], axis=0)
    xb = x_pad[buf_tok].reshape(n_blocks, MOE_BLOCK, D)

    def expert_block(args):
        xblk, e = args
        gt = jnp.minimum(xblk @ w_g[e] + b_g[e], SWIGLU_LIMIT)
        up = jnp.clip(xblk @ w_u[e] + b_u[e], -SWIGLU_LIMIT, SWIGLU_LIMIT)
        return ((up + 1) * (gt * jax.nn.sigmoid(SWIGLU_ALPHA * gt))) @ w_d[e] + b_d[e]

    yb = lax.map(expert_block, (xb, block_e)).reshape(cap, D)
    y = jax.ops.segment_sum(yb[dest] * sorted_w[:, None].astype(yb.dtype), sorted_tok, num_segments=T)
    return y.reshape(B, S, D)


def trunk_layer(x, mods, lw, ctx=None, pos=None):
    (norm1_g, norm2_g, w_in, b_branch_gate, kv_norm_g, w_ukv, q_norm_g, k_norm_g, w_o_attn,
     w_pool, pool_scale, w_o_pool, w_out, router_w, router_b,
     w_exp_gate, b_exp_gate, w_exp_up, b_exp_up, w_exp_down, b_exp_down) = lw
    sh1, sc1, g1, sh2, sc2, g2 = mods
    h = modulate(x, norm1_g, sh1, sc1)
    q, ckv, krope, pool_in, gates = split_in(h @ w_in)
    q = rms_norm(q, q_norm_g)
    ckv_n = rms_norm(ckv, kv_norm_g)
    k, v = expand_kv(ckv_n, krope, w_ukv, k_norm_g)
    if ctx is not None:
        row, col = pos
        q = rope_on_tail(q, row, col)
        k = rope_on_tail(k, row, col)
        k_ctx, v_ctx = expand_kv(ctx[0], ctx[1], w_ukv, k_norm_g)
        k = jnp.concatenate([k, k_ctx], axis=1)
        v = jnp.concatenate([v, v_ctx], axis=1)
    attn_o = block_attention(q, k, v)
    pool_o = multiscale_pool(pool_in, w_pool, pool_scale)
    x = x + g1 * merge_branches(attn_o, pool_o, gates, b_branch_gate, w_o_attn, w_o_pool, w_out)
    h2 = modulate(x, norm2_g, sh2, sc2)
    x = x + g2 * moe_ffn(h2, router_w, router_b, w_exp_gate, b_exp_gate, w_exp_up, b_exp_up, w_exp_down, b_exp_down)
    return x, ckv_n, krope


def setup_inputs(seed: int = 0) -> dict:
    key = jax.random.key(seed)
    ks = jax.random.split(key, 32)
    f32 = jnp.float32

    def nrm(k, shape, scale):
        return jax.random.normal(k, shape, f32) * scale

    def gain(k, n):
        return 1.0 + 0.02 * jax.random.normal(k, (DEPTH, n), f32)

    return {
        "x_prompt": nrm(ks[0], (BATCH, SEQ, D_MODEL), 1.0),
        "x_sample": nrm(ks[1], (DEC_BATCH, DEC_SEQ, D_MODEL), 1.0),
        "cache_ckv": nrm(ks[2], (DEC_BATCH, DEPTH, PAST_LEN, KV_RANK), 1.0),
        "cache_krope": nrm(ks[3], (DEC_BATCH, DEPTH, PAST_LEN, QK_ROPE_DIM), 1.0),
        "c": nrm(ks[4], (DEC_BATCH, D_MODEL), 1.0),
        "c_ctx": nrm(ks[5], (D_MODEL,), 1.0),
        "norm1_g": gain(ks[6], D_MODEL),
        "norm2_g": gain(ks[7], D_MODEL),
        "w_ada": nrm(ks[8], (DEPTH, D_MODEL, 6 * D_MODEL), 0.5 * D_MODEL ** -0.5),
        "b_ada": nrm(ks[9], (DEPTH, 6 * D_MODEL), 0.01),
        "w_in": nrm(ks[10], (DEPTH, D_MODEL, N_IN), D_MODEL ** -0.5),
        "b_branch_gate": nrm(ks[11], (DEPTH, 2 * D_MODEL), 0.01),
        "kv_norm_g": gain(ks[12], KV_RANK),
        "w_ukv": nrm(ks[13], (DEPTH, KV_RANK, N_HEADS * (QK_NOPE_DIM + V_DIM)), KV_RANK ** -0.5),
        "q_norm_g": gain(ks[14], QK_DIM),
        "k_norm_g": gain(ks[15], QK_DIM),
        "w_o_attn": nrm(ks[16], (DEPTH, N_HEADS * V_DIM, D_MODEL), (N_HEADS * V_DIM) ** -0.5),
        "w_pool": nrm(ks[17], (DEPTH, len(POOL_WINDOWS), POOL_GROUP, POOL_GROUP), POOL_GROUP ** -0.5),
        "pool_scale": gain(ks[18], POOL_WIDTH),
        "w_o_pool": nrm(ks[19], (DEPTH, POOL_WIDTH, D_MODEL), POOL_WIDTH ** -0.5),
        "w_out": nrm(ks[20], (DEPTH, D_MODEL, D_MODEL), D_MODEL ** -0.5),
        "router_w": nrm(ks[21], (DEPTH, D_MODEL, N_EXPERTS), D_MODEL ** -0.5),
        "router_b": nrm(ks[22], (DEPTH, N_EXPERTS), 0.01),
        "w_exp_gate": nrm(ks[23], (DEPTH, N_EXPERTS, D_MODEL, D_FF), D_MODEL ** -0.5),
        "b_exp_gate": nrm(ks[24], (DEPTH, N_EXPERTS, D_FF), 0.01),
        "w_exp_up": nrm(ks[25], (DEPTH, N_EXPERTS, D_MODEL, D_FF), D_MODEL ** -0.5),
        "b_exp_up": nrm(ks[26], (DEPTH, N_EXPERTS, D_FF), 0.01),
        "w_exp_down": nrm(ks[27], (DEPTH, N_EXPERTS, D_FF, D_MODEL), D_FF ** -0.5),
        "b_exp_down": nrm(ks[28], (DEPTH, N_EXPERTS, D_MODEL), 0.01),
    }


def reference(x_prompt, x_sample, cache_ckv, cache_krope, c, c_ctx, norm1_g, norm2_g, w_ada, b_ada,
              w_in, b_branch_gate, kv_norm_g, w_ukv, q_norm_g, k_norm_g, w_o_attn, w_pool, pool_scale,
              w_o_pool, w_out, router_w, router_b, w_exp_gate, b_exp_gate, w_exp_up, b_exp_up,
              w_exp_down, b_exp_down):
    n_lat = x_sample.shape[1]
    rows = n_lat // GRID_W
    row = jnp.repeat(jnp.arange(rows, dtype=jnp.float32), GRID_W)
    col = jnp.tile(jnp.arange(GRID_W, dtype=jnp.float32), rows)
    y_p = x_prompt
    y_s = x_sample
    ckv_list = []
    krope_list = []
    for l in range(DEPTH):
        lw = (norm1_g[l], norm2_g[l], w_in[l], b_branch_gate[l], kv_norm_g[l], w_ukv[l], q_norm_g[l],
              k_norm_g[l], w_o_attn[l], w_pool[l], pool_scale[l], w_o_pool[l], w_out[l], router_w[l],
              router_b[l], w_exp_gate[l], b_exp_gate[l], w_exp_up[l], b_exp_up[l], w_exp_down[l],
              b_exp_down[l])
        mods_ctx = adaln_params(c_ctx, w_ada[l], b_ada[l])
        mods_lat = adaln_params(c[:, None, :], w_ada[l], b_ada[l])
        y_p, ckv_l, krope_l = trunk_layer(y_p, mods_ctx, lw)
        ckv_list.append(ckv_l)
        krope_list.append(krope_l)
        y_s, _, _ = trunk_layer(y_s, mods_lat, lw, ctx=(cache_ckv[:, l], cache_krope[:, l]), pos=(row, col))
    state_ckv = jnp.stack(ckv_list, axis=1)
    state_krope = jnp.stack(krope_list, axis=1)
    return (y_p, y_s, state_ckv, state_krope)
```

```python
import functools

import jax
import jax.numpy as jnp
from jax import lax
from jax.experimental import pallas as pl
from jax.experimental.pallas import tpu as pltpu

F32 = jnp.float32
BF16 = jnp.bfloat16
I32 = jnp.int32

D_MODEL = 2048
N_HEADS = 16
NOPE = 128
ROPE = 64
QK = NOPE + ROPE
V_DIM = 128
KV_RANK = 512
POOL_WINDOWS = (2, 4, 8, 16)
POOL_WIDTH = 1024
POOL_GROUP = POOL_WIDTH // len(POOL_WINDOWS)
N_EXPERTS = 32
TOP_K = 4
D_FF = 2048
SWIGLU_LIMIT = 7.0
SWIGLU_ALPHA = 1.702
ROPE_BASE = 10000.0
RMS_EPS = 1e-6
GRID_W = 64
ATTN_SCALE = QK ** -0.5

LANES = 128
HEAD_PAD = 2 * LANES
Q_COLS = N_HEADS * HEAD_PAD
IN_TILE = 1024
N_Q_TILES = Q_COLS // IN_TILE
POOL_TILE = N_Q_TILES
GATE_TILE0 = POOL_TILE + 1
N_GATE_TILES = 2 * D_MODEL // IN_TILE
KV_TILE = GATE_TILE0 + N_GATE_TILES
N_IN_TILES = KV_TILE + 1
HEADS_PER_TILE = IN_TILE // HEAD_PAD

MOE_CHUNK = 256
MOE_ITEM_CHUNKS = 8
MOE_ITEM_ROWS = MOE_CHUNK * MOE_ITEM_CHUNKS
FF_TILE = 256
N_FF_TILES = D_FF // FF_TILE
NEG_BIG = -1e30

VMEM_LIMIT = 56 * 1024 * 1024


def _cparams(sem, vmem=VMEM_LIMIT):
    return pltpu.CompilerParams(dimension_semantics=sem, vmem_limit_bytes=vmem)


def _dot(a, b):
    return jnp.dot(a, b, preferred_element_type=F32)


def _sigmoid(z):
    return 1.0 / (1.0 + jnp.exp(-z))


def _rope_lanes(y, cos, sin_lo, sin_hi):
    return y * cos + pltpu.roll(y, LANES - 16, axis=1) * sin_lo + pltpu.roll(y, 16, axis=1) * sin_hi


def _adaln_kernel(c_ref, w_ref, b_ref, o_ref):
    c = c_ref[...]
    s = (c * _sigmoid(c)).astype(BF16)
    o_ref[...] = _dot(s, w_ref[...].astype(BF16)) + b_ref[...]


def _adaln(cond, w_ada, b_ada):
    rows = cond.shape[0]
    n = w_ada.shape[1]
    tn = 1024
    return pl.pallas_call(
        _adaln_kernel,
        out_shape=jax.ShapeDtypeStruct((rows, n), F32),
        grid=(n // tn,),
        in_specs=[pl.BlockSpec((rows, D_MODEL), lambda j: (0, 0)),
                  pl.BlockSpec((D_MODEL, tn), lambda j: (0, j)),
                  pl.BlockSpec((1, tn), lambda j: (0, j))],
        out_specs=pl.BlockSpec((rows, tn), lambda j: (0, j)),
        compiler_params=_cparams(("arbitrary",)),
        name="adaln",
    )(cond, w_ada, b_ada)


def _in_proj_kernel(rope, x_ref, mods_ref, g1_ref, w_ref, qg_ref, kvg_ref, *rest):
    if rope:
        cos_ref, slo_ref, shi_ref = rest[:3]
        rest = rest[3:]
    q_ref, pool_ref, gates_ref, ckv_ref, kr_ref, krp_ref, h_scr = rest
    j = pl.program_id(1)

    @pl.when(j == 0)
    def _():
        x = x_ref[...]
        y = x * lax.rsqrt(jnp.mean(x * x, axis=-1, keepdims=True) + RMS_EPS) * g1_ref[...]
        h = y * (1.0 + mods_ref[1:2, :]) + mods_ref[0:1, :]
        h_scr[...] = h.astype(BF16)

    acc = _dot(h_scr[...], w_ref[...])

    @pl.when(j < N_Q_TILES)
    def _():
        for hh in range(HEADS_PER_TILE):
            a = acc[:, hh * HEAD_PAD:(hh + 1) * HEAD_PAD]
            r = lax.rsqrt(jnp.sum(a * a, axis=-1, keepdims=True) / QK + RMS_EPS)
            y = a * r * qg_ref[...]
            if rope:
                yr = _rope_lanes(y[:, LANES:], cos_ref[...], slo_ref[...], shi_ref[...])
                q_ref[:, hh * HEAD_PAD:hh * HEAD_PAD + LANES] = y[:, :LANES].astype(BF16)
                q_ref[:, hh * HEAD_PAD + LANES:(hh + 1) * HEAD_PAD] = yr.astype(BF16)
            else:
                q_ref[:, hh * HEAD_PAD:(hh + 1) * HEAD_PAD] = y.astype(BF16)

    @pl.when(j == POOL_TILE)
    def _():
        pool_ref[...] = acc

    @pl.when(jnp.logical_and(j >= GATE_TILE0, j < KV_TILE))
    def _():
        gates_ref[...] = acc

    @pl.when(j == KV_TILE)
    def _():
        a = acc[:, :KV_RANK]
        r = lax.rsqrt(jnp.mean(a * a, axis=-1, keepdims=True) + RMS_EPS)
        ckv_ref[...] = a * r * kvg_ref[...]
        krp = acc[:, KV_RANK:KV_RANK + LANES]
        krp_ref[...] = krp
        kr_ref[...] = krp[:, :ROPE]


def _in_proj(x, mods, mod_row, g1, w_cat, qg, kvg, rope_tabs, tm):
    t = x.shape[0]
    rope = rope_tabs is not None
    in_specs = [
        pl.BlockSpec((tm, D_MODEL), lambda i, j: (i, 0)),
        pl.BlockSpec((None, 6, D_MODEL), lambda i, j: (mod_row(i), 0, 0)),
        pl.BlockSpec((1, D_MODEL), lambda i, j: (0, 0)),
        pl.BlockSpec((D_MODEL, IN_TILE), lambda i, j: (0, j)),
        pl.BlockSpec((1, HEAD_PAD), lambda i, j: (0, 0)),
        pl.BlockSpec((1, KV_RANK), lambda i, j: (0, 0)),
    ]
    args = [x, mods, g1, w_cat, qg, kvg]
    if rope:
        seq_tiles = rope_tabs[0].shape[0] // tm
        in_specs += [pl.BlockSpec((tm, LANES), lambda i, j: (i % seq_tiles, 0))] * 3
        args += list(rope_tabs)
    out_shape = (
        jax.ShapeDtypeStruct((t, Q_COLS), BF16),
        jax.ShapeDtypeStruct((t, POOL_WIDTH), F32),
        jax.ShapeDtypeStruct((t, 2 * D_MODEL), F32),
        jax.ShapeDtypeStruct((t, KV_RANK), F32),
        jax.ShapeDtypeStruct((t, ROPE), F32),
        jax.ShapeDtypeStruct((t, LANES), F32),
    )
    out_specs = (
        pl.BlockSpec((tm, IN_TILE), lambda i, j: (i, jnp.minimum(j, N_Q_TILES - 1))),
        pl.BlockSpec((tm, POOL_WIDTH), lambda i, j: (i, 0)),
        pl.BlockSpec((tm, IN_TILE), lambda i, j: (i, jnp.clip(j - GATE_TILE0, 0, N_GATE_TILES - 1))),
        pl.BlockSpec((tm, KV_RANK), lambda i, j: (i, 0)),
        pl.BlockSpec((tm, ROPE), lambda i, j: (i, 0)),
        pl.BlockSpec((tm, LANES), lambda i, j: (i, 0)),
    )
    return pl.pallas_call(
        functools.partial(_in_proj_kernel, rope),
        out_shape=out_shape,
        grid=(t // tm, N_IN_TILES),
        in_specs=in_specs,
        out_specs=out_specs,
        scratch_shapes=[pltpu.VMEM((tm, D_MODEL), BF16)],
        compiler_params=_cparams(("arbitrary", "arbitrary")),
        name="in_proj_rope" if rope else "in_proj",
    )(*args)


def _kv_expand_kernel(rope, ckv_ref, krp_ref, w_ref, kgn_ref, kgr_ref, *rest):
    if rope:
        cos_ref, slo_ref, shi_ref = rest[:3]
        rest = rest[3:]
    k_ref, v_ref = rest
    kv = _dot(ckv_ref[...].astype(BF16), w_ref[...])
    kr = krp_ref[...]
    ssq_r = jnp.sum(kr * kr, axis=-1, keepdims=True)
    krg = kr * kgr_ref[...]
    if rope:
        krg = _rope_lanes(krg, cos_ref[...], slo_ref[...], shi_ref[...])
    for h in range(N_HEADS):
        kn = kv[:, h * HEAD_PAD:h * HEAD_PAD + NOPE]
        r = lax.rsqrt((jnp.sum(kn * kn, axis=-1, keepdims=True) + ssq_r) / QK + RMS_EPS)
        k_ref[:, h * HEAD_PAD:h * HEAD_PAD + NOPE] = (kn * r * kgn_ref[...]).astype(BF16)
        k_ref[:, h * HEAD_PAD + NOPE:(h + 1) * HEAD_PAD] = (krg * r).astype(BF16)
        v_ref[:, h * V_DIM:(h + 1) * V_DIM] = kv[:, h * HEAD_PAD + NOPE:(h + 1) * HEAD_PAD].astype(BF16)


def _kv_expand(ckv_n, krp, w_ukv, kgn, kgr, rope_tabs, tr):
    r = ckv_n.shape[0]
    rope = rope_tabs is not None
    in_specs = [
        pl.BlockSpec((tr, KV_RANK), lambda i: (i, 0)),
        pl.BlockSpec((tr, LANES), lambda i: (i, 0)),
        pl.BlockSpec((KV_RANK, N_HEADS * HEAD_PAD), lambda i: (0, 0)),
        pl.BlockSpec((1, LANES), lambda i: (0, 0)),
        pl.BlockSpec((1, LANES), lambda i: (0, 0)),
    ]
    args = [ckv_n, krp, w_ukv, kgn, kgr]
    if rope:
        seq_tiles = rope_tabs[0].shape[0] // tr
        in_specs += [pl.BlockSpec((tr, LANES), lambda i: (i % seq_tiles, 0))] * 3
        args += list(rope_tabs)
    return pl.pallas_call(
        functools.partial(_kv_expand_kernel, rope),
        out_shape=(jax.ShapeDtypeStruct((r, N_HEADS * HEAD_PAD), BF16),
                   jax.ShapeDtypeStruct((r, N_HEADS * V_DIM), BF16)),
        grid=(r // tr,),
        in_specs=in_specs,
        out_specs=(pl.BlockSpec((tr, N_HEADS * HEAD_PAD), lambda i: (i, 0)),
                   pl.BlockSpec((tr, N_HEADS * V_DIM), lambda i: (i, 0))),
        compiler_params=_cparams(("arbitrary",)),
        name="kv_expand_rope" if rope else "kv_expand",
    )(*args)


def _qk(q, k):
    return lax.dot_general(q, k, (((1,), (1,)), ((), ())), preferred_element_type=F32) * ATTN_SCALE


def _attn_ctx_kernel(q_ref, k_ref, v_ref, o_ref):
    for h in range(N_HEADS):
        s = _qk(q_ref[:, h * HEAD_PAD:(h + 1) * HEAD_PAD], k_ref[:, h * HEAD_PAD:(h + 1) * HEAD_PAD])
        p = jnp.exp(s - jnp.max(s, axis=-1, keepdims=True))
        l = jnp.sum(p, axis=-1, keepdims=True)
        o = _dot(p.astype(BF16), v_ref[:, h * V_DIM:(h + 1) * V_DIM])
        o_ref[:, h * V_DIM:(h + 1) * V_DIM] = (o / l).astype(BF16)


def _attn_ctx(q, k, v, seq):
    t = q.shape[0]
    return pl.pallas_call(
        _attn_ctx_kernel,
        out_shape=jax.ShapeDtypeStruct((t, N_HEADS * V_DIM), BF16),
        grid=(t // seq,),
        in_specs=[pl.BlockSpec((seq, Q_COLS), lambda b: (b, 0)),
                  pl.BlockSpec((seq, Q_COLS), lambda b: (b, 0)),
                  pl.BlockSpec((seq, N_HEADS * V_DIM), lambda b: (b, 0))],
        out_specs=pl.BlockSpec((seq, N_HEADS * V_DIM), lambda b: (b, 0)),
        compiler_params=_cparams(("arbitrary",)),
        name="attn_ctx",
    )(q, k, v)


def _attn_lat_kernel(q_ref, k_ref, v_ref, kc_ref, vc_ref, o_ref):
    q = q_ref[...]
    s1 = _qk(q, k_ref[...])
    s2 = _qk(q, kc_ref[...])
    m = jnp.maximum(jnp.max(s1, axis=-1, keepdims=True), jnp.max(s2, axis=-1, keepdims=True))
    p1 = jnp.exp(s1 - m)
    p2 = jnp.exp(s2 - m)
    l = jnp.sum(p1, axis=-1, keepdims=True) + jnp.sum(p2, axis=-1, keepdims=True)
    o = _dot(p1.astype(BF16), v_ref[...]) + _dot(p2.astype(BF16), vc_ref[...])
    o_ref[...] = (o / l).astype(BF16)


def _attn_lat(q, k, v, kc, vc, seq, past, tq):
    t = q.shape[0]
    nq = seq // tq
    return pl.pallas_call(
        _attn_lat_kernel,
        out_shape=jax.ShapeDtypeStruct((t, N_HEADS * V_DIM), BF16),
        grid=(t // seq, N_HEADS, nq),
        in_specs=[pl.BlockSpec((tq, HEAD_PAD), lambda b, h, i: (b * nq + i, h)),
                  pl.BlockSpec((seq, HEAD_PAD), lambda b, h, i: (b, h)),
                  pl.BlockSpec((seq, V_DIM), lambda b, h, i: (b, h)),
                  pl.BlockSpec((past, HEAD_PAD), lambda b, h, i: (b, h)),
                  pl.BlockSpec((past, V_DIM), lambda b, h, i: (b, h))],
        out_specs=pl.BlockSpec((tq, V_DIM), lambda b, h, i: (b * nq + i, h)),
        compiler_params=_cparams(("arbitrary", "arbitrary", "arbitrary")),
        name="attn_lat",
    )(q, k, v, kc, vc)


POOL_HALO = 8


def _pool_kernel(seq, u_ref, w_ref, sc_ref, o_ref, pad_scr):
    zeros = jnp.zeros((POOL_HALO, POOL_WIDTH), F32)
    pad_scr[0:POOL_HALO, :] = zeros
    pad_scr[POOL_HALO + seq:2 * POOL_HALO + seq, :] = zeros
    pad_scr[POOL_HALO:POOL_HALO + seq, :] = u_ref[...]
    t = lax.broadcasted_iota(I32, (seq, 1), 0)
    for g, w in enumerate(POOL_WINDOWS):
        cols = slice(g * POOL_GROUP, (g + 1) * POOL_GROUP)
        tot = None
        for d in range(-(w // 2), w - w // 2):
            piece = pad_scr[POOL_HALO + d:POOL_HALO + d + seq, cols]
            tot = piece if tot is None else tot + piece
        cnt = (jnp.minimum(t + (w - w // 2), seq) - jnp.maximum(t - w // 2, 0)).astype(F32)
        mixed = tot / cnt - u_ref[:, cols]
        o_ref[:, cols] = (_dot(mixed.astype(BF16), w_ref[g]) * sc_ref[:, cols]).astype(BF16)


def _pool(u, w_pool, pool_scale, seq):
    t = u.shape[0]
    n_groups = len(POOL_WINDOWS)
    return pl.pallas_call(
        functools.partial(_pool_kernel, seq),
        out_shape=jax.ShapeDtypeStruct((t, POOL_WIDTH), BF16),
        grid=(t // seq,),
        in_specs=[pl.BlockSpec((seq, POOL_WIDTH), lambda b: (b, 0)),
                  pl.BlockSpec((n_groups, POOL_GROUP, POOL_GROUP), lambda b: (0, 0, 0)),
                  pl.BlockSpec((1, POOL_WIDTH), lambda b: (0, 0))],
        out_specs=pl.BlockSpec((seq, POOL_WIDTH), lambda b: (b, 0)),
        scratch_shapes=[pltpu.VMEM((seq + 2 * POOL_HALO, POOL_WIDTH), F32)],
        compiler_params=_cparams(("arbitrary",)),
        name="pool",
    )(u, w_pool, pool_scale)


def _merge_kernel(attn_ref, pool_ref, gates_ref, x_ref, mods_ref, bbg_ref, woa_ref, wop_ref, wout_ref,
                  g2_ref, rw_ref, rb_ref, x1_ref, h2_ref, tidx_ref, tw_ref):
    a = _dot(attn_ref[...], woa_ref[...])
    p = _dot(pool_ref[...], wop_ref[...])
    ga = _sigmoid(gates_ref[:, :D_MODEL] + bbg_ref[:, :D_MODEL])
    gp = _sigmoid(gates_ref[:, D_MODEL:] + bbg_ref[:, D_MODEL:])
    merged = (ga * a + gp * p).astype(BF16)
    x1 = x_ref[...] + mods_ref[2:3, :] * _dot(merged, wout_ref[...])
    x1_ref[...] = x1
    y = x1 * lax.rsqrt(jnp.mean(x1 * x1, axis=-1, keepdims=True) + RMS_EPS) * g2_ref[...]
    h2 = y * (1.0 + mods_ref[4:5, :]) + mods_ref[3:4, :]
    h2_ref[...] = h2
    logits = _dot(h2.astype(BF16), rw_ref[...]) + rb_ref[...]
    lane = lax.broadcasted_iota(I32, logits.shape, 1).astype(F32)
    vals, idxs = [], []
    for _ in range(TOP_K):
        m = jnp.max(logits, axis=-1, keepdims=True)
        ix = jnp.min(jnp.where(logits == m, lane, float(LANES)), axis=-1, keepdims=True)
        vals.append(m)
        idxs.append(ix)
        logits = jnp.where(lane == ix, -jnp.inf, logits)
    es = [jnp.exp(v - vals[0]) for v in vals]
    tot = es[0] + es[1] + es[2] + es[3]
    tidx = jnp.zeros(logits.shape, F32)
    tw = jnp.zeros(logits.shape, F32)
    for k in range(TOP_K):
        tidx = jnp.where(lane == k, idxs[k], tidx)
        tw = jnp.where(lane == k, es[k] / tot, tw)
    tidx_ref[...] = tidx.astype(I32)
    tw_ref[...] = tw


def _merge(attn_o, pool_o, gates, x, mods, mod_row, bbg, woa, wop, wout, g2, rw, rb, tm):
    t = x.shape[0]
    const = lambda shape: pl.BlockSpec(shape, lambda i: (0, 0), pipeline_mode=pl.Buffered(1))
    return pl.pallas_call(
        _merge_kernel,
        out_shape=(jax.ShapeDtypeStruct((t, D_MODEL), F32),
                   jax.ShapeDtypeStruct((t, D_MODEL), F32),
                   jax.ShapeDtypeStruct((t, LANES), I32),
                   jax.ShapeDtypeStruct((t, LANES), F32)),
        grid=(t // tm,),
        in_specs=[pl.BlockSpec((tm, D_MODEL), lambda i: (i, 0)),
                  pl.BlockSpec((tm, POOL_WIDTH), lambda i: (i, 0)),
                  pl.BlockSpec((tm, 2 * D_MODEL), lambda i: (i, 0)),
                  pl.BlockSpec((tm, D_MODEL), lambda i: (i, 0)),
                  pl.BlockSpec((None, 6, D_MODEL), lambda i: (mod_row(i), 0, 0)),
                  const((1, 2 * D_MODEL)),
                  const((D_MODEL, D_MODEL)),
                  const((POOL_WIDTH, D_MODEL)),
                  const((D_MODEL, D_MODEL)),
                  const((1, D_MODEL)),
                  const((D_MODEL, LANES)),
                  const((1, LANES))],
        out_specs=(pl.BlockSpec((tm, D_MODEL), lambda i: (i, 0)),
                   pl.BlockSpec((tm, D_MODEL), lambda i: (i, 0)),
                   pl.BlockSpec((tm, LANES), lambda i: (i, 0)),
                   pl.BlockSpec((tm, LANES), lambda i: (i, 0))),
        compiler_params=_cparams(("arbitrary",)),
        name="merge",
    )(attn_o, pool_o, gates, x, mods, bbg, woa, wop, wout, g2, rw, rb)


DISPATCH_ROWS = 512


def _dispatch_kernel(n_ctx, tok_ref, hc_ref, hl_ref, xs_ref, sem):
    base = pl.program_id(0) * DISPATCH_ROWS

    def row_copy(src_ref, src_row, slot):
        return pltpu.make_async_copy(src_ref.at[pl.ds(src_row, 1), :], xs_ref.at[pl.ds(slot, 1), :], sem)

    def issue(r, carry):
        tok = tok_ref[base + r]

        @pl.when(tok < n_ctx)
        def _():
            row_copy(hc_ref, tok, base + r).start()

        @pl.when(tok >= n_ctx)
        def _():
            row_copy(hl_ref, tok - n_ctx, base + r).start()

        return carry

    lax.fori_loop(0, DISPATCH_ROWS, issue, 0)

    def drain(r, carry):
        row_copy(hc_ref, 0, base + r).wait()
        return carry

    lax.fori_loop(0, DISPATCH_ROWS, drain, 0)


def _dispatch(buf_tok, h2_ctx, h2_lat, cap):
    return pl.pallas_call(
        functools.partial(_dispatch_kernel, h2_ctx.shape[0]),
        out_shape=jax.ShapeDtypeStruct((cap, D_MODEL), F32),
        grid_spec=pltpu.PrefetchScalarGridSpec(
            num_scalar_prefetch=1,
            grid=(cap // DISPATCH_ROWS,),
            in_specs=[pl.BlockSpec(memory_space=pl.ANY), pl.BlockSpec(memory_space=pl.ANY)],
            out_specs=pl.BlockSpec(memory_space=pl.ANY),
            scratch_shapes=[pltpu.SemaphoreType.DMA]),
        compiler_params=_cparams(("arbitrary",)),
        name="moe_dispatch",
    )(buf_tok, h2_ctx, h2_lat)


def _moe_kernel(ie_ref, ir_ref, in_ref, xs_ref, wg_ref, bg_ref, wu_ref, bu_ref, wd_ref, bd_ref, ys_ref,
                xb_scr, acc_scr, stage_scr, wg_scr, wu_scr, wd_scr, sem):
    i = pl.program_id(0)
    f = pl.program_id(1)
    nch = in_ref[i]
    row0 = ir_ref[i]

    def rows_of(c):
        return pl.ds(pl.multiple_of(c * MOE_CHUNK, MOE_CHUNK), MOE_CHUNK)

    def hbm_rows(c):
        return pl.ds(pl.multiple_of(row0 + c * MOE_CHUNK, MOE_CHUNK), MOE_CHUNK)

    @pl.when(nch > 0)
    def _():
        @pl.when(f == 0)
        def _():
            def load(c, carry):
                cp = pltpu.make_async_copy(xs_ref.at[hbm_rows(c), :], stage_scr, sem)
                cp.start()
                cp.wait()
                xb_scr[rows_of(c), :] = stage_scr[...].astype(BF16)
                return carry

            lax.fori_loop(0, nch, load, 0)

        wg_scr[...] = wg_ref[...].astype(BF16)
        wu_scr[...] = wu_ref[...].astype(BF16)
        wd_scr[...] = wd_ref[...].astype(BF16)

        def contribution(c):
            x = xb_scr[rows_of(c), :]
            gt = jnp.minimum(_dot(x, wg_scr[...]) + bg_ref[...], SWIGLU_LIMIT)
            up = jnp.clip(_dot(x, wu_scr[...]) + bu_ref[...], -SWIGLU_LIMIT, SWIGLU_LIMIT)
            act = (up + 1.0) * (gt * _sigmoid(SWIGLU_ALPHA * gt))
            return _dot(act.astype(BF16), wd_scr[...])

        @pl.when(f == 0)
        def _():
            def first(c, carry):
                acc_scr[rows_of(c), :] = contribution(c) + bd_ref[...]
                return carry

            lax.fori_loop(0, nch, first, 0)

        @pl.when(f > 0)
        def _():
            def more(c, carry):
                acc_scr[rows_of(c), :] += contribution(c)
                return carry

            lax.fori_loop(0, nch, more, 0)

        @pl.when(f == N_FF_TILES - 1)
        def _():
            def store(c, carry):
                cp = pltpu.make_async_copy(acc_scr.at[rows_of(c), :], ys_ref.at[hbm_rows(c), :], sem)
                cp.start()
                cp.wait()
                return carry

            lax.fori_loop(0, nch, store, 0)


def _moe_experts(item_e, item_row0, item_nch, xs, w_g, b_g, w_u, b_u, w_d, b_d):
    cap = xs.shape[0]
    n_items = item_e.shape[0]

    def ff_idx(i, f, ie, ir, inch):
        return jnp.where(inch[i] > 0, f, N_FF_TILES - 1)

    return pl.pallas_call(
        _moe_kernel,
        out_shape=jax.ShapeDtypeStruct((cap, D_MODEL), F32),
        grid_spec=pltpu.PrefetchScalarGridSpec(
            num_scalar_prefetch=3,
            grid=(n_items, N_FF_TILES),
            in_specs=[
                pl.BlockSpec(memory_space=pl.ANY),
                pl.BlockSpec((None, D_MODEL, FF_TILE), lambda i, f, ie, ir, inch: (ie[i], 0, ff_idx(i, f, ie, ir, inch))),
                pl.BlockSpec((None, 1, FF_TILE), lambda i, f, ie, ir, inch: (ie[i], 0, ff_idx(i, f, ie, ir, inch))),
                pl.BlockSpec((None, D_MODEL, FF_TILE), lambda i, f, ie, ir, inch: (ie[i], 0, ff_idx(i, f, ie, ir, inch))),
                pl.BlockSpec((None, 1, FF_TILE), lambda i, f, ie, ir, inch: (ie[i], 0, ff_idx(i, f, ie, ir, inch))),
                pl.BlockSpec((None, FF_TILE, D_MODEL), lambda i, f, ie, ir, inch: (ie[i], ff_idx(i, f, ie, ir, inch), 0)),
                pl.BlockSpec((None, 1, D_MODEL), lambda i, f, ie, ir, inch: (ie[i], 0, 0)),
            ],
            out_specs=pl.BlockSpec(memory_space=pl.ANY),
            scratch_shapes=[
                pltpu.VMEM((MOE_ITEM_ROWS, D_MODEL), BF16),
                pltpu.VMEM((MOE_ITEM_ROWS, D_MODEL), F32),
                pltpu.VMEM((MOE_CHUNK, D_MODEL), F32),
                pltpu.VMEM((D_MODEL, FF_TILE), BF16),
                pltpu.VMEM((D_MODEL, FF_TILE), BF16),
                pltpu.VMEM((FF_TILE, D_MODEL), BF16),
                pltpu.SemaphoreType.DMA,
            ]),
        compiler_params=_cparams(("arbitrary", "arbitrary")),
        name="moe_experts",
    )(item_e, item_row0, item_nch, xs, w_g, b_g, w_u, b_u, w_d, b_d)


COMBINE_ROWS = 128


def _combine_kernel(dest_ref, ys_ref, x1_ref, tw_ref, mods_ref, o_ref, ybuf, sem):
    base = pl.program_id(0) * (COMBINE_ROWS * TOP_K)

    def row_copy(src_row, r):
        return pltpu.make_async_copy(ys_ref.at[pl.ds(src_row, 1), :],
                                     ybuf.at[r % TOP_K, pl.ds(r // TOP_K, 1), :], sem)

    def issue(r, carry):
        row_copy(dest_ref[base + r], r).start()
        return carry

    lax.fori_loop(0, COMBINE_ROWS * TOP_K, issue, 0)

    def drain(r, carry):
        row_copy(0, r).wait()
        return carry

    lax.fori_loop(0, COMBINE_ROWS * TOP_K, drain, 0)

    y = tw_ref[:, 0:1] * ybuf[0]
    for k in range(1, TOP_K):
        y = y + tw_ref[:, k:k + 1] * ybuf[k]
    o_ref[...] = x1_ref[...] + mods_ref[5:6, :] * y


def _combine(dest, ys, x1, tw, mods, mod_row):
    t = x1.shape[0]
    return pl.pallas_call(
        _combine_kernel,
        out_shape=jax.ShapeDtypeStruct((t, D_MODEL), F32),
        grid_spec=pltpu.PrefetchScalarGridSpec(
            num_scalar_prefetch=1,
            grid=(t // COMBINE_ROWS,),
            in_specs=[pl.BlockSpec(memory_space=pl.ANY),
                      pl.BlockSpec((COMBINE_ROWS, D_MODEL), lambda i, d: (i, 0)),
                      pl.BlockSpec((COMBINE_ROWS, LANES), lambda i, d: (i, 0)),
                      pl.BlockSpec((None, 6, D_MODEL), lambda i, d: (mod_row(i * COMBINE_ROWS), 0, 0))],
            out_specs=pl.BlockSpec((COMBINE_ROWS, D_MODEL), lambda i, d: (i, 0)),
            scratch_shapes=[pltpu.VMEM((TOP_K, COMBINE_ROWS, D_MODEL), F32), pltpu.SemaphoreType.DMA]),
        compiler_params=_cparams(("arbitrary",)),
        name="moe_combine",
    )(dest, ys, x1, tw, mods)


def _routing_tables(top_idx, n_items, cap):
    n_tok = top_idx.shape[0]
    flat_e = top_idx.reshape(-1)
    onehot = (flat_e[:, None] == jnp.arange(N_EXPERTS, dtype=I32)[None, :]).astype(I32)
    csum = jnp.cumsum(onehot, axis=0)
    rank = jnp.take_along_axis(csum, flat_e[:, None], axis=1)[:, 0] - 1
    counts = csum[-1]
    nch = (counts + MOE_CHUNK - 1) // MOE_CHUNK
    pad_rows = nch * MOE_CHUNK
    pad_end = jnp.cumsum(pad_rows)
    pad_start = pad_end - pad_rows
    dest = (pad_start[flat_e] + rank).astype(I32)
    buf_tok = jnp.zeros((cap,), I32).at[dest].set(jnp.arange(n_tok * TOP_K, dtype=I32) // TOP_K)
    items_e = (nch + MOE_ITEM_CHUNKS - 1) // MOE_ITEM_CHUNKS
    item_end = jnp.cumsum(items_e)
    item_start = item_end - items_e
    total = item_end[-1]
    i = jnp.arange(n_items, dtype=I32)
    ii = jnp.minimum(i, total - 1)
    e_i = jnp.minimum(jnp.searchsorted(item_end, ii, side="right"), N_EXPERTS - 1).astype(I32)
    local = ii - item_start[e_i]
    row0 = (pad_start[e_i] + local * MOE_ITEM_ROWS).astype(I32)
    n_i = jnp.where(i < total, jnp.minimum(MOE_ITEM_CHUNKS, nch[e_i] - local * MOE_ITEM_CHUNKS), 0).astype(I32)
    return dest, buf_tok, e_i, row0, n_i


def _rope_tables(n_lat):
    nf = ROPE // 4
    inv = ROPE_BASE ** (-jnp.arange(nf, dtype=F32) / nf)
    t = jnp.arange(n_lat)
    row = (t // GRID_W).astype(F32)
    col = (t % GRID_W).astype(F32)
    ang_r = row[:, None] * inv[None, :]
    ang_c = col[:, None] * inv[None, :]
    z = jnp.zeros((n_lat, nf), F32)
    tail = jnp.zeros((n_lat, LANES - ROPE), F32)
    cos = jnp.concatenate([jnp.cos(ang_r), jnp.cos(ang_r), jnp.cos(ang_c), jnp.cos(ang_c), tail], axis=1)
    sin_lo = jnp.concatenate([-jnp.sin(ang_r), z, -jnp.sin(ang_c), z, tail], axis=1)
    sin_hi = jnp.concatenate([z, jnp.sin(ang_r), z, jnp.sin(ang_c), tail], axis=1)
    return cos, sin_lo, sin_hi


def kernel(x_prompt, x_sample, cache_ckv, cache_krope, c, c_ctx, norm1_g, norm2_g, w_ada, b_ada, w_in,
           b_branch_gate, kv_norm_g, w_ukv, q_norm_g, k_norm_g, w_o_attn, w_pool, pool_scale, w_o_pool, w_out,
           router_w, router_b, w_exp_gate, b_exp_gate, w_exp_up, b_exp_up, w_exp_down, b_exp_down):
    assert w_in.shape[0] == 1, "single-layer trunk"
    batch, seq, _ = x_prompt.shape
    dec_batch, n_lat, _ = x_sample.shape
    past = cache_ckv.shape[2]
    n_ctx = batch * seq
    n_dec = dec_batch * n_lat

    cond = jnp.concatenate([c_ctx[None, :], c, jnp.zeros((8 - 1 - dec_batch, D_MODEL), F32)], axis=0)
    mods = _adaln(cond, w_ada[0], b_ada).reshape(8, 6, D_MODEL)

    w = w_in[0]
    i0 = N_HEADS * QK
    i1 = i0 + KV_RANK
    i2 = i1 + ROPE
    i3 = i2 + POOL_WIDTH
    wq = jnp.pad(w[:, :i0].reshape(D_MODEL, N_HEADS, QK), ((0, 0), (0, 0), (0, HEAD_PAD - QK)))
    w_cat = jnp.concatenate([wq.reshape(D_MODEL, Q_COLS), w[:, i2:i3], w[:, i3:], w[:, i0:i1], w[:, i1:i2],
                             jnp.zeros((D_MODEL, IN_TILE - KV_RANK - ROPE), F32)], axis=1).astype(BF16)
    qg = jnp.pad(q_norm_g, ((0, 0), (0, HEAD_PAD - QK)))
    kgn = k_norm_g[:, :NOPE]
    kgr = jnp.pad(k_norm_g[:, NOPE:], ((0, 0), (0, LANES - ROPE)))
    w_ukv_b = w_ukv[0].astype(BF16)
    rope_tabs = _rope_tables(n_lat)

    ctx_row = lambda i: 0
    tm_in = 512
    lat_row_in = lambda i: 1 + (i * tm_in) // n_lat
    q_c, pool_c, gates_c, ckv_c, kr_c, krp_c = _in_proj(
        x_prompt.reshape(n_ctx, D_MODEL), mods, ctx_row, norm1_g, w_cat, qg, kv_norm_g, None, tm_in)
    q_l, pool_l, gates_l, ckv_l, _, krp_l = _in_proj(
        x_sample.reshape(n_dec, D_MODEL), mods, lat_row_in, norm1_g, w_cat, qg, kv_norm_g, rope_tabs, tm_in)

    tr = 256
    k_c, v_c = _kv_expand(ckv_c, krp_c, w_ukv_b, kgn, kgr, None, tr)
    k_l, v_l = _kv_expand(ckv_l, krp_l, w_ukv_b, kgn, kgr, rope_tabs, tr)
    cache_krp = jnp.pad(cache_krope.reshape(dec_batch * past, ROPE), ((0, 0), (0, LANES - ROPE)))
    k_p, v_p = _kv_expand(cache_ckv.reshape(dec_batch * past, KV_RANK), cache_krp, w_ukv_b, kgn, kgr, None, tr)

    attn_c = _attn_ctx(q_c, k_c, v_c, seq)
    attn_l = _attn_lat(q_l, k_l, v_l, k_p, v_p, n_lat, past, 512)

    w_pool_b = w_pool[0].astype(BF16)
    poolo_c = _pool(pool_c, w_pool_b, pool_scale, seq)
    poolo_l = _pool(pool_l, w_pool_b, pool_scale, n_lat)

    woa = w_o_attn[0].astype(BF16)
    wop = w_o_pool[0].astype(BF16)
    wout = w_out[0].astype(BF16)
    rw = jnp.pad(router_w[0], ((0, 0), (0, LANES - N_EXPERTS))).astype(BF16)
    rb = jnp.pad(router_b, ((0, 0), (0, LANES - N_EXPERTS)), constant_values=NEG_BIG)
    tm_mg = 256
    lat_row_mg = lambda i: 1 + (i * tm_mg) // n_lat
    x1_c, h2_c, tidx_c, tw_c = _merge(attn_c, poolo_c, gates_c, x_prompt.reshape(n_ctx, D_MODEL), mods, ctx_row,
                                      b_branch_gate, woa, wop, wout, norm2_g, rw, rb, tm_mg)
    x1_l, h2_l, tidx_l, tw_l = _merge(attn_l, poolo_l, gates_l, x_sample.reshape(n_dec, D_MODEL), mods, lat_row_mg,
                                      b_branch_gate, woa, wop, wout, norm2_g, rw, rb, tm_mg)

    n_assign = (n_ctx + n_dec) * TOP_K
    max_chunks = (n_assign + N_EXPERTS * (MOE_CHUNK - 1)) // MOE_CHUNK
    cap = -(-max_chunks * MOE_CHUNK // DISPATCH_ROWS) * DISPATCH_ROWS
    n_items = (max_chunks + N_EXPERTS * (MOE_ITEM_CHUNKS - 1)) // MOE_ITEM_CHUNKS
    top_idx = jnp.concatenate([tidx_c[:, :TOP_K], tidx_l[:, :TOP_K]], axis=0)
    dest, buf_tok, item_e, item_row0, item_nch = _routing_tables(top_idx, n_items, cap)

    xs = _dispatch(buf_tok, h2_c, h2_l, cap)
    ys = _moe_experts(item_e, item_row0, item_nch, xs,
                      w_exp_gate[0], b_exp_gate[0][:, None, :], w_exp_up[0], b_exp_up[0][:, None, :],
                      w_exp_down[0], b_exp_down[0][:, None, :])

    y_c = _combine(dest[:n_ctx * TOP_K], ys, x1_c, tw_c, mods, lambda r: 0)
    y_l = _combine(dest[n_ctx * TOP_K:], ys, x1_l, tw_l, mods, lambda r: 1 + r // n_lat)

    return (y_c.reshape(batch, seq, D_MODEL),
            y_l.reshape(dec_batch, n_lat, D_MODEL),
            ckv_c.reshape(batch, 1, seq, KV_RANK),
            kr_c.reshape(batch, 1, seq, ROPE))
```

```python
import functools

import jax
import jax.numpy as jnp
from jax import lax
from jax.experimental import pallas as pl
from jax.experimental.pallas import tpu as pltpu

F32 = jnp.float32
BF16 = jnp.bfloat16
I32 = jnp.int32

D_MODEL = 2048
N_HEADS = 16
NOPE = 128
ROPE = 64
QK = NOPE + ROPE
V_DIM = 128
KV_RANK = 512
POOL_WINDOWS = (2, 4, 8, 16)
POOL_WIDTH = 1024
POOL_GROUP = POOL_WIDTH // len(POOL_WINDOWS)
N_EXPERTS = 32
TOP_K = 4
D_FF = 2048
SWIGLU_LIMIT = 7.0
SWIGLU_ALPHA = 1.702
ROPE_BASE = 10000.0
RMS_EPS = 1e-6
GRID_W = 64
ATTN_SCALE = QK ** -0.5

LANES = 128
HEAD_PAD = 2 * LANES
Q_COLS = N_HEADS * HEAD_PAD
IN_TILE = 1024
N_Q_TILES = Q_COLS // IN_TILE
POOL_TILE = N_Q_TILES
GATE_TILE0 = POOL_TILE + 1
N_GATE_TILES = 2 * D_MODEL // IN_TILE
KV_TILE = GATE_TILE0 + N_GATE_TILES
N_IN_TILES = KV_TILE + 1
HEADS_PER_TILE = IN_TILE // HEAD_PAD

MOE_CHUNK = 256
MOE_ITEM_CHUNKS = 8
MOE_ITEM_ROWS = MOE_CHUNK * MOE_ITEM_CHUNKS
FF_TILE = 256
N_FF_TILES = D_FF // FF_TILE
NEG_BIG = -1e30

VMEM_LIMIT = 56 * 1024 * 1024


def _cparams(sem, vmem=VMEM_LIMIT):
    return pltpu.CompilerParams(dimension_semantics=sem, vmem_limit_bytes=vmem)


def _dot(a, b):
    return jnp.dot(a, b, preferred_element_type=F32)


def _sigmoid(z):
    return 1.0 / (1.0 + jnp.exp(-z))


def _rope_lanes(y, cos, sin_lo, sin_hi):
    return y * cos + pltpu.roll(y, LANES - 16, axis=1) * sin_lo + pltpu.roll(y, 16, axis=1) * sin_hi


def _adaln_kernel(c_ref, w_ref, b_ref, o_ref):
    c = c_ref[...]
    s = (c * _sigmoid(c)).astype(BF16)
    o_ref[...] = _dot(s, w_ref[...].astype(BF16)) + b_ref[...]


def _adaln(cond, w_ada, b_ada):
    rows = cond.shape[0]
    n = w_ada.shape[1]
    tn = 1024
    return pl.pallas_call(
        _adaln_kernel,
        out_shape=jax.ShapeDtypeStruct((rows, n), F32),
        grid=(n // tn,),
        in_specs=[pl.BlockSpec((rows, D_MODEL), lambda j: (0, 0)),
                  pl.BlockSpec((D_MODEL, tn), lambda j: (0, j)),
                  pl.BlockSpec((1, tn), lambda j: (0, j))],
        out_specs=pl.BlockSpec((rows, tn), lambda j: (0, j)),
        compiler_params=_cparams(("arbitrary",)),
        name="adaln",
    )(cond, w_ada, b_ada)


def _in_proj_kernel(rope, x_ref, mods_ref, g1_ref, w_ref, qg_ref, kvg_ref, *rest):
    if rope:
        cos_ref, slo_ref, shi_ref = rest[:3]
        rest = rest[3:]
    q_ref, pool_ref, gates_ref, ckv_ref, kr_ref, krp_ref, h_scr = rest
    j = pl.program_id(1)

    @pl.when(j == 0)
    def _():
        x = x_ref[...]
        y = x * lax.rsqrt(jnp.mean(x * x, axis=-1, keepdims=True) + RMS_EPS) * g1_ref[...]
        h = y * (1.0 + mods_ref[1:2, :]) + mods_ref[0:1, :]
        h_scr[...] = h.astype(BF16)

    acc = _dot(h_scr[...], w_ref[...])

    @pl.when(j < N_Q_TILES)
    def _():
        for hh in range(HEADS_PER_TILE):
            a = acc[:, hh * HEAD_PAD:(hh + 1) * HEAD_PAD]
            r = lax.rsqrt(jnp.sum(a * a, axis=-1, keepdims=True) / QK + RMS_EPS)
            y = a * r * qg_ref[...]
            if rope:
                yr = _rope_lanes(y[:, LANES:], cos_ref[...], slo_ref[...], shi_ref[...])
                q_ref[:, hh * HEAD_PAD:hh * HEAD_PAD + LANES] = y[:, :LANES].astype(BF16)
                q_ref[:, hh * HEAD_PAD + LANES:(hh + 1) * HEAD_PAD] = yr.astype(BF16)
            else:
                q_ref[:, hh * HEAD_PAD:(hh + 1) * HEAD_PAD] = y.astype(BF16)

    @pl.when(j == POOL_TILE)
    def _():
        pool_ref[...] = acc

    @pl.when(jnp.logical_and(j >= GATE_TILE0, j < KV_TILE))
    def _():
        gates_ref[...] = acc

    @pl.when(j == KV_TILE)
    def _():
        a = acc[:, :KV_RANK]
        r = lax.rsqrt(jnp.mean(a * a, axis=-1, keepdims=True) + RMS_EPS)
        ckv_ref[...] = a * r * kvg_ref[...]
        krp = acc[:, KV_RANK:KV_RANK + LANES]
        krp_ref[...] = krp
        kr_ref[...] = krp[:, :ROPE]


def _in_proj(x, mods, mod_row, g1, w_cat, qg, kvg, rope_tabs, tm):
    t = x.shape[0]
    rope = rope_tabs is not None
    in_specs = [
        pl.BlockSpec((tm, D_MODEL), lambda i, j: (i, 0)),
        pl.BlockSpec((None, 6, D_MODEL), lambda i, j: (mod_row(i), 0, 0)),
        pl.BlockSpec((1, D_MODEL), lambda i, j: (0, 0)),
        pl.BlockSpec((D_MODEL, IN_TILE), lambda i, j: (0, j)),
        pl.BlockSpec((1, HEAD_PAD), lambda i, j: (0, 0)),
        pl.BlockSpec((1, KV_RANK), lambda i, j: (0, 0)),
    ]
    args = [x, mods, g1, w_cat, qg, kvg]
    if rope:
        seq_tiles = rope_tabs[0].shape[0] // tm
        in_specs += [pl.BlockSpec((tm, LANES), lambda i, j: (i % seq_tiles, 0))] * 3
        args += list(rope_tabs)
    out_shape = (
        jax.ShapeDtypeStruct((t, Q_COLS), BF16),
        jax.ShapeDtypeStruct((t, POOL_WIDTH), F32),
        jax.ShapeDtypeStruct((t, 2 * D_MODEL), F32),
        jax.ShapeDtypeStruct((t, KV_RANK), F32),
        jax.ShapeDtypeStruct((t, ROPE), F32),
        jax.ShapeDtypeStruct((t, LANES), F32),
    )
    out_specs = (
        pl.BlockSpec((tm, IN_TILE), lambda i, j: (i, jnp.minimum(j, N_Q_TILES - 1))),
        pl.BlockSpec((tm, POOL_WIDTH), lambda i, j: (i, 0)),
        pl.BlockSpec((tm, IN_TILE), lambda i, j: (i, jnp.clip(j - GATE_TILE0, 0, N_GATE_TILES - 1))),
        pl.BlockSpec((tm, KV_RANK), lambda i, j: (i, 0)),
        pl.BlockSpec((tm, ROPE), lambda i, j: (i, 0)),
        pl.BlockSpec((tm, LANES), lambda i, j: (i, 0)),
    )
    return pl.pallas_call(
        functools.partial(_in_proj_kernel, rope),
        out_shape=out_shape,
        grid=(t // tm, N_IN_TILES),
        in_specs=in_specs,
        out_specs=out_specs,
        scratch_shapes=[pltpu.VMEM((tm, D_MODEL), BF16)],
        compiler_params=_cparams(("arbitrary", "arbitrary")),
        name="in_proj_rope" if rope else "in_proj",
    )(*args)


def _kv_expand_kernel(rope, ckv_ref, krp_ref, w_ref, kgn_ref, kgr_ref, *rest):
    if rope:
        cos_ref, slo_ref, shi_ref = rest[:3]
        rest = rest[3:]
    k_ref, v_ref = rest
    kv = _dot(ckv_ref[...].astype(BF16), w_ref[...])
    kr = krp_ref[...]
    ssq_r = jnp.sum(kr * kr, axis=-1, keepdims=True)
    krg = kr * kgr_ref[...]
    if rope:
        krg = _rope_lanes(krg, cos_ref[...], slo_ref[...], shi_ref[...])
    for h in range(N_HEADS):
        kn = kv[:, h * HEAD_PAD:h * HEAD_PAD + NOPE]
        r = lax.rsqrt((jnp.sum(kn * kn, axis=-1, keepdims=True) + ssq_r) / QK + RMS_EPS)
        k_ref[:, h * HEAD_PAD:h * HEAD_PAD + NOPE] = (kn * r * kgn_ref[...]).astype(BF16)
        k_ref[:, h * HEAD_PAD + NOPE:(h + 1) * HEAD_PAD] = (krg * r).astype(BF16)
        v_ref[:, h * V_DIM:(h + 1) * V_DIM] = kv[:, h * HEAD_PAD + NOPE:(h + 1) * HEAD_PAD].astype(BF16)


def _kv_expand(ckv_n, krp, w_ukv, kgn, kgr, rope_tabs, tr):
    r = ckv_n.shape[0]
    rope = rope_tabs is not None
    in_specs = [
        pl.BlockSpec((tr, KV_RANK), lambda i: (i, 0)),
        pl.BlockSpec((tr, LANES), lambda i: (i, 0)),
        pl.BlockSpec((KV_RANK, N_HEADS * HEAD_PAD), lambda i: (0, 0)),
        pl.BlockSpec((1, LANES), lambda i: (0, 0)),
        pl.BlockSpec((1, LANES), lambda i: (0, 0)),
    ]
    args = [ckv_n, krp, w_ukv, kgn, kgr]
    if rope:
        seq_tiles = rope_tabs[0].shape[0] // tr
        in_specs += [pl.BlockSpec((tr, LANES), lambda i: (i % seq_tiles, 0))] * 3
        args += list(rope_tabs)
    return pl.pallas_call(
        functools.partial(_kv_expand_kernel, rope),
        out_shape=(jax.ShapeDtypeStruct((r, N_HEADS * HEAD_PAD), BF16),
                   jax.ShapeDtypeStruct((r, N_HEADS * V_DIM), BF16)),
        grid=(r // tr,),
        in_specs=in_specs,
        out_specs=(pl.BlockSpec((tr, N_HEADS * HEAD_PAD), lambda i: (i, 0)),
                   pl.BlockSpec((tr, N_HEADS * V_DIM), lambda i: (i, 0))),
        compiler_params=_cparams(("arbitrary",)),
        name="kv_expand_rope" if rope else "kv_expand",
    )(*args)


def _qk(q, k):
    return lax.dot_general(q, k, (((1,), (1,)), ((), ())), preferred_element_type=F32) * ATTN_SCALE


def _attn_ctx_kernel(q_ref, k_ref, v_ref, o_ref):
    for h in range(N_HEADS):
        s = _qk(q_ref[:, h * HEAD_PAD:(h + 1) * HEAD_PAD], k_ref[:, h * HEAD_PAD:(h + 1) * HEAD_PAD])
        p = jnp.exp(s - jnp.max(s, axis=-1, keepdims=True))
        l = jnp.sum(p, axis=-1, keepdims=True)
        o = _dot(p.astype(BF16), v_ref[:, h * V_DIM:(h + 1) * V_DIM])
        o_ref[:, h * V_DIM:(h + 1) * V_DIM] = (o / l).astype(BF16)


def _attn_ctx(q, k, v, seq):
    t = q.shape[0]
    return pl.pallas_call(
        _attn_ctx_kernel,
        out_shape=jax.ShapeDtypeStruct((t, N_HEADS * V_DIM), BF16),
        grid=(t // seq,),
        in_specs=[pl.BlockSpec((seq, Q_COLS), lambda b: (b, 0)),
                  pl.BlockSpec((seq, Q_COLS), lambda b: (b, 0)),
                  pl.BlockSpec((seq, N_HEADS * V_DIM), lambda b: (b, 0))],
        out_specs=pl.BlockSpec((seq, N_HEADS * V_DIM), lambda b: (b, 0)),
        compiler_params=_cparams(("arbitrary",)),
        name="attn_ctx",
    )(q, k, v)


def _attn_lat_kernel(q_ref, k_ref, v_ref, kc_ref, vc_ref, o_ref):
    q = q_ref[...]
    s1 = _qk(q, k_ref[...])
    s2 = _qk(q, kc_ref[...])
    m = jnp.maximum(jnp.max(s1, axis=-1, keepdims=True), jnp.max(s2, axis=-1, keepdims=True))
    p1 = jnp.exp(s1 - m)
    p2 = jnp.exp(s2 - m)
    l = jnp.sum(p1, axis=-1, keepdims=True) + jnp.sum(p2, axis=-1, keepdims=True)
    o = _dot(p1.astype(BF16), v_ref[...]) + _dot(p2.astype(BF16), vc_ref[...])
    o_ref[...] = (o / l).astype(BF16)


def _attn_lat(q, k, v, kc, vc, seq, past, tq):
    t = q.shape[0]
    nq = seq // tq
    return pl.pallas_call(
        _attn_lat_kernel,
        out_shape=jax.ShapeDtypeStruct((t, N_HEADS * V_DIM), BF16),
        grid=(t // seq, N_HEADS, nq),
        in_specs=[pl.BlockSpec((tq, HEAD_PAD), lambda b, h, i: (b * nq + i, h)),
                  pl.BlockSpec((seq, HEAD_PAD), lambda b, h, i: (b, h)),
                  pl.BlockSpec((seq, V_DIM), lambda b, h, i: (b, h)),
                  pl.BlockSpec((past, HEAD_PAD), lambda b, h, i: (b, h)),
                  pl.BlockSpec((past, V_DIM), lambda b, h, i: (b, h))],
        out_specs=pl.BlockSpec((tq, V_DIM), lambda b, h, i: (b * nq + i, h)),
        compiler_params=_cparams(("arbitrary", "arbitrary", "arbitrary")),
        name="attn_lat",
    )(q, k, v, kc, vc)


POOL_HALO = 8


def _pool_kernel(seq, u_ref, w_ref, sc_ref, o_ref, pad_scr):
    zeros = jnp.zeros((POOL_HALO, POOL_WIDTH), F32)
    pad_scr[0:POOL_HALO, :] = zeros
    pad_scr[POOL_HALO + seq:2 * POOL_HALO + seq, :] = zeros
    pad_scr[POOL_HALO:POOL_HALO + seq, :] = u_ref[...]
    t = lax.broadcasted_iota(I32, (seq, 1), 0)
    for g, w in enumerate(POOL_WINDOWS):
        cols = slice(g * POOL_GROUP, (g + 1) * POOL_GROUP)
        tot = None
        for d in range(-(w // 2), w - w // 2):
            piece = pad_scr[POOL_HALO + d:POOL_HALO + d + seq, cols]
            tot = piece if tot is None else tot + piece
        cnt = (jnp.minimum(t + (w - w // 2), seq) - jnp.maximum(t - w // 2, 0)).astype(F32)
        mixed = tot / cnt - u_ref[:, cols]
        o_ref[:, cols] = (_dot(mixed.astype(BF16), w_ref[g]) * sc_ref[:, cols]).astype(BF16)


def _pool(u, w_pool, pool_scale, seq):
    t = u.shape[0]
    n_groups = len(POOL_WINDOWS)
    return pl.pallas_call(
        functools.partial(_pool_kernel, seq),
        out_shape=jax.ShapeDtypeStruct((t, POOL_WIDTH), BF16),
        grid=(t // seq,),
        in_specs=[pl.BlockSpec((seq, POOL_WIDTH), lambda b: (b, 0)),
                  pl.BlockSpec((n_groups, POOL_GROUP, POOL_GROUP), lambda b: (0, 0, 0)),
                  pl.BlockSpec((1, POOL_WIDTH), lambda b: (0, 0))],
        out_specs=pl.BlockSpec((seq, POOL_WIDTH), lambda b: (b, 0)),
        scratch_shapes=[pltpu.VMEM((seq + 2 * POOL_HALO, POOL_WIDTH), F32)],
        compiler_params=_cparams(("arbitrary",)),
        name="pool",
    )(u, w_pool, pool_scale)


def _merge_kernel(attn_ref, pool_ref, gates_ref, x_ref, mods_ref, bbg_ref, woa_ref, wop_ref, wout_ref,
                  g2_ref, rw_ref, rb_ref, *rest):
    x1_ref, h2_ref, tidx_ref, tw_ref = rest[-4:]
    a = _dot(attn_ref[...], woa_ref[...])
    p = _dot(pool_ref[...], wop_ref[...])
    ga = _sigmoid(gates_ref[:, :D_MODEL] + bbg_ref[:, :D_MODEL])
    gp = _sigmoid(gates_ref[:, D_MODEL:] + bbg_ref[:, D_MODEL:])
    merged = (ga * a + gp * p).astype(BF16)
    x1 = x_ref[...] + mods_ref[2:3, :] * _dot(merged, wout_ref[...])
    x1_ref[...] = x1
    y = x1 * lax.rsqrt(jnp.mean(x1 * x1, axis=-1, keepdims=True) + RMS_EPS) * g2_ref[...]
    h2 = y * (1.0 + mods_ref[4:5, :]) + mods_ref[3:4, :]
    h2_ref[...] = h2
    logits = _dot(h2.astype(BF16), rw_ref[...]) + rb_ref[...]
    lane = lax.broadcasted_iota(I32, logits.shape, 1).astype(F32)
    vals, idxs = [], []
    for _ in range(TOP_K):
        m = jnp.max(logits, axis=-1, keepdims=True)
        ix = jnp.min(jnp.where(logits == m, lane, float(LANES)), axis=-1, keepdims=True)
        vals.append(m)
        idxs.append(ix)
        logits = jnp.where(lane == ix, -jnp.inf, logits)
    es = [jnp.exp(v - vals[0]) for v in vals]
    tot = es[0] + es[1] + es[2] + es[3]
    tidx = jnp.zeros(logits.shape, F32)
    tw = jnp.zeros(logits.shape, F32)
    for k in range(TOP_K):
        tidx = jnp.where(lane == k, idxs[k], tidx)
        tw = jnp.where(lane == k, es[k] / tot, tw)
    tidx_ref[...] = tidx.astype(I32)
    tw_ref[...] = tw


def _merge(attn_o, pool_o, gates, x, mods, mod_row, bbg, woa, wop, wout, g2, rw, rb, tm, h2_rows, h2_row0, h2_prev):
    t = x.shape[0]
    const = lambda shape: pl.BlockSpec(shape, lambda i: (0, 0), pipeline_mode=pl.Buffered(1))
    h2_blk0 = h2_row0 // tm
    extra_specs = [] if h2_prev is None else [pl.BlockSpec(memory_space=pl.ANY)]
    extra_args = [] if h2_prev is None else [h2_prev]
    return pl.pallas_call(
        _merge_kernel,
        out_shape=(jax.ShapeDtypeStruct((t, D_MODEL), F32),
                   jax.ShapeDtypeStruct((h2_rows, D_MODEL), F32),
                   jax.ShapeDtypeStruct((t, LANES), I32),
                   jax.ShapeDtypeStruct((t, LANES), F32)),
        grid=(t // tm,),
        input_output_aliases={} if h2_prev is None else {12: 1},
        in_specs=[pl.BlockSpec((tm, D_MODEL), lambda i: (i, 0)),
                  pl.BlockSpec((tm, POOL_WIDTH), lambda i: (i, 0)),
                  pl.BlockSpec((tm, 2 * D_MODEL), lambda i: (i, 0)),
                  pl.BlockSpec((tm, D_MODEL), lambda i: (i, 0)),
                  pl.BlockSpec((None, 6, D_MODEL), lambda i: (mod_row(i), 0, 0)),
                  const((1, 2 * D_MODEL)),
                  const((D_MODEL, D_MODEL)),
                  const((POOL_WIDTH, D_MODEL)),
                  const((D_MODEL, D_MODEL)),
                  const((1, D_MODEL)),
                  const((D_MODEL, LANES)),
                  const((1, LANES))] + extra_specs,
        out_specs=(pl.BlockSpec((tm, D_MODEL), lambda i: (i, 0)),
                   pl.BlockSpec((tm, D_MODEL), lambda i: (i + h2_blk0, 0)),
                   pl.BlockSpec((tm, LANES), lambda i: (i, 0)),
                   pl.BlockSpec((tm, LANES), lambda i: (i, 0))),
        compiler_params=_cparams(("arbitrary",)),
        name="merge",
    )(attn_o, pool_o, gates, x, mods, bbg, woa, wop, wout, g2, rw, rb, *extra_args)


DISPATCH_ROWS = 256
DMA_UNROLL = 8


def _dispatch_kernel(tok_ref, h_ref, xs_ref, sem):
    base = pl.program_id(0) * DISPATCH_ROWS

    def issue(r, carry):
        pltpu.make_async_copy(h_ref.at[pl.ds(tok_ref[base + r], 1), :], xs_ref.at[pl.ds(r, 1), :], sem).start()
        return carry

    lax.fori_loop(0, DISPATCH_ROWS, issue, 0, unroll=DMA_UNROLL)
    pltpu.make_async_copy(h_ref.at[pl.ds(0, DISPATCH_ROWS), :], xs_ref, sem).wait()


def _dispatch(buf_tok, h2, cap):
    return pl.pallas_call(
        _dispatch_kernel,
        out_shape=jax.ShapeDtypeStruct((cap, D_MODEL), F32),
        grid_spec=pltpu.PrefetchScalarGridSpec(
            num_scalar_prefetch=1,
            grid=(cap // DISPATCH_ROWS,),
            in_specs=[pl.BlockSpec(memory_space=pl.ANY)],
            out_specs=pl.BlockSpec((DISPATCH_ROWS, D_MODEL), lambda i, tok: (i, 0)),
            scratch_shapes=[pltpu.SemaphoreType.DMA]),
        compiler_params=_cparams(("arbitrary",)),
        name="moe_dispatch",
    )(buf_tok, h2)


def _moe_kernel(ie_ref, ir_ref, in_ref, xs_ref, wg_ref, bg_ref, wu_ref, bu_ref, wd_ref, bd_ref, ys_ref,
                xb_scr, acc_scr, stage_scr, wg_scr, wu_scr, wd_scr, sem):
    i = pl.program_id(0)
    f = pl.program_id(1)
    nch = in_ref[i]
    row0 = ir_ref[i]

    def rows_of(c):
        return pl.ds(pl.multiple_of(c * MOE_CHUNK, MOE_CHUNK), MOE_CHUNK)

    def hbm_rows(c):
        return pl.ds(pl.multiple_of(row0 + c * MOE_CHUNK, MOE_CHUNK), MOE_CHUNK)

    @pl.when(nch > 0)
    def _():
        @pl.when(f == 0)
        def _():
            def load(c, carry):
                cp = pltpu.make_async_copy(xs_ref.at[hbm_rows(c), :], stage_scr, sem)
                cp.start()
                cp.wait()
                xb_scr[rows_of(c), :] = stage_scr[...].astype(BF16)
                return carry

            lax.fori_loop(0, nch, load, 0)

        wg_scr[...] = wg_ref[...].astype(BF16)
        wu_scr[...] = wu_ref[...].astype(BF16)
        wd_scr[...] = wd_ref[...].astype(BF16)

        def contribution(c):
            x = xb_scr[rows_of(c), :]
            gt = jnp.minimum(_dot(x, wg_scr[...]) + bg_ref[...], SWIGLU_LIMIT)
            up = jnp.clip(_dot(x, wu_scr[...]) + bu_ref[...], -SWIGLU_LIMIT, SWIGLU_LIMIT)
            act = (up + 1.0) * (gt * _sigmoid(SWIGLU_ALPHA * gt))
            return _dot(act.astype(BF16), wd_scr[...])

        @pl.when(f == 0)
        def _():
            def first(c, carry):
                acc_scr[rows_of(c), :] = contribution(c) + bd_ref[...]
                return carry

            lax.fori_loop(0, nch, first, 0)

        @pl.when(f > 0)
        def _():
            def more(c, carry):
                acc_scr[rows_of(c), :] += contribution(c)
                return carry

            lax.fori_loop(0, nch, more, 0)

        @pl.when(f == N_FF_TILES - 1)
        def _():
            def store(c, carry):
                cp = pltpu.make_async_copy(acc_scr.at[rows_of(c), :], ys_ref.at[hbm_rows(c), :], sem)
                cp.start()
                cp.wait()
                return carry

            lax.fori_loop(0, nch, store, 0)


def _moe_experts(item_e, item_row0, item_nch, xs, w_g, b_g, w_u, b_u, w_d, b_d):
    cap = xs.shape[0]
    n_items = item_e.shape[0]

    def ff_idx(i, f, ie, ir, inch):
        return jnp.where(inch[i] > 0, f, N_FF_TILES - 1)

    return pl.pallas_call(
        _moe_kernel,
        out_shape=jax.ShapeDtypeStruct((cap, D_MODEL), F32),
        grid_spec=pltpu.PrefetchScalarGridSpec(
            num_scalar_prefetch=3,
            grid=(n_items, N_FF_TILES),
            in_specs=[
                pl.BlockSpec(memory_space=pl.ANY),
                pl.BlockSpec((None, D_MODEL, FF_TILE), lambda i, f, ie, ir, inch: (ie[i], 0, ff_idx(i, f, ie, ir, inch))),
                pl.BlockSpec((None, 1, FF_TILE), lambda i, f, ie, ir, inch: (ie[i], 0, ff_idx(i, f, ie, ir, inch))),
                pl.BlockSpec((None, D_MODEL, FF_TILE), lambda i, f, ie, ir, inch: (ie[i], 0, ff_idx(i, f, ie, ir, inch))),
                pl.BlockSpec((None, 1, FF_TILE), lambda i, f, ie, ir, inch: (ie[i], 0, ff_idx(i, f, ie, ir, inch))),
                pl.BlockSpec((None, FF_TILE, D_MODEL), lambda i, f, ie, ir, inch: (ie[i], ff_idx(i, f, ie, ir, inch), 0)),
                pl.BlockSpec((None, 1, D_MODEL), lambda i, f, ie, ir, inch: (ie[i], 0, 0)),
            ],
            out_specs=pl.BlockSpec(memory_space=pl.ANY),
            scratch_shapes=[
                pltpu.VMEM((MOE_ITEM_ROWS, D_MODEL), BF16),
                pltpu.VMEM((MOE_ITEM_ROWS, D_MODEL), F32),
                pltpu.VMEM((MOE_CHUNK, D_MODEL), F32),
                pltpu.VMEM((D_MODEL, FF_TILE), BF16),
                pltpu.VMEM((D_MODEL, FF_TILE), BF16),
                pltpu.VMEM((FF_TILE, D_MODEL), BF16),
                pltpu.SemaphoreType.DMA,
            ]),
        compiler_params=_cparams(("arbitrary", "arbitrary")),
        name="moe_experts",
    )(item_e, item_row0, item_nch, xs, w_g, b_g, w_u, b_u, w_d, b_d)


COMBINE_ROWS = 128


def _combine_kernel(dest_ref, ys_ref, x1_ref, tw_ref, mods_ref, o_ref, ybuf, sem):
    base = pl.program_id(0) * (COMBINE_ROWS * TOP_K)

    def row_copy(src_row, r):
        return pltpu.make_async_copy(ys_ref.at[pl.ds(src_row, 1), :],
                                     ybuf.at[r % TOP_K, pl.ds(r // TOP_K, 1), :], sem)

    def issue(r, carry):
        row_copy(dest_ref[base + r], r).start()
        return carry

    lax.fori_loop(0, COMBINE_ROWS * TOP_K, issue, 0, unroll=DMA_UNROLL)
    for k in range(TOP_K):
        pltpu.make_async_copy(ys_ref.at[pl.ds(0, COMBINE_ROWS), :], ybuf.at[k], sem).wait()

    y = tw_ref[:, 0:1] * ybuf[0]
    for k in range(1, TOP_K):
        y = y + tw_ref[:, k:k + 1] * ybuf[k]
    o_ref[...] = x1_ref[...] + mods_ref[5:6, :] * y


def _combine(dest, ys, x1, tw, mods, mod_row):
    t = x1.shape[0]
    return pl.pallas_call(
        _combine_kernel,
        out_shape=jax.ShapeDtypeStruct((t, D_MODEL), F32),
        grid_spec=pltpu.PrefetchScalarGridSpec(
            num_scalar_prefetch=1,
            grid=(t // COMBINE_ROWS,),
            in_specs=[pl.BlockSpec(memory_space=pl.ANY),
                      pl.BlockSpec((COMBINE_ROWS, D_MODEL), lambda i, d: (i, 0)),
                      pl.BlockSpec((COMBINE_ROWS, LANES), lambda i, d: (i, 0)),
                      pl.BlockSpec((None, 6, D_MODEL), lambda i, d: (mod_row(i * COMBINE_ROWS), 0, 0))],
            out_specs=pl.BlockSpec((COMBINE_ROWS, D_MODEL), lambda i, d: (i, 0)),
            scratch_shapes=[pltpu.VMEM((TOP_K, COMBINE_ROWS, D_MODEL), F32), pltpu.SemaphoreType.DMA]),
        compiler_params=_cparams(("arbitrary",)),
        name="moe_combine",
    )(dest, ys, x1, tw, mods)


def _routing_tables(top_idx, n_items, cap):
    n_tok = top_idx.shape[0]
    flat_e = top_idx.reshape(-1)
    onehot = (flat_e[:, None] == jnp.arange(N_EXPERTS, dtype=I32)[None, :]).astype(I32)
    csum = jnp.cumsum(onehot, axis=0)
    rank = jnp.take_along_axis(csum, flat_e[:, None], axis=1)[:, 0] - 1
    counts = csum[-1]
    nch = (counts + MOE_CHUNK - 1) // MOE_CHUNK
    pad_rows = nch * MOE_CHUNK
    pad_end = jnp.cumsum(pad_rows)
    pad_start = pad_end - pad_rows
    dest = (pad_start[flat_e] + rank).astype(I32)
    buf_tok = jnp.zeros((cap,), I32).at[dest].set(jnp.arange(n_tok * TOP_K, dtype=I32) // TOP_K)
    items_e = (nch + MOE_ITEM_CHUNKS - 1) // MOE_ITEM_CHUNKS
    item_end = jnp.cumsum(items_e)
    item_start = item_end - items_e
    total = item_end[-1]
    i = jnp.arange(n_items, dtype=I32)
    ii = jnp.minimum(i, total - 1)
    e_i = jnp.minimum(jnp.searchsorted(item_end, ii, side="right"), N_EXPERTS - 1).astype(I32)
    local = ii - item_start[e_i]
    row0 = (pad_start[e_i] + local * MOE_ITEM_ROWS).astype(I32)
    n_i = jnp.where(i < total, jnp.minimum(MOE_ITEM_CHUNKS, nch[e_i] - local * MOE_ITEM_CHUNKS), 0).astype(I32)
    return dest, buf_tok, e_i, row0, n_i


def _rope_tables(n_lat):
    nf = ROPE // 4
    inv = ROPE_BASE ** (-jnp.arange(nf, dtype=F32) / nf)
    t = jnp.arange(n_lat)
    row = (t // GRID_W).astype(F32)
    col = (t % GRID_W).astype(F32)
    ang_r = row[:, None] * inv[None, :]
    ang_c = col[:, None] * inv[None, :]
    z = jnp.zeros((n_lat, nf), F32)
    tail = jnp.zeros((n_lat, LANES - ROPE), F32)
    cos = jnp.concatenate([jnp.cos(ang_r), jnp.cos(ang_r), jnp.cos(ang_c), jnp.cos(ang_c), tail], axis=1)
    sin_lo = jnp.concatenate([-jnp.sin(ang_r), z, -jnp.sin(ang_c), z, tail], axis=1)
    sin_hi = jnp.concatenate([z, jnp.sin(ang_r), z, jnp.sin(ang_c), tail], axis=1)
    return cos, sin_lo, sin_hi


def kernel(x_prompt, x_sample, cache_ckv, cache_krope, c, c_ctx, norm1_g, norm2_g, w_ada, b_ada, w_in,
           b_branch_gate, kv_norm_g, w_ukv, q_norm_g, k_norm_g, w_o_attn, w_pool, pool_scale, w_o_pool, w_out,
           router_w, router_b, w_exp_gate, b_exp_gate, w_exp_up, b_exp_up, w_exp_down, b_exp_down):
    assert w_in.shape[0] == 1, "single-layer trunk"
    batch, seq, _ = x_prompt.shape
    dec_batch, n_lat, _ = x_sample.shape
    past = cache_ckv.shape[2]
    n_ctx = batch * seq
    n_dec = dec_batch * n_lat

    cond = jnp.concatenate([c_ctx[None, :], c, jnp.zeros((8 - 1 - dec_batch, D_MODEL), F32)], axis=0)
    mods = _adaln(cond, w_ada[0], b_ada).reshape(8, 6, D_MODEL)

    w = w_in[0]
    i0 = N_HEADS * QK
    i1 = i0 + KV_RANK
    i2 = i1 + ROPE
    i3 = i2 + POOL_WIDTH
    wq = jnp.pad(w[:, :i0].reshape(D_MODEL, N_HEADS, QK), ((0, 0), (0, 0), (0, HEAD_PAD - QK)))
    w_cat = jnp.concatenate([wq.reshape(D_MODEL, Q_COLS), w[:, i2:i3], w[:, i3:], w[:, i0:i1], w[:, i1:i2],
                             jnp.zeros((D_MODEL, IN_TILE - KV_RANK - ROPE), F32)], axis=1).astype(BF16)
    qg = jnp.pad(q_norm_g, ((0, 0), (0, HEAD_PAD - QK)))
    kgn = k_norm_g[:, :NOPE]
    kgr = jnp.pad(k_norm_g[:, NOPE:], ((0, 0), (0, LANES - ROPE)))
    w_ukv_b = w_ukv[0].astype(BF16)
    rope_tabs = _rope_tables(n_lat)

    ctx_row = lambda i: 0
    tm_in = 512
    lat_row_in = lambda i: 1 + (i * tm_in) // n_lat
    q_c, pool_c, gates_c, ckv_c, kr_c, krp_c = _in_proj(
        x_prompt.reshape(n_ctx, D_MODEL), mods, ctx_row, norm1_g, w_cat, qg, kv_norm_g, None, tm_in)
    q_l, pool_l, gates_l, ckv_l, _, krp_l = _in_proj(
        x_sample.reshape(n_dec, D_MODEL), mods, lat_row_in, norm1_g, w_cat, qg, kv_norm_g, rope_tabs, tm_in)

    tr = 256
    k_c, v_c = _kv_expand(ckv_c, krp_c, w_ukv_b, kgn, kgr, None, tr)
    k_l, v_l = _kv_expand(ckv_l, krp_l, w_ukv_b, kgn, kgr, rope_tabs, tr)
    cache_krp = jnp.pad(cache_krope.reshape(dec_batch * past, ROPE), ((0, 0), (0, LANES - ROPE)))
    k_p, v_p = _kv_expand(cache_ckv.reshape(dec_batch * past, KV_RANK), cache_krp, w_ukv_b, kgn, kgr, None, tr)

    attn_c = _attn_ctx(q_c, k_c, v_c, seq)
    attn_l = _attn_lat(q_l, k_l, v_l, k_p, v_p, n_lat, past, 512)

    w_pool_b = w_pool[0].astype(BF16)
    poolo_c = _pool(pool_c, w_pool_b, pool_scale, seq)
    poolo_l = _pool(pool_l, w_pool_b, pool_scale, n_lat)

    woa = w_o_attn[0].astype(BF16)
    wop = w_o_pool[0].astype(BF16)
    wout = w_out[0].astype(BF16)
    rw = jnp.pad(router_w[0], ((0, 0), (0, LANES - N_EXPERTS))).astype(BF16)
    rb = jnp.pad(router_b, ((0, 0), (0, LANES - N_EXPERTS)), constant_values=NEG_BIG)
    tm_mg = 256
    lat_row_mg = lambda i: 1 + (i * tm_mg) // n_lat
    x1_c, h2, tidx_c, tw_c = _merge(attn_c, poolo_c, gates_c, x_prompt.reshape(n_ctx, D_MODEL), mods, ctx_row,
                                    b_branch_gate, woa, wop, wout, norm2_g, rw, rb, tm_mg, n_ctx + n_dec, 0, None)
    x1_l, h2, tidx_l, tw_l = _merge(attn_l, poolo_l, gates_l, x_sample.reshape(n_dec, D_MODEL), mods, lat_row_mg,
                                    b_branch_gate, woa, wop, wout, norm2_g, rw, rb, tm_mg, n_ctx + n_dec, n_ctx, h2)

    n_assign = (n_ctx + n_dec) * TOP_K
    max_chunks = (n_assign + N_EXPERTS * (MOE_CHUNK - 1)) // MOE_CHUNK
    cap = -(-max_chunks * MOE_CHUNK // DISPATCH_ROWS) * DISPATCH_ROWS
    n_items = (max_chunks + N_EXPERTS * (MOE_ITEM_CHUNKS - 1)) // MOE_ITEM_CHUNKS
    top_idx = jnp.concatenate([tidx_c[:, :TOP_K], tidx_l[:, :TOP_K]], axis=0)
    dest, buf_tok, item_e, item_row0, item_nch = _routing_tables(top_idx, n_items, cap)

    xs = _dispatch(buf_tok, h2, cap)
    ys = _moe_experts(item_e, item_row0, item_nch, xs,
                      w_exp_gate[0], b_exp_gate[0][:, None, :], w_exp_up[0], b_exp_up[0][:, None, :],
                      w_exp_down[0], b_exp_down[0][:, None, :])

    y_c = _combine(dest[:n_ctx * TOP_K], ys, x1_c, tw_c, mods, lambda r: 0)
    y_l = _combine(dest[n_ctx * TOP_K:], ys, x1_l, tw_l, mods, lambda r: 1 + r // n_lat)

    return (y_c.reshape(batch, seq, D_MODEL),
            y_l.reshape(dec_batch, n_lat, D_MODEL),
            ckv_c.reshape(batch, 1, seq, KV_RANK),
            kr_c.reshape(batch, 1, seq, ROPE))
```

```python
import functools

import jax
import jax.numpy as jnp
from jax import lax
from jax.experimental import pallas as pl
from jax.experimental.pallas import tpu as pltpu

F32 = jnp.float32
BF16 = jnp.bfloat16
I32 = jnp.int32

D_MODEL = 2048
N_HEADS = 16
NOPE = 128
ROPE = 64
QK = NOPE + ROPE
V_DIM = 128
KV_RANK = 512
POOL_WINDOWS = (2, 4, 8, 16)
POOL_WIDTH = 1024
POOL_GROUP = POOL_WIDTH // len(POOL_WINDOWS)
N_EXPERTS = 32
TOP_K = 4
D_FF = 2048
SWIGLU_LIMIT = 7.0
SWIGLU_ALPHA = 1.702
ROPE_BASE = 10000.0
RMS_EPS = 1e-6
GRID_W = 64
ATTN_SCALE = QK ** -0.5

LANES = 128
HEAD_PAD = 2 * LANES
Q_COLS = N_HEADS * HEAD_PAD
IN_TILE = 1024
N_Q_TILES = Q_COLS // IN_TILE
POOL_TILE = N_Q_TILES
GATE_TILE0 = POOL_TILE + 1
N_GATE_TILES = 2 * D_MODEL // IN_TILE
KV_TILE = GATE_TILE0 + N_GATE_TILES
N_IN_TILES = KV_TILE + 1
HEADS_PER_TILE = IN_TILE // HEAD_PAD

MOE_CHUNK = 256
MOE_ITEM_CHUNKS = 8
MOE_ITEM_ROWS = MOE_CHUNK * MOE_ITEM_CHUNKS
FF_TILE = 256
N_FF_TILES = D_FF // FF_TILE
NEG_BIG = -1e30

VMEM_LIMIT = 56 * 1024 * 1024


def _cparams(sem, vmem=VMEM_LIMIT):
    return pltpu.CompilerParams(dimension_semantics=sem, vmem_limit_bytes=vmem)


def _dot(a, b):
    return jnp.dot(a, b, preferred_element_type=F32)


def _sigmoid(z):
    return 1.0 / (1.0 + jnp.exp(-z))


HALF = D_MODEL // 2
HI_MASK = -65536


def _pack_halves(v):
    lo = lax.bitcast_convert_type(v[:, :HALF].astype(BF16).astype(F32), I32)
    hi = lax.bitcast_convert_type(v[:, HALF:].astype(BF16).astype(F32), I32)
    return jnp.bitwise_or(jnp.bitwise_and(hi, HI_MASK), lax.shift_right_logical(lo, 16))


def _unpack_halves(w):
    lo = lax.bitcast_convert_type(lax.shift_left(w, 16), F32)
    hi = lax.bitcast_convert_type(jnp.bitwise_and(w, HI_MASK), F32)
    return lo, hi


def _rope_lanes(y, cos, sin_lo, sin_hi):
    return y * cos + pltpu.roll(y, LANES - 16, axis=1) * sin_lo + pltpu.roll(y, 16, axis=1) * sin_hi


def _adaln_kernel(c_ref, w_ref, b_ref, o_ref):
    c = c_ref[...]
    s = (c * _sigmoid(c)).astype(BF16)
    o_ref[...] = _dot(s, w_ref[...].astype(BF16)) + b_ref[...]


def _adaln(cond, w_ada, b_ada):
    rows = cond.shape[0]
    n = w_ada.shape[1]
    tn = 1024
    return pl.pallas_call(
        _adaln_kernel,
        out_shape=jax.ShapeDtypeStruct((rows, n), F32),
        grid=(n // tn,),
        in_specs=[pl.BlockSpec((rows, D_MODEL), lambda j: (0, 0)),
                  pl.BlockSpec((D_MODEL, tn), lambda j: (0, j)),
                  pl.BlockSpec((1, tn), lambda j: (0, j))],
        out_specs=pl.BlockSpec((rows, tn), lambda j: (0, j)),
        compiler_params=_cparams(("arbitrary",)),
        name="adaln",
    )(cond, w_ada, b_ada)


def _in_proj_kernel(rope, x_ref, mods_ref, g1_ref, w_ref, qg_ref, kvg_ref, *rest):
    if rope:
        cos_ref, slo_ref, shi_ref = rest[:3]
        rest = rest[3:]
    q_ref, pool_ref, gates_ref, ckv_ref, kr_ref, krp_ref, h_scr = rest
    j = pl.program_id(1)

    @pl.when(j == 0)
    def _():
        x = x_ref[...]
        y = x * lax.rsqrt(jnp.mean(x * x, axis=-1, keepdims=True) + RMS_EPS) * g1_ref[...]
        h = y * (1.0 + mods_ref[1:2, :]) + mods_ref[0:1, :]
        h_scr[...] = h.astype(BF16)

    acc = _dot(h_scr[...], w_ref[...])

    @pl.when(j < N_Q_TILES)
    def _():
        for hh in range(HEADS_PER_TILE):
            a = acc[:, hh * HEAD_PAD:(hh + 1) * HEAD_PAD]
            r = lax.rsqrt(jnp.sum(a * a, axis=-1, keepdims=True) / QK + RMS_EPS)
            y = a * r * qg_ref[...]
            if rope:
                yr = _rope_lanes(y[:, LANES:], cos_ref[...], slo_ref[...], shi_ref[...])
                q_ref[:, hh * HEAD_PAD:hh * HEAD_PAD + LANES] = y[:, :LANES].astype(BF16)
                q_ref[:, hh * HEAD_PAD + LANES:(hh + 1) * HEAD_PAD] = yr.astype(BF16)
            else:
                q_ref[:, hh * HEAD_PAD:(hh + 1) * HEAD_PAD] = y.astype(BF16)

    @pl.when(j == POOL_TILE)
    def _():
        pool_ref[...] = acc

    @pl.when(jnp.logical_and(j >= GATE_TILE0, j < KV_TILE))
    def _():
        gates_ref[...] = acc

    @pl.when(j == KV_TILE)
    def _():
        a = acc[:, :KV_RANK]
        r = lax.rsqrt(jnp.mean(a * a, axis=-1, keepdims=True) + RMS_EPS)
        ckv_ref[...] = a * r * kvg_ref[...]
        krp = acc[:, KV_RANK:KV_RANK + LANES]
        krp_ref[...] = krp
        kr_ref[...] = krp[:, :ROPE]


def _in_proj(x, mods, mod_row, g1, w_cat, qg, kvg, rope_tabs, tm):
    t = x.shape[0]
    rope = rope_tabs is not None
    in_specs = [
        pl.BlockSpec((tm, D_MODEL), lambda i, j: (i, 0)),
        pl.BlockSpec((None, 6, D_MODEL), lambda i, j: (mod_row(i), 0, 0)),
        pl.BlockSpec((1, D_MODEL), lambda i, j: (0, 0)),
        pl.BlockSpec((D_MODEL, IN_TILE), lambda i, j: (0, j)),
        pl.BlockSpec((1, HEAD_PAD), lambda i, j: (0, 0)),
        pl.BlockSpec((1, KV_RANK), lambda i, j: (0, 0)),
    ]
    args = [x, mods, g1, w_cat, qg, kvg]
    if rope:
        seq_tiles = rope_tabs[0].shape[0] // tm
        in_specs += [pl.BlockSpec((tm, LANES), lambda i, j: (i % seq_tiles, 0))] * 3
        args += list(rope_tabs)
    out_shape = (
        jax.ShapeDtypeStruct((t, Q_COLS), BF16),
        jax.ShapeDtypeStruct((t, POOL_WIDTH), F32),
        jax.ShapeDtypeStruct((t, 2 * D_MODEL), F32),
        jax.ShapeDtypeStruct((t, KV_RANK), F32),
        jax.ShapeDtypeStruct((t, ROPE), F32),
        jax.ShapeDtypeStruct((t, LANES), F32),
    )
    out_specs = (
        pl.BlockSpec((tm, IN_TILE), lambda i, j: (i, jnp.minimum(j, N_Q_TILES - 1))),
        pl.BlockSpec((tm, POOL_WIDTH), lambda i, j: (i, 0)),
        pl.BlockSpec((tm, IN_TILE), lambda i, j: (i, jnp.clip(j - GATE_TILE0, 0, N_GATE_TILES - 1))),
        pl.BlockSpec((tm, KV_RANK), lambda i, j: (i, 0)),
        pl.BlockSpec((tm, ROPE), lambda i, j: (i, 0)),
        pl.BlockSpec((tm, LANES), lambda i, j: (i, 0)),
    )
    return pl.pallas_call(
        functools.partial(_in_proj_kernel, rope),
        out_shape=out_shape,
        grid=(t // tm, N_IN_TILES),
        in_specs=in_specs,
        out_specs=out_specs,
        scratch_shapes=[pltpu.VMEM((tm, D_MODEL), BF16)],
        compiler_params=_cparams(("arbitrary", "arbitrary")),
        name="in_proj_rope" if rope else "in_proj",
    )(*args)


def _kv_expand_kernel(rope, ckv_ref, krp_ref, w_ref, kgn_ref, kgr_ref, *rest):
    if rope:
        cos_ref, slo_ref, shi_ref = rest[:3]
        rest = rest[3:]
    k_ref, v_ref = rest
    kv = _dot(ckv_ref[...].astype(BF16), w_ref[...])
    kr = krp_ref[...]
    ssq_r = jnp.sum(kr * kr, axis=-1, keepdims=True)
    krg = kr * kgr_ref[...]
    if rope:
        krg = _rope_lanes(krg, cos_ref[...], slo_ref[...], shi_ref[...])
    for h in range(N_HEADS):
        kn = kv[:, h * HEAD_PAD:h * HEAD_PAD + NOPE]
        r = lax.rsqrt((jnp.sum(kn * kn, axis=-1, keepdims=True) + ssq_r) / QK + RMS_EPS)
        k_ref[:, h * HEAD_PAD:h * HEAD_PAD + NOPE] = (kn * r * kgn_ref[...]).astype(BF16)
        k_ref[:, h * HEAD_PAD + NOPE:(h + 1) * HEAD_PAD] = (krg * r).astype(BF16)
        v_ref[:, h * V_DIM:(h + 1) * V_DIM] = kv[:, h * HEAD_PAD + NOPE:(h + 1) * HEAD_PAD].astype(BF16)


def _kv_expand(ckv_n, krp, w_ukv, kgn, kgr, rope_tabs, tr):
    r = ckv_n.shape[0]
    rope = rope_tabs is not None
    in_specs = [
        pl.BlockSpec((tr, KV_RANK), lambda i: (i, 0)),
        pl.BlockSpec((tr, LANES), lambda i: (i, 0)),
        pl.BlockSpec((KV_RANK, N_HEADS * HEAD_PAD), lambda i: (0, 0)),
        pl.BlockSpec((1, LANES), lambda i: (0, 0)),
        pl.BlockSpec((1, LANES), lambda i: (0, 0)),
    ]
    args = [ckv_n, krp, w_ukv, kgn, kgr]
    if rope:
        seq_tiles = rope_tabs[0].shape[0] // tr
        in_specs += [pl.BlockSpec((tr, LANES), lambda i: (i % seq_tiles, 0))] * 3
        args += list(rope_tabs)
    return pl.pallas_call(
        functools.partial(_kv_expand_kernel, rope),
        out_shape=(jax.ShapeDtypeStruct((r, N_HEADS * HEAD_PAD), BF16),
                   jax.ShapeDtypeStruct((r, N_HEADS * V_DIM), BF16)),
        grid=(r // tr,),
        in_specs=in_specs,
        out_specs=(pl.BlockSpec((tr, N_HEADS * HEAD_PAD), lambda i: (i, 0)),
                   pl.BlockSpec((tr, N_HEADS * V_DIM), lambda i: (i, 0))),
        compiler_params=_cparams(("arbitrary",)),
        name="kv_expand_rope" if rope else "kv_expand",
    )(*args)


def _qk(q, k):
    return lax.dot_general(q, k, (((1,), (1,)), ((), ())), preferred_element_type=F32) * ATTN_SCALE


def _attn_ctx_kernel(q_ref, k_ref, v_ref, o_ref):
    for h in range(N_HEADS):
        s = _qk(q_ref[:, h * HEAD_PAD:(h + 1) * HEAD_PAD], k_ref[:, h * HEAD_PAD:(h + 1) * HEAD_PAD])
        p = jnp.exp(s - jnp.max(s, axis=-1, keepdims=True))
        l = jnp.sum(p, axis=-1, keepdims=True)
        o = _dot(p.astype(BF16), v_ref[:, h * V_DIM:(h + 1) * V_DIM])
        o_ref[:, h * V_DIM:(h + 1) * V_DIM] = (o / l).astype(BF16)


def _attn_ctx(q, k, v, seq):
    t = q.shape[0]
    return pl.pallas_call(
        _attn_ctx_kernel,
        out_shape=jax.ShapeDtypeStruct((t, N_HEADS * V_DIM), BF16),
        grid=(t // seq,),
        in_specs=[pl.BlockSpec((seq, Q_COLS), lambda b: (b, 0)),
                  pl.BlockSpec((seq, Q_COLS), lambda b: (b, 0)),
                  pl.BlockSpec((seq, N_HEADS * V_DIM), lambda b: (b, 0))],
        out_specs=pl.BlockSpec((seq, N_HEADS * V_DIM), lambda b: (b, 0)),
        compiler_params=_cparams(("arbitrary",)),
        name="attn_ctx",
    )(q, k, v)


def _attn_lat_kernel(q_ref, k_ref, v_ref, kc_ref, vc_ref, o_ref):
    q = q_ref[...]
    s1 = _qk(q, k_ref[...])
    s2 = _qk(q, kc_ref[...])
    m = jnp.maximum(jnp.max(s1, axis=-1, keepdims=True), jnp.max(s2, axis=-1, keepdims=True))
    p1 = jnp.exp(s1 - m)
    p2 = jnp.exp(s2 - m)
    l = jnp.sum(p1, axis=-1, keepdims=True) + jnp.sum(p2, axis=-1, keepdims=True)
    o = _dot(p1.astype(BF16), v_ref[...]) + _dot(p2.astype(BF16), vc_ref[...])
    o_ref[...] = (o / l).astype(BF16)


def _attn_lat(q, k, v, kc, vc, seq, past, tq):
    t = q.shape[0]
    nq = seq // tq
    return pl.pallas_call(
        _attn_lat_kernel,
        out_shape=jax.ShapeDtypeStruct((t, N_HEADS * V_DIM), BF16),
        grid=(t // seq, N_HEADS, nq),
        in_specs=[pl.BlockSpec((tq, HEAD_PAD), lambda b, h, i: (b * nq + i, h)),
                  pl.BlockSpec((seq, HEAD_PAD), lambda b, h, i: (b, h)),
                  pl.BlockSpec((seq, V_DIM), lambda b, h, i: (b, h)),
                  pl.BlockSpec((past, HEAD_PAD), lambda b, h, i: (b, h)),
                  pl.BlockSpec((past, V_DIM), lambda b, h, i: (b, h))],
        out_specs=pl.BlockSpec((tq, V_DIM), lambda b, h, i: (b * nq + i, h)),
        compiler_params=_cparams(("arbitrary", "arbitrary", "arbitrary")),
        name="attn_lat",
    )(q, k, v, kc, vc)


POOL_HALO = 8


def _pool_kernel(seq, u_ref, w_ref, sc_ref, o_ref, pad_scr):
    zeros = jnp.zeros((POOL_HALO, POOL_WIDTH), F32)
    pad_scr[0:POOL_HALO, :] = zeros
    pad_scr[POOL_HALO + seq:2 * POOL_HALO + seq, :] = zeros
    pad_scr[POOL_HALO:POOL_HALO + seq, :] = u_ref[...]
    t = lax.broadcasted_iota(I32, (seq, 1), 0)
    for g, w in enumerate(POOL_WINDOWS):
        cols = slice(g * POOL_GROUP, (g + 1) * POOL_GROUP)
        tot = None
        for d in range(-(w // 2), w - w // 2):
            piece = pad_scr[POOL_HALO + d:POOL_HALO + d + seq, cols]
            tot = piece if tot is None else tot + piece
        cnt = (jnp.minimum(t + (w - w // 2), seq) - jnp.maximum(t - w // 2, 0)).astype(F32)
        mixed = tot / cnt - u_ref[:, cols]
        o_ref[:, cols] = (_dot(mixed.astype(BF16), w_ref[g]) * sc_ref[:, cols]).astype(BF16)


def _pool(u, w_pool, pool_scale, seq):
    t = u.shape[0]
    n_groups = len(POOL_WINDOWS)
    return pl.pallas_call(
        functools.partial(_pool_kernel, seq),
        out_shape=jax.ShapeDtypeStruct((t, POOL_WIDTH), BF16),
        grid=(t // seq,),
        in_specs=[pl.BlockSpec((seq, POOL_WIDTH), lambda b: (b, 0)),
                  pl.BlockSpec((n_groups, POOL_GROUP, POOL_GROUP), lambda b: (0, 0, 0)),
                  pl.BlockSpec((1, POOL_WIDTH), lambda b: (0, 0))],
        out_specs=pl.BlockSpec((seq, POOL_WIDTH), lambda b: (b, 0)),
        scratch_shapes=[pltpu.VMEM((seq + 2 * POOL_HALO, POOL_WIDTH), F32)],
        compiler_params=_cparams(("arbitrary",)),
        name="pool",
    )(u, w_pool, pool_scale)


def _merge_kernel(attn_ref, pool_ref, gates_ref, x_ref, mods_ref, bbg_ref, woa_ref, wop_ref, wout_ref,
                  g2_ref, rw_ref, rb_ref, *rest):
    x1_ref, h2_ref, tidx_ref, tw_ref = rest[-4:]
    a = _dot(attn_ref[...], woa_ref[...])
    p = _dot(pool_ref[...], wop_ref[...])
    ga = _sigmoid(gates_ref[:, :D_MODEL] + bbg_ref[:, :D_MODEL])
    gp = _sigmoid(gates_ref[:, D_MODEL:] + bbg_ref[:, D_MODEL:])
    merged = (ga * a + gp * p).astype(BF16)
    x1 = x_ref[...] + mods_ref[2:3, :] * _dot(merged, wout_ref[...])
    x1_ref[...] = x1
    y = x1 * lax.rsqrt(jnp.mean(x1 * x1, axis=-1, keepdims=True) + RMS_EPS) * g2_ref[...]
    h2 = y * (1.0 + mods_ref[4:5, :]) + mods_ref[3:4, :]
    h2_ref[...] = _pack_halves(h2)
    logits = _dot(h2.astype(BF16), rw_ref[...]) + rb_ref[...]
    lane = lax.broadcasted_iota(I32, logits.shape, 1).astype(F32)
    vals, idxs = [], []
    for _ in range(TOP_K):
        m = jnp.max(logits, axis=-1, keepdims=True)
        ix = jnp.min(jnp.where(logits == m, lane, float(LANES)), axis=-1, keepdims=True)
        vals.append(m)
        idxs.append(ix)
        logits = jnp.where(lane == ix, -jnp.inf, logits)
    es = [jnp.exp(v - vals[0]) for v in vals]
    tot = es[0] + es[1] + es[2] + es[3]
    tidx = jnp.zeros(logits.shape, F32)
    tw = jnp.zeros(logits.shape, F32)
    for k in range(TOP_K):
        tidx = jnp.where(lane == k, idxs[k], tidx)
        tw = jnp.where(lane == k, es[k] / tot, tw)
    tidx_ref[...] = tidx.astype(I32)
    tw_ref[...] = tw


def _merge(attn_o, pool_o, gates, x, mods, mod_row, bbg, woa, wop, wout, g2, rw, rb, tm, h2_rows, h2_row0, h2_prev):
    t = x.shape[0]
    const = lambda shape: pl.BlockSpec(shape, lambda i: (0, 0), pipeline_mode=pl.Buffered(1))
    h2_blk0 = h2_row0 // tm
    extra_specs = [] if h2_prev is None else [pl.BlockSpec(memory_space=pl.ANY)]
    extra_args = [] if h2_prev is None else [h2_prev]
    return pl.pallas_call(
        _merge_kernel,
        out_shape=(jax.ShapeDtypeStruct((t, D_MODEL), F32),
                   jax.ShapeDtypeStruct((h2_rows, HALF), I32),
                   jax.ShapeDtypeStruct((t, LANES), I32),
                   jax.ShapeDtypeStruct((t, LANES), F32)),
        grid=(t // tm,),
        input_output_aliases={} if h2_prev is None else {12: 1},
        in_specs=[pl.BlockSpec((tm, D_MODEL), lambda i: (i, 0)),
                  pl.BlockSpec((tm, POOL_WIDTH), lambda i: (i, 0)),
                  pl.BlockSpec((tm, 2 * D_MODEL), lambda i: (i, 0)),
                  pl.BlockSpec((tm, D_MODEL), lambda i: (i, 0)),
                  pl.BlockSpec((None, 6, D_MODEL), lambda i: (mod_row(i), 0, 0)),
                  const((1, 2 * D_MODEL)),
                  const((D_MODEL, D_MODEL)),
                  const((POOL_WIDTH, D_MODEL)),
                  const((D_MODEL, D_MODEL)),
                  const((1, D_MODEL)),
                  const((D_MODEL, LANES)),
                  const((1, LANES))] + extra_specs,
        out_specs=(pl.BlockSpec((tm, D_MODEL), lambda i: (i, 0)),
                   pl.BlockSpec((tm, HALF), lambda i: (i + h2_blk0, 0)),
                   pl.BlockSpec((tm, LANES), lambda i: (i, 0)),
                   pl.BlockSpec((tm, LANES), lambda i: (i, 0))),
        compiler_params=_cparams(("arbitrary",)),
        name="merge",
    )(attn_o, pool_o, gates, x, mods, bbg, woa, wop, wout, g2, rw, rb, *extra_args)


DISPATCH_ROWS = 256
DMA_UNROLL = 8


def _dispatch_kernel(dest_ref, h_ref, xs_in_ref, xs_ref, sem):
    del xs_in_ref
    base = pl.program_id(0) * (DISPATCH_ROWS * TOP_K)

    def issue(a, carry):
        pltpu.make_async_copy(h_ref.at[pl.ds(a // TOP_K, 1), :], xs_ref.at[pl.ds(dest_ref[base + a], 1), :],
                              sem).start()
        return carry

    lax.fori_loop(0, DISPATCH_ROWS * TOP_K, issue, 0, unroll=DMA_UNROLL)
    for _ in range(TOP_K):
        pltpu.make_async_copy(h_ref, xs_ref.at[pl.ds(0, DISPATCH_ROWS), :], sem).wait()


def _dispatch(dest, h2, cap):
    t = h2.shape[0]
    return pl.pallas_call(
        _dispatch_kernel,
        out_shape=jax.ShapeDtypeStruct((cap, HALF), I32),
        grid_spec=pltpu.PrefetchScalarGridSpec(
            num_scalar_prefetch=1,
            grid=(t // DISPATCH_ROWS,),
            in_specs=[pl.BlockSpec((DISPATCH_ROWS, HALF), lambda i, d: (i, 0)),
                      pl.BlockSpec(memory_space=pl.ANY)],
            out_specs=pl.BlockSpec(memory_space=pl.ANY),
            scratch_shapes=[pltpu.SemaphoreType.DMA]),
        input_output_aliases={2: 0},
        compiler_params=_cparams(("arbitrary",)),
        name="moe_dispatch",
    )(dest, h2, jnp.zeros((cap, HALF), I32))


MOE_SPAN = 4


def _moe_kernel(ie_ref, ir_ref, in_ref, xs_ref, wg_ref, bg_ref, wu_ref, bu_ref, wd_ref, bd_ref, ys_ref,
                xin_scr, acc_scr, wg_scr, wu_scr, wd_scr, sem_x, sem_y):
    i = pl.program_id(0)
    f = pl.program_id(1)
    n_items = pl.num_programs(0)
    nch = in_ref[i]
    row0 = ir_ref[i]
    slot = i % 2

    def rows_of(c, k=1):
        return pl.ds(pl.multiple_of(c * MOE_CHUNK, MOE_CHUNK), k * MOE_CHUNK)

    def hbm_rows(item_row0, c):
        return pl.ds(pl.multiple_of(item_row0 + c * MOE_CHUNK, MOE_CHUNK), MOE_CHUNK)

    def x_copy(item_row0, c, s):
        return pltpu.make_async_copy(xs_ref.at[hbm_rows(item_row0, c), :], xin_scr.at[s, rows_of(c), :],
                                     sem_x.at[s])

    def y_copy(item_row0, c, s):
        return pltpu.make_async_copy(xin_scr.at[s, rows_of(c), :], ys_ref.at[hbm_rows(item_row0, c), :], sem_y)

    def for_chunks(n, body):
        def step(c, carry):
            body(c)
            return carry

        lax.fori_loop(0, n, step, 0)

    def for_spans(n, body):
        def span(g, carry):
            body(g * MOE_SPAN, MOE_SPAN)
            return carry

        lax.fori_loop(0, n // MOE_SPAN, span, 0)
        k = MOE_SPAN // 2
        while k >= 1:
            start = n // (2 * k) * (2 * k)

            @pl.when(n % (2 * k) >= k)
            def _(start=start, k=k):
                body(start, k)

            k //= 2

    @pl.when(f == 0)
    def _():
        @pl.when(i > 0)
        def _():
            prev = jnp.maximum(i - 1, 0)
            for_chunks(in_ref[prev], lambda c: y_copy(ir_ref[prev], c, 1 - slot).wait())

        @pl.when(i == 0)
        def _():
            for_chunks(nch, lambda c: x_copy(row0, c, slot).start())

        @pl.when(i + 1 < n_items)
        def _():
            nxt = jnp.minimum(i + 1, n_items - 1)
            for_chunks(in_ref[nxt], lambda c: x_copy(ir_ref[nxt], c, 1 - slot).start())

        for_chunks(nch, lambda c: x_copy(row0, c, slot).wait())

    @pl.when(nch > 0)
    def _():
        wg_scr[...] = wg_ref[...].astype(BF16)
        wu_scr[...] = wu_ref[...].astype(BF16)
        wd_scr[...] = wd_ref[...].astype(BF16)

        def contribution(c, k):
            lo, hi = _unpack_halves(xin_scr[slot, rows_of(c, k), :])
            x = jnp.concatenate([lo.astype(BF16), hi.astype(BF16)], axis=1)
            gt = jnp.minimum(_dot(x, wg_scr[...]) + bg_ref[...], SWIGLU_LIMIT)
            up = jnp.clip(_dot(x, wu_scr[...]) + bu_ref[...], -SWIGLU_LIMIT, SWIGLU_LIMIT)
            act = (up + 1.0) * (gt * _sigmoid(SWIGLU_ALPHA * gt))
            return _dot(act.astype(BF16), wd_scr[...])

        def first(c, k):
            acc_scr[rows_of(c, k), :] = contribution(c, k) + bd_ref[...]

        def middle(c, k):
            acc_scr[rows_of(c, k), :] += contribution(c, k)

        def last(c, k):
            y = acc_scr[rows_of(c, k), :] + contribution(c, k)
            xin_scr[slot, rows_of(c, k), :] = _pack_halves(y)
            for u in range(k):
                y_copy(row0, c + u, slot).start()

        @pl.when(f == 0)
        def _():
            for_spans(nch, first)

        @pl.when(jnp.logical_and(f > 0, f < N_FF_TILES - 1))
        def _():
            for_spans(nch, middle)

        @pl.when(f == N_FF_TILES - 1)
        def _():
            for_spans(nch, last)

            @pl.when(i == n_items - 1)
            def _():
                for_chunks(nch, lambda c: y_copy(row0, c, slot).wait())


def _moe_experts(item_e, item_row0, item_nch, xs, w_g, b_g, w_u, b_u, w_d, b_d):
    cap = xs.shape[0]
    n_items = item_e.shape[0]

    def ff_idx(i, f, ie, ir, inch):
        return jnp.where(inch[i] > 0, f, N_FF_TILES - 1)

    return pl.pallas_call(
        _moe_kernel,
        out_shape=jax.ShapeDtypeStruct((cap, HALF), I32),
        grid_spec=pltpu.PrefetchScalarGridSpec(
            num_scalar_prefetch=3,
            grid=(n_items, N_FF_TILES),
            in_specs=[
                pl.BlockSpec(memory_space=pl.ANY),
                pl.BlockSpec((None, D_MODEL, FF_TILE), lambda i, f, ie, ir, inch: (ie[i], 0, ff_idx(i, f, ie, ir, inch))),
                pl.BlockSpec((None, 1, FF_TILE), lambda i, f, ie, ir, inch: (ie[i], 0, ff_idx(i, f, ie, ir, inch))),
                pl.BlockSpec((None, D_MODEL, FF_TILE), lambda i, f, ie, ir, inch: (ie[i], 0, ff_idx(i, f, ie, ir, inch))),
                pl.BlockSpec((None, 1, FF_TILE), lambda i, f, ie, ir, inch: (ie[i], 0, ff_idx(i, f, ie, ir, inch))),
                pl.BlockSpec((None, FF_TILE, D_MODEL), lambda i, f, ie, ir, inch: (ie[i], ff_idx(i, f, ie, ir, inch), 0)),
                pl.BlockSpec((None, 1, D_MODEL), lambda i, f, ie, ir, inch: (ie[i], 0, 0)),
            ],
            out_specs=pl.BlockSpec(memory_space=pl.ANY),
            scratch_shapes=[
                pltpu.VMEM((2, MOE_ITEM_ROWS, HALF), I32),
                pltpu.VMEM((MOE_ITEM_ROWS, D_MODEL), F32),
                pltpu.VMEM((D_MODEL, FF_TILE), BF16),
                pltpu.VMEM((D_MODEL, FF_TILE), BF16),
                pltpu.VMEM((FF_TILE, D_MODEL), BF16),
                pltpu.SemaphoreType.DMA((2,)),
                pltpu.SemaphoreType.DMA,
            ]),
        compiler_params=_cparams(("arbitrary", "arbitrary")),
        name="moe_experts",
    )(item_e, item_row0, item_nch, xs, w_g, b_g, w_u, b_u, w_d, b_d)


COMBINE_ROWS = 128


def _combine_kernel(dest_ref, ys_ref, x1_ref, tw_ref, mods_ref, o_ref, ybuf, sem):
    base = pl.program_id(0) * (COMBINE_ROWS * TOP_K)

    def row_copy(src_row, r):
        return pltpu.make_async_copy(ys_ref.at[pl.ds(src_row, 1), :],
                                     ybuf.at[r % TOP_K, pl.ds(r // TOP_K, 1), :], sem)

    def issue(r, carry):
        row_copy(dest_ref[base + r], r).start()
        return carry

    lax.fori_loop(0, COMBINE_ROWS * TOP_K, issue, 0, unroll=DMA_UNROLL)
    for k in range(TOP_K):
        pltpu.make_async_copy(ys_ref.at[pl.ds(0, COMBINE_ROWS), :], ybuf.at[k], sem).wait()

    y_lo = y_hi = None
    for k in range(TOP_K):
        lo, hi = _unpack_halves(ybuf[k])
        w = tw_ref[:, k:k + 1]
        y_lo = w * lo if y_lo is None else y_lo + w * lo
        y_hi = w * hi if y_hi is None else y_hi + w * hi
    o_ref[:, :HALF] = x1_ref[:, :HALF] + mods_ref[5:6, :HALF] * y_lo
    o_ref[:, HALF:] = x1_ref[:, HALF:] + mods_ref[5:6, HALF:] * y_hi


def _combine(dest, ys, x1, tw, mods, mod_row):
    t = x1.shape[0]
    return pl.pallas_call(
        _combine_kernel,
        out_shape=jax.ShapeDtypeStruct((t, D_MODEL), F32),
        grid_spec=pltpu.PrefetchScalarGridSpec(
            num_scalar_prefetch=1,
            grid=(t // COMBINE_ROWS,),
            in_specs=[pl.BlockSpec(memory_space=pl.ANY),
                      pl.BlockSpec((COMBINE_ROWS, D_MODEL), lambda i, d: (i, 0)),
                      pl.BlockSpec((COMBINE_ROWS, LANES), lambda i, d: (i, 0)),
                      pl.BlockSpec((None, 6, D_MODEL), lambda i, d: (mod_row(i * COMBINE_ROWS), 0, 0))],
            out_specs=pl.BlockSpec((COMBINE_ROWS, D_MODEL), lambda i, d: (i, 0)),
            scratch_shapes=[pltpu.VMEM((TOP_K, COMBINE_ROWS, HALF), I32), pltpu.SemaphoreType.DMA]),
        compiler_params=_cparams(("arbitrary",)),
        name="moe_combine",
    )(dest, ys, x1, tw, mods)


def _routing_tables(top_idx, n_items):
    flat_e = top_idx.reshape(-1)
    onehot = (flat_e[:, None] == jnp.arange(N_EXPERTS, dtype=I32)[None, :]).astype(I32)
    csum = jnp.cumsum(onehot, axis=0)
    rank = jnp.take_along_axis(csum, flat_e[:, None], axis=1)[:, 0] - 1
    counts = csum[-1]
    nch = (counts + MOE_CHUNK - 1) // MOE_CHUNK
    pad_rows = nch * MOE_CHUNK
    pad_end = jnp.cumsum(pad_rows)
    pad_start = pad_end - pad_rows
    dest = (pad_start[flat_e] + rank).astype(I32)
    items_e = (nch + MOE_ITEM_CHUNKS - 1) // MOE_ITEM_CHUNKS
    item_end = jnp.cumsum(items_e)
    item_start = item_end - items_e
    total = item_end[-1]
    i = jnp.arange(n_items, dtype=I32)
    ii = jnp.minimum(i, total - 1)
    e_i = jnp.minimum(jnp.searchsorted(item_end, ii, side="right"), N_EXPERTS - 1).astype(I32)
    local = ii - item_start[e_i]
    row0 = (pad_start[e_i] + local * MOE_ITEM_ROWS).astype(I32)
    n_i = jnp.where(i < total, jnp.minimum(MOE_ITEM_CHUNKS, nch[e_i] - local * MOE_ITEM_CHUNKS), 0).astype(I32)
    return dest, e_i, row0, n_i


def _rope_tables(n_lat):
    nf = ROPE // 4
    inv = ROPE_BASE ** (-jnp.arange(nf, dtype=F32) / nf)
    t = jnp.arange(n_lat)
    row = (t // GRID_W).astype(F32)
    col = (t % GRID_W).astype(F32)
    ang_r = row[:, None] * inv[None, :]
    ang_c = col[:, None] * inv[None, :]
    z = jnp.zeros((n_lat, nf), F32)
    tail = jnp.zeros((n_lat, LANES - ROPE), F32)
    cos = jnp.concatenate([jnp.cos(ang_r), jnp.cos(ang_r), jnp.cos(ang_c), jnp.cos(ang_c), tail], axis=1)
    sin_lo = jnp.concatenate([-jnp.sin(ang_r), z, -jnp.sin(ang_c), z, tail], axis=1)
    sin_hi = jnp.concatenate([z, jnp.sin(ang_r), z, jnp.sin(ang_c), tail], axis=1)
    return cos, sin_lo, sin_hi


def kernel(x_prompt, x_sample, cache_ckv, cache_krope, c, c_ctx, norm1_g, norm2_g, w_ada, b_ada, w_in,
           b_branch_gate, kv_norm_g, w_ukv, q_norm_g, k_norm_g, w_o_attn, w_pool, pool_scale, w_o_pool, w_out,
           router_w, router_b, w_exp_gate, b_exp_gate, w_exp_up, b_exp_up, w_exp_down, b_exp_down):
    assert w_in.shape[0] == 1, "single-layer trunk"
    batch, seq, _ = x_prompt.shape
    dec_batch, n_lat, _ = x_sample.shape
    past = cache_ckv.shape[2]
    n_ctx = batch * seq
    n_dec = dec_batch * n_lat

    cond = jnp.concatenate([c_ctx[None, :], c, jnp.zeros((8 - 1 - dec_batch, D_MODEL), F32)], axis=0)
    mods = _adaln(cond, w_ada[0], b_ada).reshape(8, 6, D_MODEL)

    w = w_in[0]
    i0 = N_HEADS * QK
    i1 = i0 + KV_RANK
    i2 = i1 + ROPE
    i3 = i2 + POOL_WIDTH
    wq = jnp.pad(w[:, :i0].reshape(D_MODEL, N_HEADS, QK), ((0, 0), (0, 0), (0, HEAD_PAD - QK)))
    w_cat = jnp.concatenate([wq.reshape(D_MODEL, Q_COLS), w[:, i2:i3], w[:, i3:], w[:, i0:i1], w[:, i1:i2],
                             jnp.zeros((D_MODEL, IN_TILE - KV_RANK - ROPE), F32)], axis=1).astype(BF16)
    qg = jnp.pad(q_norm_g, ((0, 0), (0, HEAD_PAD - QK)))
    kgn = k_norm_g[:, :NOPE]
    kgr = jnp.pad(k_norm_g[:, NOPE:], ((0, 0), (0, LANES - ROPE)))
    w_ukv_b = w_ukv[0].astype(BF16)
    rope_tabs = _rope_tables(n_lat)

    ctx_row = lambda i: 0
    tm_in = 512
    lat_row_in = lambda i: 1 + (i * tm_in) // n_lat
    q_c, pool_c, gates_c, ckv_c, kr_c, krp_c = _in_proj(
        x_prompt.reshape(n_ctx, D_MODEL), mods, ctx_row, norm1_g, w_cat, qg, kv_norm_g, None, tm_in)
    q_l, pool_l, gates_l, ckv_l, _, krp_l = _in_proj(
        x_sample.reshape(n_dec, D_MODEL), mods, lat_row_in, norm1_g, w_cat, qg, kv_norm_g, rope_tabs, tm_in)

    tr = 256
    k_c, v_c = _kv_expand(ckv_c, krp_c, w_ukv_b, kgn, kgr, None, tr)
    k_l, v_l = _kv_expand(ckv_l, krp_l, w_ukv_b, kgn, kgr, rope_tabs, tr)
    cache_krp = jnp.pad(cache_krope.reshape(dec_batch * past, ROPE), ((0, 0), (0, LANES - ROPE)))
    k_p, v_p = _kv_expand(cache_ckv.reshape(dec_batch * past, KV_RANK), cache_krp, w_ukv_b, kgn, kgr, None, tr)

    attn_c = _attn_ctx(q_c, k_c, v_c, seq)
    attn_l = _attn_lat(q_l, k_l, v_l, k_p, v_p, n_lat, past, 512)

    w_pool_b = w_pool[0].astype(BF16)
    poolo_c = _pool(pool_c, w_pool_b, pool_scale, seq)
    poolo_l = _pool(pool_l, w_pool_b, pool_scale, n_lat)

    woa = w_o_attn[0].astype(BF16)
    wop = w_o_pool[0].astype(BF16)
    wout = w_out[0].astype(BF16)
    rw = jnp.pad(router_w[0], ((0, 0), (0, LANES - N_EXPERTS))).astype(BF16)
    rb = jnp.pad(router_b, ((0, 0), (0, LANES - N_EXPERTS)), constant_values=NEG_BIG)
    tm_mg = 256
    lat_row_mg = lambda i: 1 + (i * tm_mg) // n_lat
    x1_c, h2, tidx_c, tw_c = _merge(attn_c, poolo_c, gates_c, x_prompt.reshape(n_ctx, D_MODEL), mods, ctx_row,
                                    b_branch_gate, woa, wop, wout, norm2_g, rw, rb, tm_mg, n_ctx + n_dec, 0, None)
    x1_l, h2, tidx_l, tw_l = _merge(attn_l, poolo_l, gates_l, x_sample.reshape(n_dec, D_MODEL), mods, lat_row_mg,
                                    b_branch_gate, woa, wop, wout, norm2_g, rw, rb, tm_mg, n_ctx + n_dec, n_ctx, h2)

    n_assign = (n_ctx + n_dec) * TOP_K
    max_chunks = (n_assign + N_EXPERTS * (MOE_CHUNK - 1)) // MOE_CHUNK
    cap = max_chunks * MOE_CHUNK
    n_items = (max_chunks + N_EXPERTS * (MOE_ITEM_CHUNKS - 1)) // MOE_ITEM_CHUNKS
    top_idx = jnp.concatenate([tidx_c[:, :TOP_K], tidx_l[:, :TOP_K]], axis=0)
    dest, item_e, item_row0, item_nch = _routing_tables(top_idx, n_items)

    xs = _dispatch(dest, h2, cap)
    ys = _moe_experts(item_e, item_row0, item_nch, xs,
                      w_exp_gate[0], b_exp_gate[0][:, None, :], w_exp_up[0], b_exp_up[0][:, None, :],
                      w_exp_down[0], b_exp_down[0][:, None, :])

    y_c = _combine(dest[:n_ctx * TOP_K], ys, x1_c, tw_c, mods, lambda r: 0)
    y_l = _combine(dest[n_ctx * TOP_K:], ys, x1_l, tw_l, mods, lambda r: 1 + r // n_lat)

    return (y_c.reshape(batch, seq, D_MODEL),
            y_l.reshape(dec_batch, n_lat, D_MODEL),
            ckv_c.reshape(batch, 1, seq, KV_RANK),
            kr_c.reshape(batch, 1, seq, ROPE))
```

```python
import functools

import jax
import jax.numpy as jnp
from jax import lax
from jax.experimental import pallas as pl
from jax.experimental.pallas import tpu as pltpu

F32 = jnp.float32
BF16 = jnp.bfloat16
I32 = jnp.int32

D_MODEL = 2048
N_HEADS = 16
NOPE = 128
ROPE = 64
QK = NOPE + ROPE
V_DIM = 128
KV_RANK = 512
POOL_WINDOWS = (2, 4, 8, 16)
POOL_WIDTH = 1024
POOL_GROUP = POOL_WIDTH // len(POOL_WINDOWS)
N_EXPERTS = 32
TOP_K = 4
D_FF = 2048
SWIGLU_LIMIT = 7.0
SWIGLU_ALPHA = 1.702
ROPE_BASE = 10000.0
RMS_EPS = 1e-6
GRID_W = 64
ATTN_SCALE = QK ** -0.5

LANES = 128
HEAD_PAD = 2 * LANES
Q_COLS = N_HEADS * HEAD_PAD
IN_TILE = 1024
N_Q_TILES = Q_COLS // IN_TILE
POOL_TILE = N_Q_TILES
GATE_TILE0 = POOL_TILE + 1
N_GATE_TILES = 2 * D_MODEL // IN_TILE
KV_TILE = GATE_TILE0 + N_GATE_TILES
N_IN_TILES = KV_TILE + 1
HEADS_PER_TILE = IN_TILE // HEAD_PAD

MOE_CHUNK = 256
MOE_ITEM_CHUNKS = 8
MOE_ITEM_ROWS = MOE_CHUNK * MOE_ITEM_CHUNKS
FF_TILE = 256
N_FF_TILES = D_FF // FF_TILE
NEG_BIG = -1e30

VMEM_LIMIT = 56 * 1024 * 1024


def _cparams(sem, vmem=VMEM_LIMIT):
    return pltpu.CompilerParams(dimension_semantics=sem, vmem_limit_bytes=vmem)


def _dot(a, b):
    return jnp.dot(a, b, preferred_element_type=F32)


def _sigmoid(z):
    return 1.0 / (1.0 + jnp.exp(-z))


HALF = D_MODEL // 2
HI_MASK = -65536


def _pack_halves(v):
    lo = lax.bitcast_convert_type(v[:, :HALF].astype(BF16).astype(F32), I32)
    hi = lax.bitcast_convert_type(v[:, HALF:].astype(BF16).astype(F32), I32)
    return jnp.bitwise_or(jnp.bitwise_and(hi, HI_MASK), lax.shift_right_logical(lo, 16))


def _unpack_halves(w):
    lo = lax.bitcast_convert_type(lax.shift_left(w, 16), F32)
    hi = lax.bitcast_convert_type(jnp.bitwise_and(w, HI_MASK), F32)
    return lo, hi


def _rope_lanes(y, cos, sin_lo, sin_hi):
    return y * cos + pltpu.roll(y, LANES - 16, axis=1) * sin_lo + pltpu.roll(y, 16, axis=1) * sin_hi


def _adaln_kernel(c_ref, w_ref, b_ref, o_ref):
    c = c_ref[...]
    s = (c * _sigmoid(c)).astype(BF16)
    o_ref[...] = _dot(s, w_ref[...].astype(BF16)) + b_ref[...]


def _adaln(cond, w_ada, b_ada):
    rows = cond.shape[0]
    n = w_ada.shape[1]
    tn = 1024
    return pl.pallas_call(
        _adaln_kernel,
        out_shape=jax.ShapeDtypeStruct((rows, n), F32),
        grid=(n // tn,),
        in_specs=[pl.BlockSpec((rows, D_MODEL), lambda j: (0, 0)),
                  pl.BlockSpec((D_MODEL, tn), lambda j: (0, j)),
                  pl.BlockSpec((1, tn), lambda j: (0, j))],
        out_specs=pl.BlockSpec((rows, tn), lambda j: (0, j)),
        compiler_params=_cparams(("arbitrary",)),
        name="adaln",
    )(cond, w_ada, b_ada)


def _in_proj_kernel(rope, x_ref, mods_ref, g1_ref, w_ref, qg_ref, kvg_ref, *rest):
    if rope:
        cos_ref, slo_ref, shi_ref = rest[:3]
        rest = rest[3:]
    q_ref, pool_ref, gates_ref, ckv_ref, kr_ref, krp_ref, h_scr = rest
    j = pl.program_id(1)

    @pl.when(j == 0)
    def _():
        x = x_ref[...]
        y = x * lax.rsqrt(jnp.mean(x * x, axis=-1, keepdims=True) + RMS_EPS) * g1_ref[...]
        h = y * (1.0 + mods_ref[1:2, :]) + mods_ref[0:1, :]
        h_scr[...] = h.astype(BF16)

    def proj(c0, c1):
        return _dot(h_scr[...], w_ref[:, c0:c1])

    @pl.when(j < N_Q_TILES)
    def _():
        for hh in range(HEADS_PER_TILE):
            a = proj(hh * HEAD_PAD, (hh + 1) * HEAD_PAD)
            r = lax.rsqrt(jnp.sum(a * a, axis=-1, keepdims=True) / QK + RMS_EPS)
            y = a * r * qg_ref[...]
            if rope:
                yr = _rope_lanes(y[:, LANES:], cos_ref[...], slo_ref[...], shi_ref[...])
                q_ref[:, hh * HEAD_PAD:hh * HEAD_PAD + LANES] = y[:, :LANES].astype(BF16)
                q_ref[:, hh * HEAD_PAD + LANES:(hh + 1) * HEAD_PAD] = yr.astype(BF16)
            else:
                q_ref[:, hh * HEAD_PAD:(hh + 1) * HEAD_PAD] = y.astype(BF16)

    @pl.when(j == POOL_TILE)
    def _():
        pool_ref[...] = proj(0, IN_TILE)

    @pl.when(jnp.logical_and(j >= GATE_TILE0, j < KV_TILE))
    def _():
        gates_ref[...] = proj(0, IN_TILE)

    @pl.when(j == KV_TILE)
    def _():
        a = proj(0, KV_RANK)
        r = lax.rsqrt(jnp.mean(a * a, axis=-1, keepdims=True) + RMS_EPS)
        ckv_ref[...] = a * r * kvg_ref[...]
        krp = proj(KV_RANK, KV_RANK + LANES)
        krp_ref[...] = krp
        kr_ref[...] = krp[:, :ROPE]


def _in_proj(x, mods, mod_row, g1, w_cat, qg, kvg, rope_tabs, tm):
    t = x.shape[0]
    rope = rope_tabs is not None
    in_specs = [
        pl.BlockSpec((tm, D_MODEL), lambda i, j: (i, 0)),
        pl.BlockSpec((None, 6, D_MODEL), lambda i, j: (mod_row(i), 0, 0)),
        pl.BlockSpec((1, D_MODEL), lambda i, j: (0, 0)),
        pl.BlockSpec((D_MODEL, IN_TILE), lambda i, j: (0, j)),
        pl.BlockSpec((1, HEAD_PAD), lambda i, j: (0, 0)),
        pl.BlockSpec((1, KV_RANK), lambda i, j: (0, 0)),
    ]
    args = [x, mods, g1, w_cat, qg, kvg]
    if rope:
        seq_tiles = rope_tabs[0].shape[0] // tm
        in_specs += [pl.BlockSpec((tm, LANES), lambda i, j: (i % seq_tiles, 0))] * 3
        args += list(rope_tabs)
    out_shape = (
        jax.ShapeDtypeStruct((t, Q_COLS), BF16),
        jax.ShapeDtypeStruct((t, POOL_WIDTH), F32),
        jax.ShapeDtypeStruct((t, 2 * D_MODEL), F32),
        jax.ShapeDtypeStruct((t, KV_RANK), F32),
        jax.ShapeDtypeStruct((t, ROPE), F32),
        jax.ShapeDtypeStruct((t, LANES), F32),
    )
    out_specs = (
        pl.BlockSpec((tm, IN_TILE), lambda i, j: (i, jnp.minimum(j, N_Q_TILES - 1))),
        pl.BlockSpec((tm, POOL_WIDTH), lambda i, j: (i, 0)),
        pl.BlockSpec((tm, IN_TILE), lambda i, j: (i, jnp.clip(j - GATE_TILE0, 0, N_GATE_TILES - 1))),
        pl.BlockSpec((tm, KV_RANK), lambda i, j: (i, 0)),
        pl.BlockSpec((tm, ROPE), lambda i, j: (i, 0)),
        pl.BlockSpec((tm, LANES), lambda i, j: (i, 0)),
    )
    return pl.pallas_call(
        functools.partial(_in_proj_kernel, rope),
        out_shape=out_shape,
        grid=(t // tm, N_IN_TILES),
        in_specs=in_specs,
        out_specs=out_specs,
        scratch_shapes=[pltpu.VMEM((tm, D_MODEL), BF16)],
        compiler_params=_cparams(("arbitrary", "arbitrary")),
        name="in_proj_rope" if rope else "in_proj",
    )(*args)


def _kv_expand_kernel(rope, ckv_ref, krp_ref, w_ref, kgn_ref, kgr_ref, *rest):
    if rope:
        cos_ref, slo_ref, shi_ref = rest[:3]
        rest = rest[3:]
    k_ref, v_ref = rest
    kv = _dot(ckv_ref[...].astype(BF16), w_ref[...])
    kr = krp_ref[...]
    ssq_r = jnp.sum(kr * kr, axis=-1, keepdims=True)
    krg = kr * kgr_ref[...]
    if rope:
        krg = _rope_lanes(krg, cos_ref[...], slo_ref[...], shi_ref[...])
    for h in range(N_HEADS):
        kn = kv[:, h * HEAD_PAD:h * HEAD_PAD + NOPE]
        r = lax.rsqrt((jnp.sum(kn * kn, axis=-1, keepdims=True) + ssq_r) / QK + RMS_EPS)
        k_ref[:, h * HEAD_PAD:h * HEAD_PAD + NOPE] = (kn * r * kgn_ref[...]).astype(BF16)
        k_ref[:, h * HEAD_PAD + NOPE:(h + 1) * HEAD_PAD] = (krg * r).astype(BF16)
        v_ref[:, h * V_DIM:(h + 1) * V_DIM] = kv[:, h * HEAD_PAD + NOPE:(h + 1) * HEAD_PAD].astype(BF16)


def _kv_expand(ckv_n, krp, w_ukv, kgn, kgr, rope_tabs, tr):
    r = ckv_n.shape[0]
    rope = rope_tabs is not None
    in_specs = [
        pl.BlockSpec((tr, KV_RANK), lambda i: (i, 0)),
        pl.BlockSpec((tr, LANES), lambda i: (i, 0)),
        pl.BlockSpec((KV_RANK, N_HEADS * HEAD_PAD), lambda i: (0, 0)),
        pl.BlockSpec((1, LANES), lambda i: (0, 0)),
        pl.BlockSpec((1, LANES), lambda i: (0, 0)),
    ]
    args = [ckv_n, krp, w_ukv, kgn, kgr]
    if rope:
        seq_tiles = rope_tabs[0].shape[0] // tr
        in_specs += [pl.BlockSpec((tr, LANES), lambda i: (i % seq_tiles, 0))] * 3
        args += list(rope_tabs)
    return pl.pallas_call(
        functools.partial(_kv_expand_kernel, rope),
        out_shape=(jax.ShapeDtypeStruct((r, N_HEADS * HEAD_PAD), BF16),
                   jax.ShapeDtypeStruct((r, N_HEADS * V_DIM), BF16)),
        grid=(r // tr,),
        in_specs=in_specs,
        out_specs=(pl.BlockSpec((tr, N_HEADS * HEAD_PAD), lambda i: (i, 0)),
                   pl.BlockSpec((tr, N_HEADS * V_DIM), lambda i: (i, 0))),
        compiler_params=_cparams(("arbitrary",)),
        name="kv_expand_rope" if rope else "kv_expand",
    )(*args)


def _qk(q, k):
    return lax.dot_general(q, k, (((1,), (1,)), ((), ())), preferred_element_type=F32) * ATTN_SCALE


def _attn_ctx_kernel(q_ref, k_ref, v_ref, o_ref):
    for h in range(N_HEADS):
        s = _qk(q_ref[:, h * HEAD_PAD:(h + 1) * HEAD_PAD], k_ref[:, h * HEAD_PAD:(h + 1) * HEAD_PAD])
        p = jnp.exp(s - jnp.max(s, axis=-1, keepdims=True))
        l = jnp.sum(p, axis=-1, keepdims=True)
        o = _dot(p.astype(BF16), v_ref[:, h * V_DIM:(h + 1) * V_DIM])
        o_ref[:, h * V_DIM:(h + 1) * V_DIM] = (o / l).astype(BF16)


def _attn_ctx(q, k, v, seq):
    t = q.shape[0]
    return pl.pallas_call(
        _attn_ctx_kernel,
        out_shape=jax.ShapeDtypeStruct((t, N_HEADS * V_DIM), BF16),
        grid=(t // seq,),
        in_specs=[pl.BlockSpec((seq, Q_COLS), lambda b: (b, 0)),
                  pl.BlockSpec((seq, Q_COLS), lambda b: (b, 0)),
                  pl.BlockSpec((seq, N_HEADS * V_DIM), lambda b: (b, 0))],
        out_specs=pl.BlockSpec((seq, N_HEADS * V_DIM), lambda b: (b, 0)),
        compiler_params=_cparams(("arbitrary",)),
        name="attn_ctx",
    )(q, k, v)


def _attn_lat_kernel(q_ref, k_ref, v_ref, kc_ref, vc_ref, o_ref):
    q = q_ref[...]
    s1 = _qk(q, k_ref[...])
    s2 = _qk(q, kc_ref[...])
    m = jnp.maximum(jnp.max(s1, axis=-1, keepdims=True), jnp.max(s2, axis=-1, keepdims=True))
    p1 = jnp.exp(s1 - m)
    p2 = jnp.exp(s2 - m)
    l = jnp.sum(p1, axis=-1, keepdims=True) + jnp.sum(p2, axis=-1, keepdims=True)
    o = _dot(p1.astype(BF16), v_ref[...]) + _dot(p2.astype(BF16), vc_ref[...])
    o_ref[...] = (o / l).astype(BF16)


def _attn_lat(q, k, v, kc, vc, seq, past, tq):
    t = q.shape[0]
    nq = seq // tq
    return pl.pallas_call(
        _attn_lat_kernel,
        out_shape=jax.ShapeDtypeStruct((t, N_HEADS * V_DIM), BF16),
        grid=(t // seq, N_HEADS, nq),
        in_specs=[pl.BlockSpec((tq, HEAD_PAD), lambda b, h, i: (b * nq + i, h)),
                  pl.BlockSpec((seq, HEAD_PAD), lambda b, h, i: (b, h)),
                  pl.BlockSpec((seq, V_DIM), lambda b, h, i: (b, h)),
                  pl.BlockSpec((past, HEAD_PAD), lambda b, h, i: (b, h)),
                  pl.BlockSpec((past, V_DIM), lambda b, h, i: (b, h))],
        out_specs=pl.BlockSpec((tq, V_DIM), lambda b, h, i: (b * nq + i, h)),
        compiler_params=_cparams(("arbitrary", "arbitrary", "arbitrary")),
        name="attn_lat",
    )(q, k, v, kc, vc)


POOL_HALO = 8


def _pool_kernel(seq, u_ref, w_ref, sc_ref, o_ref, pad_scr):
    zeros = jnp.zeros((POOL_HALO, POOL_WIDTH), F32)
    pad_scr[0:POOL_HALO, :] = zeros
    pad_scr[POOL_HALO + seq:2 * POOL_HALO + seq, :] = zeros
    pad_scr[POOL_HALO:POOL_HALO + seq, :] = u_ref[...]
    t = lax.broadcasted_iota(I32, (seq, 1), 0)
    for g, w in enumerate(POOL_WINDOWS):
        cols = slice(g * POOL_GROUP, (g + 1) * POOL_GROUP)
        tot = None
        for d in range(-(w // 2), w - w // 2):
            piece = pad_scr[POOL_HALO + d:POOL_HALO + d + seq, cols]
            tot = piece if tot is None else tot + piece
        cnt = (jnp.minimum(t + (w - w // 2), seq) - jnp.maximum(t - w // 2, 0)).astype(F32)
        mixed = tot / cnt - u_ref[:, cols]
        o_ref[:, cols] = (_dot(mixed.astype(BF16), w_ref[g]) * sc_ref[:, cols]).astype(BF16)


def _pool(u, w_pool, pool_scale, seq):
    t = u.shape[0]
    n_groups = len(POOL_WINDOWS)
    return pl.pallas_call(
        functools.partial(_pool_kernel, seq),
        out_shape=jax.ShapeDtypeStruct((t, POOL_WIDTH), BF16),
        grid=(t // seq,),
        in_specs=[pl.BlockSpec((seq, POOL_WIDTH), lambda b: (b, 0)),
                  pl.BlockSpec((n_groups, POOL_GROUP, POOL_GROUP), lambda b: (0, 0, 0)),
                  pl.BlockSpec((1, POOL_WIDTH), lambda b: (0, 0))],
        out_specs=pl.BlockSpec((seq, POOL_WIDTH), lambda b: (b, 0)),
        scratch_shapes=[pltpu.VMEM((seq + 2 * POOL_HALO, POOL_WIDTH), F32)],
        compiler_params=_cparams(("arbitrary",)),
        name="pool",
    )(u, w_pool, pool_scale)


def _merge_kernel(attn_ref, pool_ref, gates_ref, x_ref, mods_ref, bbg_ref, woa_ref, wop_ref, wout_ref,
                  g2_ref, rw_ref, rb_ref, *rest):
    x1_ref, h2_ref, tidx_ref, tw_ref = rest[-4:]
    a = _dot(attn_ref[...], woa_ref[...])
    p = _dot(pool_ref[...], wop_ref[...])
    ga = _sigmoid(gates_ref[:, :D_MODEL] + bbg_ref[:, :D_MODEL])
    gp = _sigmoid(gates_ref[:, D_MODEL:] + bbg_ref[:, D_MODEL:])
    merged = (ga * a + gp * p).astype(BF16)
    x1 = x_ref[...] + mods_ref[2:3, :] * _dot(merged, wout_ref[...])
    x1_ref[...] = x1
    y = x1 * lax.rsqrt(jnp.mean(x1 * x1, axis=-1, keepdims=True) + RMS_EPS) * g2_ref[...]
    h2 = y * (1.0 + mods_ref[4:5, :]) + mods_ref[3:4, :]
    h2_ref[...] = _pack_halves(h2)
    logits = _dot(h2.astype(BF16), rw_ref[...]) + rb_ref[...]
    lane = lax.broadcasted_iota(I32, logits.shape, 1).astype(F32)
    vals, idxs = [], []
    for _ in range(TOP_K):
        m = jnp.max(logits, axis=-1, keepdims=True)
        ix = jnp.min(jnp.where(logits == m, lane, float(LANES)), axis=-1, keepdims=True)
        vals.append(m)
        idxs.append(ix)
        logits = jnp.where(lane == ix, -jnp.inf, logits)
    es = [jnp.exp(v - vals[0]) for v in vals]
    tot = es[0] + es[1] + es[2] + es[3]
    tidx = jnp.zeros(logits.shape, F32)
    tw = jnp.zeros(logits.shape, F32)
    for k in range(TOP_K):
        tidx = jnp.where(lane == k, idxs[k], tidx)
        tw = jnp.where(lane == k, es[k] / tot, tw)
    tidx_ref[...] = tidx.astype(I32)
    tw_ref[...] = tw


def _merge(attn_o, pool_o, gates, x, mods, mod_row, bbg, woa, wop, wout, g2, rw, rb, tm, h2_rows, h2_row0, h2_prev):
    t = x.shape[0]
    const = lambda shape: pl.BlockSpec(shape, lambda i: (0, 0), pipeline_mode=pl.Buffered(1))
    h2_blk0 = h2_row0 // tm
    extra_specs = [] if h2_prev is None else [pl.BlockSpec(memory_space=pl.ANY)]
    extra_args = [] if h2_prev is None else [h2_prev]
    return pl.pallas_call(
        _merge_kernel,
        out_shape=(jax.ShapeDtypeStruct((t, D_MODEL), F32),
                   jax.ShapeDtypeStruct((h2_rows, HALF), I32),
                   jax.ShapeDtypeStruct((t, LANES), I32),
                   jax.ShapeDtypeStruct((t, LANES), F32)),
        grid=(t // tm,),
        input_output_aliases={} if h2_prev is None else {12: 1},
        in_specs=[pl.BlockSpec((tm, D_MODEL), lambda i: (i, 0)),
                  pl.BlockSpec((tm, POOL_WIDTH), lambda i: (i, 0)),
                  pl.BlockSpec((tm, 2 * D_MODEL), lambda i: (i, 0)),
                  pl.BlockSpec((tm, D_MODEL), lambda i: (i, 0)),
                  pl.BlockSpec((None, 6, D_MODEL), lambda i: (mod_row(i), 0, 0)),
                  const((1, 2 * D_MODEL)),
                  const((D_MODEL, D_MODEL)),
                  const((POOL_WIDTH, D_MODEL)),
                  const((D_MODEL, D_MODEL)),
                  const((1, D_MODEL)),
                  const((D_MODEL, LANES)),
                  const((1, LANES))] + extra_specs,
        out_specs=(pl.BlockSpec((tm, D_MODEL), lambda i: (i, 0)),
                   pl.BlockSpec((tm, HALF), lambda i: (i + h2_blk0, 0)),
                   pl.BlockSpec((tm, LANES), lambda i: (i, 0)),
                   pl.BlockSpec((tm, LANES), lambda i: (i, 0))),
        compiler_params=_cparams(("arbitrary",)),
        name="merge",
    )(attn_o, pool_o, gates, x, mods, bbg, woa, wop, wout, g2, rw, rb, *extra_args)


DISPATCH_ROWS = 256
DMA_UNROLL = 8


def _dispatch_kernel(dest_ref, h_ref, xs_in_ref, xs_ref, sem):
    del xs_in_ref
    base = pl.program_id(0) * (DISPATCH_ROWS * TOP_K)

    def issue(g, carry):
        for u in range(DMA_UNROLL):
            a = g * DMA_UNROLL + u
            pltpu.make_async_copy(h_ref.at[pl.ds(a // TOP_K, 1), :], xs_ref.at[pl.ds(dest_ref[base + a], 1), :],
                                  sem).start(priority=u % 2)
        return carry

    lax.fori_loop(0, DISPATCH_ROWS * TOP_K // DMA_UNROLL, issue, 0)
    for _ in range(TOP_K):
        pltpu.make_async_copy(h_ref, xs_ref.at[pl.ds(0, DISPATCH_ROWS), :], sem).wait()


def _dispatch(dest, h2, cap):
    t = h2.shape[0]
    return pl.pallas_call(
        _dispatch_kernel,
        out_shape=jax.ShapeDtypeStruct((cap, HALF), I32),
        grid_spec=pltpu.PrefetchScalarGridSpec(
            num_scalar_prefetch=1,
            grid=(t // DISPATCH_ROWS,),
            in_specs=[pl.BlockSpec((DISPATCH_ROWS, HALF), lambda i, d: (i, 0)),
                      pl.BlockSpec(memory_space=pl.ANY)],
            out_specs=pl.BlockSpec(memory_space=pl.ANY),
            scratch_shapes=[pltpu.SemaphoreType.DMA]),
        input_output_aliases={2: 0},
        compiler_params=_cparams(("arbitrary",)),
        name="moe_dispatch",
    )(dest, h2, jnp.zeros((cap, HALF), I32))


MOE_SPAN = 4


def _moe_kernel(ie_ref, ir_ref, in_ref, xs_ref, wg_ref, bg_ref, wu_ref, bu_ref, wd_ref, bd_ref, ys_ref,
                xin_scr, acc_scr, wg_scr, wu_scr, wd_scr, sem_x, sem_y):
    i = pl.program_id(0)
    f = pl.program_id(1)
    n_items = pl.num_programs(0)
    nch = in_ref[i]
    row0 = ir_ref[i]
    slot = i % 2

    def rows_of(c, k=1):
        return pl.ds(pl.multiple_of(c * MOE_CHUNK, MOE_CHUNK), k * MOE_CHUNK)

    def hbm_rows(item_row0, c):
        return pl.ds(pl.multiple_of(item_row0 + c * MOE_CHUNK, MOE_CHUNK), MOE_CHUNK)

    def x_copy(item_row0, c, s):
        return pltpu.make_async_copy(xs_ref.at[hbm_rows(item_row0, c), :], xin_scr.at[s, rows_of(c), :],
                                     sem_x.at[s])

    def y_copy(item_row0, c, s):
        return pltpu.make_async_copy(xin_scr.at[s, rows_of(c), :], ys_ref.at[hbm_rows(item_row0, c), :], sem_y)

    def for_chunks(n, body):
        def step(c, carry):
            body(c)
            return carry

        lax.fori_loop(0, n, step, 0)

    def for_spans(n, body):
        def span(g, carry):
            body(g * MOE_SPAN, MOE_SPAN)
            return carry

        lax.fori_loop(0, n // MOE_SPAN, span, 0)
        k = MOE_SPAN // 2
        while k >= 1:
            start = n // (2 * k) * (2 * k)

            @pl.when(n % (2 * k) >= k)
            def _(start=start, k=k):
                body(start, k)

            k //= 2

    @pl.when(f == 0)
    def _():
        @pl.when(i > 0)
        def _():
            prev = jnp.maximum(i - 1, 0)
            for_chunks(in_ref[prev], lambda c: y_copy(ir_ref[prev], c, 1 - slot).wait())

        @pl.when(i == 0)
        def _():
            for_chunks(nch, lambda c: x_copy(row0, c, slot).start())

        @pl.when(i + 1 < n_items)
        def _():
            nxt = jnp.minimum(i + 1, n_items - 1)
            for_chunks(in_ref[nxt], lambda c: x_copy(ir_ref[nxt], c, 1 - slot).start())

        for_chunks(nch, lambda c: x_copy(row0, c, slot).wait())

    @pl.when(nch > 0)
    def _():
        wg_scr[...] = wg_ref[...].astype(BF16)
        wu_scr[...] = wu_ref[...].astype(BF16)
        wd_scr[...] = wd_ref[...].astype(BF16)

        def contribution(c, k):
            lo, hi = _unpack_halves(xin_scr[slot, rows_of(c, k), :])
            x = jnp.concatenate([lo.astype(BF16), hi.astype(BF16)], axis=1)
            gt = jnp.minimum(_dot(x, wg_scr[...]) + bg_ref[...], SWIGLU_LIMIT)
            up = jnp.clip(_dot(x, wu_scr[...]) + bu_ref[...], -SWIGLU_LIMIT, SWIGLU_LIMIT)
            act = (up + 1.0) * (gt * _sigmoid(SWIGLU_ALPHA * gt))
            return _dot(act.astype(BF16), wd_scr[...])

        def first(c, k):
            acc_scr[rows_of(c, k), :] = contribution(c, k) + bd_ref[...]

        def middle(c, k):
            acc_scr[rows_of(c, k), :] += contribution(c, k)

        def last(c, k):
            y = acc_scr[rows_of(c, k), :] + contribution(c, k)
            xin_scr[slot, rows_of(c, k), :] = _pack_halves(y)
            for u in range(k):
                y_copy(row0, c + u, slot).start()

        @pl.when(f == 0)
        def _():
            for_spans(nch, first)

        @pl.when(jnp.logical_and(f > 0, f < N_FF_TILES - 1))
        def _():
            for_spans(nch, middle)

        @pl.when(f == N_FF_TILES - 1)
        def _():
            for_spans(nch, last)

            @pl.when(i == n_items - 1)
            def _():
                for_chunks(nch, lambda c: y_copy(row0, c, slot).wait())


def _moe_experts(item_e, item_row0, item_nch, xs, w_g, b_g, w_u, b_u, w_d, b_d):
    cap = xs.shape[0]
    n_items = item_e.shape[0]

    def ff_idx(i, f, ie, ir, inch):
        return jnp.where(inch[i] > 0, f, N_FF_TILES - 1)

    return pl.pallas_call(
        _moe_kernel,
        out_shape=jax.ShapeDtypeStruct((cap, HALF), I32),
        grid_spec=pltpu.PrefetchScalarGridSpec(
            num_scalar_prefetch=3,
            grid=(n_items, N_FF_TILES),
            in_specs=[
                pl.BlockSpec(memory_space=pl.ANY),
                pl.BlockSpec((None, D_MODEL, FF_TILE), lambda i, f, ie, ir, inch: (ie[i], 0, ff_idx(i, f, ie, ir, inch))),
                pl.BlockSpec((None, 1, FF_TILE), lambda i, f, ie, ir, inch: (ie[i], 0, ff_idx(i, f, ie, ir, inch))),
                pl.BlockSpec((None, D_MODEL, FF_TILE), lambda i, f, ie, ir, inch: (ie[i], 0, ff_idx(i, f, ie, ir, inch))),
                pl.BlockSpec((None, 1, FF_TILE), lambda i, f, ie, ir, inch: (ie[i], 0, ff_idx(i, f, ie, ir, inch))),
                pl.BlockSpec((None, FF_TILE, D_MODEL), lambda i, f, ie, ir, inch: (ie[i], ff_idx(i, f, ie, ir, inch), 0)),
                pl.BlockSpec((None, 1, D_MODEL), lambda i, f, ie, ir, inch: (ie[i], 0, 0)),
            ],
            out_specs=pl.BlockSpec(memory_space=pl.ANY),
            scratch_shapes=[
                pltpu.VMEM((2, MOE_ITEM_ROWS, HALF), I32),
                pltpu.VMEM((MOE_ITEM_ROWS, D_MODEL), F32),
                pltpu.VMEM((D_MODEL, FF_TILE), BF16),
                pltpu.VMEM((D_MODEL, FF_TILE), BF16),
                pltpu.VMEM((FF_TILE, D_MODEL), BF16),
                pltpu.SemaphoreType.DMA((2,)),
                pltpu.SemaphoreType.DMA,
            ]),
        compiler_params=_cparams(("arbitrary", "arbitrary")),
        name="moe_experts",
    )(item_e, item_row0, item_nch, xs, w_g, b_g, w_u, b_u, w_d, b_d)


COMBINE_ROWS = 128


def _combine_kernel(dest_ref, ys_ref, x1_ref, tw_ref, mods_ref, o_ref, ybuf, sem):
    base = pl.program_id(0) * (COMBINE_ROWS * TOP_K)

    def row_copy(src_row, r):
        return pltpu.make_async_copy(ys_ref.at[pl.ds(src_row, 1), :],
                                     ybuf.at[r % TOP_K, pl.ds(r // TOP_K, 1), :], sem)

    def issue(g, carry):
        for u in range(DMA_UNROLL):
            r = g * DMA_UNROLL + u
            row_copy(dest_ref[base + r], r).start(priority=u % 2)
        return carry

    lax.fori_loop(0, COMBINE_ROWS * TOP_K // DMA_UNROLL, issue, 0)
    for k in range(TOP_K):
        pltpu.make_async_copy(ys_ref.at[pl.ds(0, COMBINE_ROWS), :], ybuf.at[k], sem).wait()

    y_lo = y_hi = None
    for k in range(TOP_K):
        lo, hi = _unpack_halves(ybuf[k])
        w = tw_ref[:, k:k + 1]
        y_lo = w * lo if y_lo is None else y_lo + w * lo
        y_hi = w * hi if y_hi is None else y_hi + w * hi
    o_ref[:, :HALF] = x1_ref[:, :HALF] + mods_ref[5:6, :HALF] * y_lo
    o_ref[:, HALF:] = x1_ref[:, HALF:] + mods_ref[5:6, HALF:] * y_hi


def _combine(dest, ys, x1, tw, mods, mod_row):
    t = x1.shape[0]
    return pl.pallas_call(
        _combine_kernel,
        out_shape=jax.ShapeDtypeStruct((t, D_MODEL), F32),
        grid_spec=pltpu.PrefetchScalarGridSpec(
            num_scalar_prefetch=1,
            grid=(t // COMBINE_ROWS,),
            in_specs=[pl.BlockSpec(memory_space=pl.ANY),
                      pl.BlockSpec((COMBINE_ROWS, D_MODEL), lambda i, d: (i, 0)),
                      pl.BlockSpec((COMBINE_ROWS, LANES), lambda i, d: (i, 0)),
                      pl.BlockSpec((None, 6, D_MODEL), lambda i, d: (mod_row(i * COMBINE_ROWS), 0, 0))],
            out_specs=pl.BlockSpec((COMBINE_ROWS, D_MODEL), lambda i, d: (i, 0)),
            scratch_shapes=[pltpu.VMEM((TOP_K, COMBINE_ROWS, HALF), I32), pltpu.SemaphoreType.DMA]),
        compiler_params=_cparams(("arbitrary",)),
        name="moe_combine",
    )(dest, ys, x1, tw, mods)


def _routing_tables(top_idx, n_items):
    flat_e = top_idx.reshape(-1)
    onehot = (flat_e[:, None] == jnp.arange(N_EXPERTS, dtype=I32)[None, :]).astype(I32)
    csum = jnp.cumsum(onehot, axis=0)
    rank = jnp.take_along_axis(csum, flat_e[:, None], axis=1)[:, 0] - 1
    counts = csum[-1]
    nch = (counts + MOE_CHUNK - 1) // MOE_CHUNK
    pad_rows = nch * MOE_CHUNK
    pad_end = jnp.cumsum(pad_rows)
    pad_start = pad_end - pad_rows
    dest = (pad_start[flat_e] + rank).astype(I32)
    items_e = (nch + MOE_ITEM_CHUNKS - 1) // MOE_ITEM_CHUNKS
    item_end = jnp.cumsum(items_e)
    item_start = item_end - items_e
    total = item_end[-1]
    i = jnp.arange(n_items, dtype=I32)
    ii = jnp.minimum(i, total - 1)
    e_i = jnp.minimum(jnp.searchsorted(item_end, ii, side="right"), N_EXPERTS - 1).astype(I32)
    local = ii - item_start[e_i]
    row0 = (pad_start[e_i] + local * MOE_ITEM_ROWS).astype(I32)
    n_i = jnp.where(i < total, jnp.minimum(MOE_ITEM_CHUNKS, nch[e_i] - local * MOE_ITEM_CHUNKS), 0).astype(I32)
    return dest, e_i, row0, n_i


def _rope_tables(n_lat):
    nf = ROPE // 4
    inv = ROPE_BASE ** (-jnp.arange(nf, dtype=F32) / nf)
    t = jnp.arange(n_lat)
    row = (t // GRID_W).astype(F32)
    col = (t % GRID_W).astype(F32)
    ang_r = row[:, None] * inv[None, :]
    ang_c = col[:, None] * inv[None, :]
    z = jnp.zeros((n_lat, nf), F32)
    tail = jnp.zeros((n_lat, LANES - ROPE), F32)
    cos = jnp.concatenate([jnp.cos(ang_r), jnp.cos(ang_r), jnp.cos(ang_c), jnp.cos(ang_c), tail], axis=1)
    sin_lo = jnp.concatenate([-jnp.sin(ang_r), z, -jnp.sin(ang_c), z, tail], axis=1)
    sin_hi = jnp.concatenate([z, jnp.sin(ang_r), z, jnp.sin(ang_c), tail], axis=1)
    return cos, sin_lo, sin_hi


def kernel(x_prompt, x_sample, cache_ckv, cache_krope, c, c_ctx, norm1_g, norm2_g, w_ada, b_ada, w_in,
           b_branch_gate, kv_norm_g, w_ukv, q_norm_g, k_norm_g, w_o_attn, w_pool, pool_scale, w_o_pool, w_out,
           router_w, router_b, w_exp_gate, b_exp_gate, w_exp_up, b_exp_up, w_exp_down, b_exp_down):
    assert w_in.shape[0] == 1, "single-layer trunk"
    batch, seq, _ = x_prompt.shape
    dec_batch, n_lat, _ = x_sample.shape
    past = cache_ckv.shape[2]
    n_ctx = batch * seq
    n_dec = dec_batch * n_lat

    cond = jnp.concatenate([c_ctx[None, :], c, jnp.zeros((8 - 1 - dec_batch, D_MODEL), F32)], axis=0)
    mods = _adaln(cond, w_ada[0], b_ada).reshape(8, 6, D_MODEL)

    w = w_in[0]
    i0 = N_HEADS * QK
    i1 = i0 + KV_RANK
    i2 = i1 + ROPE
    i3 = i2 + POOL_WIDTH
    wq = jnp.pad(w[:, :i0].reshape(D_MODEL, N_HEADS, QK), ((0, 0), (0, 0), (0, HEAD_PAD - QK)))
    w_cat = jnp.concatenate([wq.reshape(D_MODEL, Q_COLS), w[:, i2:i3], w[:, i3:], w[:, i0:i1], w[:, i1:i2],
                             jnp.zeros((D_MODEL, IN_TILE - KV_RANK - ROPE), F32)], axis=1).astype(BF16)
    qg = jnp.pad(q_norm_g, ((0, 0), (0, HEAD_PAD - QK)))
    kgn = k_norm_g[:, :NOPE]
    kgr = jnp.pad(k_norm_g[:, NOPE:], ((0, 0), (0, LANES - ROPE)))
    w_ukv_b = w_ukv[0].astype(BF16)
    rope_tabs = _rope_tables(n_lat)

    ctx_row = lambda i: 0
    tm_in = 512
    lat_row_in = lambda i: 1 + (i * tm_in) // n_lat
    q_c, pool_c, gates_c, ckv_c, kr_c, krp_c = _in_proj(
        x_prompt.reshape(n_ctx, D_MODEL), mods, ctx_row, norm1_g, w_cat, qg, kv_norm_g, None, tm_in)
    q_l, pool_l, gates_l, ckv_l, _, krp_l = _in_proj(
        x_sample.reshape(n_dec, D_MODEL), mods, lat_row_in, norm1_g, w_cat, qg, kv_norm_g, rope_tabs, tm_in)

    tr = 256
    k_c, v_c = _kv_expand(ckv_c, krp_c, w_ukv_b, kgn, kgr, None, tr)
    k_l, v_l = _kv_expand(ckv_l, krp_l, w_ukv_b, kgn, kgr, rope_tabs, tr)
    cache_krp = jnp.pad(cache_krope.reshape(dec_batch * past, ROPE), ((0, 0), (0, LANES - ROPE)))
    k_p, v_p = _kv_expand(cache_ckv.reshape(dec_batch * past, KV_RANK), cache_krp, w_ukv_b, kgn, kgr, None, tr)

    attn_c = _attn_ctx(q_c, k_c, v_c, seq)
    attn_l = _attn_lat(q_l, k_l, v_l, k_p, v_p, n_lat, past, 512)

    w_pool_b = w_pool[0].astype(BF16)
    poolo_c = _pool(pool_c, w_pool_b, pool_scale, seq)
    poolo_l = _pool(pool_l, w_pool_b, pool_scale, n_lat)

    woa = w_o_attn[0].astype(BF16)
    wop = w_o_pool[0].astype(BF16)
    wout = w_out[0].astype(BF16)
    rw = jnp.pad(router_w[0], ((0, 0), (0, LANES - N_EXPERTS))).astype(BF16)
    rb = jnp.pad(router_b, ((0, 0), (0, LANES - N_EXPERTS)), constant_values=NEG_BIG)
    tm_mg = 256
    lat_row_mg = lambda i: 1 + (i * tm_mg) // n_lat
    x1_c, h2, tidx_c, tw_c = _merge(attn_c, poolo_c, gates_c, x_prompt.reshape(n_ctx, D_MODEL), mods, ctx_row,
                                    b_branch_gate, woa, wop, wout, norm2_g, rw, rb, tm_mg, n_ctx + n_dec, 0, None)
    x1_l, h2, tidx_l, tw_l = _merge(attn_l, poolo_l, gates_l, x_sample.reshape(n_dec, D_MODEL), mods, lat_row_mg,
                                    b_branch_gate, woa, wop, wout, norm2_g, rw, rb, tm_mg, n_ctx + n_dec, n_ctx, h2)

    n_assign = (n_ctx + n_dec) * TOP_K
    max_chunks = (n_assign + N_EXPERTS * (MOE_CHUNK - 1)) // MOE_CHUNK
    cap = max_chunks * MOE_CHUNK
    n_items = (max_chunks + N_EXPERTS * (MOE_ITEM_CHUNKS - 1)) // MOE_ITEM_CHUNKS
    top_idx = jnp.concatenate([tidx_c[:, :TOP_K], tidx_l[:, :TOP_K]], axis=0)
    dest, item_e, item_row0, item_nch = _routing_tables(top_idx, n_items)

    xs = _dispatch(dest, h2, cap)
    ys = _moe_experts(item_e, item_row0, item_nch, xs,
                      w_exp_gate[0], b_exp_gate[0][:, None, :], w_exp_up[0], b_exp_up[0][:, None, :],
                      w_exp_down[0], b_exp_down[0][:, None, :])

    y_c = _combine(dest[:n_ctx * TOP_K], ys, x1_c, tw_c, mods, lambda r: 0)
    y_l = _combine(dest[n_ctx * TOP_K:], ys, x1_l, tw_l, mods, lambda r: 1 + r // n_lat)

    return (y_c.reshape(batch, seq, D_MODEL),
            y_l.reshape(dec_batch, n_lat, D_MODEL),
            ckv_c.reshape(batch, 1, seq, KV_RANK),
            kr_c.reshape(batch, 1, seq, ROPE))
```

```python
import functools

import jax
import jax.numpy as jnp
from jax import lax
from jax.experimental import pallas as pl
from jax.experimental.pallas import tpu as pltpu

F32 = jnp.float32
BF16 = jnp.bfloat16
I32 = jnp.int32

D_MODEL = 2048
N_HEADS = 16
NOPE = 128
ROPE = 64
QK = NOPE + ROPE
V_DIM = 128
KV_RANK = 512
POOL_WINDOWS = (2, 4, 8, 16)
POOL_WIDTH = 1024
POOL_GROUP = POOL_WIDTH // len(POOL_WINDOWS)
N_EXPERTS = 32
TOP_K = 4
D_FF = 2048
SWIGLU_LIMIT = 7.0
SWIGLU_ALPHA = 1.702
ROPE_BASE = 10000.0
RMS_EPS = 1e-6
GRID_W = 64
ATTN_SCALE = QK ** -0.5

LANES = 128
HEAD_PAD = 2 * LANES
Q_COLS = N_HEADS * HEAD_PAD
IN_TILE = 1024
N_Q_TILES = Q_COLS // IN_TILE
POOL_TILE = N_Q_TILES
GATE_TILE0 = POOL_TILE + 1
N_GATE_TILES = 2 * D_MODEL // IN_TILE
KV_TILE = GATE_TILE0 + N_GATE_TILES
N_IN_TILES = KV_TILE + 1
HEADS_PER_TILE = IN_TILE // HEAD_PAD

MOE_CHUNK = 256
MOE_ITEM_CHUNKS = 8
MOE_ITEM_ROWS = MOE_CHUNK * MOE_ITEM_CHUNKS
FF_TILE = 256
N_FF_TILES = D_FF // FF_TILE
NEG_BIG = -1e30

VMEM_LIMIT = 56 * 1024 * 1024


def _cparams(sem, vmem=VMEM_LIMIT):
    return pltpu.CompilerParams(dimension_semantics=sem, vmem_limit_bytes=vmem)


def _dot(a, b):
    return jnp.dot(a, b, preferred_element_type=F32)


def _sigmoid(z):
    return 1.0 / (1.0 + jnp.exp(-z))


HALF = D_MODEL // 2
HI_MASK = -65536


def _pack_halves(v):
    lo = lax.bitcast_convert_type(v[:, :HALF].astype(BF16).astype(F32), I32)
    hi = lax.bitcast_convert_type(v[:, HALF:].astype(BF16).astype(F32), I32)
    return jnp.bitwise_or(jnp.bitwise_and(hi, HI_MASK), lax.shift_right_logical(lo, 16))


TOK_SUB = HALF // LANES


def _tok_rows(tok0, n):
    row0 = tok0 * TOK_SUB
    return pl.ds(row0 if isinstance(row0, int) else pl.multiple_of(row0, TOK_SUB), n * TOK_SUB)


def _tok_store(ref, tok0, n, words):
    for s in range(TOK_SUB):
        ref[pl.ds(tok0 * TOK_SUB + s, n, stride=TOK_SUB), :] = words[:, s * LANES:(s + 1) * LANES]


def _tok_load(ref, tok0, n):
    return jnp.concatenate([ref[pl.ds(tok0 * TOK_SUB + s, n, stride=TOK_SUB), :] for s in range(TOK_SUB)], axis=1)


def _unpack_halves(w):
    lo = lax.bitcast_convert_type(lax.shift_left(w, 16), F32)
    hi = lax.bitcast_convert_type(jnp.bitwise_and(w, HI_MASK), F32)
    return lo, hi


def _rope_lanes(y, cos, sin_lo, sin_hi):
    return y * cos + pltpu.roll(y, LANES - 16, axis=1) * sin_lo + pltpu.roll(y, 16, axis=1) * sin_hi


def _adaln_kernel(c_ref, w_ref, b_ref, o_ref):
    c = c_ref[...]
    s = (c * _sigmoid(c)).astype(BF16)
    o_ref[...] = _dot(s, w_ref[...].astype(BF16)) + b_ref[...]


def _adaln(cond, w_ada, b_ada):
    rows = cond.shape[0]
    n = w_ada.shape[1]
    tn = 1024
    return pl.pallas_call(
        _adaln_kernel,
        out_shape=jax.ShapeDtypeStruct((rows, n), F32),
        grid=(n // tn,),
        in_specs=[pl.BlockSpec((rows, D_MODEL), lambda j: (0, 0)),
                  pl.BlockSpec((D_MODEL, tn), lambda j: (0, j)),
                  pl.BlockSpec((1, tn), lambda j: (0, j))],
        out_specs=pl.BlockSpec((rows, tn), lambda j: (0, j)),
        compiler_params=_cparams(("arbitrary",)),
        name="adaln",
    )(cond, w_ada, b_ada)


def _in_proj_kernel(rope, x_ref, mods_ref, g1_ref, w_ref, qg_ref, kvg_ref, *rest):
    if rope:
        cos_ref, slo_ref, shi_ref = rest[:3]
        rest = rest[3:]
    q_ref, pool_ref, gates_ref, ckv_ref, kr_ref, krp_ref, h_scr = rest
    j = pl.program_id(1)

    @pl.when(j == 0)
    def _():
        x = x_ref[...]
        y = x * lax.rsqrt(jnp.mean(x * x, axis=-1, keepdims=True) + RMS_EPS) * g1_ref[...]
        h = y * (1.0 + mods_ref[1:2, :]) + mods_ref[0:1, :]
        h_scr[...] = h.astype(BF16)

    def proj(c0, c1):
        return _dot(h_scr[...], w_ref[:, c0:c1])

    @pl.when(j < N_Q_TILES)
    def _():
        for hh in range(HEADS_PER_TILE):
            a = proj(hh * HEAD_PAD, (hh + 1) * HEAD_PAD)
            r = lax.rsqrt(jnp.sum(a * a, axis=-1, keepdims=True) / QK + RMS_EPS)
            y = a * r * qg_ref[...]
            if rope:
                yr = _rope_lanes(y[:, LANES:], cos_ref[...], slo_ref[...], shi_ref[...])
                q_ref[:, hh * HEAD_PAD:hh * HEAD_PAD + LANES] = y[:, :LANES].astype(BF16)
                q_ref[:, hh * HEAD_PAD + LANES:(hh + 1) * HEAD_PAD] = yr.astype(BF16)
            else:
                q_ref[:, hh * HEAD_PAD:(hh + 1) * HEAD_PAD] = y.astype(BF16)

    @pl.when(j == POOL_TILE)
    def _():
        pool_ref[...] = proj(0, IN_TILE)

    @pl.when(jnp.logical_and(j >= GATE_TILE0, j < KV_TILE))
    def _():
        gates_ref[...] = proj(0, IN_TILE)

    @pl.when(j == KV_TILE)
    def _():
        a = proj(0, KV_RANK)
        r = lax.rsqrt(jnp.mean(a * a, axis=-1, keepdims=True) + RMS_EPS)
        ckv_ref[...] = a * r * kvg_ref[...]
        krp = proj(KV_RANK, KV_RANK + LANES)
        krp_ref[...] = krp
        kr_ref[...] = krp[:, :ROPE]


def _in_proj(x, mods, mod_row, g1, w_cat, qg, kvg, rope_tabs, tm):
    t = x.shape[0]
    rope = rope_tabs is not None
    in_specs = [
        pl.BlockSpec((tm, D_MODEL), lambda i, j: (i, 0)),
        pl.BlockSpec((None, 6, D_MODEL), lambda i, j: (mod_row(i), 0, 0)),
        pl.BlockSpec((1, D_MODEL), lambda i, j: (0, 0)),
        pl.BlockSpec((D_MODEL, IN_TILE), lambda i, j: (0, j)),
        pl.BlockSpec((1, HEAD_PAD), lambda i, j: (0, 0)),
        pl.BlockSpec((1, KV_RANK), lambda i, j: (0, 0)),
    ]
    args = [x, mods, g1, w_cat, qg, kvg]
    if rope:
        seq_tiles = rope_tabs[0].shape[0] // tm
        in_specs += [pl.BlockSpec((tm, LANES), lambda i, j: (i % seq_tiles, 0))] * 3
        args += list(rope_tabs)
    out_shape = (
        jax.ShapeDtypeStruct((t, Q_COLS), BF16),
        jax.ShapeDtypeStruct((t, POOL_WIDTH), F32),
        jax.ShapeDtypeStruct((t, 2 * D_MODEL), F32),
        jax.ShapeDtypeStruct((t, KV_RANK), F32),
        jax.ShapeDtypeStruct((t, ROPE), F32),
        jax.ShapeDtypeStruct((t, LANES), F32),
    )
    out_specs = (
        pl.BlockSpec((tm, IN_TILE), lambda i, j: (i, jnp.minimum(j, N_Q_TILES - 1))),
        pl.BlockSpec((tm, POOL_WIDTH), lambda i, j: (i, 0)),
        pl.BlockSpec((tm, IN_TILE), lambda i, j: (i, jnp.clip(j - GATE_TILE0, 0, N_GATE_TILES - 1))),
        pl.BlockSpec((tm, KV_RANK), lambda i, j: (i, 0)),
        pl.BlockSpec((tm, ROPE), lambda i, j: (i, 0)),
        pl.BlockSpec((tm, LANES), lambda i, j: (i, 0)),
    )
    return pl.pallas_call(
        functools.partial(_in_proj_kernel, rope),
        out_shape=out_shape,
        grid=(t // tm, N_IN_TILES),
        in_specs=in_specs,
        out_specs=out_specs,
        scratch_shapes=[pltpu.VMEM((tm, D_MODEL), BF16)],
        compiler_params=_cparams(("arbitrary", "arbitrary")),
        name="in_proj_rope" if rope else "in_proj",
    )(*args)


def _kv_expand_kernel(rope, ckv_ref, krp_ref, w_ref, kgn_ref, kgr_ref, *rest):
    if rope:
        cos_ref, slo_ref, shi_ref = rest[:3]
        rest = rest[3:]
    k_ref, v_ref = rest
    kv = _dot(ckv_ref[...].astype(BF16), w_ref[...])
    kr = krp_ref[...]
    ssq_r = jnp.sum(kr * kr, axis=-1, keepdims=True)
    krg = kr * kgr_ref[...]
    if rope:
        krg = _rope_lanes(krg, cos_ref[...], slo_ref[...], shi_ref[...])
    for h in range(N_HEADS):
        kn = kv[:, h * HEAD_PAD:h * HEAD_PAD + NOPE]
        r = lax.rsqrt((jnp.sum(kn * kn, axis=-1, keepdims=True) + ssq_r) / QK + RMS_EPS)
        k_ref[:, h * HEAD_PAD:h * HEAD_PAD + NOPE] = (kn * r * kgn_ref[...]).astype(BF16)
        k_ref[:, h * HEAD_PAD + NOPE:(h + 1) * HEAD_PAD] = (krg * r).astype(BF16)
        v_ref[:, h * V_DIM:(h + 1) * V_DIM] = kv[:, h * HEAD_PAD + NOPE:(h + 1) * HEAD_PAD].astype(BF16)


def _kv_expand(ckv_n, krp, w_ukv, kgn, kgr, rope_tabs, tr):
    r = ckv_n.shape[0]
    rope = rope_tabs is not None
    in_specs = [
        pl.BlockSpec((tr, KV_RANK), lambda i: (i, 0)),
        pl.BlockSpec((tr, LANES), lambda i: (i, 0)),
        pl.BlockSpec((KV_RANK, N_HEADS * HEAD_PAD), lambda i: (0, 0)),
        pl.BlockSpec((1, LANES), lambda i: (0, 0)),
        pl.BlockSpec((1, LANES), lambda i: (0, 0)),
    ]
    args = [ckv_n, krp, w_ukv, kgn, kgr]
    if rope:
        seq_tiles = rope_tabs[0].shape[0] // tr
        in_specs += [pl.BlockSpec((tr, LANES), lambda i: (i % seq_tiles, 0))] * 3
        args += list(rope_tabs)
    return pl.pallas_call(
        functools.partial(_kv_expand_kernel, rope),
        out_shape=(jax.ShapeDtypeStruct((r, N_HEADS * HEAD_PAD), BF16),
                   jax.ShapeDtypeStruct((r, N_HEADS * V_DIM), BF16)),
        grid=(r // tr,),
        in_specs=in_specs,
        out_specs=(pl.BlockSpec((tr, N_HEADS * HEAD_PAD), lambda i: (i, 0)),
                   pl.BlockSpec((tr, N_HEADS * V_DIM), lambda i: (i, 0))),
        compiler_params=_cparams(("arbitrary",)),
        name="kv_expand_rope" if rope else "kv_expand",
    )(*args)


def _qk(q, k):
    return lax.dot_general(q, k, (((1,), (1,)), ((), ())), preferred_element_type=F32) * ATTN_SCALE


def _attn_ctx_kernel(q_ref, k_ref, v_ref, o_ref):
    for h in range(N_HEADS):
        s = _qk(q_ref[:, h * HEAD_PAD:(h + 1) * HEAD_PAD], k_ref[:, h * HEAD_PAD:(h + 1) * HEAD_PAD])
        p = jnp.exp(s - jnp.max(s, axis=-1, keepdims=True))
        l = jnp.sum(p, axis=-1, keepdims=True)
        o = _dot(p.astype(BF16), v_ref[:, h * V_DIM:(h + 1) * V_DIM])
        o_ref[:, h * V_DIM:(h + 1) * V_DIM] = (o / l).astype(BF16)


def _attn_ctx(q, k, v, seq):
    t = q.shape[0]
    return pl.pallas_call(
        _attn_ctx_kernel,
        out_shape=jax.ShapeDtypeStruct((t, N_HEADS * V_DIM), BF16),
        grid=(t // seq,),
        in_specs=[pl.BlockSpec((seq, Q_COLS), lambda b: (b, 0)),
                  pl.BlockSpec((seq, Q_COLS), lambda b: (b, 0)),
                  pl.BlockSpec((seq, N_HEADS * V_DIM), lambda b: (b, 0))],
        out_specs=pl.BlockSpec((seq, N_HEADS * V_DIM), lambda b: (b, 0)),
        compiler_params=_cparams(("arbitrary",)),
        name="attn_ctx",
    )(q, k, v)


def _attn_lat_kernel(q_ref, k_ref, v_ref, kc_ref, vc_ref, o_ref):
    q = q_ref[...]
    s1 = _qk(q, k_ref[...])
    s2 = _qk(q, kc_ref[...])
    m = jnp.maximum(jnp.max(s1, axis=-1, keepdims=True), jnp.max(s2, axis=-1, keepdims=True))
    p1 = jnp.exp(s1 - m)
    p2 = jnp.exp(s2 - m)
    l = jnp.sum(p1, axis=-1, keepdims=True) + jnp.sum(p2, axis=-1, keepdims=True)
    o = _dot(p1.astype(BF16), v_ref[...]) + _dot(p2.astype(BF16), vc_ref[...])
    o_ref[...] = (o / l).astype(BF16)


def _attn_lat(q, k, v, kc, vc, seq, past, tq):
    t = q.shape[0]
    nq = seq // tq
    return pl.pallas_call(
        _attn_lat_kernel,
        out_shape=jax.ShapeDtypeStruct((t, N_HEADS * V_DIM), BF16),
        grid=(t // seq, N_HEADS, nq),
        in_specs=[pl.BlockSpec((tq, HEAD_PAD), lambda b, h, i: (b * nq + i, h)),
                  pl.BlockSpec((seq, HEAD_PAD), lambda b, h, i: (b, h)),
                  pl.BlockSpec((seq, V_DIM), lambda b, h, i: (b, h)),
                  pl.BlockSpec((past, HEAD_PAD), lambda b, h, i: (b, h)),
                  pl.BlockSpec((past, V_DIM), lambda b, h, i: (b, h))],
        out_specs=pl.BlockSpec((tq, V_DIM), lambda b, h, i: (b * nq + i, h)),
        compiler_params=_cparams(("arbitrary", "arbitrary", "arbitrary")),
        name="attn_lat",
    )(q, k, v, kc, vc)


POOL_HALO = 8


def _pool_kernel(seq, u_ref, w_ref, sc_ref, o_ref, pad_scr):
    zeros = jnp.zeros((POOL_HALO, POOL_WIDTH), F32)
    pad_scr[0:POOL_HALO, :] = zeros
    pad_scr[POOL_HALO + seq:2 * POOL_HALO + seq, :] = zeros
    pad_scr[POOL_HALO:POOL_HALO + seq, :] = u_ref[...]
    t = lax.broadcasted_iota(I32, (seq, 1), 0)
    for g, w in enumerate(POOL_WINDOWS):
        cols = slice(g * POOL_GROUP, (g + 1) * POOL_GROUP)
        tot = None
        for d in range(-(w // 2), w - w // 2):
            piece = pad_scr[POOL_HALO + d:POOL_HALO + d + seq, cols]
            tot = piece if tot is None else tot + piece
        cnt = (jnp.minimum(t + (w - w // 2), seq) - jnp.maximum(t - w // 2, 0)).astype(F32)
        mixed = tot / cnt - u_ref[:, cols]
        o_ref[:, cols] = (_dot(mixed.astype(BF16), w_ref[g]) * sc_ref[:, cols]).astype(BF16)


def _pool(u, w_pool, pool_scale, seq):
    t = u.shape[0]
    n_groups = len(POOL_WINDOWS)
    return pl.pallas_call(
        functools.partial(_pool_kernel, seq),
        out_shape=jax.ShapeDtypeStruct((t, POOL_WIDTH), BF16),
        grid=(t // seq,),
        in_specs=[pl.BlockSpec((seq, POOL_WIDTH), lambda b: (b, 0)),
                  pl.BlockSpec((n_groups, POOL_GROUP, POOL_GROUP), lambda b: (0, 0, 0)),
                  pl.BlockSpec((1, POOL_WIDTH), lambda b: (0, 0))],
        out_specs=pl.BlockSpec((seq, POOL_WIDTH), lambda b: (b, 0)),
        scratch_shapes=[pltpu.VMEM((seq + 2 * POOL_HALO, POOL_WIDTH), F32)],
        compiler_params=_cparams(("arbitrary",)),
        name="pool",
    )(u, w_pool, pool_scale)


def _merge_kernel(attn_ref, pool_ref, gates_ref, x_ref, mods_ref, bbg_ref, woa_ref, wop_ref, wout_ref,
                  g2_ref, rw_ref, rb_ref, *rest):
    x1_ref, h2_ref, tidx_ref, tw_ref = rest[-4:]
    a = _dot(attn_ref[...], woa_ref[...])
    p = _dot(pool_ref[...], wop_ref[...])
    ga = _sigmoid(gates_ref[:, :D_MODEL] + bbg_ref[:, :D_MODEL])
    gp = _sigmoid(gates_ref[:, D_MODEL:] + bbg_ref[:, D_MODEL:])
    merged = (ga * a + gp * p).astype(BF16)
    x1 = x_ref[...] + mods_ref[2:3, :] * _dot(merged, wout_ref[...])
    x1_ref[...] = x1
    y = x1 * lax.rsqrt(jnp.mean(x1 * x1, axis=-1, keepdims=True) + RMS_EPS) * g2_ref[...]
    h2 = y * (1.0 + mods_ref[4:5, :]) + mods_ref[3:4, :]
    _tok_store(h2_ref, 0, h2.shape[0], _pack_halves(h2))
    logits = _dot(h2.astype(BF16), rw_ref[...]) + rb_ref[...]
    lane = lax.broadcasted_iota(I32, logits.shape, 1).astype(F32)
    vals, idxs = [], []
    for _ in range(TOP_K):
        m = jnp.max(logits, axis=-1, keepdims=True)
        ix = jnp.min(jnp.where(logits == m, lane, float(LANES)), axis=-1, keepdims=True)
        vals.append(m)
        idxs.append(ix)
        logits = jnp.where(lane == ix, -jnp.inf, logits)
    es = [jnp.exp(v - vals[0]) for v in vals]
    tot = es[0] + es[1] + es[2] + es[3]
    tidx = jnp.zeros(logits.shape, F32)
    tw = jnp.zeros(logits.shape, F32)
    for k in range(TOP_K):
        tidx = jnp.where(lane == k, idxs[k], tidx)
        tw = jnp.where(lane == k, es[k] / tot, tw)
    tidx_ref[...] = tidx.astype(I32)
    tw_ref[...] = tw


def _merge(attn_o, pool_o, gates, x, mods, mod_row, bbg, woa, wop, wout, g2, rw, rb, tm, h2_rows, h2_row0, h2_prev):
    t = x.shape[0]
    const = lambda shape: pl.BlockSpec(shape, lambda i: (0, 0), pipeline_mode=pl.Buffered(1))
    h2_blk0 = h2_row0 // tm
    extra_specs = [] if h2_prev is None else [pl.BlockSpec(memory_space=pl.ANY)]
    extra_args = [] if h2_prev is None else [h2_prev]
    return pl.pallas_call(
        _merge_kernel,
        out_shape=(jax.ShapeDtypeStruct((t, D_MODEL), F32),
                   jax.ShapeDtypeStruct((h2_rows * TOK_SUB, LANES), I32),
                   jax.ShapeDtypeStruct((t, LANES), I32),
                   jax.ShapeDtypeStruct((t, LANES), F32)),
        grid=(t // tm,),
        input_output_aliases={} if h2_prev is None else {12: 1},
        in_specs=[pl.BlockSpec((tm, D_MODEL), lambda i: (i, 0)),
                  pl.BlockSpec((tm, POOL_WIDTH), lambda i: (i, 0)),
                  pl.BlockSpec((tm, 2 * D_MODEL), lambda i: (i, 0)),
                  pl.BlockSpec((tm, D_MODEL), lambda i: (i, 0)),
                  pl.BlockSpec((None, 6, D_MODEL), lambda i: (mod_row(i), 0, 0)),
                  const((1, 2 * D_MODEL)),
                  const((D_MODEL, D_MODEL)),
                  const((POOL_WIDTH, D_MODEL)),
                  const((D_MODEL, D_MODEL)),
                  const((1, D_MODEL)),
                  const((D_MODEL, LANES)),
                  const((1, LANES))] + extra_specs,
        out_specs=(pl.BlockSpec((tm, D_MODEL), lambda i: (i, 0)),
                   pl.BlockSpec((tm * TOK_SUB, LANES), lambda i: (i + h2_blk0, 0)),
                   pl.BlockSpec((tm, LANES), lambda i: (i, 0)),
                   pl.BlockSpec((tm, LANES), lambda i: (i, 0))),
        compiler_params=_cparams(("arbitrary",)),
        name="merge",
    )(attn_o, pool_o, gates, x, mods, bbg, woa, wop, wout, g2, rw, rb, *extra_args)


DISPATCH_ROWS = 256
DMA_UNROLL = 8


def _dispatch_kernel(dest_ref, h_ref, xs_in_ref, xs_ref, sem):
    del xs_in_ref
    base = pl.program_id(0) * (DISPATCH_ROWS * TOP_K)

    def issue(t, carry):
        for k in range(TOP_K):
            pltpu.make_async_copy(h_ref.at[_tok_rows(t, 1), :],
                                  xs_ref.at[_tok_rows(dest_ref[base + t * TOP_K + k], 1), :], sem).start()
        return carry

    lax.fori_loop(0, DISPATCH_ROWS, issue, 0, unroll=DMA_UNROLL // TOP_K)
    for _ in range(TOP_K):
        pltpu.make_async_copy(h_ref, xs_ref.at[_tok_rows(0, DISPATCH_ROWS), :], sem).wait()


def _dispatch(dest, h2, cap):
    t = h2.shape[0] // TOK_SUB
    return pl.pallas_call(
        _dispatch_kernel,
        out_shape=jax.ShapeDtypeStruct((cap * TOK_SUB, LANES), I32),
        grid_spec=pltpu.PrefetchScalarGridSpec(
            num_scalar_prefetch=1,
            grid=(t // DISPATCH_ROWS,),
            in_specs=[pl.BlockSpec((DISPATCH_ROWS * TOK_SUB, LANES), lambda i, d: (i, 0)),
                      pl.BlockSpec(memory_space=pl.ANY)],
            out_specs=pl.BlockSpec(memory_space=pl.ANY),
            scratch_shapes=[pltpu.SemaphoreType.DMA]),
        input_output_aliases={2: 0},
        compiler_params=_cparams(("arbitrary",)),
        name="moe_dispatch",
    )(dest, h2, jnp.zeros((cap * TOK_SUB, LANES), I32))


MOE_SPAN = 4


def _moe_kernel(ie_ref, ir_ref, in_ref, xs_ref, wg_ref, bg_ref, wu_ref, bu_ref, wd_ref, bd_ref, ys_ref,
                xin_scr, acc_scr, wg_scr, wu_scr, wd_scr, sem_x, sem_y):
    i = pl.program_id(0)
    f = pl.program_id(1)
    n_items = pl.num_programs(0)
    nch = in_ref[i]
    row0 = ir_ref[i]
    slot = i % 2

    def rows_of(c, k=1):
        return pl.ds(pl.multiple_of(c * MOE_CHUNK, MOE_CHUNK), k * MOE_CHUNK)

    def x_copy(item_row0, c, s):
        return pltpu.make_async_copy(xs_ref.at[_tok_rows(item_row0 + c * MOE_CHUNK, MOE_CHUNK), :],
                                     xin_scr.at[s, _tok_rows(c * MOE_CHUNK, MOE_CHUNK), :], sem_x.at[s])

    def y_copy(item_row0, c, s):
        return pltpu.make_async_copy(xin_scr.at[s, _tok_rows(c * MOE_CHUNK, MOE_CHUNK), :],
                                     ys_ref.at[_tok_rows(item_row0 + c * MOE_CHUNK, MOE_CHUNK), :], sem_y)

    def for_chunks(n, body):
        def step(c, carry):
            body(c)
            return carry

        lax.fori_loop(0, n, step, 0)

    def for_spans(n, body):
        def span(g, carry):
            body(g * MOE_SPAN, MOE_SPAN)
            return carry

        lax.fori_loop(0, n // MOE_SPAN, span, 0)
        k = MOE_SPAN // 2
        while k >= 1:
            start = n // (2 * k) * (2 * k)

            @pl.when(n % (2 * k) >= k)
            def _(start=start, k=k):
                body(start, k)

            k //= 2

    @pl.when(f == 0)
    def _():
        @pl.when(i > 0)
        def _():
            prev = jnp.maximum(i - 1, 0)
            for_chunks(in_ref[prev], lambda c: y_copy(ir_ref[prev], c, 1 - slot).wait())

        @pl.when(i == 0)
        def _():
            for_chunks(nch, lambda c: x_copy(row0, c, slot).start())

        @pl.when(i + 1 < n_items)
        def _():
            nxt = jnp.minimum(i + 1, n_items - 1)
            for_chunks(in_ref[nxt], lambda c: x_copy(ir_ref[nxt], c, 1 - slot).start())

        for_chunks(nch, lambda c: x_copy(row0, c, slot).wait())

    @pl.when(nch > 0)
    def _():
        wg_scr[...] = wg_ref[...].astype(BF16)
        wu_scr[...] = wu_ref[...].astype(BF16)
        wd_scr[...] = wd_ref[...].astype(BF16)

        def contribution(c, k):
            lo, hi = _unpack_halves(_tok_load(xin_scr.at[slot], c * MOE_CHUNK, k * MOE_CHUNK))
            x = jnp.concatenate([lo.astype(BF16), hi.astype(BF16)], axis=1)
            gt = jnp.minimum(_dot(x, wg_scr[...]) + bg_ref[...], SWIGLU_LIMIT)
            up = jnp.clip(_dot(x, wu_scr[...]) + bu_ref[...], -SWIGLU_LIMIT, SWIGLU_LIMIT)
            act = (up + 1.0) * (gt * _sigmoid(SWIGLU_ALPHA * gt))
            return _dot(act.astype(BF16), wd_scr[...])

        def first(c, k):
            acc_scr[rows_of(c, k), :] = contribution(c, k) + bd_ref[...]

        def middle(c, k):
            acc_scr[rows_of(c, k), :] += contribution(c, k)

        def last(c, k):
            y = acc_scr[rows_of(c, k), :] + contribution(c, k)
            _tok_store(xin_scr.at[slot], c * MOE_CHUNK, k * MOE_CHUNK, _pack_halves(y))
            for u in range(k):
                y_copy(row0, c + u, slot).start()

        @pl.when(f == 0)
        def _():
            for_spans(nch, first)

        @pl.when(jnp.logical_and(f > 0, f < N_FF_TILES - 1))
        def _():
            for_spans(nch, middle)

        @pl.when(f == N_FF_TILES - 1)
        def _():
            for_spans(nch, last)

            @pl.when(i == n_items - 1)
            def _():
                for_chunks(nch, lambda c: y_copy(row0, c, slot).wait())


def _moe_experts(item_e, item_row0, item_nch, xs, w_g, b_g, w_u, b_u, w_d, b_d):
    n_items = item_e.shape[0]

    def ff_idx(i, f, ie, ir, inch):
        return jnp.where(inch[i] > 0, f, N_FF_TILES - 1)

    return pl.pallas_call(
        _moe_kernel,
        out_shape=jax.ShapeDtypeStruct(xs.shape, I32),
        grid_spec=pltpu.PrefetchScalarGridSpec(
            num_scalar_prefetch=3,
            grid=(n_items, N_FF_TILES),
            in_specs=[
                pl.BlockSpec(memory_space=pl.ANY),
                pl.BlockSpec((None, D_MODEL, FF_TILE), lambda i, f, ie, ir, inch: (ie[i], 0, ff_idx(i, f, ie, ir, inch))),
                pl.BlockSpec((None, 1, FF_TILE), lambda i, f, ie, ir, inch: (ie[i], 0, ff_idx(i, f, ie, ir, inch))),
                pl.BlockSpec((None, D_MODEL, FF_TILE), lambda i, f, ie, ir, inch: (ie[i], 0, ff_idx(i, f, ie, ir, inch))),
                pl.BlockSpec((None, 1, FF_TILE), lambda i, f, ie, ir, inch: (ie[i], 0, ff_idx(i, f, ie, ir, inch))),
                pl.BlockSpec((None, FF_TILE, D_MODEL), lambda i, f, ie, ir, inch: (ie[i], ff_idx(i, f, ie, ir, inch), 0)),
                pl.BlockSpec((None, 1, D_MODEL), lambda i, f, ie, ir, inch: (ie[i], 0, 0)),
            ],
            out_specs=pl.BlockSpec(memory_space=pl.ANY),
            scratch_shapes=[
                pltpu.VMEM((2, MOE_ITEM_ROWS * TOK_SUB, LANES), I32),
                pltpu.VMEM((MOE_ITEM_ROWS, D_MODEL), F32),
                pltpu.VMEM((D_MODEL, FF_TILE), BF16),
                pltpu.VMEM((D_MODEL, FF_TILE), BF16),
                pltpu.VMEM((FF_TILE, D_MODEL), BF16),
                pltpu.SemaphoreType.DMA((2,)),
                pltpu.SemaphoreType.DMA,
            ]),
        compiler_params=_cparams(("arbitrary", "arbitrary")),
        name="moe_experts",
    )(item_e, item_row0, item_nch, xs, w_g, b_g, w_u, b_u, w_d, b_d)


COMBINE_ROWS = 128


def _combine_kernel(dest_ref, ys_ref, x1_ref, tw_ref, mods_ref, o_ref, ybuf, sem):
    base = pl.program_id(0) * (COMBINE_ROWS * TOP_K)

    def row_copy(src_row, t, k):
        return pltpu.make_async_copy(ys_ref.at[_tok_rows(src_row, 1), :], ybuf.at[k, _tok_rows(t, 1), :], sem)

    def issue(t, carry):
        for k in range(TOP_K):
            row_copy(dest_ref[base + t * TOP_K + k], t, k).start()
        return carry

    lax.fori_loop(0, COMBINE_ROWS, issue, 0, unroll=DMA_UNROLL // TOP_K)
    for k in range(TOP_K):
        pltpu.make_async_copy(ys_ref.at[_tok_rows(0, COMBINE_ROWS), :], ybuf.at[k], sem).wait()

    y_lo = y_hi = None
    for k in range(TOP_K):
        lo, hi = _unpack_halves(_tok_load(ybuf.at[k], 0, COMBINE_ROWS))
        w = tw_ref[:, k:k + 1]
        y_lo = w * lo if y_lo is None else y_lo + w * lo
        y_hi = w * hi if y_hi is None else y_hi + w * hi
    o_ref[:, :HALF] = x1_ref[:, :HALF] + mods_ref[5:6, :HALF] * y_lo
    o_ref[:, HALF:] = x1_ref[:, HALF:] + mods_ref[5:6, HALF:] * y_hi


def _combine(dest, ys, x1, tw, mods, mod_row):
    t = x1.shape[0]
    return pl.pallas_call(
        _combine_kernel,
        out_shape=jax.ShapeDtypeStruct((t, D_MODEL), F32),
        grid_spec=pltpu.PrefetchScalarGridSpec(
            num_scalar_prefetch=1,
            grid=(t // COMBINE_ROWS,),
            in_specs=[pl.BlockSpec(memory_space=pl.ANY),
                      pl.BlockSpec((COMBINE_ROWS, D_MODEL), lambda i, d: (i, 0)),
                      pl.BlockSpec((COMBINE_ROWS, LANES), lambda i, d: (i, 0)),
                      pl.BlockSpec((None, 6, D_MODEL), lambda i, d: (mod_row(i * COMBINE_ROWS), 0, 0))],
            out_specs=pl.BlockSpec((COMBINE_ROWS, D_MODEL), lambda i, d: (i, 0)),
            scratch_shapes=[pltpu.VMEM((TOP_K, COMBINE_ROWS * TOK_SUB, LANES), I32), pltpu.SemaphoreType.DMA]),
        compiler_params=_cparams(("arbitrary",)),
        name="moe_combine",
    )(dest, ys, x1, tw, mods)


def _routing_tables(top_idx, n_items):
    flat_e = top_idx.reshape(-1)
    onehot = (flat_e[:, None] == jnp.arange(N_EXPERTS, dtype=I32)[None, :]).astype(I32)
    csum = jnp.cumsum(onehot, axis=0)
    rank = jnp.take_along_axis(csum, flat_e[:, None], axis=1)[:, 0] - 1
    counts = csum[-1]
    nch = (counts + MOE_CHUNK - 1) // MOE_CHUNK
    pad_rows = nch * MOE_CHUNK
    pad_end = jnp.cumsum(pad_rows)
    pad_start = pad_end - pad_rows
    dest = (pad_start[flat_e] + rank).astype(I32)
    items_e = (nch + MOE_ITEM_CHUNKS - 1) // MOE_ITEM_CHUNKS
    item_end = jnp.cumsum(items_e)
    item_start = item_end - items_e
    total = item_end[-1]
    i = jnp.arange(n_items, dtype=I32)
    ii = jnp.minimum(i, total - 1)
    e_i = jnp.minimum(jnp.searchsorted(item_end, ii, side="right"), N_EXPERTS - 1).astype(I32)
    local = ii - item_start[e_i]
    row0 = (pad_start[e_i] + local * MOE_ITEM_ROWS).astype(I32)
    n_i = jnp.where(i < total, jnp.minimum(MOE_ITEM_CHUNKS, nch[e_i] - local * MOE_ITEM_CHUNKS), 0).astype(I32)
    return dest, e_i, row0, n_i


def _rope_tables(n_lat):
    nf = ROPE // 4
    inv = ROPE_BASE ** (-jnp.arange(nf, dtype=F32) / nf)
    t = jnp.arange(n_lat)
    row = (t // GRID_W).astype(F32)
    col = (t % GRID_W).astype(F32)
    ang_r = row[:, None] * inv[None, :]
    ang_c = col[:, None] * inv[None, :]
    z = jnp.zeros((n_lat, nf), F32)
    tail = jnp.zeros((n_lat, LANES - ROPE), F32)
    cos = jnp.concatenate([jnp.cos(ang_r), jnp.cos(ang_r), jnp.cos(ang_c), jnp.cos(ang_c), tail], axis=1)
    sin_lo = jnp.concatenate([-jnp.sin(ang_r), z, -jnp.sin(ang_c), z, tail], axis=1)
    sin_hi = jnp.concatenate([z, jnp.sin(ang_r), z, jnp.sin(ang_c), tail], axis=1)
    return cos, sin_lo, sin_hi


def kernel(x_prompt, x_sample, cache_ckv, cache_krope, c, c_ctx, norm1_g, norm2_g, w_ada, b_ada, w_in,
           b_branch_gate, kv_norm_g, w_ukv, q_norm_g, k_norm_g, w_o_attn, w_pool, pool_scale, w_o_pool, w_out,
           router_w, router_b, w_exp_gate, b_exp_gate, w_exp_up, b_exp_up, w_exp_down, b_exp_down):
    assert w_in.shape[0] == 1, "single-layer trunk"
    batch, seq, _ = x_prompt.shape
    dec_batch, n_lat, _ = x_sample.shape
    past = cache_ckv.shape[2]
    n_ctx = batch * seq
    n_dec = dec_batch * n_lat

    cond = jnp.concatenate([c_ctx[None, :], c, jnp.zeros((8 - 1 - dec_batch, D_MODEL), F32)], axis=0)
    mods = _adaln(cond, w_ada[0], b_ada).reshape(8, 6, D_MODEL)

    w = w_in[0]
    i0 = N_HEADS * QK
    i1 = i0 + KV_RANK
    i2 = i1 + ROPE
    i3 = i2 + POOL_WIDTH
    wq = jnp.pad(w[:, :i0].reshape(D_MODEL, N_HEADS, QK), ((0, 0), (0, 0), (0, HEAD_PAD - QK)))
    w_cat = jnp.concatenate([wq.reshape(D_MODEL, Q_COLS), w[:, i2:i3], w[:, i3:], w[:, i0:i1], w[:, i1:i2],
                             jnp.zeros((D_MODEL, IN_TILE - KV_RANK - ROPE), F32)], axis=1).astype(BF16)
    qg = jnp.pad(q_norm_g, ((0, 0), (0, HEAD_PAD - QK)))
    kgn = k_norm_g[:, :NOPE]
    kgr = jnp.pad(k_norm_g[:, NOPE:], ((0, 0), (0, LANES - ROPE)))
    w_ukv_b = w_ukv[0].astype(BF16)
    rope_tabs = _rope_tables(n_lat)

    ctx_row = lambda i: 0
    tm_in = 512
    lat_row_in = lambda i: 1 + (i * tm_in) // n_lat
    q_c, pool_c, gates_c, ckv_c, kr_c, krp_c = _in_proj(
        x_prompt.reshape(n_ctx, D_MODEL), mods, ctx_row, norm1_g, w_cat, qg, kv_norm_g, None, tm_in)
    q_l, pool_l, gates_l, ckv_l, _, krp_l = _in_proj(
        x_sample.reshape(n_dec, D_MODEL), mods, lat_row_in, norm1_g, w_cat, qg, kv_norm_g, rope_tabs, tm_in)

    tr = 256
    k_c, v_c = _kv_expand(ckv_c, krp_c, w_ukv_b, kgn, kgr, None, tr)
    k_l, v_l = _kv_expand(ckv_l, krp_l, w_ukv_b, kgn, kgr, rope_tabs, tr)
    cache_krp = jnp.pad(cache_krope.reshape(dec_batch * past, ROPE), ((0, 0), (0, LANES - ROPE)))
    k_p, v_p = _kv_expand(cache_ckv.reshape(dec_batch * past, KV_RANK), cache_krp, w_ukv_b, kgn, kgr, None, tr)

    attn_c = _attn_ctx(q_c, k_c, v_c, seq)
    attn_l = _attn_lat(q_l, k_l, v_l, k_p, v_p, n_lat, past, 512)

    w_pool_b = w_pool[0].astype(BF16)
    poolo_c = _pool(pool_c, w_pool_b, pool_scale, seq)
    poolo_l = _pool(pool_l, w_pool_b, pool_scale, n_lat)

    woa = w_o_attn[0].astype(BF16)
    wop = w_o_pool[0].astype(BF16)
    wout = w_out[0].astype(BF16)
    rw = jnp.pad(router_w[0], ((0, 0), (0, LANES - N_EXPERTS))).astype(BF16)
    rb = jnp.pad(router_b, ((0, 0), (0, LANES - N_EXPERTS)), constant_values=NEG_BIG)
    tm_mg = 256
    lat_row_mg = lambda i: 1 + (i * tm_mg) // n_lat
    x1_c, h2, tidx_c, tw_c = _merge(attn_c, poolo_c, gates_c, x_prompt.reshape(n_ctx, D_MODEL), mods, ctx_row,
                                    b_branch_gate, woa, wop, wout, norm2_g, rw, rb, tm_mg, n_ctx + n_dec, 0, None)
    x1_l, h2, tidx_l, tw_l = _merge(attn_l, poolo_l, gates_l, x_sample.reshape(n_dec, D_MODEL), mods, lat_row_mg,
                                    b_branch_gate, woa, wop, wout, norm2_g, rw, rb, tm_mg, n_ctx + n_dec, n_ctx, h2)

    n_assign = (n_ctx + n_dec) * TOP_K
    max_chunks = (n_assign + N_EXPERTS * (MOE_CHUNK - 1)) // MOE_CHUNK
    cap = max_chunks * MOE_CHUNK
    n_items = (max_chunks + N_EXPERTS * (MOE_ITEM_CHUNKS - 1)) // MOE_ITEM_CHUNKS
    top_idx = jnp.concatenate([tidx_c[:, :TOP_K], tidx_l[:, :TOP_K]], axis=0)
    dest, item_e, item_row0, item_nch = _routing_tables(top_idx, n_items)

    xs = _dispatch(dest, h2, cap)
    ys = _moe_experts(item_e, item_row0, item_nch, xs,
                      w_exp_gate[0], b_exp_gate[0][:, None, :], w_exp_up[0], b_exp_up[0][:, None, :],
                      w_exp_down[0], b_exp_down[0][:, None, :])

    y_c = _combine(dest[:n_ctx * TOP_K], ys, x1_c, tw_c, mods, lambda r: 0)
    y_l = _combine(dest[n_ctx * TOP_K:], ys, x1_l, tw_l, mods, lambda r: 1 + r // n_lat)

    return (y_c.reshape(batch, seq, D_MODEL),
            y_l.reshape(dec_batch, n_lat, D_MODEL),
            ckv_c.reshape(batch, 1, seq, KV_RANK),
            kr_c.reshape(batch, 1, seq, ROPE))
```

```python
import functools

import jax
import jax.numpy as jnp
from jax import lax
from jax.experimental import pallas as pl
from jax.experimental.pallas import tpu as pltpu

F32 = jnp.float32
BF16 = jnp.bfloat16
I32 = jnp.int32

D_MODEL = 2048
N_HEADS = 16
NOPE = 128
ROPE = 64
QK = NOPE + ROPE
V_DIM = 128
KV_RANK = 512
POOL_WINDOWS = (2, 4, 8, 16)
POOL_WIDTH = 1024
POOL_GROUP = POOL_WIDTH // len(POOL_WINDOWS)
N_EXPERTS = 32
TOP_K = 4
D_FF = 2048
SWIGLU_LIMIT = 7.0
SWIGLU_ALPHA = 1.702
ROPE_BASE = 10000.0
RMS_EPS = 1e-6
GRID_W = 64
ATTN_SCALE = QK ** -0.5

LANES = 128
HEAD_PAD = 2 * LANES
Q_COLS = N_HEADS * HEAD_PAD
IN_TILE = 1024
N_Q_TILES = Q_COLS // IN_TILE
POOL_TILE = N_Q_TILES
GATE_TILE0 = POOL_TILE + 1
N_GATE_TILES = 2 * D_MODEL // IN_TILE
KV_TILE = GATE_TILE0 + N_GATE_TILES
N_IN_TILES = KV_TILE + 1
HEADS_PER_TILE = IN_TILE // HEAD_PAD

MOE_CHUNK = 128
MOE_ITEM_CHUNKS = 16
MOE_ITEM_ROWS = MOE_CHUNK * MOE_ITEM_CHUNKS
FF_TILE = 256
N_FF_TILES = D_FF // FF_TILE
NEG_BIG = -1e30

VMEM_LIMIT = 56 * 1024 * 1024


def _cparams(sem, vmem=VMEM_LIMIT):
    return pltpu.CompilerParams(dimension_semantics=sem, vmem_limit_bytes=vmem)


def _dot(a, b):
    return jnp.dot(a, b, preferred_element_type=F32)


def _sigmoid(z):
    return 1.0 / (1.0 + jnp.exp(-z))


HALF = D_MODEL // 2
HI_MASK = -65536


def _pack_halves(v):
    lo = lax.bitcast_convert_type(v[:, :HALF].astype(BF16).astype(F32), I32)
    hi = lax.bitcast_convert_type(v[:, HALF:].astype(BF16).astype(F32), I32)
    return jnp.bitwise_or(jnp.bitwise_and(hi, HI_MASK), lax.shift_right_logical(lo, 16))


TOK_SUB = HALF // LANES


def _tok_rows(tok0, n):
    row0 = tok0 * TOK_SUB
    return pl.ds(row0 if isinstance(row0, int) else pl.multiple_of(row0, TOK_SUB), n * TOK_SUB)


def _tok_store(ref, tok0, n, words):
    for s in range(TOK_SUB):
        ref[pl.ds(tok0 * TOK_SUB + s, n, stride=TOK_SUB), :] = words[:, s * LANES:(s + 1) * LANES]


def _tok_load(ref, tok0, n):
    return jnp.concatenate([ref[pl.ds(tok0 * TOK_SUB + s, n, stride=TOK_SUB), :] for s in range(TOK_SUB)], axis=1)


def _unpack_halves(w):
    lo = lax.bitcast_convert_type(lax.shift_left(w, 16), F32)
    hi = lax.bitcast_convert_type(jnp.bitwise_and(w, HI_MASK), F32)
    return lo, hi


def _rope_lanes(y, cos, sin_lo, sin_hi):
    return y * cos + pltpu.roll(y, LANES - 16, axis=1) * sin_lo + pltpu.roll(y, 16, axis=1) * sin_hi


def _adaln_kernel(c_ref, w_ref, b_ref, o_ref):
    c = c_ref[...]
    s = (c * _sigmoid(c)).astype(BF16)
    o_ref[...] = _dot(s, w_ref[...].astype(BF16)) + b_ref[...]


def _adaln(cond, w_ada, b_ada):
    rows = cond.shape[0]
    n = w_ada.shape[1]
    tn = 1024
    return pl.pallas_call(
        _adaln_kernel,
        out_shape=jax.ShapeDtypeStruct((rows, n), F32),
        grid=(n // tn,),
        in_specs=[pl.BlockSpec((rows, D_MODEL), lambda j: (0, 0)),
                  pl.BlockSpec((D_MODEL, tn), lambda j: (0, j)),
                  pl.BlockSpec((1, tn), lambda j: (0, j))],
        out_specs=pl.BlockSpec((rows, tn), lambda j: (0, j)),
        compiler_params=_cparams(("arbitrary",)),
        name="adaln",
    )(cond, w_ada, b_ada)


def _in_proj_kernel(rope, x_ref, mods_ref, g1_ref, w_ref, qg_ref, kvg_ref, *rest):
    if rope:
        cos_ref, slo_ref, shi_ref = rest[:3]
        rest = rest[3:]
    q_ref, pool_ref, gates_ref, ckv_ref, kr_ref, krp_ref, h_scr = rest
    j = pl.program_id(1)

    @pl.when(j == 0)
    def _():
        x = x_ref[...]
        y = x * lax.rsqrt(jnp.mean(x * x, axis=-1, keepdims=True) + RMS_EPS) * g1_ref[...]
        h = y * (1.0 + mods_ref[1:2, :]) + mods_ref[0:1, :]
        h_scr[...] = h.astype(BF16)

    def proj(c0, c1):
        return _dot(h_scr[...], w_ref[:, c0:c1])

    @pl.when(j < N_Q_TILES)
    def _():
        for hh in range(HEADS_PER_TILE):
            a = proj(hh * HEAD_PAD, (hh + 1) * HEAD_PAD)
            r = lax.rsqrt(jnp.sum(a * a, axis=-1, keepdims=True) / QK + RMS_EPS)
            y = a * r * qg_ref[...]
            if rope:
                yr = _rope_lanes(y[:, LANES:], cos_ref[...], slo_ref[...], shi_ref[...])
                q_ref[:, hh * HEAD_PAD:hh * HEAD_PAD + LANES] = y[:, :LANES].astype(BF16)
                q_ref[:, hh * HEAD_PAD + LANES:(hh + 1) * HEAD_PAD] = yr.astype(BF16)
            else:
                q_ref[:, hh * HEAD_PAD:(hh + 1) * HEAD_PAD] = y.astype(BF16)

    @pl.when(j == POOL_TILE)
    def _():
        pool_ref[...] = proj(0, IN_TILE)

    @pl.when(jnp.logical_and(j >= GATE_TILE0, j < KV_TILE))
    def _():
        gates_ref[...] = proj(0, IN_TILE)

    @pl.when(j == KV_TILE)
    def _():
        a = proj(0, KV_RANK)
        r = lax.rsqrt(jnp.mean(a * a, axis=-1, keepdims=True) + RMS_EPS)
        ckv_ref[...] = a * r * kvg_ref[...]
        krp = proj(KV_RANK, KV_RANK + LANES)
        krp_ref[...] = krp
        kr_ref[...] = krp[:, :ROPE]


def _in_proj(x, mods, mod_row, g1, w_cat, qg, kvg, rope_tabs, tm):
    t = x.shape[0]
    rope = rope_tabs is not None
    in_specs = [
        pl.BlockSpec((tm, D_MODEL), lambda i, j: (i, 0)),
        pl.BlockSpec((None, 6, D_MODEL), lambda i, j: (mod_row(i), 0, 0)),
        pl.BlockSpec((1, D_MODEL), lambda i, j: (0, 0)),
        pl.BlockSpec((D_MODEL, IN_TILE), lambda i, j: (0, j)),
        pl.BlockSpec((1, HEAD_PAD), lambda i, j: (0, 0)),
        pl.BlockSpec((1, KV_RANK), lambda i, j: (0, 0)),
    ]
    args = [x, mods, g1, w_cat, qg, kvg]
    if rope:
        seq_tiles = rope_tabs[0].shape[0] // tm
        in_specs += [pl.BlockSpec((tm, LANES), lambda i, j: (i % seq_tiles, 0))] * 3
        args += list(rope_tabs)
    out_shape = (
        jax.ShapeDtypeStruct((t, Q_COLS), BF16),
        jax.ShapeDtypeStruct((t, POOL_WIDTH), F32),
        jax.ShapeDtypeStruct((t, 2 * D_MODEL), F32),
        jax.ShapeDtypeStruct((t, KV_RANK), F32),
        jax.ShapeDtypeStruct((t, ROPE), F32),
        jax.ShapeDtypeStruct((t, LANES), F32),
    )
    out_specs = (
        pl.BlockSpec((tm, IN_TILE), lambda i, j: (i, jnp.minimum(j, N_Q_TILES - 1))),
        pl.BlockSpec((tm, POOL_WIDTH), lambda i, j: (i, 0)),
        pl.BlockSpec((tm, IN_TILE), lambda i, j: (i, jnp.clip(j - GATE_TILE0, 0, N_GATE_TILES - 1))),
        pl.BlockSpec((tm, KV_RANK), lambda i, j: (i, 0)),
        pl.BlockSpec((tm, ROPE), lambda i, j: (i, 0)),
        pl.BlockSpec((tm, LANES), lambda i, j: (i, 0)),
    )
    return pl.pallas_call(
        functools.partial(_in_proj_kernel, rope),
        out_shape=out_shape,
        grid=(t // tm, N_IN_TILES),
        in_specs=in_specs,
        out_specs=out_specs,
        scratch_shapes=[pltpu.VMEM((tm, D_MODEL), BF16)],
        compiler_params=_cparams(("arbitrary", "arbitrary")),
        name="in_proj_rope" if rope else "in_proj",
    )(*args)


def _kv_expand_kernel(rope, ckv_ref, krp_ref, w_ref, kgn_ref, kgr_ref, *rest):
    if rope:
        cos_ref, slo_ref, shi_ref = rest[:3]
        rest = rest[3:]
    k_ref, v_ref = rest
    kv = _dot(ckv_ref[...].astype(BF16), w_ref[...])
    kr = krp_ref[...]
    ssq_r = jnp.sum(kr * kr, axis=-1, keepdims=True)
    krg = kr * kgr_ref[...]
    if rope:
        krg = _rope_lanes(krg, cos_ref[...], slo_ref[...], shi_ref[...])
    for h in range(N_HEADS):
        kn = kv[:, h * HEAD_PAD:h * HEAD_PAD + NOPE]
        r = lax.rsqrt((jnp.sum(kn * kn, axis=-1, keepdims=True) + ssq_r) / QK + RMS_EPS)
        k_ref[:, h * HEAD_PAD:h * HEAD_PAD + NOPE] = (kn * r * kgn_ref[...]).astype(BF16)
        k_ref[:, h * HEAD_PAD + NOPE:(h + 1) * HEAD_PAD] = (krg * r).astype(BF16)
        v_ref[:, h * V_DIM:(h + 1) * V_DIM] = kv[:, h * HEAD_PAD + NOPE:(h + 1) * HEAD_PAD].astype(BF16)


def _kv_expand(ckv_n, krp, w_ukv, kgn, kgr, rope_tabs, tr):
    r = ckv_n.shape[0]
    rope = rope_tabs is not None
    in_specs = [
        pl.BlockSpec((tr, KV_RANK), lambda i: (i, 0)),
        pl.BlockSpec((tr, LANES), lambda i: (i, 0)),
        pl.BlockSpec((KV_RANK, N_HEADS * HEAD_PAD), lambda i: (0, 0)),
        pl.BlockSpec((1, LANES), lambda i: (0, 0)),
        pl.BlockSpec((1, LANES), lambda i: (0, 0)),
    ]
    args = [ckv_n, krp, w_ukv, kgn, kgr]
    if rope:
        seq_tiles = rope_tabs[0].shape[0] // tr
        in_specs += [pl.BlockSpec((tr, LANES), lambda i: (i % seq_tiles, 0))] * 3
        args += list(rope_tabs)
    return pl.pallas_call(
        functools.partial(_kv_expand_kernel, rope),
        out_shape=(jax.ShapeDtypeStruct((r, N_HEADS * HEAD_PAD), BF16),
                   jax.ShapeDtypeStruct((r, N_HEADS * V_DIM), BF16)),
        grid=(r // tr,),
        in_specs=in_specs,
        out_specs=(pl.BlockSpec((tr, N_HEADS * HEAD_PAD), lambda i: (i, 0)),
                   pl.BlockSpec((tr, N_HEADS * V_DIM), lambda i: (i, 0))),
        compiler_params=_cparams(("arbitrary",)),
        name="kv_expand_rope" if rope else "kv_expand",
    )(*args)


def _qk(q, k):
    return lax.dot_general(q, k, (((1,), (1,)), ((), ())), preferred_element_type=F32) * ATTN_SCALE


def _attn_ctx_kernel(q_ref, k_ref, v_ref, o_ref):
    for h in range(N_HEADS):
        s = _qk(q_ref[:, h * HEAD_PAD:(h + 1) * HEAD_PAD], k_ref[:, h * HEAD_PAD:(h + 1) * HEAD_PAD])
        p = jnp.exp(s - jnp.max(s, axis=-1, keepdims=True))
        l = jnp.sum(p, axis=-1, keepdims=True)
        o = _dot(p.astype(BF16), v_ref[:, h * V_DIM:(h + 1) * V_DIM])
        o_ref[:, h * V_DIM:(h + 1) * V_DIM] = (o / l).astype(BF16)


def _attn_ctx(q, k, v, seq):
    t = q.shape[0]
    return pl.pallas_call(
        _attn_ctx_kernel,
        out_shape=jax.ShapeDtypeStruct((t, N_HEADS * V_DIM), BF16),
        grid=(t // seq,),
        in_specs=[pl.BlockSpec((seq, Q_COLS), lambda b: (b, 0)),
                  pl.BlockSpec((seq, Q_COLS), lambda b: (b, 0)),
                  pl.BlockSpec((seq, N_HEADS * V_DIM), lambda b: (b, 0))],
        out_specs=pl.BlockSpec((seq, N_HEADS * V_DIM), lambda b: (b, 0)),
        compiler_params=_cparams(("arbitrary",)),
        name="attn_ctx",
    )(q, k, v)


def _attn_lat_kernel(q_ref, k_ref, v_ref, kc_ref, vc_ref, o_ref):
    q = q_ref[...]
    s1 = _qk(q, k_ref[...])
    s2 = _qk(q, kc_ref[...])
    m = jnp.maximum(jnp.max(s1, axis=-1, keepdims=True), jnp.max(s2, axis=-1, keepdims=True))
    p1 = jnp.exp(s1 - m)
    p2 = jnp.exp(s2 - m)
    l = jnp.sum(p1, axis=-1, keepdims=True) + jnp.sum(p2, axis=-1, keepdims=True)
    o = _dot(p1.astype(BF16), v_ref[...]) + _dot(p2.astype(BF16), vc_ref[...])
    o_ref[...] = (o / l).astype(BF16)


def _attn_lat(q, k, v, kc, vc, seq, past, tq):
    t = q.shape[0]
    nq = seq // tq
    return pl.pallas_call(
        _attn_lat_kernel,
        out_shape=jax.ShapeDtypeStruct((t, N_HEADS * V_DIM), BF16),
        grid=(t // seq, N_HEADS, nq),
        in_specs=[pl.BlockSpec((tq, HEAD_PAD), lambda b, h, i: (b * nq + i, h)),
                  pl.BlockSpec((seq, HEAD_PAD), lambda b, h, i: (b, h)),
                  pl.BlockSpec((seq, V_DIM), lambda b, h, i: (b, h)),
                  pl.BlockSpec((past, HEAD_PAD), lambda b, h, i: (b, h)),
                  pl.BlockSpec((past, V_DIM), lambda b, h, i: (b, h))],
        out_specs=pl.BlockSpec((tq, V_DIM), lambda b, h, i: (b * nq + i, h)),
        compiler_params=_cparams(("arbitrary", "arbitrary", "arbitrary")),
        name="attn_lat",
    )(q, k, v, kc, vc)


POOL_HALO = 8


def _pool_kernel(seq, u_ref, w_ref, sc_ref, o_ref, pad_scr):
    zeros = jnp.zeros((POOL_HALO, POOL_WIDTH), F32)
    pad_scr[0:POOL_HALO, :] = zeros
    pad_scr[POOL_HALO + seq:2 * POOL_HALO + seq, :] = zeros
    pad_scr[POOL_HALO:POOL_HALO + seq, :] = u_ref[...]
    t = lax.broadcasted_iota(I32, (seq, 1), 0)
    for g, w in enumerate(POOL_WINDOWS):
        cols = slice(g * POOL_GROUP, (g + 1) * POOL_GROUP)
        tot = None
        for d in range(-(w // 2), w - w // 2):
            piece = pad_scr[POOL_HALO + d:POOL_HALO + d + seq, cols]
            tot = piece if tot is None else tot + piece
        cnt = (jnp.minimum(t + (w - w // 2), seq) - jnp.maximum(t - w // 2, 0)).astype(F32)
        mixed = tot / cnt - u_ref[:, cols]
        o_ref[:, cols] = (_dot(mixed.astype(BF16), w_ref[g]) * sc_ref[:, cols]).astype(BF16)


def _pool(u, w_pool, pool_scale, seq):
    t = u.shape[0]
    n_groups = len(POOL_WINDOWS)
    return pl.pallas_call(
        functools.partial(_pool_kernel, seq),
        out_shape=jax.ShapeDtypeStruct((t, POOL_WIDTH), BF16),
        grid=(t // seq,),
        in_specs=[pl.BlockSpec((seq, POOL_WIDTH), lambda b: (b, 0)),
                  pl.BlockSpec((n_groups, POOL_GROUP, POOL_GROUP), lambda b: (0, 0, 0)),
                  pl.BlockSpec((1, POOL_WIDTH), lambda b: (0, 0))],
        out_specs=pl.BlockSpec((seq, POOL_WIDTH), lambda b: (b, 0)),
        scratch_shapes=[pltpu.VMEM((seq + 2 * POOL_HALO, POOL_WIDTH), F32)],
        compiler_params=_cparams(("arbitrary",)),
        name="pool",
    )(u, w_pool, pool_scale)


def _merge_kernel(attn_ref, pool_ref, gates_ref, x_ref, mods_ref, bbg_ref, woa_ref, wop_ref, wout_ref,
                  g2_ref, rw_ref, rb_ref, *rest):
    x1_ref, h2_ref, tidx_ref, tw_ref = rest[-4:]
    a = _dot(attn_ref[...], woa_ref[...])
    p = _dot(pool_ref[...], wop_ref[...])
    ga = _sigmoid(gates_ref[:, :D_MODEL] + bbg_ref[:, :D_MODEL])
    gp = _sigmoid(gates_ref[:, D_MODEL:] + bbg_ref[:, D_MODEL:])
    merged = (ga * a + gp * p).astype(BF16)
    x1 = x_ref[...] + mods_ref[2:3, :] * _dot(merged, wout_ref[...])
    x1_ref[...] = x1
    y = x1 * lax.rsqrt(jnp.mean(x1 * x1, axis=-1, keepdims=True) + RMS_EPS) * g2_ref[...]
    h2 = y * (1.0 + mods_ref[4:5, :]) + mods_ref[3:4, :]
    _tok_store(h2_ref, 0, h2.shape[0], _pack_halves(h2))
    logits = _dot(h2.astype(BF16), rw_ref[...]) + rb_ref[...]
    lane = lax.broadcasted_iota(I32, logits.shape, 1).astype(F32)
    vals, idxs = [], []
    for _ in range(TOP_K):
        m = jnp.max(logits, axis=-1, keepdims=True)
        ix = jnp.min(jnp.where(logits == m, lane, float(LANES)), axis=-1, keepdims=True)
        vals.append(m)
        idxs.append(ix)
        logits = jnp.where(lane == ix, -jnp.inf, logits)
    es = [jnp.exp(v - vals[0]) for v in vals]
    tot = es[0] + es[1] + es[2] + es[3]
    tidx = jnp.zeros(logits.shape, F32)
    tw = jnp.zeros(logits.shape, F32)
    for k in range(TOP_K):
        tidx = jnp.where(lane == k, idxs[k], tidx)
        tw = jnp.where(lane == k, es[k] / tot, tw)
    tidx_ref[...] = tidx.astype(I32)
    tw_ref[...] = tw


def _merge(attn_o, pool_o, gates, x, mods, mod_row, bbg, woa, wop, wout, g2, rw, rb, tm, h2_rows, h2_row0, h2_prev):
    t = x.shape[0]
    const = lambda shape: pl.BlockSpec(shape, lambda i: (0, 0), pipeline_mode=pl.Buffered(1))
    h2_blk0 = h2_row0 // tm
    extra_specs = [] if h2_prev is None else [pl.BlockSpec(memory_space=pl.ANY)]
    extra_args = [] if h2_prev is None else [h2_prev]
    return pl.pallas_call(
        _merge_kernel,
        out_shape=(jax.ShapeDtypeStruct((t, D_MODEL), F32),
                   jax.ShapeDtypeStruct((h2_rows * TOK_SUB, LANES), I32),
                   jax.ShapeDtypeStruct((t, LANES), I32),
                   jax.ShapeDtypeStruct((t, LANES), F32)),
        grid=(t // tm,),
        input_output_aliases={} if h2_prev is None else {12: 1},
        in_specs=[pl.BlockSpec((tm, D_MODEL), lambda i: (i, 0)),
                  pl.BlockSpec((tm, POOL_WIDTH), lambda i: (i, 0)),
                  pl.BlockSpec((tm, 2 * D_MODEL), lambda i: (i, 0)),
                  pl.BlockSpec((tm, D_MODEL), lambda i: (i, 0)),
                  pl.BlockSpec((None, 6, D_MODEL), lambda i: (mod_row(i), 0, 0)),
                  const((1, 2 * D_MODEL)),
                  const((D_MODEL, D_MODEL)),
                  const((POOL_WIDTH, D_MODEL)),
                  const((D_MODEL, D_MODEL)),
                  const((1, D_MODEL)),
                  const((D_MODEL, LANES)),
                  const((1, LANES))] + extra_specs,
        out_specs=(pl.BlockSpec((tm, D_MODEL), lambda i: (i, 0)),
                   pl.BlockSpec((tm * TOK_SUB, LANES), lambda i: (i + h2_blk0, 0)),
                   pl.BlockSpec((tm, LANES), lambda i: (i, 0)),
                   pl.BlockSpec((tm, LANES), lambda i: (i, 0))),
        compiler_params=_cparams(("arbitrary",)),
        name="merge",
    )(attn_o, pool_o, gates, x, mods, bbg, woa, wop, wout, g2, rw, rb, *extra_args)


DISPATCH_ROWS = 256
DMA_UNROLL = 8


def _dispatch_kernel(dest_ref, fill_ref, h_ref, xs_ref, zero_scr, sem):
    base = pl.program_id(0) * (DISPATCH_ROWS * TOP_K)

    @pl.when(pl.program_id(0) == 0)
    def _():
        zero_scr[...] = jnp.zeros(zero_scr.shape, I32)

        def fill_copy(slot):
            return pltpu.make_async_copy(zero_scr, xs_ref.at[_tok_rows(slot, 1), :], sem)

        def fill(e, carry):
            def one(r, c):
                fill_copy(fill_ref[e] + r).start()
                return c

            return lax.fori_loop(0, fill_ref[N_EXPERTS + e], one, carry)

        lax.fori_loop(0, N_EXPERTS, fill, 0)

        def drain(e, carry):
            def one(r, c):
                fill_copy(0).wait()
                return c

            return lax.fori_loop(0, fill_ref[N_EXPERTS + e], one, carry)

        lax.fori_loop(0, N_EXPERTS, drain, 0)

    def issue(t, carry):
        for k in range(TOP_K):
            pltpu.make_async_copy(h_ref.at[_tok_rows(t, 1), :],
                                  xs_ref.at[_tok_rows(dest_ref[base + t * TOP_K + k], 1), :], sem).start()
        return carry

    lax.fori_loop(0, DISPATCH_ROWS, issue, 0, unroll=DMA_UNROLL // TOP_K)
    for _ in range(TOP_K):
        pltpu.make_async_copy(h_ref, xs_ref.at[_tok_rows(0, DISPATCH_ROWS), :], sem).wait()


def _dispatch(dest, fill, h2, cap):
    t = h2.shape[0] // TOK_SUB
    return pl.pallas_call(
        _dispatch_kernel,
        out_shape=jax.ShapeDtypeStruct((cap * TOK_SUB, LANES), I32),
        grid_spec=pltpu.PrefetchScalarGridSpec(
            num_scalar_prefetch=2,
            grid=(t // DISPATCH_ROWS,),
            in_specs=[pl.BlockSpec((DISPATCH_ROWS * TOK_SUB, LANES), lambda i, d, z: (i, 0))],
            out_specs=pl.BlockSpec(memory_space=pl.ANY),
            scratch_shapes=[pltpu.VMEM((TOK_SUB, LANES), I32), pltpu.SemaphoreType.DMA]),
        compiler_params=_cparams(("arbitrary",)),
        name="moe_dispatch",
    )(dest, fill, h2)


MOE_SPAN = 8


def _moe_kernel(ie_ref, ir_ref, in_ref, xs_ref, wg_ref, bg_ref, wu_ref, bu_ref, wd_ref, bd_ref, ys_ref,
                xin_scr, acc_scr, wg_scr, wu_scr, wd_scr, sem_x, sem_y):
    i = pl.program_id(0)
    f = pl.program_id(1)
    n_items = pl.num_programs(0)
    nch = in_ref[i]
    row0 = ir_ref[i]
    slot = i % 2

    def rows_of(c, k=1):
        return pl.ds(pl.multiple_of(c * MOE_CHUNK, MOE_CHUNK), k * MOE_CHUNK)

    def x_copy(item_row0, c, s):
        return pltpu.make_async_copy(xs_ref.at[_tok_rows(item_row0 + c * MOE_CHUNK, MOE_CHUNK), :],
                                     xin_scr.at[s, _tok_rows(c * MOE_CHUNK, MOE_CHUNK), :], sem_x.at[s])

    def y_copy(item_row0, c, s):
        return pltpu.make_async_copy(xin_scr.at[s, _tok_rows(c * MOE_CHUNK, MOE_CHUNK), :],
                                     ys_ref.at[_tok_rows(item_row0 + c * MOE_CHUNK, MOE_CHUNK), :], sem_y)

    def for_chunks(n, body):
        def step(c, carry):
            body(c)
            return carry

        lax.fori_loop(0, n, step, 0)

    def for_spans(n, body, first_span):
        def span(g, carry):
            body(g * MOE_SPAN, MOE_SPAN)
            return carry

        lax.fori_loop(first_span, n // MOE_SPAN, span, 0)
        k = MOE_SPAN // 2
        while k >= 1:
            start = n // (2 * k) * (2 * k)

            @pl.when(n % (2 * k) >= k)
            def _(start=start, k=k):
                body(start, k)

            k //= 2

    @pl.when(f == 0)
    def _():
        @pl.when(i > 0)
        def _():
            prev = jnp.maximum(i - 1, 0)
            for_chunks(in_ref[prev], lambda c: y_copy(ir_ref[prev], c, 1 - slot).wait())

        @pl.when(i == 0)
        def _():
            for_chunks(nch, lambda c: x_copy(row0, c, slot).start())

        @pl.when(i + 1 < n_items)
        def _():
            nxt = jnp.minimum(i + 1, n_items - 1)
            for_chunks(in_ref[nxt], lambda c: x_copy(ir_ref[nxt], c, 1 - slot).start())

        for_chunks(nch, lambda c: x_copy(row0, c, slot).wait())

    @pl.when(nch > 0)
    def _():
        def cast_weights():
            w = (wg_ref[...].astype(BF16), wu_ref[...].astype(BF16), wd_ref[...].astype(BF16))
            wg_scr[...], wu_scr[...], wd_scr[...] = w
            return w

        def contribution(c, k, w):
            wg, wu, wd = (wg_scr[...], wu_scr[...], wd_scr[...]) if w is None else w
            lo, hi = _unpack_halves(_tok_load(xin_scr.at[slot], c * MOE_CHUNK, k * MOE_CHUNK))
            x = jnp.concatenate([lo.astype(BF16), hi.astype(BF16)], axis=1)
            gt = jnp.minimum(_dot(x, wg) + bg_ref[...], SWIGLU_LIMIT)
            up = jnp.clip(_dot(x, wu) + bu_ref[...], -SWIGLU_LIMIT, SWIGLU_LIMIT)
            act = (up + 1.0) * (gt * _sigmoid(SWIGLU_ALPHA * gt))
            return _dot(act.astype(BF16), wd)

        def first(c, k, w=None):
            acc_scr[rows_of(c, k), :] = contribution(c, k, w) + bd_ref[...]

        def middle(c, k, w=None):
            acc_scr[rows_of(c, k), :] += contribution(c, k, w)

        def last(c, k, w=None):
            y = acc_scr[rows_of(c, k), :] + contribution(c, k, w)
            _tok_store(xin_scr.at[slot], c * MOE_CHUNK, k * MOE_CHUNK, _pack_halves(y))
            for u in range(k):
                y_copy(row0, c + u, slot).start()

        def run(body):
            @pl.when(nch >= MOE_SPAN)
            def _():
                body(0, MOE_SPAN, cast_weights())

            @pl.when(nch < MOE_SPAN)
            def _():
                cast_weights()

            for_spans(nch, body, 1)

        @pl.when(f == 0)
        def _():
            run(first)

        @pl.when(jnp.logical_and(f > 0, f < N_FF_TILES - 1))
        def _():
            run(middle)

        @pl.when(f == N_FF_TILES - 1)
        def _():
            run(last)

            @pl.when(i == n_items - 1)
            def _():
                for_chunks(nch, lambda c: y_copy(row0, c, slot).wait())


def _moe_experts(item_e, item_row0, item_nch, xs, w_g, b_g, w_u, b_u, w_d, b_d):
    n_items = item_e.shape[0]

    def ff_idx(i, f, ie, ir, inch):
        return jnp.where(inch[i] > 0, f, N_FF_TILES - 1)

    return pl.pallas_call(
        _moe_kernel,
        out_shape=jax.ShapeDtypeStruct(xs.shape, I32),
        grid_spec=pltpu.PrefetchScalarGridSpec(
            num_scalar_prefetch=3,
            grid=(n_items, N_FF_TILES),
            in_specs=[
                pl.BlockSpec(memory_space=pl.ANY),
                pl.BlockSpec((None, D_MODEL, FF_TILE), lambda i, f, ie, ir, inch: (ie[i], 0, ff_idx(i, f, ie, ir, inch))),
                pl.BlockSpec((None, 1, FF_TILE), lambda i, f, ie, ir, inch: (ie[i], 0, ff_idx(i, f, ie, ir, inch))),
                pl.BlockSpec((None, D_MODEL, FF_TILE), lambda i, f, ie, ir, inch: (ie[i], 0, ff_idx(i, f, ie, ir, inch))),
                pl.BlockSpec((None, 1, FF_TILE), lambda i, f, ie, ir, inch: (ie[i], 0, ff_idx(i, f, ie, ir, inch))),
                pl.BlockSpec((None, FF_TILE, D_MODEL), lambda i, f, ie, ir, inch: (ie[i], ff_idx(i, f, ie, ir, inch), 0)),
                pl.BlockSpec((None, 1, D_MODEL), lambda i, f, ie, ir, inch: (ie[i], 0, 0)),
            ],
            out_specs=pl.BlockSpec(memory_space=pl.ANY),
            scratch_shapes=[
                pltpu.VMEM((2, MOE_ITEM_ROWS * TOK_SUB, LANES), I32),
                pltpu.VMEM((MOE_ITEM_ROWS, D_MODEL), F32),
                pltpu.VMEM((D_MODEL, FF_TILE), BF16),
                pltpu.VMEM((D_MODEL, FF_TILE), BF16),
                pltpu.VMEM((FF_TILE, D_MODEL), BF16),
                pltpu.SemaphoreType.DMA((2,)),
                pltpu.SemaphoreType.DMA,
            ]),
        compiler_params=_cparams(("arbitrary", "arbitrary")),
        name="moe_experts",
    )(item_e, item_row0, item_nch, xs, w_g, b_g, w_u, b_u, w_d, b_d)


COMBINE_ROWS = 128


def _combine_kernel(dest_ref, ys_ref, x1_ref, tw_ref, mods_ref, o_ref, ybuf, sem):
    base = pl.program_id(0) * (COMBINE_ROWS * TOP_K)

    def row_copy(src_row, t, k):
        return pltpu.make_async_copy(ys_ref.at[_tok_rows(src_row, 1), :], ybuf.at[k, _tok_rows(t, 1), :], sem)

    def issue(t, carry):
        for k in range(TOP_K):
            row_copy(dest_ref[base + t * TOP_K + k], t, k).start()
        return carry

    lax.fori_loop(0, COMBINE_ROWS, issue, 0, unroll=DMA_UNROLL // TOP_K)
    for k in range(TOP_K):
        pltpu.make_async_copy(ys_ref.at[_tok_rows(0, COMBINE_ROWS), :], ybuf.at[k], sem).wait()

    y_lo = y_hi = None
    for k in range(TOP_K):
        lo, hi = _unpack_halves(_tok_load(ybuf.at[k], 0, COMBINE_ROWS))
        w = tw_ref[:, k:k + 1]
        y_lo = w * lo if y_lo is None else y_lo + w * lo
        y_hi = w * hi if y_hi is None else y_hi + w * hi
    o_ref[:, :HALF] = x1_ref[:, :HALF] + mods_ref[5:6, :HALF] * y_lo
    o_ref[:, HALF:] = x1_ref[:, HALF:] + mods_ref[5:6, HALF:] * y_hi


def _combine(dest, ys, x1, tw, mods, mod_row):
    t = x1.shape[0]
    return pl.pallas_call(
        _combine_kernel,
        out_shape=jax.ShapeDtypeStruct((t, D_MODEL), F32),
        grid_spec=pltpu.PrefetchScalarGridSpec(
            num_scalar_prefetch=1,
            grid=(t // COMBINE_ROWS,),
            in_specs=[pl.BlockSpec(memory_space=pl.ANY),
                      pl.BlockSpec((COMBINE_ROWS, D_MODEL), lambda i, d: (i, 0)),
                      pl.BlockSpec((COMBINE_ROWS, LANES), lambda i, d: (i, 0)),
                      pl.BlockSpec((None, 6, D_MODEL), lambda i, d: (mod_row(i * COMBINE_ROWS), 0, 0))],
            out_specs=pl.BlockSpec((COMBINE_ROWS, D_MODEL), lambda i, d: (i, 0)),
            scratch_shapes=[pltpu.VMEM((TOP_K, COMBINE_ROWS * TOK_SUB, LANES), I32), pltpu.SemaphoreType.DMA]),
        compiler_params=_cparams(("arbitrary",)),
        name="moe_combine",
    )(dest, ys, x1, tw, mods)


def _routing_tables(top_idx, n_items):
    flat_e = top_idx.reshape(-1)
    onehot = (flat_e[:, None] == jnp.arange(N_EXPERTS, dtype=I32)[None, :]).astype(I32)
    csum = jnp.cumsum(onehot, axis=0)
    rank = jnp.take_along_axis(csum, flat_e[:, None], axis=1)[:, 0] - 1
    counts = csum[-1]
    nch = (counts + MOE_CHUNK - 1) // MOE_CHUNK
    pad_rows = nch * MOE_CHUNK
    pad_end = jnp.cumsum(pad_rows)
    pad_start = pad_end - pad_rows
    dest = (pad_start[flat_e] + rank).astype(I32)
    items_e = (nch + MOE_ITEM_CHUNKS - 1) // MOE_ITEM_CHUNKS
    item_end = jnp.cumsum(items_e)
    item_start = item_end - items_e
    total = item_end[-1]
    i = jnp.arange(n_items, dtype=I32)
    ii = jnp.minimum(i, total - 1)
    e_i = jnp.minimum(jnp.searchsorted(item_end, ii, side="right"), N_EXPERTS - 1).astype(I32)
    local = ii - item_start[e_i]
    row0 = (pad_start[e_i] + local * MOE_ITEM_ROWS).astype(I32)
    n_i = jnp.where(i < total, jnp.minimum(MOE_ITEM_CHUNKS, nch[e_i] - local * MOE_ITEM_CHUNKS), 0).astype(I32)
    fill = jnp.concatenate([pad_start + counts, pad_rows - counts]).astype(I32)
    return dest, fill, e_i, row0, n_i


def _rope_tables(n_lat):
    nf = ROPE // 4
    inv = ROPE_BASE ** (-jnp.arange(nf, dtype=F32) / nf)
    t = jnp.arange(n_lat)
    row = (t // GRID_W).astype(F32)
    col = (t % GRID_W).astype(F32)
    ang_r = row[:, None] * inv[None, :]
    ang_c = col[:, None] * inv[None, :]
    z = jnp.zeros((n_lat, nf), F32)
    tail = jnp.zeros((n_lat, LANES - ROPE), F32)
    cos = jnp.concatenate([jnp.cos(ang_r), jnp.cos(ang_r), jnp.cos(ang_c), jnp.cos(ang_c), tail], axis=1)
    sin_lo = jnp.concatenate([-jnp.sin(ang_r), z, -jnp.sin(ang_c), z, tail], axis=1)
    sin_hi = jnp.concatenate([z, jnp.sin(ang_r), z, jnp.sin(ang_c), tail], axis=1)
    return cos, sin_lo, sin_hi


def kernel(x_prompt, x_sample, cache_ckv, cache_krope, c, c_ctx, norm1_g, norm2_g, w_ada, b_ada, w_in,
           b_branch_gate, kv_norm_g, w_ukv, q_norm_g, k_norm_g, w_o_attn, w_pool, pool_scale, w_o_pool, w_out,
           router_w, router_b, w_exp_gate, b_exp_gate, w_exp_up, b_exp_up, w_exp_down, b_exp_down):
    assert w_in.shape[0] == 1, "single-layer trunk"
    batch, seq, _ = x_prompt.shape
    dec_batch, n_lat, _ = x_sample.shape
    past = cache_ckv.shape[2]
    n_ctx = batch * seq
    n_dec = dec_batch * n_lat

    cond = jnp.concatenate([c_ctx[None, :], c, jnp.zeros((8 - 1 - dec_batch, D_MODEL), F32)], axis=0)
    mods = _adaln(cond, w_ada[0], b_ada).reshape(8, 6, D_MODEL)

    w = w_in[0]
    i0 = N_HEADS * QK
    i1 = i0 + KV_RANK
    i2 = i1 + ROPE
    i3 = i2 + POOL_WIDTH
    wq = jnp.pad(w[:, :i0].reshape(D_MODEL, N_HEADS, QK), ((0, 0), (0, 0), (0, HEAD_PAD - QK)))
    w_cat = jnp.concatenate([wq.reshape(D_MODEL, Q_COLS), w[:, i2:i3], w[:, i3:], w[:, i0:i1], w[:, i1:i2],
                             jnp.zeros((D_MODEL, IN_TILE - KV_RANK - ROPE), F32)], axis=1).astype(BF16)
    qg = jnp.pad(q_norm_g, ((0, 0), (0, HEAD_PAD - QK)))
    kgn = k_norm_g[:, :NOPE]
    kgr = jnp.pad(k_norm_g[:, NOPE:], ((0, 0), (0, LANES - ROPE)))
    w_ukv_b = w_ukv[0].astype(BF16)
    rope_tabs = _rope_tables(n_lat)

    ctx_row = lambda i: 0
    tm_in = 512
    lat_row_in = lambda i: 1 + (i * tm_in) // n_lat
    q_c, pool_c, gates_c, ckv_c, kr_c, krp_c = _in_proj(
        x_prompt.reshape(n_ctx, D_MODEL), mods, ctx_row, norm1_g, w_cat, qg, kv_norm_g, None, tm_in)
    q_l, pool_l, gates_l, ckv_l, _, krp_l = _in_proj(
        x_sample.reshape(n_dec, D_MODEL), mods, lat_row_in, norm1_g, w_cat, qg, kv_norm_g, rope_tabs, tm_in)

    tr = 256
    k_c, v_c = _kv_expand(ckv_c, krp_c, w_ukv_b, kgn, kgr, None, tr)
    k_l, v_l = _kv_expand(ckv_l, krp_l, w_ukv_b, kgn, kgr, rope_tabs, tr)
    cache_krp = jnp.pad(cache_krope.reshape(dec_batch * past, ROPE), ((0, 0), (0, LANES - ROPE)))
    k_p, v_p = _kv_expand(cache_ckv.reshape(dec_batch * past, KV_RANK), cache_krp, w_ukv_b, kgn, kgr, None, tr)

    attn_c = _attn_ctx(q_c, k_c, v_c, seq)
    attn_l = _attn_lat(q_l, k_l, v_l, k_p, v_p, n_lat, past, 512)

    w_pool_b = w_pool[0].astype(BF16)
    poolo_c = _pool(pool_c, w_pool_b, pool_scale, seq)
    poolo_l = _pool(pool_l, w_pool_b, pool_scale, n_lat)

    woa = w_o_attn[0].astype(BF16)
    wop = w_o_pool[0].astype(BF16)
    wout = w_out[0].astype(BF16)
    rw = jnp.pad(router_w[0], ((0, 0), (0, LANES - N_EXPERTS))).astype(BF16)
    rb = jnp.pad(router_b, ((0, 0), (0, LANES - N_EXPERTS)), constant_values=NEG_BIG)
    tm_mg = 256
    lat_row_mg = lambda i: 1 + (i * tm_mg) // n_lat
    x1_c, h2, tidx_c, tw_c = _merge(attn_c, poolo_c, gates_c, x_prompt.reshape(n_ctx, D_MODEL), mods, ctx_row,
                                    b_branch_gate, woa, wop, wout, norm2_g, rw, rb, tm_mg, n_ctx + n_dec, 0, None)
    x1_l, h2, tidx_l, tw_l = _merge(attn_l, poolo_l, gates_l, x_sample.reshape(n_dec, D_MODEL), mods, lat_row_mg,
                                    b_branch_gate, woa, wop, wout, norm2_g, rw, rb, tm_mg, n_ctx + n_dec, n_ctx, h2)

    n_assign = (n_ctx + n_dec) * TOP_K
    max_chunks = (n_assign + N_EXPERTS * (MOE_CHUNK - 1)) // MOE_CHUNK
    cap = max_chunks * MOE_CHUNK
    n_items = (max_chunks + N_EXPERTS * (MOE_ITEM_CHUNKS - 1)) // MOE_ITEM_CHUNKS
    top_idx = jnp.concatenate([tidx_c[:, :TOP_K], tidx_l[:, :TOP_K]], axis=0)
    dest, fill, item_e, item_row0, item_nch = _routing_tables(top_idx, n_items)

    xs = _dispatch(dest, fill, h2, cap)
    ys = _moe_experts(item_e, item_row0, item_nch, xs,
                      w_exp_gate[0], b_exp_gate[0][:, None, :], w_exp_up[0], b_exp_up[0][:, None, :],
                      w_exp_down[0], b_exp_down[0][:, None, :])

    y_c = _combine(dest[:n_ctx * TOP_K], ys, x1_c, tw_c, mods, lambda r: 0)
    y_l = _combine(dest[n_ctx * TOP_K:], ys, x1_l, tw_l, mods, lambda r: 1 + r // n_lat)

    return (y_c.reshape(batch, seq, D_MODEL),
            y_l.reshape(dec_batch, n_lat, D_MODEL),
            ckv_c.reshape(batch, 1, seq, KV_RANK),
            kr_c.reshape(batch, 1, seq, ROPE))
```

```python
import functools

import jax
import jax.numpy as jnp
from jax import lax
from jax.experimental import pallas as pl
from jax.experimental.pallas import tpu as pltpu

F32 = jnp.float32
BF16 = jnp.bfloat16
I32 = jnp.int32

D_MODEL = 2048
N_HEADS = 16
NOPE = 128
ROPE = 64
QK = NOPE + ROPE
V_DIM = 128
KV_RANK = 512
POOL_WINDOWS = (2, 4, 8, 16)
POOL_WIDTH = 1024
POOL_GROUP = POOL_WIDTH // len(POOL_WINDOWS)
N_EXPERTS = 32
TOP_K = 4
D_FF = 2048
SWIGLU_LIMIT = 7.0
SWIGLU_ALPHA = 1.702
ROPE_BASE = 10000.0
RMS_EPS = 1e-6
GRID_W = 64
ATTN_SCALE = QK ** -0.5

LANES = 128
HEAD_PAD = 2 * LANES
Q_COLS = N_HEADS * HEAD_PAD
IN_TILE = 1024
N_Q_TILES = Q_COLS // IN_TILE
POOL_TILE = N_Q_TILES
GATE_TILE0 = POOL_TILE + 1
N_GATE_TILES = 2 * D_MODEL // IN_TILE
KV_TILE = GATE_TILE0 + N_GATE_TILES
N_IN_TILES = KV_TILE + 1
HEADS_PER_TILE = IN_TILE // HEAD_PAD

MOE_CHUNK = 128
MOE_ITEM_CHUNKS = 16
MOE_ITEM_ROWS = MOE_CHUNK * MOE_ITEM_CHUNKS
FF_TILE = 256
N_FF_TILES = D_FF // FF_TILE
NEG_BIG = -1e30

VMEM_LIMIT = 56 * 1024 * 1024


def _cparams(sem, vmem=VMEM_LIMIT):
    return pltpu.CompilerParams(dimension_semantics=sem, vmem_limit_bytes=vmem)


def _dot(a, b):
    return jnp.dot(a, b, preferred_element_type=F32)


def _sigmoid(z):
    return 1.0 / (1.0 + jnp.exp(-z))


HALF = D_MODEL // 2
HI_MASK = -65536


def _pack_halves(v):
    lo = lax.bitcast_convert_type(v[:, :HALF].astype(BF16).astype(F32), I32)
    hi = lax.bitcast_convert_type(v[:, HALF:].astype(BF16).astype(F32), I32)
    return jnp.bitwise_or(jnp.bitwise_and(hi, HI_MASK), lax.shift_right_logical(lo, 16))


TOK_SUB = HALF // LANES


def _tok_rows(tok0, n):
    row0 = tok0 * TOK_SUB
    return pl.ds(row0 if isinstance(row0, int) else pl.multiple_of(row0, TOK_SUB), n * TOK_SUB)


def _tok_store(ref, tok0, n, words):
    for s in range(TOK_SUB):
        ref[pl.ds(tok0 * TOK_SUB + s, n, stride=TOK_SUB), :] = words[:, s * LANES:(s + 1) * LANES]


def _tok_load(ref, tok0, n):
    return jnp.concatenate([ref[pl.ds(tok0 * TOK_SUB + s, n, stride=TOK_SUB), :] for s in range(TOK_SUB)], axis=1)


def _unpack_halves(w):
    lo = lax.bitcast_convert_type(lax.shift_left(w, 16), F32)
    hi = lax.bitcast_convert_type(jnp.bitwise_and(w, HI_MASK), F32)
    return lo, hi


def _rope_lanes(y, cos, sin_lo, sin_hi):
    return y * cos + pltpu.roll(y, LANES - 16, axis=1) * sin_lo + pltpu.roll(y, 16, axis=1) * sin_hi


def _adaln_kernel(c_ref, w_ref, b_ref, o_ref):
    c = c_ref[...]
    s = (c * _sigmoid(c)).astype(BF16)
    o_ref[...] = _dot(s, w_ref[...].astype(BF16)) + b_ref[...]


def _adaln(cond, w_ada, b_ada):
    rows = cond.shape[0]
    n = w_ada.shape[1]
    tn = 1024
    return pl.pallas_call(
        _adaln_kernel,
        out_shape=jax.ShapeDtypeStruct((rows, n), F32),
        grid=(n // tn,),
        in_specs=[pl.BlockSpec((rows, D_MODEL), lambda j: (0, 0)),
                  pl.BlockSpec((D_MODEL, tn), lambda j: (0, j)),
                  pl.BlockSpec((1, tn), lambda j: (0, j))],
        out_specs=pl.BlockSpec((rows, tn), lambda j: (0, j)),
        compiler_params=_cparams(("arbitrary",)),
        name="adaln",
    )(cond, w_ada, b_ada)


def _in_proj_kernel(rope, x_ref, mods_ref, g1_ref, w_ref, qg_ref, kvg_ref, *rest):
    if rope:
        cos_ref, slo_ref, shi_ref = rest[:3]
        rest = rest[3:]
    q_ref, pool_ref, gates_ref, ckv_ref, kr_ref, krp_ref, h_scr = rest
    j = pl.program_id(1)

    @pl.when(j == 0)
    def _():
        x = x_ref[...]
        y = x * lax.rsqrt(jnp.mean(x * x, axis=-1, keepdims=True) + RMS_EPS) * g1_ref[...]
        h = y * (1.0 + mods_ref[1:2, :]) + mods_ref[0:1, :]
        h_scr[...] = h.astype(BF16)

    def proj(c0, c1):
        return _dot(h_scr[...], w_ref[:, c0:c1])

    @pl.when(j < N_Q_TILES)
    def _():
        for hh in range(HEADS_PER_TILE):
            a = proj(hh * HEAD_PAD, (hh + 1) * HEAD_PAD)
            r = lax.rsqrt(jnp.sum(a * a, axis=-1, keepdims=True) / QK + RMS_EPS)
            y = a * r * qg_ref[...]
            if rope:
                yr = _rope_lanes(y[:, LANES:], cos_ref[...], slo_ref[...], shi_ref[...])
                q_ref[:, hh * HEAD_PAD:hh * HEAD_PAD + LANES] = y[:, :LANES].astype(BF16)
                q_ref[:, hh * HEAD_PAD + LANES:(hh + 1) * HEAD_PAD] = yr.astype(BF16)
            else:
                q_ref[:, hh * HEAD_PAD:(hh + 1) * HEAD_PAD] = y.astype(BF16)

    @pl.when(j == POOL_TILE)
    def _():
        pool_ref[...] = proj(0, IN_TILE)

    @pl.when(jnp.logical_and(j >= GATE_TILE0, j < KV_TILE))
    def _():
        gates_ref[...] = proj(0, IN_TILE)

    @pl.when(j == KV_TILE)
    def _():
        a = proj(0, KV_RANK)
        r = lax.rsqrt(jnp.mean(a * a, axis=-1, keepdims=True) + RMS_EPS)
        ckv_ref[...] = a * r * kvg_ref[...]
        krp = proj(KV_RANK, KV_RANK + LANES)
        krp_ref[...] = krp
        kr_ref[...] = krp[:, :ROPE]


def _in_proj(x, mods, mod_row, g1, w_cat, qg, kvg, rope_tabs, tm):
    t = x.shape[0]
    rope = rope_tabs is not None
    in_specs = [
        pl.BlockSpec((tm, D_MODEL), lambda i, j: (i, 0)),
        pl.BlockSpec((None, 6, D_MODEL), lambda i, j: (mod_row(i), 0, 0)),
        pl.BlockSpec((1, D_MODEL), lambda i, j: (0, 0)),
        pl.BlockSpec((D_MODEL, IN_TILE), lambda i, j: (0, j)),
        pl.BlockSpec((1, HEAD_PAD), lambda i, j: (0, 0)),
        pl.BlockSpec((1, KV_RANK), lambda i, j: (0, 0)),
    ]
    args = [x, mods, g1, w_cat, qg, kvg]
    if rope:
        seq_tiles = rope_tabs[0].shape[0] // tm
        in_specs += [pl.BlockSpec((tm, LANES), lambda i, j: (i % seq_tiles, 0))] * 3
        args += list(rope_tabs)
    out_shape = (
        jax.ShapeDtypeStruct((t, Q_COLS), BF16),
        jax.ShapeDtypeStruct((t, POOL_WIDTH), F32),
        jax.ShapeDtypeStruct((t, 2 * D_MODEL), F32),
        jax.ShapeDtypeStruct((t, KV_RANK), F32),
        jax.ShapeDtypeStruct((t, ROPE), F32),
        jax.ShapeDtypeStruct((t, LANES), F32),
    )
    out_specs = (
        pl.BlockSpec((tm, IN_TILE), lambda i, j: (i, jnp.minimum(j, N_Q_TILES - 1))),
        pl.BlockSpec((tm, POOL_WIDTH), lambda i, j: (i, 0)),
        pl.BlockSpec((tm, IN_TILE), lambda i, j: (i, jnp.clip(j - GATE_TILE0, 0, N_GATE_TILES - 1))),
        pl.BlockSpec((tm, KV_RANK), lambda i, j: (i, 0)),
        pl.BlockSpec((tm, ROPE), lambda i, j: (i, 0)),
        pl.BlockSpec((tm, LANES), lambda i, j: (i, 0)),
    )
    return pl.pallas_call(
        functools.partial(_in_proj_kernel, rope),
        out_shape=out_shape,
        grid=(t // tm, N_IN_TILES),
        in_specs=in_specs,
        out_specs=out_specs,
        scratch_shapes=[pltpu.VMEM((tm, D_MODEL), BF16)],
        compiler_params=_cparams(("arbitrary", "arbitrary")),
        name="in_proj_rope" if rope else "in_proj",
    )(*args)


def _kv_expand_kernel(rope, ckv_ref, krp_ref, w_ref, kgn_ref, kgr_ref, *rest):
    if rope:
        cos_ref, slo_ref, shi_ref = rest[:3]
        rest = rest[3:]
    k_ref, v_ref = rest
    kv = _dot(ckv_ref[...].astype(BF16), w_ref[...])
    kr = krp_ref[...]
    ssq_r = jnp.sum(kr * kr, axis=-1, keepdims=True)
    krg = kr * kgr_ref[...]
    if rope:
        krg = _rope_lanes(krg, cos_ref[...], slo_ref[...], shi_ref[...])
    for h in range(N_HEADS):
        kn = kv[:, h * HEAD_PAD:h * HEAD_PAD + NOPE]
        r = lax.rsqrt((jnp.sum(kn * kn, axis=-1, keepdims=True) + ssq_r) / QK + RMS_EPS)
        k_ref[:, h * HEAD_PAD:h * HEAD_PAD + NOPE] = (kn * r * kgn_ref[...]).astype(BF16)
        k_ref[:, h * HEAD_PAD + NOPE:(h + 1) * HEAD_PAD] = (krg * r).astype(BF16)
        v_ref[:, h * V_DIM:(h + 1) * V_DIM] = kv[:, h * HEAD_PAD + NOPE:(h + 1) * HEAD_PAD].astype(BF16)


def _kv_expand(ckv_n, krp, w_ukv, kgn, kgr, rope_tabs, tr):
    r = ckv_n.shape[0]
    rope = rope_tabs is not None
    in_specs = [
        pl.BlockSpec((tr, KV_RANK), lambda i: (i, 0)),
        pl.BlockSpec((tr, LANES), lambda i: (i, 0)),
        pl.BlockSpec((KV_RANK, N_HEADS * HEAD_PAD), lambda i: (0, 0)),
        pl.BlockSpec((1, LANES), lambda i: (0, 0)),
        pl.BlockSpec((1, LANES), lambda i: (0, 0)),
    ]
    args = [ckv_n, krp, w_ukv, kgn, kgr]
    if rope:
        seq_tiles = rope_tabs[0].shape[0] // tr
        in_specs += [pl.BlockSpec((tr, LANES), lambda i: (i % seq_tiles, 0))] * 3
        args += list(rope_tabs)
    return pl.pallas_call(
        functools.partial(_kv_expand_kernel, rope),
        out_shape=(jax.ShapeDtypeStruct((r, N_HEADS * HEAD_PAD), BF16),
                   jax.ShapeDtypeStruct((r, N_HEADS * V_DIM), BF16)),
        grid=(r // tr,),
        in_specs=in_specs,
        out_specs=(pl.BlockSpec((tr, N_HEADS * HEAD_PAD), lambda i: (i, 0)),
                   pl.BlockSpec((tr, N_HEADS * V_DIM), lambda i: (i, 0))),
        compiler_params=_cparams(("arbitrary",)),
        name="kv_expand_rope" if rope else "kv_expand",
    )(*args)


def _qk(q, k):
    return lax.dot_general(q, k, (((1,), (1,)), ((), ())), preferred_element_type=F32)


def _attn_ctx_kernel(q_ref, k_ref, v_ref, o_ref):
    for h in range(N_HEADS):
        s = _qk(q_ref[:, h * HEAD_PAD:(h + 1) * HEAD_PAD], k_ref[:, h * HEAD_PAD:(h + 1) * HEAD_PAD])
        p = jnp.exp(s - jnp.max(s, axis=-1, keepdims=True))
        l = jnp.sum(p, axis=-1, keepdims=True)
        o = _dot(p.astype(BF16), v_ref[:, h * V_DIM:(h + 1) * V_DIM])
        o_ref[:, h * V_DIM:(h + 1) * V_DIM] = (o / l).astype(BF16)


def _attn_ctx(q, k, v, seq):
    t = q.shape[0]
    return pl.pallas_call(
        _attn_ctx_kernel,
        out_shape=jax.ShapeDtypeStruct((t, N_HEADS * V_DIM), BF16),
        grid=(t // seq,),
        in_specs=[pl.BlockSpec((seq, Q_COLS), lambda b: (b, 0)),
                  pl.BlockSpec((seq, Q_COLS), lambda b: (b, 0)),
                  pl.BlockSpec((seq, N_HEADS * V_DIM), lambda b: (b, 0))],
        out_specs=pl.BlockSpec((seq, N_HEADS * V_DIM), lambda b: (b, 0)),
        compiler_params=_cparams(("arbitrary",)),
        name="attn_ctx",
    )(q, k, v)


ATTN_SUB = 128


def _attn_lat_kernel(q_ref, k_ref, v_ref, kc_ref, vc_ref, o_ref):
    for j in range(q_ref.shape[0] // ATTN_SUB):
        rows = slice(j * ATTN_SUB, (j + 1) * ATTN_SUB)
        q = q_ref[rows, :]
        s1 = _qk(q, k_ref[...])
        s2 = _qk(q, kc_ref[...])
        m = jnp.maximum(jnp.max(s1, axis=-1, keepdims=True), jnp.max(s2, axis=-1, keepdims=True))
        p1 = jnp.exp(s1 - m)
        p2 = jnp.exp(s2 - m)
        l = jnp.sum(p1, axis=-1, keepdims=True) + jnp.sum(p2, axis=-1, keepdims=True)
        o = _dot(p1.astype(BF16), v_ref[...]) + _dot(p2.astype(BF16), vc_ref[...])
        o_ref[rows, :] = (o / l).astype(BF16)


def _attn_lat(q, k, v, kc, vc, seq, past, tq):
    t = q.shape[0]
    nq = seq // tq
    return pl.pallas_call(
        _attn_lat_kernel,
        out_shape=jax.ShapeDtypeStruct((t, N_HEADS * V_DIM), BF16),
        grid=(t // seq, N_HEADS, nq),
        in_specs=[pl.BlockSpec((tq, HEAD_PAD), lambda b, h, i: (b * nq + i, h)),
                  pl.BlockSpec((seq, HEAD_PAD), lambda b, h, i: (b, h)),
                  pl.BlockSpec((seq, V_DIM), lambda b, h, i: (b, h)),
                  pl.BlockSpec((past, HEAD_PAD), lambda b, h, i: (b, h)),
                  pl.BlockSpec((past, V_DIM), lambda b, h, i: (b, h))],
        out_specs=pl.BlockSpec((tq, V_DIM), lambda b, h, i: (b * nq + i, h)),
        compiler_params=_cparams(("arbitrary", "arbitrary", "arbitrary")),
        name="attn_lat",
    )(q, k, v, kc, vc)


POOL_HALO = 8


def _pool_kernel(seq, u_ref, w_ref, sc_ref, o_ref, pad_scr):
    zeros = jnp.zeros((POOL_HALO, POOL_WIDTH), F32)
    pad_scr[0:POOL_HALO, :] = zeros
    pad_scr[POOL_HALO + seq:2 * POOL_HALO + seq, :] = zeros
    pad_scr[POOL_HALO:POOL_HALO + seq, :] = u_ref[...]
    t = lax.broadcasted_iota(I32, (seq, 1), 0)
    for g, w in enumerate(POOL_WINDOWS):
        cols = slice(g * POOL_GROUP, (g + 1) * POOL_GROUP)
        tot = None
        for d in range(-(w // 2), w - w // 2):
            piece = pad_scr[POOL_HALO + d:POOL_HALO + d + seq, cols]
            tot = piece if tot is None else tot + piece
        cnt = (jnp.minimum(t + (w - w // 2), seq) - jnp.maximum(t - w // 2, 0)).astype(F32)
        mixed = tot / cnt - u_ref[:, cols]
        o_ref[:, cols] = (_dot(mixed.astype(BF16), w_ref[g]) * sc_ref[:, cols]).astype(BF16)


def _pool(u, w_pool, pool_scale, seq):
    t = u.shape[0]
    n_groups = len(POOL_WINDOWS)
    return pl.pallas_call(
        functools.partial(_pool_kernel, seq),
        out_shape=jax.ShapeDtypeStruct((t, POOL_WIDTH), BF16),
        grid=(t // seq,),
        in_specs=[pl.BlockSpec((seq, POOL_WIDTH), lambda b: (b, 0)),
                  pl.BlockSpec((n_groups, POOL_GROUP, POOL_GROUP), lambda b: (0, 0, 0)),
                  pl.BlockSpec((1, POOL_WIDTH), lambda b: (0, 0))],
        out_specs=pl.BlockSpec((seq, POOL_WIDTH), lambda b: (b, 0)),
        scratch_shapes=[pltpu.VMEM((seq + 2 * POOL_HALO, POOL_WIDTH), F32)],
        compiler_params=_cparams(("arbitrary",)),
        name="pool",
    )(u, w_pool, pool_scale)


def _merge_kernel(attn_ref, pool_ref, gates_ref, x_ref, mods_ref, bbg_ref, woa_ref, wop_ref, wout_ref,
                  g2_ref, rw_ref, rb_ref, *rest):
    x1_ref, h2_ref, tidx_ref, tw_ref = rest[-4:]
    a = _dot(attn_ref[...], woa_ref[...])
    p = _dot(pool_ref[...], wop_ref[...])
    ga = _sigmoid(gates_ref[:, :D_MODEL] + bbg_ref[:, :D_MODEL])
    gp = _sigmoid(gates_ref[:, D_MODEL:] + bbg_ref[:, D_MODEL:])
    merged = (ga * a + gp * p).astype(BF16)
    x1 = x_ref[...] + mods_ref[2:3, :] * _dot(merged, wout_ref[...])
    x1_ref[...] = x1
    y = x1 * lax.rsqrt(jnp.mean(x1 * x1, axis=-1, keepdims=True) + RMS_EPS) * g2_ref[...]
    h2 = y * (1.0 + mods_ref[4:5, :]) + mods_ref[3:4, :]
    _tok_store(h2_ref, 0, h2.shape[0], _pack_halves(h2))
    logits = _dot(h2.astype(BF16), rw_ref[...]) + rb_ref[...]
    lane = lax.broadcasted_iota(I32, logits.shape, 1).astype(F32)
    vals, idxs = [], []
    for _ in range(TOP_K):
        m = jnp.max(logits, axis=-1, keepdims=True)
        ix = jnp.min(jnp.where(logits == m, lane, float(LANES)), axis=-1, keepdims=True)
        vals.append(m)
        idxs.append(ix)
        logits = jnp.where(lane == ix, -jnp.inf, logits)
    es = [jnp.exp(v - vals[0]) for v in vals]
    tot = es[0] + es[1] + es[2] + es[3]
    tidx = jnp.zeros(logits.shape, F32)
    tw = jnp.zeros(logits.shape, F32)
    for k in range(TOP_K):
        tidx = jnp.where(lane == k, idxs[k], tidx)
        tw = jnp.where(lane == k, es[k] / tot, tw)
    tidx_ref[...] = tidx.astype(I32)
    tw_ref[...] = tw


def _merge(attn_o, pool_o, gates, x, mods, mod_row, bbg, woa, wop, wout, g2, rw, rb, tm, h2_rows, h2_row0, h2_prev):
    t = x.shape[0]
    const = lambda shape: pl.BlockSpec(shape, lambda i: (0, 0), pipeline_mode=pl.Buffered(1))
    h2_blk0 = h2_row0 // tm
    extra_specs = [] if h2_prev is None else [pl.BlockSpec(memory_space=pl.ANY)]
    extra_args = [] if h2_prev is None else [h2_prev]
    return pl.pallas_call(
        _merge_kernel,
        out_shape=(jax.ShapeDtypeStruct((t, D_MODEL), F32),
                   jax.ShapeDtypeStruct((h2_rows * TOK_SUB, LANES), I32),
                   jax.ShapeDtypeStruct((t, LANES), I32),
                   jax.ShapeDtypeStruct((t, LANES), F32)),
        grid=(t // tm,),
        input_output_aliases={} if h2_prev is None else {12: 1},
        in_specs=[pl.BlockSpec((tm, D_MODEL), lambda i: (i, 0)),
                  pl.BlockSpec((tm, POOL_WIDTH), lambda i: (i, 0)),
                  pl.BlockSpec((tm, 2 * D_MODEL), lambda i: (i, 0)),
                  pl.BlockSpec((tm, D_MODEL), lambda i: (i, 0)),
                  pl.BlockSpec((None, 6, D_MODEL), lambda i: (mod_row(i), 0, 0)),
                  const((1, 2 * D_MODEL)),
                  const((D_MODEL, D_MODEL)),
                  const((POOL_WIDTH, D_MODEL)),
                  const((D_MODEL, D_MODEL)),
                  const((1, D_MODEL)),
                  const((D_MODEL, LANES)),
                  const((1, LANES))] + extra_specs,
        out_specs=(pl.BlockSpec((tm, D_MODEL), lambda i: (i, 0)),
                   pl.BlockSpec((tm * TOK_SUB, LANES), lambda i: (i + h2_blk0, 0)),
                   pl.BlockSpec((tm, LANES), lambda i: (i, 0)),
                   pl.BlockSpec((tm, LANES), lambda i: (i, 0))),
        compiler_params=_cparams(("arbitrary",)),
        name="merge",
    )(attn_o, pool_o, gates, x, mods, bbg, woa, wop, wout, g2, rw, rb, *extra_args)


DISPATCH_ROWS = 256
DMA_UNROLL = 8


def _dispatch_kernel(dest_ref, fill_ref, h_ref, xs_ref, zero_scr, sem):
    base = pl.program_id(0) * (DISPATCH_ROWS * TOP_K)

    @pl.when(pl.program_id(0) == 0)
    def _():
        zero_scr[...] = jnp.zeros(zero_scr.shape, I32)

        def fill_copy(slot):
            return pltpu.make_async_copy(zero_scr, xs_ref.at[_tok_rows(slot, 1), :], sem)

        def fill(e, carry):
            def one(r, c):
                fill_copy(fill_ref[e] + r).start()
                return c

            return lax.fori_loop(0, fill_ref[N_EXPERTS + e], one, carry)

        lax.fori_loop(0, N_EXPERTS, fill, 0)

        def drain(e, carry):
            def one(r, c):
                fill_copy(0).wait()
                return c

            return lax.fori_loop(0, fill_ref[N_EXPERTS + e], one, carry)

        lax.fori_loop(0, N_EXPERTS, drain, 0)

    def issue(t, carry):
        for k in range(TOP_K):
            pltpu.make_async_copy(h_ref.at[_tok_rows(t, 1), :],
                                  xs_ref.at[_tok_rows(dest_ref[base + t * TOP_K + k], 1), :], sem).start(priority=k % 2)
        return carry

    lax.fori_loop(0, DISPATCH_ROWS, issue, 0, unroll=DMA_UNROLL // TOP_K)
    for _ in range(TOP_K):
        pltpu.make_async_copy(h_ref, xs_ref.at[_tok_rows(0, DISPATCH_ROWS), :], sem).wait()


def _dispatch(dest, fill, h2, cap):
    t = h2.shape[0] // TOK_SUB
    return pl.pallas_call(
        _dispatch_kernel,
        out_shape=jax.ShapeDtypeStruct((cap * TOK_SUB, LANES), I32),
        grid_spec=pltpu.PrefetchScalarGridSpec(
            num_scalar_prefetch=2,
            grid=(t // DISPATCH_ROWS,),
            in_specs=[pl.BlockSpec((DISPATCH_ROWS * TOK_SUB, LANES), lambda i, d, z: (i, 0))],
            out_specs=pl.BlockSpec(memory_space=pl.ANY),
            scratch_shapes=[pltpu.VMEM((TOK_SUB, LANES), I32), pltpu.SemaphoreType.DMA]),
        compiler_params=_cparams(("arbitrary",)),
        name="moe_dispatch",
    )(dest, fill, h2)


MOE_SPAN = 8


def _moe_kernel(ie_ref, ir_ref, in_ref, xs_ref, wg_ref, bg_ref, wu_ref, bu_ref, wd_ref, bd_ref, ys_ref,
                xin_scr, acc_scr, wg_scr, wu_scr, wd_scr, sem_x, sem_y):
    i = pl.program_id(0)
    f = pl.program_id(1)
    n_items = pl.num_programs(0)
    nch = in_ref[i]
    row0 = ir_ref[i]
    slot = i % 2
    bias_row = pl.ds(ie_ref[i] * N_FF_TILES + f, 1)
    down_row = pl.ds(ie_ref[i], 1)

    def rows_of(c, k=1):
        return pl.ds(pl.multiple_of(c * MOE_CHUNK, MOE_CHUNK), k * MOE_CHUNK)

    def x_copy(item_row0, c, s):
        return pltpu.make_async_copy(xs_ref.at[_tok_rows(item_row0 + c * MOE_CHUNK, MOE_CHUNK), :],
                                     xin_scr.at[s, _tok_rows(c * MOE_CHUNK, MOE_CHUNK), :], sem_x.at[s])

    def y_copy(item_row0, c, s):
        return pltpu.make_async_copy(xin_scr.at[s, _tok_rows(c * MOE_CHUNK, MOE_CHUNK), :],
                                     ys_ref.at[_tok_rows(item_row0 + c * MOE_CHUNK, MOE_CHUNK), :], sem_y)

    def for_chunks(n, body):
        def step(c, carry):
            body(c)
            return carry

        lax.fori_loop(0, n, step, 0)

    def for_spans(n, body, first_span):
        def span(g, carry):
            body(g * MOE_SPAN, MOE_SPAN)
            return carry

        lax.fori_loop(first_span, n // MOE_SPAN, span, 0)
        k = MOE_SPAN // 2
        while k >= 1:
            start = n // (2 * k) * (2 * k)

            @pl.when(n % (2 * k) >= k)
            def _(start=start, k=k):
                body(start, k)

            k //= 2

    @pl.when(f == 0)
    def _():
        @pl.when(i > 0)
        def _():
            prev = jnp.maximum(i - 1, 0)
            for_chunks(in_ref[prev], lambda c: y_copy(ir_ref[prev], c, 1 - slot).wait())

        @pl.when(i == 0)
        def _():
            for_chunks(nch, lambda c: x_copy(row0, c, slot).start())

        @pl.when(i + 1 < n_items)
        def _():
            nxt = jnp.minimum(i + 1, n_items - 1)
            for_chunks(in_ref[nxt], lambda c: x_copy(ir_ref[nxt], c, 1 - slot).start())

        for_chunks(nch, lambda c: x_copy(row0, c, slot).wait())

    @pl.when(nch > 0)
    def _():
        def cast_weights():
            w = (wg_ref[...].astype(BF16), wu_ref[...].astype(BF16), wd_ref[...].astype(BF16))
            wg_scr[...], wu_scr[...], wd_scr[...] = w
            return w

        def contribution(c, k, w):
            wg, wu, wd = (wg_scr[...], wu_scr[...], wd_scr[...]) if w is None else w
            lo, hi = _unpack_halves(_tok_load(xin_scr.at[slot], c * MOE_CHUNK, k * MOE_CHUNK))
            x = jnp.concatenate([lo.astype(BF16), hi.astype(BF16)], axis=1)
            gt = jnp.minimum(_dot(x, wg) + bg_ref[bias_row, :], SWIGLU_LIMIT)
            up = jnp.clip(_dot(x, wu) + bu_ref[bias_row, :], -SWIGLU_LIMIT, SWIGLU_LIMIT)
            act = (up + 1.0) * (gt * _sigmoid(SWIGLU_ALPHA * gt))
            return _dot(act.astype(BF16), wd)

        def first(c, k, w=None):
            acc_scr[rows_of(c, k), :] = contribution(c, k, w) + bd_ref[down_row, :]

        def middle(c, k, w=None):
            acc_scr[rows_of(c, k), :] += contribution(c, k, w)

        def last(c, k, w=None):
            y = acc_scr[rows_of(c, k), :] + contribution(c, k, w)
            _tok_store(xin_scr.at[slot], c * MOE_CHUNK, k * MOE_CHUNK, _pack_halves(y))
            for u in range(k):
                y_copy(row0, c + u, slot).start()

        def run(body):
            @pl.when(nch >= MOE_SPAN)
            def _():
                body(0, MOE_SPAN, cast_weights())

            @pl.when(nch < MOE_SPAN)
            def _():
                cast_weights()

            for_spans(nch, body, 1)

        @pl.when(f == 0)
        def _():
            run(first)

        @pl.when(jnp.logical_and(f > 0, f < N_FF_TILES - 1))
        def _():
            run(middle)

        @pl.when(f == N_FF_TILES - 1)
        def _():
            run(last)

            @pl.when(i == n_items - 1)
            def _():
                for_chunks(nch, lambda c: y_copy(row0, c, slot).wait())


def _moe_experts(item_e, item_row0, item_nch, xs, w_g, b_g, w_u, b_u, w_d, b_d):
    n_items = item_e.shape[0]

    def ff_idx(i, f, ie, ir, inch):
        return jnp.where(inch[i] > 0, f, N_FF_TILES - 1)

    return pl.pallas_call(
        _moe_kernel,
        out_shape=jax.ShapeDtypeStruct(xs.shape, I32),
        grid_spec=pltpu.PrefetchScalarGridSpec(
            num_scalar_prefetch=3,
            grid=(n_items, N_FF_TILES),
            in_specs=[
                pl.BlockSpec(memory_space=pl.ANY),
                pl.BlockSpec((None, D_MODEL, FF_TILE), lambda i, f, ie, ir, inch: (ie[i], 0, ff_idx(i, f, ie, ir, inch))),
                pl.BlockSpec(b_g.shape, lambda i, f, ie, ir, inch: (0, 0)),
                pl.BlockSpec((None, D_MODEL, FF_TILE), lambda i, f, ie, ir, inch: (ie[i], 0, ff_idx(i, f, ie, ir, inch))),
                pl.BlockSpec(b_u.shape, lambda i, f, ie, ir, inch: (0, 0)),
                pl.BlockSpec((None, FF_TILE, D_MODEL), lambda i, f, ie, ir, inch: (ie[i], ff_idx(i, f, ie, ir, inch), 0)),
                pl.BlockSpec(b_d.shape, lambda i, f, ie, ir, inch: (0, 0)),
            ],
            out_specs=pl.BlockSpec(memory_space=pl.ANY),
            scratch_shapes=[
                pltpu.VMEM((2, MOE_ITEM_ROWS * TOK_SUB, LANES), I32),
                pltpu.VMEM((MOE_ITEM_ROWS, D_MODEL), F32),
                pltpu.VMEM((D_MODEL, FF_TILE), BF16),
                pltpu.VMEM((D_MODEL, FF_TILE), BF16),
                pltpu.VMEM((FF_TILE, D_MODEL), BF16),
                pltpu.SemaphoreType.DMA((2,)),
                pltpu.SemaphoreType.DMA,
            ]),
        compiler_params=_cparams(("arbitrary", "arbitrary")),
        name="moe_experts",
    )(item_e, item_row0, item_nch, xs, w_g, b_g, w_u, b_u, w_d, b_d)


COMBINE_ROWS = 128


def _combine_kernel(dest_ref, ys_ref, x1_ref, tw_ref, mods_ref, o_ref, ybuf, sem):
    base = pl.program_id(0) * (COMBINE_ROWS * TOP_K)

    def row_copy(src_row, t, k):
        return pltpu.make_async_copy(ys_ref.at[_tok_rows(src_row, 1), :], ybuf.at[k, _tok_rows(t, 1), :], sem)

    def issue(t, carry):
        for k in range(TOP_K):
            row_copy(dest_ref[base + t * TOP_K + k], t, k).start(priority=k % 2)
        return carry

    lax.fori_loop(0, COMBINE_ROWS, issue, 0, unroll=DMA_UNROLL // TOP_K)
    for k in range(TOP_K):
        pltpu.make_async_copy(ys_ref.at[_tok_rows(0, COMBINE_ROWS), :], ybuf.at[k], sem).wait()

    y_lo = y_hi = None
    for k in range(TOP_K):
        lo, hi = _unpack_halves(_tok_load(ybuf.at[k], 0, COMBINE_ROWS))
        w = tw_ref[:, k:k + 1]
        y_lo = w * lo if y_lo is None else y_lo + w * lo
        y_hi = w * hi if y_hi is None else y_hi + w * hi
    o_ref[:, :HALF] = x1_ref[:, :HALF] + mods_ref[5:6, :HALF] * y_lo
    o_ref[:, HALF:] = x1_ref[:, HALF:] + mods_ref[5:6, HALF:] * y_hi


def _combine(dest, ys, x1, tw, mods, mod_row):
    t = x1.shape[0]
    return pl.pallas_call(
        _combine_kernel,
        out_shape=jax.ShapeDtypeStruct((t, D_MODEL), F32),
        grid_spec=pltpu.PrefetchScalarGridSpec(
            num_scalar_prefetch=1,
            grid=(t // COMBINE_ROWS,),
            in_specs=[pl.BlockSpec(memory_space=pl.ANY),
                      pl.BlockSpec((COMBINE_ROWS, D_MODEL), lambda i, d: (i, 0)),
                      pl.BlockSpec((COMBINE_ROWS, LANES), lambda i, d: (i, 0)),
                      pl.BlockSpec((None, 6, D_MODEL), lambda i, d: (mod_row(i * COMBINE_ROWS), 0, 0))],
            out_specs=pl.BlockSpec((COMBINE_ROWS, D_MODEL), lambda i, d: (i, 0)),
            scratch_shapes=[pltpu.VMEM((TOP_K, COMBINE_ROWS * TOK_SUB, LANES), I32), pltpu.SemaphoreType.DMA]),
        compiler_params=_cparams(("arbitrary",)),
        name="moe_combine",
    )(dest, ys, x1, tw, mods)


def _routing_tables(top_idx, n_items):
    flat_e = top_idx.reshape(-1)
    onehot = (flat_e[:, None] == jnp.arange(N_EXPERTS, dtype=I32)[None, :]).astype(I32)
    csum = jnp.cumsum(onehot, axis=0)
    rank = jnp.take_along_axis(csum, flat_e[:, None], axis=1)[:, 0] - 1
    counts = csum[-1]
    nch = (counts + MOE_CHUNK - 1) // MOE_CHUNK
    pad_rows = nch * MOE_CHUNK
    pad_end = jnp.cumsum(pad_rows)
    pad_start = pad_end - pad_rows
    dest = (pad_start[flat_e] + rank).astype(I32)
    items_e = (nch + MOE_ITEM_CHUNKS - 1) // MOE_ITEM_CHUNKS
    item_end = jnp.cumsum(items_e)
    item_start = item_end - items_e
    total = item_end[-1]
    i = jnp.arange(n_items, dtype=I32)
    ii = jnp.minimum(i, total - 1)
    e_i = jnp.minimum(jnp.searchsorted(item_end, ii, side="right"), N_EXPERTS - 1).astype(I32)
    local = ii - item_start[e_i]
    row0 = (pad_start[e_i] + local * MOE_ITEM_ROWS).astype(I32)
    n_i = jnp.where(i < total, jnp.minimum(MOE_ITEM_CHUNKS, nch[e_i] - local * MOE_ITEM_CHUNKS), 0).astype(I32)
    fill = jnp.concatenate([pad_start + counts, pad_rows - counts]).astype(I32)
    return dest, fill, e_i, row0, n_i


def _rope_tables(n_lat):
    nf = ROPE // 4
    inv = ROPE_BASE ** (-jnp.arange(nf, dtype=F32) / nf)
    t = jnp.arange(n_lat)
    row = (t // GRID_W).astype(F32)
    col = (t % GRID_W).astype(F32)
    ang_r = row[:, None] * inv[None, :]
    ang_c = col[:, None] * inv[None, :]
    z = jnp.zeros((n_lat, nf), F32)
    tail = jnp.zeros((n_lat, LANES - ROPE), F32)
    cos = jnp.concatenate([jnp.cos(ang_r), jnp.cos(ang_r), jnp.cos(ang_c), jnp.cos(ang_c), tail], axis=1)
    sin_lo = jnp.concatenate([-jnp.sin(ang_r), z, -jnp.sin(ang_c), z, tail], axis=1)
    sin_hi = jnp.concatenate([z, jnp.sin(ang_r), z, jnp.sin(ang_c), tail], axis=1)
    return cos, sin_lo, sin_hi


def kernel(x_prompt, x_sample, cache_ckv, cache_krope, c, c_ctx, norm1_g, norm2_g, w_ada, b_ada, w_in,
           b_branch_gate, kv_norm_g, w_ukv, q_norm_g, k_norm_g, w_o_attn, w_pool, pool_scale, w_o_pool, w_out,
           router_w, router_b, w_exp_gate, b_exp_gate, w_exp_up, b_exp_up, w_exp_down, b_exp_down):
    assert w_in.shape[0] == 1, "single-layer trunk"
    batch, seq, _ = x_prompt.shape
    dec_batch, n_lat, _ = x_sample.shape
    past = cache_ckv.shape[2]
    n_ctx = batch * seq
    n_dec = dec_batch * n_lat

    cond = jnp.concatenate([c_ctx[None, :], c, jnp.zeros((8 - 1 - dec_batch, D_MODEL), F32)], axis=0)
    mods = _adaln(cond, w_ada[0], b_ada).reshape(8, 6, D_MODEL)

    w = w_in[0]
    i0 = N_HEADS * QK
    i1 = i0 + KV_RANK
    i2 = i1 + ROPE
    i3 = i2 + POOL_WIDTH
    wq = jnp.pad(w[:, :i0].reshape(D_MODEL, N_HEADS, QK), ((0, 0), (0, 0), (0, HEAD_PAD - QK)))
    w_cat = jnp.concatenate([wq.reshape(D_MODEL, Q_COLS), w[:, i2:i3], w[:, i3:], w[:, i0:i1], w[:, i1:i2],
                             jnp.zeros((D_MODEL, IN_TILE - KV_RANK - ROPE), F32)], axis=1).astype(BF16)
    qg = jnp.pad(q_norm_g * ATTN_SCALE, ((0, 0), (0, HEAD_PAD - QK)))
    kgn = k_norm_g[:, :NOPE]
    kgr = jnp.pad(k_norm_g[:, NOPE:], ((0, 0), (0, LANES - ROPE)))
    w_ukv_b = w_ukv[0].astype(BF16)
    rope_tabs = _rope_tables(n_lat)

    ctx_row = lambda i: 0
    tm_in = 512
    lat_row_in = lambda i: 1 + (i * tm_in) // n_lat
    q_c, pool_c, gates_c, ckv_c, kr_c, krp_c = _in_proj(
        x_prompt.reshape(n_ctx, D_MODEL), mods, ctx_row, norm1_g, w_cat, qg, kv_norm_g, None, tm_in)
    q_l, pool_l, gates_l, ckv_l, _, krp_l = _in_proj(
        x_sample.reshape(n_dec, D_MODEL), mods, lat_row_in, norm1_g, w_cat, qg, kv_norm_g, rope_tabs, tm_in)

    tr = 256
    k_c, v_c = _kv_expand(ckv_c, krp_c, w_ukv_b, kgn, kgr, None, tr)
    k_l, v_l = _kv_expand(ckv_l, krp_l, w_ukv_b, kgn, kgr, rope_tabs, tr)
    cache_krp = jnp.pad(cache_krope.reshape(dec_batch * past, ROPE), ((0, 0), (0, LANES - ROPE)))
    k_p, v_p = _kv_expand(cache_ckv.reshape(dec_batch * past, KV_RANK), cache_krp, w_ukv_b, kgn, kgr, None, tr)

    attn_c = _attn_ctx(q_c, k_c, v_c, seq)
    attn_l = _attn_lat(q_l, k_l, v_l, k_p, v_p, n_lat, past, n_lat)

    w_pool_b = w_pool[0].astype(BF16)
    poolo_c = _pool(pool_c, w_pool_b, pool_scale, seq)
    poolo_l = _pool(pool_l, w_pool_b, pool_scale, n_lat)

    woa = w_o_attn[0].astype(BF16)
    wop = w_o_pool[0].astype(BF16)
    wout = w_out[0].astype(BF16)
    rw = jnp.pad(router_w[0], ((0, 0), (0, LANES - N_EXPERTS))).astype(BF16)
    rb = jnp.pad(router_b, ((0, 0), (0, LANES - N_EXPERTS)), constant_values=NEG_BIG)
    tm_mg = 256
    lat_row_mg = lambda i: 1 + (i * tm_mg) // n_lat
    x1_c, h2, tidx_c, tw_c = _merge(attn_c, poolo_c, gates_c, x_prompt.reshape(n_ctx, D_MODEL), mods, ctx_row,
                                    b_branch_gate, woa, wop, wout, norm2_g, rw, rb, tm_mg, n_ctx + n_dec, 0, None)
    x1_l, h2, tidx_l, tw_l = _merge(attn_l, poolo_l, gates_l, x_sample.reshape(n_dec, D_MODEL), mods, lat_row_mg,
                                    b_branch_gate, woa, wop, wout, norm2_g, rw, rb, tm_mg, n_ctx + n_dec, n_ctx, h2)

    n_assign = (n_ctx + n_dec) * TOP_K
    max_chunks = (n_assign + N_EXPERTS * (MOE_CHUNK - 1)) // MOE_CHUNK
    cap = max_chunks * MOE_CHUNK
    n_items = (max_chunks + N_EXPERTS * (MOE_ITEM_CHUNKS - 1)) // MOE_ITEM_CHUNKS
    top_idx = jnp.concatenate([tidx_c[:, :TOP_K], tidx_l[:, :TOP_K]], axis=0)
    dest, fill, item_e, item_row0, item_nch = _routing_tables(top_idx, n_items)

    xs = _dispatch(dest, fill, h2, cap)
    ys = _moe_experts(item_e, item_row0, item_nch, xs,
                      w_exp_gate[0], b_exp_gate[0].reshape(N_EXPERTS * N_FF_TILES, FF_TILE),
                      w_exp_up[0], b_exp_up[0].reshape(N_EXPERTS * N_FF_TILES, FF_TILE),
                      w_exp_down[0], b_exp_down[0])

    y_c = _combine(dest[:n_ctx * TOP_K], ys, x1_c, tw_c, mods, lambda r: 0)
    y_l = _combine(dest[n_ctx * TOP_K:], ys, x1_l, tw_l, mods, lambda r: 1 + r // n_lat)

    return (y_c.reshape(batch, seq, D_MODEL),
            y_l.reshape(dec_batch, n_lat, D_MODEL),
            ckv_c.reshape(batch, 1, seq, KV_RANK),
            kr_c.reshape(batch, 1, seq, ROPE))
```

```python
import functools

import jax
import jax.numpy as jnp
from jax import lax
from jax.experimental import pallas as pl
from jax.experimental.pallas import tpu as pltpu

F32 = jnp.float32
BF16 = jnp.bfloat16
I32 = jnp.int32

D_MODEL = 2048
N_HEADS = 16
NOPE = 128
ROPE = 64
QK = NOPE + ROPE
V_DIM = 128
KV_RANK = 512
POOL_WINDOWS = (2, 4, 8, 16)
POOL_WIDTH = 1024
POOL_GROUP = POOL_WIDTH // len(POOL_WINDOWS)
N_EXPERTS = 32
TOP_K = 4
D_FF = 2048
SWIGLU_LIMIT = 7.0
SWIGLU_ALPHA = 1.702
ROPE_BASE = 10000.0
RMS_EPS = 1e-6
GRID_W = 64
ATTN_SCALE = QK ** -0.5

LANES = 128
HEAD_PAD = 2 * LANES
Q_COLS = N_HEADS * HEAD_PAD
IN_TILE = 1024
N_Q_TILES = Q_COLS // IN_TILE
POOL_TILE = N_Q_TILES
GATE_TILE0 = POOL_TILE + 1
N_GATE_TILES = 2 * D_MODEL // IN_TILE
KV_TILE = GATE_TILE0 + N_GATE_TILES
N_IN_TILES = KV_TILE + 1
HEADS_PER_TILE = IN_TILE // HEAD_PAD

MOE_CHUNK = 128
MOE_ITEM_CHUNKS = 16
MOE_ITEM_ROWS = MOE_CHUNK * MOE_ITEM_CHUNKS
FF_TILE = 256
N_FF_TILES = D_FF // FF_TILE
NEG_BIG = -1e30

VMEM_LIMIT = 56 * 1024 * 1024


def _cparams(sem, vmem=VMEM_LIMIT):
    return pltpu.CompilerParams(dimension_semantics=sem, vmem_limit_bytes=vmem)


def _dot(a, b):
    return jnp.dot(a, b, preferred_element_type=F32)


def _sigmoid(z):
    return 1.0 / (1.0 + jnp.exp(-z))


HALF = D_MODEL // 2
HI_MASK = -65536


def _pack_halves(v):
    lo = lax.bitcast_convert_type(v[:, :HALF].astype(BF16).astype(F32), I32)
    hi = lax.bitcast_convert_type(v[:, HALF:].astype(BF16).astype(F32), I32)
    return jnp.bitwise_or(jnp.bitwise_and(hi, HI_MASK), lax.shift_right_logical(lo, 16))


TOK_SUB = HALF // LANES


def _tok_rows(tok0, n):
    row0 = tok0 * TOK_SUB
    return pl.ds(row0 if isinstance(row0, int) else pl.multiple_of(row0, TOK_SUB), n * TOK_SUB)


def _tok_store(ref, tok0, n, words):
    for s in range(TOK_SUB):
        ref[pl.ds(tok0 * TOK_SUB + s, n, stride=TOK_SUB), :] = words[:, s * LANES:(s + 1) * LANES]


def _tok_load(ref, tok0, n):
    return jnp.concatenate([ref[pl.ds(tok0 * TOK_SUB + s, n, stride=TOK_SUB), :] for s in range(TOK_SUB)], axis=1)


def _unpack_halves(w):
    lo = lax.bitcast_convert_type(lax.shift_left(w, 16), F32)
    hi = lax.bitcast_convert_type(jnp.bitwise_and(w, HI_MASK), F32)
    return lo, hi


def _rope_lanes(y, cos, sin_lo, sin_hi):
    return y * cos + pltpu.roll(y, LANES - 16, axis=1) * sin_lo + pltpu.roll(y, 16, axis=1) * sin_hi


def _adaln_kernel(c_ref, w_ref, b_ref, o_ref):
    c = c_ref[...]
    s = (c * _sigmoid(c)).astype(BF16)
    o_ref[...] = _dot(s, w_ref[...].astype(BF16)) + b_ref[...]


def _adaln(cond, w_ada, b_ada):
    rows = cond.shape[0]
    n = w_ada.shape[1]
    tn = 1024
    return pl.pallas_call(
        _adaln_kernel,
        out_shape=jax.ShapeDtypeStruct((rows, n), F32),
        grid=(n // tn,),
        in_specs=[pl.BlockSpec((rows, D_MODEL), lambda j: (0, 0)),
                  pl.BlockSpec((D_MODEL, tn), lambda j: (0, j)),
                  pl.BlockSpec((1, tn), lambda j: (0, j))],
        out_specs=pl.BlockSpec((rows, tn), lambda j: (0, j)),
        compiler_params=_cparams(("arbitrary",)),
        name="adaln",
    )(cond, w_ada, b_ada)


W_Q_END = N_HEADS * QK
W_CKV_END = W_Q_END + KV_RANK
W_KR_END = W_CKV_END + ROPE
N_IN_COLS = W_KR_END + POOL_WIDTH + 2 * D_MODEL
W_PREP_ROWS = 256


def _w_in_layout_kernel(w_ref, o_ref):
    rows = w_ref.shape[0]
    for h in range(N_HEADS):
        o_ref[:, h * HEAD_PAD:h * HEAD_PAD + QK] = w_ref[:, h * QK:(h + 1) * QK].astype(BF16)
        o_ref[:, h * HEAD_PAD + QK:(h + 1) * HEAD_PAD] = jnp.zeros((rows, HEAD_PAD - QK), BF16)
    rest = N_IN_COLS - W_KR_END
    o_ref[:, Q_COLS:Q_COLS + rest] = w_ref[:, W_KR_END:].astype(BF16)
    kv0 = Q_COLS + rest
    o_ref[:, kv0:kv0 + KV_RANK + ROPE] = w_ref[:, W_Q_END:W_KR_END].astype(BF16)
    o_ref[:, kv0 + KV_RANK + ROPE:] = jnp.zeros((rows, IN_TILE - KV_RANK - ROPE), BF16)


def _w_in_layout(w):
    return pl.pallas_call(
        _w_in_layout_kernel,
        out_shape=jax.ShapeDtypeStruct((D_MODEL, N_IN_TILES * IN_TILE), BF16),
        grid=(D_MODEL // W_PREP_ROWS,),
        in_specs=[pl.BlockSpec((W_PREP_ROWS, N_IN_COLS), lambda i: (i, 0))],
        out_specs=pl.BlockSpec((W_PREP_ROWS, N_IN_TILES * IN_TILE), lambda i: (i, 0)),
        compiler_params=_cparams(("arbitrary",)),
        name="w_in_layout",
    )(w)


def _in_proj_kernel(rope, x_ref, mods_ref, g1_ref, w_ref, qg_ref, kvg_ref, *rest):
    if rope:
        cos_ref, slo_ref, shi_ref = rest[:3]
        rest = rest[3:]
    q_ref, pool_ref, gates_ref, ckv_ref, kr_ref, krp_ref, h_scr = rest
    j = pl.program_id(1)

    @pl.when(j == 0)
    def _():
        x = x_ref[...]
        y = x * lax.rsqrt(jnp.mean(x * x, axis=-1, keepdims=True) + RMS_EPS) * g1_ref[...]
        h = y * (1.0 + mods_ref[1:2, :]) + mods_ref[0:1, :]
        h_scr[...] = h.astype(BF16)

    def proj(c0, c1):
        return _dot(h_scr[...], w_ref[:, c0:c1])

    @pl.when(j < N_Q_TILES)
    def _():
        for hh in range(HEADS_PER_TILE):
            a = proj(hh * HEAD_PAD, (hh + 1) * HEAD_PAD)
            r = lax.rsqrt(jnp.sum(a * a, axis=-1, keepdims=True) / QK + RMS_EPS)
            y = a * r * qg_ref[...]
            if rope:
                yr = _rope_lanes(y[:, LANES:], cos_ref[...], slo_ref[...], shi_ref[...])
                q_ref[:, hh * HEAD_PAD:hh * HEAD_PAD + LANES] = y[:, :LANES].astype(BF16)
                q_ref[:, hh * HEAD_PAD + LANES:(hh + 1) * HEAD_PAD] = yr.astype(BF16)
            else:
                q_ref[:, hh * HEAD_PAD:(hh + 1) * HEAD_PAD] = y.astype(BF16)

    @pl.when(j == POOL_TILE)
    def _():
        pool_ref[...] = proj(0, IN_TILE)

    @pl.when(jnp.logical_and(j >= GATE_TILE0, j < KV_TILE))
    def _():
        gates_ref[...] = proj(0, IN_TILE)

    @pl.when(j == KV_TILE)
    def _():
        a = proj(0, KV_RANK)
        r = lax.rsqrt(jnp.mean(a * a, axis=-1, keepdims=True) + RMS_EPS)
        ckv_ref[...] = a * r * kvg_ref[...]
        krp = proj(KV_RANK, KV_RANK + LANES)
        krp_ref[...] = krp
        kr_ref[...] = krp[:, :ROPE]


def _in_proj(x, mods, mod_row, g1, w_cat, qg, kvg, rope_tabs, tm):
    t = x.shape[0]
    rope = rope_tabs is not None
    in_specs = [
        pl.BlockSpec((tm, D_MODEL), lambda i, j: (i, 0)),
        pl.BlockSpec((None, 6, D_MODEL), lambda i, j: (mod_row(i), 0, 0)),
        pl.BlockSpec((1, D_MODEL), lambda i, j: (0, 0)),
        pl.BlockSpec((D_MODEL, IN_TILE), lambda i, j: (0, j)),
        pl.BlockSpec((1, HEAD_PAD), lambda i, j: (0, 0)),
        pl.BlockSpec((1, KV_RANK), lambda i, j: (0, 0)),
    ]
    args = [x, mods, g1, w_cat, qg, kvg]
    if rope:
        seq_tiles = rope_tabs[0].shape[0] // tm
        in_specs += [pl.BlockSpec((tm, LANES), lambda i, j: (i % seq_tiles, 0))] * 3
        args += list(rope_tabs)
    out_shape = (
        jax.ShapeDtypeStruct((t, Q_COLS), BF16),
        jax.ShapeDtypeStruct((t, POOL_WIDTH), F32),
        jax.ShapeDtypeStruct((t, 2 * D_MODEL), F32),
        jax.ShapeDtypeStruct((t, KV_RANK), F32),
        jax.ShapeDtypeStruct((t, ROPE), F32),
        jax.ShapeDtypeStruct((t, LANES), F32),
    )
    out_specs = (
        pl.BlockSpec((tm, IN_TILE), lambda i, j: (i, jnp.minimum(j, N_Q_TILES - 1))),
        pl.BlockSpec((tm, POOL_WIDTH), lambda i, j: (i, 0)),
        pl.BlockSpec((tm, IN_TILE), lambda i, j: (i, jnp.clip(j - GATE_TILE0, 0, N_GATE_TILES - 1))),
        pl.BlockSpec((tm, KV_RANK), lambda i, j: (i, 0)),
        pl.BlockSpec((tm, ROPE), lambda i, j: (i, 0)),
        pl.BlockSpec((tm, LANES), lambda i, j: (i, 0)),
    )
    return pl.pallas_call(
        functools.partial(_in_proj_kernel, rope),
        out_shape=out_shape,
        grid=(t // tm, N_IN_TILES),
        in_specs=in_specs,
        out_specs=out_specs,
        scratch_shapes=[pltpu.VMEM((tm, D_MODEL), BF16)],
        compiler_params=_cparams(("arbitrary", "arbitrary")),
        name="in_proj_rope" if rope else "in_proj",
    )(*args)


def _kv_expand_kernel(rope, ckv_ref, krp_ref, w_ref, kgn_ref, kgr_ref, *rest):
    if rope:
        cos_ref, slo_ref, shi_ref = rest[:3]
        rest = rest[3:]
    k_ref, v_ref = rest
    kv = _dot(ckv_ref[...].astype(BF16), w_ref[...])
    kr = krp_ref[...]
    ssq_r = jnp.sum(kr * kr, axis=-1, keepdims=True)
    krg = kr * kgr_ref[...]
    if rope:
        krg = _rope_lanes(krg, cos_ref[...], slo_ref[...], shi_ref[...])
    for h in range(N_HEADS):
        kn = kv[:, h * HEAD_PAD:h * HEAD_PAD + NOPE]
        r = lax.rsqrt((jnp.sum(kn * kn, axis=-1, keepdims=True) + ssq_r) / QK + RMS_EPS)
        k_ref[:, h * HEAD_PAD:h * HEAD_PAD + NOPE] = (kn * r * kgn_ref[...]).astype(BF16)
        k_ref[:, h * HEAD_PAD + NOPE:(h + 1) * HEAD_PAD] = (krg * r).astype(BF16)
        v_ref[:, h * V_DIM:(h + 1) * V_DIM] = kv[:, h * HEAD_PAD + NOPE:(h + 1) * HEAD_PAD].astype(BF16)


def _kv_expand(ckv_n, krp, w_ukv, kgn, kgr, rope_tabs, tr):
    r = ckv_n.shape[0]
    rope = rope_tabs is not None
    in_specs = [
        pl.BlockSpec((tr, KV_RANK), lambda i: (i, 0)),
        pl.BlockSpec((tr, LANES), lambda i: (i, 0)),
        pl.BlockSpec((KV_RANK, N_HEADS * HEAD_PAD), lambda i: (0, 0)),
        pl.BlockSpec((1, LANES), lambda i: (0, 0)),
        pl.BlockSpec((1, LANES), lambda i: (0, 0)),
    ]
    args = [ckv_n, krp, w_ukv, kgn, kgr]
    if rope:
        seq_tiles = rope_tabs[0].shape[0] // tr
        in_specs += [pl.BlockSpec((tr, LANES), lambda i: (i % seq_tiles, 0))] * 3
        args += list(rope_tabs)
    return pl.pallas_call(
        functools.partial(_kv_expand_kernel, rope),
        out_shape=(jax.ShapeDtypeStruct((r, N_HEADS * HEAD_PAD), BF16),
                   jax.ShapeDtypeStruct((r, N_HEADS * V_DIM), BF16)),
        grid=(r // tr,),
        in_specs=in_specs,
        out_specs=(pl.BlockSpec((tr, N_HEADS * HEAD_PAD), lambda i: (i, 0)),
                   pl.BlockSpec((tr, N_HEADS * V_DIM), lambda i: (i, 0))),
        compiler_params=_cparams(("arbitrary",)),
        name="kv_expand_rope" if rope else "kv_expand",
    )(*args)


def _qk(q, k):
    return lax.dot_general(q, k, (((1,), (1,)), ((), ())), preferred_element_type=F32)


def _attn_ctx_kernel(q_ref, k_ref, v_ref, o_ref):
    for h in range(N_HEADS):
        s = _qk(q_ref[:, h * HEAD_PAD:(h + 1) * HEAD_PAD], k_ref[:, h * HEAD_PAD:(h + 1) * HEAD_PAD])
        p = jnp.exp(s - jnp.max(s, axis=-1, keepdims=True))
        l = jnp.sum(p, axis=-1, keepdims=True)
        o = _dot(p.astype(BF16), v_ref[:, h * V_DIM:(h + 1) * V_DIM])
        o_ref[:, h * V_DIM:(h + 1) * V_DIM] = (o / l).astype(BF16)


def _attn_ctx(q, k, v, seq):
    t = q.shape[0]
    return pl.pallas_call(
        _attn_ctx_kernel,
        out_shape=jax.ShapeDtypeStruct((t, N_HEADS * V_DIM), BF16),
        grid=(t // seq,),
        in_specs=[pl.BlockSpec((seq, Q_COLS), lambda b: (b, 0)),
                  pl.BlockSpec((seq, Q_COLS), lambda b: (b, 0)),
                  pl.BlockSpec((seq, N_HEADS * V_DIM), lambda b: (b, 0))],
        out_specs=pl.BlockSpec((seq, N_HEADS * V_DIM), lambda b: (b, 0)),
        compiler_params=_cparams(("arbitrary",)),
        name="attn_ctx",
    )(q, k, v)


def _attn_lat_kernel(q_ref, k_ref, v_ref, kc_ref, vc_ref, o_ref):
    q = q_ref[...]
    s1 = _qk(q, k_ref[...])
    s2 = _qk(q, kc_ref[...])
    m = jnp.maximum(jnp.max(s1, axis=-1, keepdims=True), jnp.max(s2, axis=-1, keepdims=True))
    p1 = jnp.exp(s1 - m)
    p2 = jnp.exp(s2 - m)
    l = jnp.sum(p1, axis=-1, keepdims=True) + jnp.sum(p2, axis=-1, keepdims=True)
    o = _dot(p1.astype(BF16), v_ref[...]) + _dot(p2.astype(BF16), vc_ref[...])
    o_ref[...] = (o / l).astype(BF16)


def _attn_lat(q, k, v, kc, vc, seq, past, tq):
    t = q.shape[0]
    nq = seq // tq
    return pl.pallas_call(
        _attn_lat_kernel,
        out_shape=jax.ShapeDtypeStruct((t, N_HEADS * V_DIM), BF16),
        grid=(t // seq, N_HEADS, nq),
        in_specs=[pl.BlockSpec((tq, HEAD_PAD), lambda b, h, i: (b * nq + i, h)),
                  pl.BlockSpec((seq, HEAD_PAD), lambda b, h, i: (b, h)),
                  pl.BlockSpec((seq, V_DIM), lambda b, h, i: (b, h)),
                  pl.BlockSpec((past, HEAD_PAD), lambda b, h, i: (b, h)),
                  pl.BlockSpec((past, V_DIM), lambda b, h, i: (b, h))],
        out_specs=pl.BlockSpec((tq, V_DIM), lambda b, h, i: (b * nq + i, h)),
        compiler_params=_cparams(("arbitrary", "arbitrary", "arbitrary")),
        name="attn_lat",
    )(q, k, v, kc, vc)


POOL_HALO = 8


def _pool_kernel(seq, u_ref, w_ref, sc_ref, o_ref, pad_scr):
    zeros = jnp.zeros((POOL_HALO, POOL_WIDTH), F32)
    pad_scr[0:POOL_HALO, :] = zeros
    pad_scr[POOL_HALO + seq:2 * POOL_HALO + seq, :] = zeros
    pad_scr[POOL_HALO:POOL_HALO + seq, :] = u_ref[...]
    t = lax.broadcasted_iota(I32, (seq, 1), 0)
    for g, w in enumerate(POOL_WINDOWS):
        cols = slice(g * POOL_GROUP, (g + 1) * POOL_GROUP)
        tot = None
        for d in range(-(w // 2), w - w // 2):
            piece = pad_scr[POOL_HALO + d:POOL_HALO + d + seq, cols]
            tot = piece if tot is None else tot + piece
        cnt = (jnp.minimum(t + (w - w // 2), seq) - jnp.maximum(t - w // 2, 0)).astype(F32)
        mixed = tot / cnt - u_ref[:, cols]
        o_ref[:, cols] = (_dot(mixed.astype(BF16), w_ref[g]) * sc_ref[:, cols]).astype(BF16)


def _pool(u, w_pool, pool_scale, seq):
    t = u.shape[0]
    n_groups = len(POOL_WINDOWS)
    return pl.pallas_call(
        functools.partial(_pool_kernel, seq),
        out_shape=jax.ShapeDtypeStruct((t, POOL_WIDTH), BF16),
        grid=(t // seq,),
        in_specs=[pl.BlockSpec((seq, POOL_WIDTH), lambda b: (b, 0)),
                  pl.BlockSpec((n_groups, POOL_GROUP, POOL_GROUP), lambda b: (0, 0, 0)),
                  pl.BlockSpec((1, POOL_WIDTH), lambda b: (0, 0))],
        out_specs=pl.BlockSpec((seq, POOL_WIDTH), lambda b: (b, 0)),
        scratch_shapes=[pltpu.VMEM((seq + 2 * POOL_HALO, POOL_WIDTH), F32)],
        compiler_params=_cparams(("arbitrary",)),
        name="pool",
    )(u, w_pool, pool_scale)


def _merge_kernel(attn_ref, pool_ref, gates_ref, x_ref, mods_ref, bbg_ref, woa_ref, wop_ref, wout_ref,
                  g2_ref, rw_ref, rb_ref, *rest):
    x1_ref, h2_ref, tidx_ref, tw_ref = rest[-4:]
    a = _dot(attn_ref[...], woa_ref[...])
    p = _dot(pool_ref[...], wop_ref[...])
    ga = _sigmoid(gates_ref[:, :D_MODEL] + bbg_ref[:, :D_MODEL])
    gp = _sigmoid(gates_ref[:, D_MODEL:] + bbg_ref[:, D_MODEL:])
    merged = (ga * a + gp * p).astype(BF16)
    x1 = x_ref[...] + mods_ref[2:3, :] * _dot(merged, wout_ref[...])
    x1_ref[...] = x1
    y = x1 * lax.rsqrt(jnp.mean(x1 * x1, axis=-1, keepdims=True) + RMS_EPS) * g2_ref[...]
    h2 = y * (1.0 + mods_ref[4:5, :]) + mods_ref[3:4, :]
    _tok_store(h2_ref, 0, h2.shape[0], _pack_halves(h2))
    logits = _dot(h2.astype(BF16), rw_ref[...]) + rb_ref[...]
    lane = lax.broadcasted_iota(I32, logits.shape, 1).astype(F32)
    vals, idxs = [], []
    for _ in range(TOP_K):
        m = jnp.max(logits, axis=-1, keepdims=True)
        ix = jnp.min(jnp.where(logits == m, lane, float(LANES)), axis=-1, keepdims=True)
        vals.append(m)
        idxs.append(ix)
        logits = jnp.where(lane == ix, -jnp.inf, logits)
    es = [jnp.exp(v - vals[0]) for v in vals]
    tot = es[0] + es[1] + es[2] + es[3]
    tidx = jnp.zeros(logits.shape, F32)
    tw = jnp.zeros(logits.shape, F32)
    for k in range(TOP_K):
        tidx = jnp.where(lane == k, idxs[k], tidx)
        tw = jnp.where(lane == k, es[k] / tot, tw)
    tidx_ref[...] = tidx.astype(I32)
    tw_ref[...] = tw


def _merge(attn_o, pool_o, gates, x, mods, mod_row, bbg, woa, wop, wout, g2, rw, rb, tm, h2_rows, h2_row0, h2_prev):
    t = x.shape[0]
    const = lambda shape: pl.BlockSpec(shape, lambda i: (0, 0), pipeline_mode=pl.Buffered(1))
    h2_blk0 = h2_row0 // tm
    extra_specs = [] if h2_prev is None else [pl.BlockSpec(memory_space=pl.ANY)]
    extra_args = [] if h2_prev is None else [h2_prev]
    return pl.pallas_call(
        _merge_kernel,
        out_shape=(jax.ShapeDtypeStruct((t, D_MODEL), F32),
                   jax.ShapeDtypeStruct((h2_rows * TOK_SUB, LANES), I32),
                   jax.ShapeDtypeStruct((t, LANES), I32),
                   jax.ShapeDtypeStruct((t, LANES), F32)),
        grid=(t // tm,),
        input_output_aliases={} if h2_prev is None else {12: 1},
        in_specs=[pl.BlockSpec((tm, D_MODEL), lambda i: (i, 0)),
                  pl.BlockSpec((tm, POOL_WIDTH), lambda i: (i, 0)),
                  pl.BlockSpec((tm, 2 * D_MODEL), lambda i: (i, 0)),
                  pl.BlockSpec((tm, D_MODEL), lambda i: (i, 0)),
                  pl.BlockSpec((None, 6, D_MODEL), lambda i: (mod_row(i), 0, 0)),
                  const((1, 2 * D_MODEL)),
                  const((D_MODEL, D_MODEL)),
                  const((POOL_WIDTH, D_MODEL)),
                  const((D_MODEL, D_MODEL)),
                  const((1, D_MODEL)),
                  const((D_MODEL, LANES)),
                  const((1, LANES))] + extra_specs,
        out_specs=(pl.BlockSpec((tm, D_MODEL), lambda i: (i, 0)),
                   pl.BlockSpec((tm * TOK_SUB, LANES), lambda i: (i + h2_blk0, 0)),
                   pl.BlockSpec((tm, LANES), lambda i: (i, 0)),
                   pl.BlockSpec((tm, LANES), lambda i: (i, 0))),
        compiler_params=_cparams(("arbitrary",)),
        name="merge",
    )(attn_o, pool_o, gates, x, mods, bbg, woa, wop, wout, g2, rw, rb, *extra_args)


DISPATCH_ROWS = 256
DMA_UNROLL = 8


def _dispatch_kernel(dest_ref, fill_ref, h_ref, xs_ref, zero_scr, sem):
    base = pl.program_id(0) * (DISPATCH_ROWS * TOP_K)

    @pl.when(pl.program_id(0) == 0)
    def _():
        zero_scr[...] = jnp.zeros(zero_scr.shape, I32)

        def fill_copy(slot):
            return pltpu.make_async_copy(zero_scr, xs_ref.at[_tok_rows(slot, 1), :], sem)

        def fill(e, carry):
            def one(r, c):
                fill_copy(fill_ref[e] + r).start()
                return c

            return lax.fori_loop(0, fill_ref[N_EXPERTS + e], one, carry)

        lax.fori_loop(0, N_EXPERTS, fill, 0)

        def drain(e, carry):
            def one(r, c):
                fill_copy(0).wait()
                return c

            return lax.fori_loop(0, fill_ref[N_EXPERTS + e], one, carry)

        lax.fori_loop(0, N_EXPERTS, drain, 0)

    def issue(t, carry):
        for k in range(TOP_K):
            pltpu.make_async_copy(h_ref.at[_tok_rows(t, 1), :],
                                  xs_ref.at[_tok_rows(dest_ref[base + t * TOP_K + k], 1), :], sem).start(priority=k % 2)
        return carry

    lax.fori_loop(0, DISPATCH_ROWS, issue, 0, unroll=DMA_UNROLL // TOP_K)
    for _ in range(TOP_K):
        pltpu.make_async_copy(h_ref, xs_ref.at[_tok_rows(0, DISPATCH_ROWS), :], sem).wait()


def _dispatch(dest, fill, h2, cap):
    t = h2.shape[0] // TOK_SUB
    return pl.pallas_call(
        _dispatch_kernel,
        out_shape=jax.ShapeDtypeStruct((cap * TOK_SUB, LANES), I32),
        grid_spec=pltpu.PrefetchScalarGridSpec(
            num_scalar_prefetch=2,
            grid=(t // DISPATCH_ROWS,),
            in_specs=[pl.BlockSpec((DISPATCH_ROWS * TOK_SUB, LANES), lambda i, d, z: (i, 0))],
            out_specs=pl.BlockSpec(memory_space=pl.ANY),
            scratch_shapes=[pltpu.VMEM((TOK_SUB, LANES), I32), pltpu.SemaphoreType.DMA]),
        compiler_params=_cparams(("arbitrary",)),
        name="moe_dispatch",
    )(dest, fill, h2)


MOE_SPAN = 8


def _moe_kernel(ie_ref, ir_ref, in_ref, xs_ref, wg_ref, bg_ref, wu_ref, bu_ref, wd_ref, bd_ref, ys_ref,
                xin_scr, acc_scr, wg_scr, wu_scr, wd_scr, sem_x, sem_y):
    i = pl.program_id(0)
    f = pl.program_id(1)
    n_items = pl.num_programs(0)
    nch = in_ref[i]
    row0 = ir_ref[i]
    slot = i % 2
    bias_row = pl.ds(ie_ref[i] * N_FF_TILES + f, 1)
    down_row = pl.ds(ie_ref[i], 1)

    def rows_of(c, k=1):
        return pl.ds(pl.multiple_of(c * MOE_CHUNK, MOE_CHUNK), k * MOE_CHUNK)

    def x_copy(item_row0, c, s):
        return pltpu.make_async_copy(xs_ref.at[_tok_rows(item_row0 + c * MOE_CHUNK, MOE_CHUNK), :],
                                     xin_scr.at[s, _tok_rows(c * MOE_CHUNK, MOE_CHUNK), :], sem_x.at[s])

    def y_copy(item_row0, c, s):
        return pltpu.make_async_copy(xin_scr.at[s, _tok_rows(c * MOE_CHUNK, MOE_CHUNK), :],
                                     ys_ref.at[_tok_rows(item_row0 + c * MOE_CHUNK, MOE_CHUNK), :], sem_y)

    def for_chunks(n, body):
        def step(c, carry):
            body(c)
            return carry

        lax.fori_loop(0, n, step, 0)

    def for_spans(n, body, first_span):
        def span(g, carry):
            body(g * MOE_SPAN, MOE_SPAN)
            return carry

        lax.fori_loop(first_span, n // MOE_SPAN, span, 0)
        k = MOE_SPAN // 2
        while k >= 1:
            start = n // (2 * k) * (2 * k)

            @pl.when(n % (2 * k) >= k)
            def _(start=start, k=k):
                body(start, k)

            k //= 2

    @pl.when(f == 0)
    def _():
        @pl.when(i > 0)
        def _():
            prev = jnp.maximum(i - 1, 0)
            for_chunks(in_ref[prev], lambda c: y_copy(ir_ref[prev], c, 1 - slot).wait())

        @pl.when(i == 0)
        def _():
            for_chunks(nch, lambda c: x_copy(row0, c, slot).start())

        @pl.when(i + 1 < n_items)
        def _():
            nxt = jnp.minimum(i + 1, n_items - 1)
            for_chunks(in_ref[nxt], lambda c: x_copy(ir_ref[nxt], c, 1 - slot).start())

        for_chunks(nch, lambda c: x_copy(row0, c, slot).wait())

    @pl.when(nch > 0)
    def _():
        def cast_weights():
            w = (wg_ref[...].astype(BF16), wu_ref[...].astype(BF16), wd_ref[...].astype(BF16))
            wg_scr[...], wu_scr[...], wd_scr[...] = w
            return w

        def contribution(c, k, w):
            wg, wu, wd = (wg_scr[...], wu_scr[...], wd_scr[...]) if w is None else w
            lo, hi = _unpack_halves(_tok_load(xin_scr.at[slot], c * MOE_CHUNK, k * MOE_CHUNK))
            x = jnp.concatenate([lo.astype(BF16), hi.astype(BF16)], axis=1)
            gt = jnp.minimum(_dot(x, wg) + bg_ref[bias_row, :], SWIGLU_LIMIT)
            up = jnp.clip(_dot(x, wu) + bu_ref[bias_row, :], -SWIGLU_LIMIT, SWIGLU_LIMIT)
            act = (up + 1.0) * (gt * _sigmoid(SWIGLU_ALPHA * gt))
            return _dot(act.astype(BF16), wd)

        def first(c, k, w=None):
            acc_scr[rows_of(c, k), :] = contribution(c, k, w) + bd_ref[down_row, :]

        def middle(c, k, w=None):
            acc_scr[rows_of(c, k), :] += contribution(c, k, w)

        def last(c, k, w=None):
            y = acc_scr[rows_of(c, k), :] + contribution(c, k, w)
            _tok_store(xin_scr.at[slot], c * MOE_CHUNK, k * MOE_CHUNK, _pack_halves(y))
            for u in range(k):
                y_copy(row0, c + u, slot).start()

        def run(body):
            @pl.when(nch >= MOE_SPAN)
            def _():
                body(0, MOE_SPAN, cast_weights())

            @pl.when(nch < MOE_SPAN)
            def _():
                cast_weights()

            for_spans(nch, body, 1)

        @pl.when(f == 0)
        def _():
            run(first)

        @pl.when(jnp.logical_and(f > 0, f < N_FF_TILES - 1))
        def _():
            run(middle)

        @pl.when(f == N_FF_TILES - 1)
        def _():
            run(last)

            @pl.when(i == n_items - 1)
            def _():
                for_chunks(nch, lambda c: y_copy(row0, c, slot).wait())


def _moe_experts(item_e, item_row0, item_nch, xs, w_g, b_g, w_u, b_u, w_d, b_d):
    n_items = item_e.shape[0]

    def ff_idx(i, f, ie, ir, inch):
        return jnp.where(inch[i] > 0, f, N_FF_TILES - 1)

    return pl.pallas_call(
        _moe_kernel,
        out_shape=jax.ShapeDtypeStruct(xs.shape, I32),
        grid_spec=pltpu.PrefetchScalarGridSpec(
            num_scalar_prefetch=3,
            grid=(n_items, N_FF_TILES),
            in_specs=[
                pl.BlockSpec(memory_space=pl.ANY),
                pl.BlockSpec((None, D_MODEL, FF_TILE), lambda i, f, ie, ir, inch: (ie[i], 0, ff_idx(i, f, ie, ir, inch))),
                pl.BlockSpec(b_g.shape, lambda i, f, ie, ir, inch: (0, 0)),
                pl.BlockSpec((None, D_MODEL, FF_TILE), lambda i, f, ie, ir, inch: (ie[i], 0, ff_idx(i, f, ie, ir, inch))),
                pl.BlockSpec(b_u.shape, lambda i, f, ie, ir, inch: (0, 0)),
                pl.BlockSpec((None, FF_TILE, D_MODEL), lambda i, f, ie, ir, inch: (ie[i], ff_idx(i, f, ie, ir, inch), 0)),
                pl.BlockSpec(b_d.shape, lambda i, f, ie, ir, inch: (0, 0)),
            ],
            out_specs=pl.BlockSpec(memory_space=pl.ANY),
            scratch_shapes=[
                pltpu.VMEM((2, MOE_ITEM_ROWS * TOK_SUB, LANES), I32),
                pltpu.VMEM((MOE_ITEM_ROWS, D_MODEL), F32),
                pltpu.VMEM((D_MODEL, FF_TILE), BF16),
                pltpu.VMEM((D_MODEL, FF_TILE), BF16),
                pltpu.VMEM((FF_TILE, D_MODEL), BF16),
                pltpu.SemaphoreType.DMA((2,)),
                pltpu.SemaphoreType.DMA,
            ]),
        compiler_params=_cparams(("arbitrary", "arbitrary")),
        name="moe_experts",
    )(item_e, item_row0, item_nch, xs, w_g, b_g, w_u, b_u, w_d, b_d)


COMBINE_ROWS = 256


def _combine_kernel(dest_ref, ys_ref, x1_ref, tw_ref, mods_ref, o_ref, ybuf, sem):
    step = pl.program_id(0)
    slot = step % 2

    def gather(blk, s):
        base = blk * (COMBINE_ROWS * TOP_K)

        def issue(t, carry):
            for k in range(TOP_K):
                pltpu.make_async_copy(ys_ref.at[_tok_rows(dest_ref[base + t * TOP_K + k], 1), :],
                                      ybuf.at[s, k, _tok_rows(t, 1), :], sem.at[s]).start(priority=k % 2)
            return carry

        lax.fori_loop(0, COMBINE_ROWS, issue, 0, unroll=DMA_UNROLL // TOP_K)

    @pl.when(step == 0)
    def _():
        gather(step, slot)

    @pl.when(step + 1 < pl.num_programs(0))
    def _():
        gather(step + 1, 1 - slot)

    for k in range(TOP_K):
        pltpu.make_async_copy(ys_ref.at[_tok_rows(0, COMBINE_ROWS), :], ybuf.at[slot, k], sem.at[slot]).wait()

    y_lo = y_hi = None
    for k in range(TOP_K):
        lo, hi = _unpack_halves(_tok_load(ybuf.at[slot, k], 0, COMBINE_ROWS))
        w = tw_ref[:, k:k + 1]
        y_lo = w * lo if y_lo is None else y_lo + w * lo
        y_hi = w * hi if y_hi is None else y_hi + w * hi
    o_ref[:, :HALF] = x1_ref[:, :HALF] + mods_ref[5:6, :HALF] * y_lo
    o_ref[:, HALF:] = x1_ref[:, HALF:] + mods_ref[5:6, HALF:] * y_hi


def _combine(dest, ys, x1, tw, mods, mod_row):
    t = x1.shape[0]
    return pl.pallas_call(
        _combine_kernel,
        out_shape=jax.ShapeDtypeStruct((t, D_MODEL), F32),
        grid_spec=pltpu.PrefetchScalarGridSpec(
            num_scalar_prefetch=1,
            grid=(t // COMBINE_ROWS,),
            in_specs=[pl.BlockSpec(memory_space=pl.ANY),
                      pl.BlockSpec((COMBINE_ROWS, D_MODEL), lambda i, d: (i, 0)),
                      pl.BlockSpec((COMBINE_ROWS, LANES), lambda i, d: (i, 0)),
                      pl.BlockSpec((None, 6, D_MODEL), lambda i, d: (mod_row(i * COMBINE_ROWS), 0, 0))],
            out_specs=pl.BlockSpec((COMBINE_ROWS, D_MODEL), lambda i, d: (i, 0)),
            scratch_shapes=[pltpu.VMEM((2, TOP_K, COMBINE_ROWS * TOK_SUB, LANES), I32),
                            pltpu.SemaphoreType.DMA((2,))]),
        compiler_params=_cparams(("arbitrary",)),
        name="moe_combine",
    )(dest, ys, x1, tw, mods)


def _routing_tables(top_idx, n_items):
    flat_e = top_idx.reshape(-1)
    onehot = (flat_e[:, None] == jnp.arange(N_EXPERTS, dtype=I32)[None, :]).astype(I32)
    csum = jnp.cumsum(onehot, axis=0)
    rank = jnp.take_along_axis(csum, flat_e[:, None], axis=1)[:, 0] - 1
    counts = csum[-1]
    nch = (counts + MOE_CHUNK - 1) // MOE_CHUNK
    pad_rows = nch * MOE_CHUNK
    pad_end = jnp.cumsum(pad_rows)
    pad_start = pad_end - pad_rows
    dest = (pad_start[flat_e] + rank).astype(I32)
    items_e = (nch + MOE_ITEM_CHUNKS - 1) // MOE_ITEM_CHUNKS
    item_end = jnp.cumsum(items_e)
    item_start = item_end - items_e
    total = item_end[-1]
    i = jnp.arange(n_items, dtype=I32)
    ii = jnp.minimum(i, total - 1)
    e_i = jnp.minimum(jnp.searchsorted(item_end, ii, side="right"), N_EXPERTS - 1).astype(I32)
    local = ii - item_start[e_i]
    row0 = (pad_start[e_i] + local * MOE_ITEM_ROWS).astype(I32)
    n_i = jnp.where(i < total, jnp.minimum(MOE_ITEM_CHUNKS, nch[e_i] - local * MOE_ITEM_CHUNKS), 0).astype(I32)
    fill = jnp.concatenate([pad_start + counts, pad_rows - counts]).astype(I32)
    return dest, fill, e_i, row0, n_i


def _rope_tables(n_lat):
    nf = ROPE // 4
    inv = ROPE_BASE ** (-jnp.arange(nf, dtype=F32) / nf)
    t = jnp.arange(n_lat)
    row = (t // GRID_W).astype(F32)
    col = (t % GRID_W).astype(F32)
    ang_r = row[:, None] * inv[None, :]
    ang_c = col[:, None] * inv[None, :]
    z = jnp.zeros((n_lat, nf), F32)
    tail = jnp.zeros((n_lat, LANES - ROPE), F32)
    cos = jnp.concatenate([jnp.cos(ang_r), jnp.cos(ang_r), jnp.cos(ang_c), jnp.cos(ang_c), tail], axis=1)
    sin_lo = jnp.concatenate([-jnp.sin(ang_r), z, -jnp.sin(ang_c), z, tail], axis=1)
    sin_hi = jnp.concatenate([z, jnp.sin(ang_r), z, jnp.sin(ang_c), tail], axis=1)
    return cos, sin_lo, sin_hi


def kernel(x_prompt, x_sample, cache_ckv, cache_krope, c, c_ctx, norm1_g, norm2_g, w_ada, b_ada, w_in,
           b_branch_gate, kv_norm_g, w_ukv, q_norm_g, k_norm_g, w_o_attn, w_pool, pool_scale, w_o_pool, w_out,
           router_w, router_b, w_exp_gate, b_exp_gate, w_exp_up, b_exp_up, w_exp_down, b_exp_down):
    assert w_in.shape[0] == 1, "single-layer trunk"
    batch, seq, _ = x_prompt.shape
    dec_batch, n_lat, _ = x_sample.shape
    past = cache_ckv.shape[2]
    n_ctx = batch * seq
    n_dec = dec_batch * n_lat

    cond = jnp.concatenate([c_ctx[None, :], c, jnp.zeros((8 - 1 - dec_batch, D_MODEL), F32)], axis=0)
    mods = _adaln(cond, w_ada[0], b_ada).reshape(8, 6, D_MODEL)

    w_cat = _w_in_layout(w_in[0])
    qg = jnp.pad(q_norm_g * ATTN_SCALE, ((0, 0), (0, HEAD_PAD - QK)))
    kgn = k_norm_g[:, :NOPE]
    kgr = jnp.pad(k_norm_g[:, NOPE:], ((0, 0), (0, LANES - ROPE)))
    w_ukv_b = w_ukv[0].astype(BF16)
    rope_tabs = _rope_tables(n_lat)

    ctx_row = lambda i: 0
    tm_in = 512
    lat_row_in = lambda i: 1 + (i * tm_in) // n_lat
    q_c, pool_c, gates_c, ckv_c, kr_c, krp_c = _in_proj(
        x_prompt.reshape(n_ctx, D_MODEL), mods, ctx_row, norm1_g, w_cat, qg, kv_norm_g, None, tm_in)
    q_l, pool_l, gates_l, ckv_l, _, krp_l = _in_proj(
        x_sample.reshape(n_dec, D_MODEL), mods, lat_row_in, norm1_g, w_cat, qg, kv_norm_g, rope_tabs, tm_in)

    tr = 256
    k_c, v_c = _kv_expand(ckv_c, krp_c, w_ukv_b, kgn, kgr, None, tr)
    k_l, v_l = _kv_expand(ckv_l, krp_l, w_ukv_b, kgn, kgr, rope_tabs, tr)
    cache_krp = jnp.pad(cache_krope.reshape(dec_batch * past, ROPE), ((0, 0), (0, LANES - ROPE)))
    k_p, v_p = _kv_expand(cache_ckv.reshape(dec_batch * past, KV_RANK), cache_krp, w_ukv_b, kgn, kgr, None, tr)

    attn_c = _attn_ctx(q_c, k_c, v_c, seq)
    attn_l = _attn_lat(q_l, k_l, v_l, k_p, v_p, n_lat, past, 512)

    w_pool_b = w_pool[0].astype(BF16)
    poolo_c = _pool(pool_c, w_pool_b, pool_scale, seq)
    poolo_l = _pool(pool_l, w_pool_b, pool_scale, n_lat)

    woa = w_o_attn[0].astype(BF16)
    wop = w_o_pool[0].astype(BF16)
    wout = w_out[0].astype(BF16)
    rw = jnp.pad(router_w[0], ((0, 0), (0, LANES - N_EXPERTS))).astype(BF16)
    rb = jnp.pad(router_b, ((0, 0), (0, LANES - N_EXPERTS)), constant_values=NEG_BIG)
    tm_mg = 256
    lat_row_mg = lambda i: 1 + (i * tm_mg) // n_lat
    x1_c, h2, tidx_c, tw_c = _merge(attn_c, poolo_c, gates_c, x_prompt.reshape(n_ctx, D_MODEL), mods, ctx_row,
                                    b_branch_gate, woa, wop, wout, norm2_g, rw, rb, tm_mg, n_ctx + n_dec, 0, None)
    x1_l, h2, tidx_l, tw_l = _merge(attn_l, poolo_l, gates_l, x_sample.reshape(n_dec, D_MODEL), mods, lat_row_mg,
                                    b_branch_gate, woa, wop, wout, norm2_g, rw, rb, tm_mg, n_ctx + n_dec, n_ctx, h2)

    n_assign = (n_ctx + n_dec) * TOP_K
    max_chunks = (n_assign + N_EXPERTS * (MOE_CHUNK - 1)) // MOE_CHUNK
    cap = max_chunks * MOE_CHUNK
    n_items = (max_chunks + N_EXPERTS * (MOE_ITEM_CHUNKS - 1)) // MOE_ITEM_CHUNKS
    top_idx = jnp.concatenate([tidx_c[:, :TOP_K], tidx_l[:, :TOP_K]], axis=0)
    dest, fill, item_e, item_row0, item_nch = _routing_tables(top_idx, n_items)

    xs = _dispatch(dest, fill, h2, cap)
    ys = _moe_experts(item_e, item_row0, item_nch, xs,
                      w_exp_gate[0], b_exp_gate[0].reshape(N_EXPERTS * N_FF_TILES, FF_TILE),
                      w_exp_up[0], b_exp_up[0].reshape(N_EXPERTS * N_FF_TILES, FF_TILE),
                      w_exp_down[0], b_exp_down[0])

    y_c = _combine(dest[:n_ctx * TOP_K], ys, x1_c, tw_c, mods, lambda r: 0)
    y_l = _combine(dest[n_ctx * TOP_K:], ys, x1_l, tw_l, mods, lambda r: 1 + r // n_lat)

    return (y_c.reshape(batch, seq, D_MODEL),
            y_l.reshape(dec_batch, n_lat, D_MODEL),
            ckv_c.reshape(batch, 1, seq, KV_RANK),
            kr_c.reshape(batch, 1, seq, ROPE))
```

```python
import functools

import jax
import jax.numpy as jnp
from jax import lax
from jax.experimental import pallas as pl
from jax.experimental.pallas import tpu as pltpu

F32 = jnp.float32
BF16 = jnp.bfloat16
I32 = jnp.int32

D_MODEL = 2048
N_HEADS = 16
NOPE = 128
ROPE = 64
QK = NOPE + ROPE
V_DIM = 128
KV_RANK = 512
POOL_WINDOWS = (2, 4, 8, 16)
POOL_WIDTH = 1024
POOL_GROUP = POOL_WIDTH // len(POOL_WINDOWS)
N_EXPERTS = 32
TOP_K = 4
D_FF = 2048
SWIGLU_LIMIT = 7.0
SWIGLU_ALPHA = 1.702
ROPE_BASE = 10000.0
RMS_EPS = 1e-6
GRID_W = 64
ATTN_SCALE = QK ** -0.5

LANES = 128
HEAD_PAD = 2 * LANES
Q_COLS = N_HEADS * HEAD_PAD
IN_TILE = 1024
N_Q_TILES = Q_COLS // IN_TILE
POOL_TILE = N_Q_TILES
GATE_TILE0 = POOL_TILE + 1
N_GATE_TILES = 2 * D_MODEL // IN_TILE
KV_TILE = GATE_TILE0 + N_GATE_TILES
N_IN_TILES = KV_TILE + 1
HEADS_PER_TILE = IN_TILE // HEAD_PAD

MOE_CHUNK = 128
MOE_ITEM_CHUNKS = 16
MOE_ITEM_ROWS = MOE_CHUNK * MOE_ITEM_CHUNKS
FF_TILE = 256
N_FF_TILES = D_FF // FF_TILE
NEG_BIG = -1e30

VMEM_LIMIT = 56 * 1024 * 1024


def _cparams(sem, vmem=VMEM_LIMIT):
    return pltpu.CompilerParams(dimension_semantics=sem, vmem_limit_bytes=vmem)


def _dot(a, b):
    return jnp.dot(a, b, preferred_element_type=F32)


def _sigmoid(z):
    return 1.0 / (1.0 + jnp.exp(-z))


HALF = D_MODEL // 2
HI_MASK = -65536


def _pack_halves(v):
    lo = lax.bitcast_convert_type(v[:, :HALF].astype(BF16).astype(F32), I32)
    hi = lax.bitcast_convert_type(v[:, HALF:].astype(BF16).astype(F32), I32)
    return jnp.bitwise_or(jnp.bitwise_and(hi, HI_MASK), lax.shift_right_logical(lo, 16))


TOK_SUB = HALF // LANES


def _tok_rows(tok0, n):
    row0 = tok0 * TOK_SUB
    return pl.ds(row0 if isinstance(row0, int) else pl.multiple_of(row0, TOK_SUB), n * TOK_SUB)


def _tok_store(ref, tok0, n, words):
    for s in range(TOK_SUB):
        ref[pl.ds(tok0 * TOK_SUB + s, n, stride=TOK_SUB), :] = words[:, s * LANES:(s + 1) * LANES]


def _tok_load(ref, tok0, n):
    return jnp.concatenate([ref[pl.ds(tok0 * TOK_SUB + s, n, stride=TOK_SUB), :] for s in range(TOK_SUB)], axis=1)


def _unpack_halves(w):
    lo = lax.bitcast_convert_type(lax.shift_left(w, 16), F32)
    hi = lax.bitcast_convert_type(jnp.bitwise_and(w, HI_MASK), F32)
    return lo, hi


def _rope_lanes(y, cos, sin_lo, sin_hi):
    return y * cos + pltpu.roll(y, LANES - 16, axis=1) * sin_lo + pltpu.roll(y, 16, axis=1) * sin_hi


def _adaln_kernel(c_ref, w_ref, b_ref, o_ref):
    c = c_ref[...]
    s = (c * _sigmoid(c)).astype(BF16)
    o_ref[...] = _dot(s, w_ref[...].astype(BF16)) + b_ref[...]


def _adaln(cond, w_ada, b_ada):
    rows = cond.shape[0]
    n = w_ada.shape[1]
    tn = 1024
    return pl.pallas_call(
        _adaln_kernel,
        out_shape=jax.ShapeDtypeStruct((rows, n), F32),
        grid=(n // tn,),
        in_specs=[pl.BlockSpec((rows, D_MODEL), lambda j: (0, 0)),
                  pl.BlockSpec((D_MODEL, tn), lambda j: (0, j)),
                  pl.BlockSpec((1, tn), lambda j: (0, j))],
        out_specs=pl.BlockSpec((rows, tn), lambda j: (0, j)),
        compiler_params=_cparams(("arbitrary",)),
        name="adaln",
    )(cond, w_ada, b_ada)


W_Q_END = N_HEADS * QK
W_CKV_END = W_Q_END + KV_RANK
W_KR_END = W_CKV_END + ROPE
N_IN_COLS = W_KR_END + POOL_WIDTH + 2 * D_MODEL
W_PREP = 256


def _w_in_layout_kernel(wt_ref, o_ref):
    lane = lax.broadcasted_iota(I32, (W_PREP, W_PREP), 1)

    def move(src_row, dst_col, keep):
        sq = wt_ref[src_row:src_row + W_PREP, :].T
        if keep < W_PREP:
            sq = jnp.where(lane < keep, sq, 0.0)
        o_ref[:, dst_col:dst_col + W_PREP] = sq.astype(BF16)

    for h in range(N_HEADS):
        move(h * QK, h * HEAD_PAD, QK)
    for p in range((N_IN_COLS - W_KR_END) // W_PREP):
        move(W_KR_END + p * W_PREP, Q_COLS + p * W_PREP, W_PREP)
    kv0 = Q_COLS + N_IN_COLS - W_KR_END
    kv_cols = KV_RANK + ROPE
    for p in range(IN_TILE // W_PREP):
        keep = min(max(kv_cols - p * W_PREP, 0), W_PREP)
        if keep:
            move(W_Q_END + p * W_PREP, kv0 + p * W_PREP, keep)
        else:
            o_ref[:, kv0 + p * W_PREP:kv0 + (p + 1) * W_PREP] = jnp.zeros((W_PREP, W_PREP), BF16)


def _w_in_layout(wt):
    return pl.pallas_call(
        _w_in_layout_kernel,
        out_shape=jax.ShapeDtypeStruct((D_MODEL, N_IN_TILES * IN_TILE), BF16),
        grid=(D_MODEL // W_PREP,),
        in_specs=[pl.BlockSpec((N_IN_COLS, W_PREP), lambda i: (0, i))],
        out_specs=pl.BlockSpec((W_PREP, N_IN_TILES * IN_TILE), lambda i: (i, 0)),
        compiler_params=_cparams(("arbitrary",)),
        name="w_in_layout",
    )(wt)


def _in_proj_kernel(rope, x_ref, mods_ref, g1_ref, w_ref, qg_ref, kvg_ref, *rest):
    if rope:
        cos_ref, slo_ref, shi_ref = rest[:3]
        rest = rest[3:]
    q_ref, pool_ref, gates_ref, ckv_ref, kr_ref, krp_ref, h_scr = rest
    j = pl.program_id(1)

    @pl.when(j == 0)
    def _():
        x = x_ref[...]
        y = x * lax.rsqrt(jnp.mean(x * x, axis=-1, keepdims=True) + RMS_EPS) * g1_ref[...]
        h = y * (1.0 + mods_ref[1:2, :]) + mods_ref[0:1, :]
        h_scr[...] = h.astype(BF16)

    def proj(c0, c1):
        return _dot(h_scr[...], w_ref[:, c0:c1])

    @pl.when(j < N_Q_TILES)
    def _():
        for hh in range(HEADS_PER_TILE):
            a = proj(hh * HEAD_PAD, (hh + 1) * HEAD_PAD)
            r = lax.rsqrt(jnp.sum(a * a, axis=-1, keepdims=True) / QK + RMS_EPS)
            y = a * r * qg_ref[...]
            if rope:
                yr = _rope_lanes(y[:, LANES:], cos_ref[...], slo_ref[...], shi_ref[...])
                q_ref[:, hh * HEAD_PAD:hh * HEAD_PAD + LANES] = y[:, :LANES].astype(BF16)
                q_ref[:, hh * HEAD_PAD + LANES:(hh + 1) * HEAD_PAD] = yr.astype(BF16)
            else:
                q_ref[:, hh * HEAD_PAD:(hh + 1) * HEAD_PAD] = y.astype(BF16)

    @pl.when(j == POOL_TILE)
    def _():
        pool_ref[...] = proj(0, IN_TILE)

    @pl.when(jnp.logical_and(j >= GATE_TILE0, j < KV_TILE))
    def _():
        gates_ref[...] = proj(0, IN_TILE)

    @pl.when(j == KV_TILE)
    def _():
        a = proj(0, KV_RANK)
        r = lax.rsqrt(jnp.mean(a * a, axis=-1, keepdims=True) + RMS_EPS)
        ckv_ref[...] = a * r * kvg_ref[...]
        krp = proj(KV_RANK, KV_RANK + LANES)
        krp_ref[...] = krp
        kr_ref[...] = krp[:, :ROPE]


def _in_proj(x, mods, mod_row, g1, w_cat, qg, kvg, rope_tabs, tm):
    t = x.shape[0]
    rope = rope_tabs is not None
    in_specs = [
        pl.BlockSpec((tm, D_MODEL), lambda i, j: (i, 0)),
        pl.BlockSpec((None, 6, D_MODEL), lambda i, j: (mod_row(i), 0, 0)),
        pl.BlockSpec((1, D_MODEL), lambda i, j: (0, 0)),
        pl.BlockSpec((D_MODEL, IN_TILE), lambda i, j: (0, j)),
        pl.BlockSpec((1, HEAD_PAD), lambda i, j: (0, 0)),
        pl.BlockSpec((1, KV_RANK), lambda i, j: (0, 0)),
    ]
    args = [x, mods, g1, w_cat, qg, kvg]
    if rope:
        seq_tiles = rope_tabs[0].shape[0] // tm
        in_specs += [pl.BlockSpec((tm, LANES), lambda i, j: (i % seq_tiles, 0))] * 3
        args += list(rope_tabs)
    out_shape = (
        jax.ShapeDtypeStruct((t, Q_COLS), BF16),
        jax.ShapeDtypeStruct((t, POOL_WIDTH), F32),
        jax.ShapeDtypeStruct((t, 2 * D_MODEL), F32),
        jax.ShapeDtypeStruct((t, KV_RANK), F32),
        jax.ShapeDtypeStruct((t, ROPE), F32),
        jax.ShapeDtypeStruct((t, LANES), F32),
    )
    out_specs = (
        pl.BlockSpec((tm, IN_TILE), lambda i, j: (i, jnp.minimum(j, N_Q_TILES - 1))),
        pl.BlockSpec((tm, POOL_WIDTH), lambda i, j: (i, 0)),
        pl.BlockSpec((tm, IN_TILE), lambda i, j: (i, jnp.clip(j - GATE_TILE0, 0, N_GATE_TILES - 1))),
        pl.BlockSpec((tm, KV_RANK), lambda i, j: (i, 0)),
        pl.BlockSpec((tm, ROPE), lambda i, j: (i, 0)),
        pl.BlockSpec((tm, LANES), lambda i, j: (i, 0)),
    )
    return pl.pallas_call(
        functools.partial(_in_proj_kernel, rope),
        out_shape=out_shape,
        grid=(t // tm, N_IN_TILES),
        in_specs=in_specs,
        out_specs=out_specs,
        scratch_shapes=[pltpu.VMEM((tm, D_MODEL), BF16)],
        compiler_params=_cparams(("arbitrary", "arbitrary")),
        name="in_proj_rope" if rope else "in_proj",
    )(*args)


def _kv_expand_kernel(rope, ckv_ref, krp_ref, w_ref, kgn_ref, kgr_ref, *rest):
    if rope:
        cos_ref, slo_ref, shi_ref = rest[:3]
        rest = rest[3:]
    k_ref, v_ref = rest
    kv = _dot(ckv_ref[...].astype(BF16), w_ref[...])
    kr = krp_ref[...]
    ssq_r = jnp.sum(kr * kr, axis=-1, keepdims=True)
    krg = kr * kgr_ref[...]
    if rope:
        krg = _rope_lanes(krg, cos_ref[...], slo_ref[...], shi_ref[...])
    for h in range(N_HEADS):
        kn = kv[:, h * HEAD_PAD:h * HEAD_PAD + NOPE]
        r = lax.rsqrt((jnp.sum(kn * kn, axis=-1, keepdims=True) + ssq_r) / QK + RMS_EPS)
        k_ref[:, h * HEAD_PAD:h * HEAD_PAD + NOPE] = (kn * r * kgn_ref[...]).astype(BF16)
        k_ref[:, h * HEAD_PAD + NOPE:(h + 1) * HEAD_PAD] = (krg * r).astype(BF16)
        v_ref[:, h * V_DIM:(h + 1) * V_DIM] = kv[:, h * HEAD_PAD + NOPE:(h + 1) * HEAD_PAD].astype(BF16)


def _kv_expand(ckv_n, krp, w_ukv, kgn, kgr, rope_tabs, tr):
    r = ckv_n.shape[0]
    rope = rope_tabs is not None
    in_specs = [
        pl.BlockSpec((tr, KV_RANK), lambda i: (i, 0)),
        pl.BlockSpec((tr, LANES), lambda i: (i, 0)),
        pl.BlockSpec((KV_RANK, N_HEADS * HEAD_PAD), lambda i: (0, 0)),
        pl.BlockSpec((1, LANES), lambda i: (0, 0)),
        pl.BlockSpec((1, LANES), lambda i: (0, 0)),
    ]
    args = [ckv_n, krp, w_ukv, kgn, kgr]
    if rope:
        seq_tiles = rope_tabs[0].shape[0] // tr
        in_specs += [pl.BlockSpec((tr, LANES), lambda i: (i % seq_tiles, 0))] * 3
        args += list(rope_tabs)
    return pl.pallas_call(
        functools.partial(_kv_expand_kernel, rope),
        out_shape=(jax.ShapeDtypeStruct((r, N_HEADS * HEAD_PAD), BF16),
                   jax.ShapeDtypeStruct((r, N_HEADS * V_DIM), BF16)),
        grid=(r // tr,),
        in_specs=in_specs,
        out_specs=(pl.BlockSpec((tr, N_HEADS * HEAD_PAD), lambda i: (i, 0)),
                   pl.BlockSpec((tr, N_HEADS * V_DIM), lambda i: (i, 0))),
        compiler_params=_cparams(("arbitrary",)),
        name="kv_expand_rope" if rope else "kv_expand",
    )(*args)


def _qk(q, k):
    return lax.dot_general(q, k, (((1,), (1,)), ((), ())), preferred_element_type=F32)


def _attn_ctx_kernel(q_ref, k_ref, v_ref, o_ref):
    for h in range(N_HEADS):
        s = _qk(q_ref[:, h * HEAD_PAD:(h + 1) * HEAD_PAD], k_ref[:, h * HEAD_PAD:(h + 1) * HEAD_PAD])
        p = jnp.exp(s - jnp.max(s, axis=-1, keepdims=True))
        l = jnp.sum(p, axis=-1, keepdims=True)
        o = _dot(p.astype(BF16), v_ref[:, h * V_DIM:(h + 1) * V_DIM])
        o_ref[:, h * V_DIM:(h + 1) * V_DIM] = (o / l).astype(BF16)


def _attn_ctx(q, k, v, seq):
    t = q.shape[0]
    return pl.pallas_call(
        _attn_ctx_kernel,
        out_shape=jax.ShapeDtypeStruct((t, N_HEADS * V_DIM), BF16),
        grid=(t // seq,),
        in_specs=[pl.BlockSpec((seq, Q_COLS), lambda b: (b, 0)),
                  pl.BlockSpec((seq, Q_COLS), lambda b: (b, 0)),
                  pl.BlockSpec((seq, N_HEADS * V_DIM), lambda b: (b, 0))],
        out_specs=pl.BlockSpec((seq, N_HEADS * V_DIM), lambda b: (b, 0)),
        compiler_params=_cparams(("arbitrary",)),
        name="attn_ctx",
    )(q, k, v)


def _attn_lat_kernel(q_ref, k_ref, v_ref, kc_ref, vc_ref, o_ref):
    q = q_ref[...]
    s1 = _qk(q, k_ref[...])
    s2 = _qk(q, kc_ref[...])
    m = jnp.maximum(jnp.max(s1, axis=-1, keepdims=True), jnp.max(s2, axis=-1, keepdims=True))
    p1 = jnp.exp(s1 - m)
    p2 = jnp.exp(s2 - m)
    l = jnp.sum(p1, axis=-1, keepdims=True) + jnp.sum(p2, axis=-1, keepdims=True)
    o = _dot(p1.astype(BF16), v_ref[...]) + _dot(p2.astype(BF16), vc_ref[...])
    o_ref[...] = (o / l).astype(BF16)


def _attn_lat(q, k, v, kc, vc, seq, past, tq):
    t = q.shape[0]
    nq = seq // tq
    return pl.pallas_call(
        _attn_lat_kernel,
        out_shape=jax.ShapeDtypeStruct((t, N_HEADS * V_DIM), BF16),
        grid=(t // seq, N_HEADS, nq),
        in_specs=[pl.BlockSpec((tq, HEAD_PAD), lambda b, h, i: (b * nq + i, h)),
                  pl.BlockSpec((seq, HEAD_PAD), lambda b, h, i: (b, h)),
                  pl.BlockSpec((seq, V_DIM), lambda b, h, i: (b, h)),
                  pl.BlockSpec((past, HEAD_PAD), lambda b, h, i: (b, h)),
                  pl.BlockSpec((past, V_DIM), lambda b, h, i: (b, h))],
        out_specs=pl.BlockSpec((tq, V_DIM), lambda b, h, i: (b * nq + i, h)),
        compiler_params=_cparams(("arbitrary", "arbitrary", "arbitrary")),
        name="attn_lat",
    )(q, k, v, kc, vc)


POOL_HALO = 8


def _pool_kernel(seq, u_ref, w_ref, sc_ref, o_ref, pad_scr):
    zeros = jnp.zeros((POOL_HALO, POOL_WIDTH), F32)
    pad_scr[0:POOL_HALO, :] = zeros
    pad_scr[POOL_HALO + seq:2 * POOL_HALO + seq, :] = zeros
    pad_scr[POOL_HALO:POOL_HALO + seq, :] = u_ref[...]
    t = lax.broadcasted_iota(I32, (seq, 1), 0)
    for g, w in enumerate(POOL_WINDOWS):
        cols = slice(g * POOL_GROUP, (g + 1) * POOL_GROUP)
        tot = None
        for d in range(-(w // 2), w - w // 2):
            piece = pad_scr[POOL_HALO + d:POOL_HALO + d + seq, cols]
            tot = piece if tot is None else tot + piece
        cnt = (jnp.minimum(t + (w - w // 2), seq) - jnp.maximum(t - w // 2, 0)).astype(F32)
        mixed = tot / cnt - u_ref[:, cols]
        o_ref[:, cols] = (_dot(mixed.astype(BF16), w_ref[g]) * sc_ref[:, cols]).astype(BF16)


def _pool(u, w_pool, pool_scale, seq):
    t = u.shape[0]
    n_groups = len(POOL_WINDOWS)
    return pl.pallas_call(
        functools.partial(_pool_kernel, seq),
        out_shape=jax.ShapeDtypeStruct((t, POOL_WIDTH), BF16),
        grid=(t // seq,),
        in_specs=[pl.BlockSpec((seq, POOL_WIDTH), lambda b: (b, 0)),
                  pl.BlockSpec((n_groups, POOL_GROUP, POOL_GROUP), lambda b: (0, 0, 0)),
                  pl.BlockSpec((1, POOL_WIDTH), lambda b: (0, 0))],
        out_specs=pl.BlockSpec((seq, POOL_WIDTH), lambda b: (b, 0)),
        scratch_shapes=[pltpu.VMEM((seq + 2 * POOL_HALO, POOL_WIDTH), F32)],
        compiler_params=_cparams(("arbitrary",)),
        name="pool",
    )(u, w_pool, pool_scale)


def _merge_kernel(attn_ref, pool_ref, gates_ref, x_ref, mods_ref, bbg_ref, woa_ref, wop_ref, wout_ref,
                  g2_ref, rw_ref, rb_ref, *rest):
    x1_ref, h2_ref, tidx_ref, tw_ref = rest[-4:]
    a = _dot(attn_ref[...], woa_ref[...])
    p = _dot(pool_ref[...], wop_ref[...])
    ga = _sigmoid(gates_ref[:, :D_MODEL] + bbg_ref[:, :D_MODEL])
    gp = _sigmoid(gates_ref[:, D_MODEL:] + bbg_ref[:, D_MODEL:])
    merged = (ga * a + gp * p).astype(BF16)
    x1 = x_ref[...] + mods_ref[2:3, :] * _dot(merged, wout_ref[...])
    x1_ref[...] = x1
    y = x1 * lax.rsqrt(jnp.mean(x1 * x1, axis=-1, keepdims=True) + RMS_EPS) * g2_ref[...]
    h2 = y * (1.0 + mods_ref[4:5, :]) + mods_ref[3:4, :]
    _tok_store(h2_ref, 0, h2.shape[0], _pack_halves(h2))
    logits = _dot(h2.astype(BF16), rw_ref[...]) + rb_ref[...]
    lane = lax.broadcasted_iota(I32, logits.shape, 1).astype(F32)
    vals, idxs = [], []
    for _ in range(TOP_K):
        m = jnp.max(logits, axis=-1, keepdims=True)
        ix = jnp.min(jnp.where(logits == m, lane, float(LANES)), axis=-1, keepdims=True)
        vals.append(m)
        idxs.append(ix)
        logits = jnp.where(lane == ix, -jnp.inf, logits)
    es = [jnp.exp(v - vals[0]) for v in vals]
    tot = es[0] + es[1] + es[2] + es[3]
    tidx = jnp.zeros(logits.shape, F32)
    tw = jnp.zeros(logits.shape, F32)
    for k in range(TOP_K):
        tidx = jnp.where(lane == k, idxs[k], tidx)
        tw = jnp.where(lane == k, es[k] / tot, tw)
    tidx_ref[...] = tidx.astype(I32)
    tw_ref[...] = tw


def _merge(attn_o, pool_o, gates, x, mods, mod_row, bbg, woa, wop, wout, g2, rw, rb, tm, h2_rows, h2_row0, h2_prev):
    t = x.shape[0]
    const = lambda shape: pl.BlockSpec(shape, lambda i: (0, 0), pipeline_mode=pl.Buffered(1))
    h2_blk0 = h2_row0 // tm
    extra_specs = [] if h2_prev is None else [pl.BlockSpec(memory_space=pl.ANY)]
    extra_args = [] if h2_prev is None else [h2_prev]
    return pl.pallas_call(
        _merge_kernel,
        out_shape=(jax.ShapeDtypeStruct((t, D_MODEL), F32),
                   jax.ShapeDtypeStruct((h2_rows * TOK_SUB, LANES), I32),
                   jax.ShapeDtypeStruct((t, LANES), I32),
                   jax.ShapeDtypeStruct((t, LANES), F32)),
        grid=(t // tm,),
        input_output_aliases={} if h2_prev is None else {12: 1},
        in_specs=[pl.BlockSpec((tm, D_MODEL), lambda i: (i, 0)),
                  pl.BlockSpec((tm, POOL_WIDTH), lambda i: (i, 0)),
                  pl.BlockSpec((tm, 2 * D_MODEL), lambda i: (i, 0)),
                  pl.BlockSpec((tm, D_MODEL), lambda i: (i, 0)),
                  pl.BlockSpec((None, 6, D_MODEL), lambda i: (mod_row(i), 0, 0)),
                  const((1, 2 * D_MODEL)),
                  const((D_MODEL, D_MODEL)),
                  const((POOL_WIDTH, D_MODEL)),
                  const((D_MODEL, D_MODEL)),
                  const((1, D_MODEL)),
                  const((D_MODEL, LANES)),
                  const((1, LANES))] + extra_specs,
        out_specs=(pl.BlockSpec((tm, D_MODEL), lambda i: (i, 0)),
                   pl.BlockSpec((tm * TOK_SUB, LANES), lambda i: (i + h2_blk0, 0)),
                   pl.BlockSpec((tm, LANES), lambda i: (i, 0)),
                   pl.BlockSpec((tm, LANES), lambda i: (i, 0))),
        compiler_params=_cparams(("arbitrary",)),
        name="merge",
    )(attn_o, pool_o, gates, x, mods, bbg, woa, wop, wout, g2, rw, rb, *extra_args)


DISPATCH_ROWS = 256
DMA_UNROLL = 8


def _dispatch_kernel(dest_ref, fill_ref, h_ref, xs_ref, zero_scr, sem):
    base = pl.program_id(0) * (DISPATCH_ROWS * TOP_K)

    @pl.when(pl.program_id(0) == 0)
    def _():
        zero_scr[...] = jnp.zeros(zero_scr.shape, I32)

        def fill_copy(slot):
            return pltpu.make_async_copy(zero_scr, xs_ref.at[_tok_rows(slot, 1), :], sem)

        def fill(e, carry):
            def one(r, c):
                fill_copy(fill_ref[e] + r).start()
                return c

            return lax.fori_loop(0, fill_ref[N_EXPERTS + e], one, carry)

        lax.fori_loop(0, N_EXPERTS, fill, 0)

        def drain(e, carry):
            def one(r, c):
                fill_copy(0).wait()
                return c

            return lax.fori_loop(0, fill_ref[N_EXPERTS + e], one, carry)

        lax.fori_loop(0, N_EXPERTS, drain, 0)

    def issue(t, carry):
        for k in range(TOP_K):
            pltpu.make_async_copy(h_ref.at[_tok_rows(t, 1), :],
                                  xs_ref.at[_tok_rows(dest_ref[base + t * TOP_K + k], 1), :], sem).start(priority=k % 2)
        return carry

    lax.fori_loop(0, DISPATCH_ROWS, issue, 0, unroll=DMA_UNROLL // TOP_K)
    for _ in range(TOP_K):
        pltpu.make_async_copy(h_ref, xs_ref.at[_tok_rows(0, DISPATCH_ROWS), :], sem).wait()


def _dispatch(dest, fill, h2, cap):
    t = h2.shape[0] // TOK_SUB
    return pl.pallas_call(
        _dispatch_kernel,
        out_shape=jax.ShapeDtypeStruct((cap * TOK_SUB, LANES), I32),
        grid_spec=pltpu.PrefetchScalarGridSpec(
            num_scalar_prefetch=2,
            grid=(t // DISPATCH_ROWS,),
            in_specs=[pl.BlockSpec((DISPATCH_ROWS * TOK_SUB, LANES), lambda i, d, z: (i, 0))],
            out_specs=pl.BlockSpec(memory_space=pl.ANY),
            scratch_shapes=[pltpu.VMEM((TOK_SUB, LANES), I32), pltpu.SemaphoreType.DMA]),
        compiler_params=_cparams(("arbitrary",)),
        name="moe_dispatch",
    )(dest, fill, h2)


MOE_SPAN = 8


def _moe_kernel(ie_ref, ir_ref, in_ref, xs_ref, wg_hbm, bg_ref, wu_hbm, bu_ref, wd_hbm, bd_ref, ys_ref,
                xin_scr, acc_scr, wgf_scr, wuf_scr, wdf_scr, wg_scr, wu_scr, wd_scr, sem_x, sem_y, sem_w):
    i = pl.program_id(0)
    n_items = pl.num_programs(0)
    nch = in_ref[i]
    row0 = ir_ref[i]
    expert = ie_ref[i]
    slot = i % 2
    down_row = pl.ds(expert, 1)

    def w_copies(e, f, ws):
        col0 = f * FF_TILE
        cols = pl.ds(col0 if isinstance(col0, int) else pl.multiple_of(col0, FF_TILE), FF_TILE)
        return (pltpu.make_async_copy(wg_hbm.at[e, :, cols], wgf_scr.at[ws], sem_w.at[ws]),
                pltpu.make_async_copy(wu_hbm.at[e, :, cols], wuf_scr.at[ws], sem_w.at[ws]),
                pltpu.make_async_copy(wd_hbm.at[e, cols, :], wdf_scr.at[ws], sem_w.at[ws]))

    def rows_of(c, k=1):
        return pl.ds(pl.multiple_of(c * MOE_CHUNK, MOE_CHUNK), k * MOE_CHUNK)

    def x_copy(item_row0, c, s):
        return pltpu.make_async_copy(xs_ref.at[_tok_rows(item_row0 + c * MOE_CHUNK, MOE_CHUNK), :],
                                     xin_scr.at[s, _tok_rows(c * MOE_CHUNK, MOE_CHUNK), :], sem_x.at[s])

    def y_copy(item_row0, c, s):
        return pltpu.make_async_copy(xin_scr.at[s, _tok_rows(c * MOE_CHUNK, MOE_CHUNK), :],
                                     ys_ref.at[_tok_rows(item_row0 + c * MOE_CHUNK, MOE_CHUNK), :], sem_y)

    def for_chunks(n, body):
        def step(c, carry):
            body(c)
            return carry

        lax.fori_loop(0, n, step, 0)

    def for_spans(n, body, first_span):
        def span(g, carry):
            body(g * MOE_SPAN, MOE_SPAN)
            return carry

        lax.fori_loop(first_span, n // MOE_SPAN, span, 0)
        k = MOE_SPAN // 2
        while k >= 1:
            start = n // (2 * k) * (2 * k)

            @pl.when(n % (2 * k) >= k)
            def _(start=start, k=k):
                body(start, k)

            k //= 2

    @pl.when(i > 0)
    def _():
        prev = jnp.maximum(i - 1, 0)
        for_chunks(in_ref[prev], lambda c: y_copy(ir_ref[prev], c, 1 - slot).wait())

    @pl.when(i == 0)
    def _():
        for_chunks(nch, lambda c: x_copy(row0, c, slot).start())
        for cp in w_copies(expert, 0, 0):
            cp.start()

    @pl.when(i + 1 < n_items)
    def _():
        nxt = jnp.minimum(i + 1, n_items - 1)
        for_chunks(in_ref[nxt], lambda c: x_copy(ir_ref[nxt], c, 1 - slot).start())

    for_chunks(nch, lambda c: x_copy(row0, c, slot).wait())

    @pl.when(nch > 0)
    def _():
        def cast_weights(ws):
            w = (wgf_scr[ws].astype(BF16), wuf_scr[ws].astype(BF16), wdf_scr[ws].astype(BF16))
            wg_scr[...], wu_scr[...], wd_scr[...] = w
            return w

        def contribution(f, c, k, w):
            wg, wu, wd = (wg_scr[...], wu_scr[...], wd_scr[...]) if w is None else w
            bias_row = pl.ds(expert * N_FF_TILES + f, 1)
            lo, hi = _unpack_halves(_tok_load(xin_scr.at[slot], c * MOE_CHUNK, k * MOE_CHUNK))
            x = jnp.concatenate([lo.astype(BF16), hi.astype(BF16)], axis=1)
            gt = jnp.minimum(_dot(x, wg) + bg_ref[bias_row, :], SWIGLU_LIMIT)
            up = jnp.clip(_dot(x, wu) + bu_ref[bias_row, :], -SWIGLU_LIMIT, SWIGLU_LIMIT)
            act = (up + 1.0) * (gt * _sigmoid(SWIGLU_ALPHA * gt))
            return _dot(act.astype(BF16), wd)

        def first(f, c, k, w=None):
            acc_scr[rows_of(c, k), :] = contribution(f, c, k, w) + bd_ref[down_row, :]

        def middle(f, c, k, w=None):
            acc_scr[rows_of(c, k), :] += contribution(f, c, k, w)

        def last(f, c, k, w=None):
            y = acc_scr[rows_of(c, k), :] + contribution(f, c, k, w)
            _tok_store(xin_scr.at[slot], c * MOE_CHUNK, k * MOE_CHUNK, _pack_halves(y))
            for u in range(k):
                y_copy(row0, c + u, slot).start()

        def ff_step(f, phase, request_next):
            ws = f % 2
            for cp in w_copies(expert, f, ws):
                cp.wait()
            request_next()
            body = functools.partial(phase, f)

            @pl.when(nch >= MOE_SPAN)
            def _():
                body(0, MOE_SPAN, cast_weights(ws))

            @pl.when(nch < MOE_SPAN)
            def _():
                cast_weights(ws)

            for_spans(nch, body, 1)

        def request_tile(f):
            def go():
                for cp in w_copies(expert, f, f % 2):
                    cp.start()
            return go

        def request_next_item():
            nxt = jnp.minimum(i + 1, n_items - 1)

            @pl.when(jnp.logical_and(i + 1 < n_items, in_ref[nxt] > 0))
            def _():
                for cp in w_copies(ie_ref[nxt], 0, 0):
                    cp.start()

        ff_step(0, first, request_tile(1))

        def mid(f, carry):
            ff_step(f, middle, request_tile(f + 1))
            return carry

        lax.fori_loop(1, N_FF_TILES - 1, mid, 0)
        ff_step(N_FF_TILES - 1, last, request_next_item)

        @pl.when(i == n_items - 1)
        def _():
            for_chunks(nch, lambda c: y_copy(row0, c, slot).wait())


def _moe_experts(item_e, item_row0, item_nch, xs, w_g, b_g, w_u, b_u, w_d, b_d):
    n_items = item_e.shape[0]
    whole = lambda a: pl.BlockSpec(a.shape, lambda i, ie, ir, inch: (0, 0))
    return pl.pallas_call(
        _moe_kernel,
        out_shape=jax.ShapeDtypeStruct(xs.shape, I32),
        grid_spec=pltpu.PrefetchScalarGridSpec(
            num_scalar_prefetch=3,
            grid=(n_items,),
            in_specs=[
                pl.BlockSpec(memory_space=pl.ANY),
                pl.BlockSpec(memory_space=pl.ANY), whole(b_g),
                pl.BlockSpec(memory_space=pl.ANY), whole(b_u),
                pl.BlockSpec(memory_space=pl.ANY), whole(b_d),
            ],
            out_specs=pl.BlockSpec(memory_space=pl.ANY),
            scratch_shapes=[
                pltpu.VMEM((2, MOE_ITEM_ROWS * TOK_SUB, LANES), I32),
                pltpu.VMEM((MOE_ITEM_ROWS, D_MODEL), F32),
                pltpu.VMEM((2, D_MODEL, FF_TILE), F32),
                pltpu.VMEM((2, D_MODEL, FF_TILE), F32),
                pltpu.VMEM((2, FF_TILE, D_MODEL), F32),
                pltpu.VMEM((D_MODEL, FF_TILE), BF16),
                pltpu.VMEM((D_MODEL, FF_TILE), BF16),
                pltpu.VMEM((FF_TILE, D_MODEL), BF16),
                pltpu.SemaphoreType.DMA((2,)),
                pltpu.SemaphoreType.DMA,
                pltpu.SemaphoreType.DMA((2,)),
            ]),
        compiler_params=_cparams(("arbitrary",)),
        name="moe_experts",
    )(item_e, item_row0, item_nch, xs, w_g, b_g, w_u, b_u, w_d, b_d)


COMBINE_ROWS = 256


def _combine_kernel(dest_ref, ys_ref, x1_ref, tw_ref, mods_ref, o_ref, ybuf, sem):
    step = pl.program_id(0)
    slot = step % 2

    def gather(blk, s):
        base = blk * (COMBINE_ROWS * TOP_K)

        def issue(t, carry):
            for k in range(TOP_K):
                pltpu.make_async_copy(ys_ref.at[_tok_rows(dest_ref[base + t * TOP_K + k], 1), :],
                                      ybuf.at[s, k, _tok_rows(t, 1), :], sem.at[s]).start(priority=k % 2)
            return carry

        lax.fori_loop(0, COMBINE_ROWS, issue, 0, unroll=DMA_UNROLL // TOP_K)

    @pl.when(step == 0)
    def _():
        gather(step, slot)

    @pl.when(step + 1 < pl.num_programs(0))
    def _():
        gather(step + 1, 1 - slot)

    for k in range(TOP_K):
        pltpu.make_async_copy(ys_ref.at[_tok_rows(0, COMBINE_ROWS), :], ybuf.at[slot, k], sem.at[slot]).wait()

    y_lo = y_hi = None
    for k in range(TOP_K):
        lo, hi = _unpack_halves(_tok_load(ybuf.at[slot, k], 0, COMBINE_ROWS))
        w = tw_ref[:, k:k + 1]
        y_lo = w * lo if y_lo is None else y_lo + w * lo
        y_hi = w * hi if y_hi is None else y_hi + w * hi
    o_ref[:, :HALF] = x1_ref[:, :HALF] + mods_ref[5:6, :HALF] * y_lo
    o_ref[:, HALF:] = x1_ref[:, HALF:] + mods_ref[5:6, HALF:] * y_hi


def _combine(dest, ys, x1, tw, mods, mod_row):
    t = x1.shape[0]
    return pl.pallas_call(
        _combine_kernel,
        out_shape=jax.ShapeDtypeStruct((t, D_MODEL), F32),
        grid_spec=pltpu.PrefetchScalarGridSpec(
            num_scalar_prefetch=1,
            grid=(t // COMBINE_ROWS,),
            in_specs=[pl.BlockSpec(memory_space=pl.ANY),
                      pl.BlockSpec((COMBINE_ROWS, D_MODEL), lambda i, d: (i, 0)),
                      pl.BlockSpec((COMBINE_ROWS, LANES), lambda i, d: (i, 0)),
                      pl.BlockSpec((None, 6, D_MODEL), lambda i, d: (mod_row(i * COMBINE_ROWS), 0, 0))],
            out_specs=pl.BlockSpec((COMBINE_ROWS, D_MODEL), lambda i, d: (i, 0)),
            scratch_shapes=[pltpu.VMEM((2, TOP_K, COMBINE_ROWS * TOK_SUB, LANES), I32),
                            pltpu.SemaphoreType.DMA((2,))]),
        compiler_params=_cparams(("arbitrary",)),
        name="moe_combine",
    )(dest, ys, x1, tw, mods)


def _routing_tables(top_idx, n_items):
    flat_e = top_idx.reshape(-1)
    onehot = (flat_e[:, None] == jnp.arange(N_EXPERTS, dtype=I32)[None, :]).astype(I32)
    csum = jnp.cumsum(onehot, axis=0)
    rank = jnp.take_along_axis(csum, flat_e[:, None], axis=1)[:, 0] - 1
    counts = csum[-1]
    nch = (counts + MOE_CHUNK - 1) // MOE_CHUNK
    pad_rows = nch * MOE_CHUNK
    pad_end = jnp.cumsum(pad_rows)
    pad_start = pad_end - pad_rows
    dest = (pad_start[flat_e] + rank).astype(I32)
    items_e = (nch + MOE_ITEM_CHUNKS - 1) // MOE_ITEM_CHUNKS
    item_end = jnp.cumsum(items_e)
    item_start = item_end - items_e
    total = item_end[-1]
    i = jnp.arange(n_items, dtype=I32)
    ii = jnp.minimum(i, total - 1)
    e_i = jnp.minimum(jnp.searchsorted(item_end, ii, side="right"), N_EXPERTS - 1).astype(I32)
    local = ii - item_start[e_i]
    row0 = (pad_start[e_i] + local * MOE_ITEM_ROWS).astype(I32)
    n_i = jnp.where(i < total, jnp.minimum(MOE_ITEM_CHUNKS, nch[e_i] - local * MOE_ITEM_CHUNKS), 0).astype(I32)
    fill = jnp.concatenate([pad_start + counts, pad_rows - counts]).astype(I32)
    return dest, fill, e_i, row0, n_i


def _rope_tables(n_lat):
    nf = ROPE // 4
    inv = ROPE_BASE ** (-jnp.arange(nf, dtype=F32) / nf)
    t = jnp.arange(n_lat)
    row = (t // GRID_W).astype(F32)
    col = (t % GRID_W).astype(F32)
    ang_r = row[:, None] * inv[None, :]
    ang_c = col[:, None] * inv[None, :]
    z = jnp.zeros((n_lat, nf), F32)
    tail = jnp.zeros((n_lat, LANES - ROPE), F32)
    cos = jnp.concatenate([jnp.cos(ang_r), jnp.cos(ang_r), jnp.cos(ang_c), jnp.cos(ang_c), tail], axis=1)
    sin_lo = jnp.concatenate([-jnp.sin(ang_r), z, -jnp.sin(ang_c), z, tail], axis=1)
    sin_hi = jnp.concatenate([z, jnp.sin(ang_r), z, jnp.sin(ang_c), tail], axis=1)
    return cos, sin_lo, sin_hi


def kernel(x_prompt, x_sample, cache_ckv, cache_krope, c, c_ctx, norm1_g, norm2_g, w_ada, b_ada, w_in,
           b_branch_gate, kv_norm_g, w_ukv, q_norm_g, k_norm_g, w_o_attn, w_pool, pool_scale, w_o_pool, w_out,
           router_w, router_b, w_exp_gate, b_exp_gate, w_exp_up, b_exp_up, w_exp_down, b_exp_down):
    assert w_in.shape[0] == 1, "single-layer trunk"
    batch, seq, _ = x_prompt.shape
    dec_batch, n_lat, _ = x_sample.shape
    past = cache_ckv.shape[2]
    n_ctx = batch * seq
    n_dec = dec_batch * n_lat

    cond = jnp.concatenate([c_ctx[None, :], c, jnp.zeros((8 - 1 - dec_batch, D_MODEL), F32)], axis=0)
    mods = _adaln(cond, w_ada[0], b_ada).reshape(8, 6, D_MODEL)

    w_cat = _w_in_layout(w_in[0].T)
    qg = jnp.pad(q_norm_g * ATTN_SCALE, ((0, 0), (0, HEAD_PAD - QK)))
    kgn = k_norm_g[:, :NOPE]
    kgr = jnp.pad(k_norm_g[:, NOPE:], ((0, 0), (0, LANES - ROPE)))
    w_ukv_b = w_ukv[0].astype(BF16)
    rope_tabs = _rope_tables(n_lat)

    ctx_row = lambda i: 0
    tm_in = 512
    lat_row_in = lambda i: 1 + (i * tm_in) // n_lat
    q_c, pool_c, gates_c, ckv_c, kr_c, krp_c = _in_proj(
        x_prompt.reshape(n_ctx, D_MODEL), mods, ctx_row, norm1_g, w_cat, qg, kv_norm_g, None, tm_in)
    q_l, pool_l, gates_l, ckv_l, _, krp_l = _in_proj(
        x_sample.reshape(n_dec, D_MODEL), mods, lat_row_in, norm1_g, w_cat, qg, kv_norm_g, rope_tabs, tm_in)

    tr = 256
    k_c, v_c = _kv_expand(ckv_c, krp_c, w_ukv_b, kgn, kgr, None, tr)
    k_l, v_l = _kv_expand(ckv_l, krp_l, w_ukv_b, kgn, kgr, rope_tabs, tr)
    cache_krp = jnp.pad(cache_krope.reshape(dec_batch * past, ROPE), ((0, 0), (0, LANES - ROPE)))
    k_p, v_p = _kv_expand(cache_ckv.reshape(dec_batch * past, KV_RANK), cache_krp, w_ukv_b, kgn, kgr, None, tr)

    attn_c = _attn_ctx(q_c, k_c, v_c, seq)
    attn_l = _attn_lat(q_l, k_l, v_l, k_p, v_p, n_lat, past, 512)

    w_pool_b = w_pool[0].astype(BF16)
    poolo_c = _pool(pool_c, w_pool_b, pool_scale, seq)
    poolo_l = _pool(pool_l, w_pool_b, pool_scale, n_lat)

    woa = w_o_attn[0].astype(BF16)
    wop = w_o_pool[0].astype(BF16)
    wout = w_out[0].astype(BF16)
    rw = jnp.pad(router_w[0], ((0, 0), (0, LANES - N_EXPERTS))).astype(BF16)
    rb = jnp.pad(router_b, ((0, 0), (0, LANES - N_EXPERTS)), constant_values=NEG_BIG)
    tm_mg = 256
    lat_row_mg = lambda i: 1 + (i * tm_mg) // n_lat
    x1_c, h2, tidx_c, tw_c = _merge(attn_c, poolo_c, gates_c, x_prompt.reshape(n_ctx, D_MODEL), mods, ctx_row,
                                    b_branch_gate, woa, wop, wout, norm2_g, rw, rb, tm_mg, n_ctx + n_dec, 0, None)
    x1_l, h2, tidx_l, tw_l = _merge(attn_l, poolo_l, gates_l, x_sample.reshape(n_dec, D_MODEL), mods, lat_row_mg,
                                    b_branch_gate, woa, wop, wout, norm2_g, rw, rb, tm_mg, n_ctx + n_dec, n_ctx, h2)

    n_assign = (n_ctx + n_dec) * TOP_K
    max_chunks = (n_assign + N_EXPERTS * (MOE_CHUNK - 1)) // MOE_CHUNK
    cap = max_chunks * MOE_CHUNK
    n_items = (max_chunks + N_EXPERTS * (MOE_ITEM_CHUNKS - 1)) // MOE_ITEM_CHUNKS
    top_idx = jnp.concatenate([tidx_c[:, :TOP_K], tidx_l[:, :TOP_K]], axis=0)
    dest, fill, item_e, item_row0, item_nch = _routing_tables(top_idx, n_items)

    xs = _dispatch(dest, fill, h2, cap)
    ys = _moe_experts(item_e, item_row0, item_nch, xs,
                      w_exp_gate[0], b_exp_gate[0].reshape(N_EXPERTS * N_FF_TILES, FF_TILE),
                      w_exp_up[0], b_exp_up[0].reshape(N_EXPERTS * N_FF_TILES, FF_TILE),
                      w_exp_down[0], b_exp_down[0])

    y_c = _combine(dest[:n_ctx * TOP_K], ys, x1_c, tw_c, mods, lambda r: 0)
    y_l = _combine(dest[n_ctx * TOP_K:], ys, x1_l, tw_l, mods, lambda r: 1 + r // n_lat)

    return (y_c.reshape(batch, seq, D_MODEL),
            y_l.reshape(dec_batch, n_lat, D_MODEL),
            ckv_c.reshape(batch, 1, seq, KV_RANK),
            kr_c.reshape(batch, 1, seq, ROPE))
```

```python
import functools

import jax
import jax.numpy as jnp
from jax import lax
from jax.experimental import pallas as pl
from jax.experimental.pallas import tpu as pltpu

F32 = jnp.float32
BF16 = jnp.bfloat16
I32 = jnp.int32

D_MODEL = 2048
N_HEADS = 16
NOPE = 128
ROPE = 64
QK = NOPE + ROPE
V_DIM = 128
KV_RANK = 512
POOL_WINDOWS = (2, 4, 8, 16)
POOL_WIDTH = 1024
POOL_GROUP = POOL_WIDTH // len(POOL_WINDOWS)
N_EXPERTS = 32
TOP_K = 4
D_FF = 2048
SWIGLU_LIMIT = 7.0
SWIGLU_ALPHA = 1.702
ROPE_BASE = 10000.0
RMS_EPS = 1e-6
GRID_W = 64
ATTN_SCALE = QK ** -0.5

LANES = 128
HEAD_PAD = 2 * LANES
Q_COLS = N_HEADS * HEAD_PAD
IN_TILE = 1024
N_Q_TILES = Q_COLS // IN_TILE
POOL_TILE = N_Q_TILES
GATE_TILE0 = POOL_TILE + 1
N_GATE_TILES = 2 * D_MODEL // IN_TILE
KV_TILE = GATE_TILE0 + N_GATE_TILES
N_IN_TILES = KV_TILE + 1
HEADS_PER_TILE = IN_TILE // HEAD_PAD

MOE_CHUNK = 128
MOE_ITEM_CHUNKS = 16
MOE_ITEM_ROWS = MOE_CHUNK * MOE_ITEM_CHUNKS
FF_TILE = 256
N_FF_TILES = D_FF // FF_TILE
NEG_BIG = -1e30

VMEM_LIMIT = 56 * 1024 * 1024


def _cparams(sem, vmem=VMEM_LIMIT):
    return pltpu.CompilerParams(dimension_semantics=sem, vmem_limit_bytes=vmem)


def _dot(a, b):
    return jnp.dot(a, b, preferred_element_type=F32)


def _sigmoid(z):
    return 1.0 / (1.0 + jnp.exp(-z))


HALF = D_MODEL // 2
HI_MASK = -65536


def _pack_halves(v):
    lo = lax.bitcast_convert_type(v[:, :HALF].astype(BF16).astype(F32), I32)
    hi = lax.bitcast_convert_type(v[:, HALF:].astype(BF16).astype(F32), I32)
    return jnp.bitwise_or(jnp.bitwise_and(hi, HI_MASK), lax.shift_right_logical(lo, 16))


TOK_SUB = HALF // LANES


def _tok_rows(tok0, n):
    row0 = tok0 * TOK_SUB
    return pl.ds(row0 if isinstance(row0, int) else pl.multiple_of(row0, TOK_SUB), n * TOK_SUB)


def _tok_store(ref, tok0, n, words):
    for s in range(TOK_SUB):
        ref[pl.ds(tok0 * TOK_SUB + s, n, stride=TOK_SUB), :] = words[:, s * LANES:(s + 1) * LANES]


def _tok_load(ref, tok0, n):
    return jnp.concatenate([ref[pl.ds(tok0 * TOK_SUB + s, n, stride=TOK_SUB), :] for s in range(TOK_SUB)], axis=1)


def _unpack_halves(w):
    lo = lax.bitcast_convert_type(lax.shift_left(w, 16), F32)
    hi = lax.bitcast_convert_type(jnp.bitwise_and(w, HI_MASK), F32)
    return lo, hi


def _rope_lanes(y, cos, sin_lo, sin_hi):
    return y * cos + pltpu.roll(y, LANES - 16, axis=1) * sin_lo + pltpu.roll(y, 16, axis=1) * sin_hi


def _adaln_kernel(c_ref, w_ref, b_ref, o_ref):
    c = c_ref[...]
    s = (c * _sigmoid(c)).astype(BF16)
    o_ref[...] = _dot(s, w_ref[...].astype(BF16)) + b_ref[...]


def _adaln(cond, w_ada, b_ada):
    rows = cond.shape[0]
    n = w_ada.shape[1]
    tn = 1024
    return pl.pallas_call(
        _adaln_kernel,
        out_shape=jax.ShapeDtypeStruct((rows, n), F32),
        grid=(n // tn,),
        in_specs=[pl.BlockSpec((rows, D_MODEL), lambda j: (0, 0)),
                  pl.BlockSpec((D_MODEL, tn), lambda j: (0, j)),
                  pl.BlockSpec((1, tn), lambda j: (0, j))],
        out_specs=pl.BlockSpec((rows, tn), lambda j: (0, j)),
        compiler_params=_cparams(("arbitrary",)),
        name="adaln",
    )(cond, w_ada, b_ada)


W_Q_END = N_HEADS * QK
W_CKV_END = W_Q_END + KV_RANK
W_KR_END = W_CKV_END + ROPE
N_IN_COLS = W_KR_END + POOL_WIDTH + 2 * D_MODEL
W_PREP = 256


def _w_in_layout_kernel(wt_ref, o_ref):
    lane = lax.broadcasted_iota(I32, (W_PREP, W_PREP), 1)

    def move(src_row, dst_col, keep):
        sq = wt_ref[src_row:src_row + W_PREP, :].T
        if keep < W_PREP:
            sq = jnp.where(lane < keep, sq, 0.0)
        o_ref[:, dst_col:dst_col + W_PREP] = sq.astype(BF16)

    for h in range(N_HEADS):
        move(h * QK, h * HEAD_PAD, QK)
    for p in range((N_IN_COLS - W_KR_END) // W_PREP):
        move(W_KR_END + p * W_PREP, Q_COLS + p * W_PREP, W_PREP)
    kv0 = Q_COLS + N_IN_COLS - W_KR_END
    kv_cols = KV_RANK + ROPE
    for p in range(IN_TILE // W_PREP):
        keep = min(max(kv_cols - p * W_PREP, 0), W_PREP)
        if keep:
            move(W_Q_END + p * W_PREP, kv0 + p * W_PREP, keep)
        else:
            o_ref[:, kv0 + p * W_PREP:kv0 + (p + 1) * W_PREP] = jnp.zeros((W_PREP, W_PREP), BF16)


def _w_in_layout(wt):
    return pl.pallas_call(
        _w_in_layout_kernel,
        out_shape=jax.ShapeDtypeStruct((D_MODEL, N_IN_TILES * IN_TILE), BF16),
        grid=(D_MODEL // W_PREP,),
        in_specs=[pl.BlockSpec((N_IN_COLS, W_PREP), lambda i: (0, i))],
        out_specs=pl.BlockSpec((W_PREP, N_IN_TILES * IN_TILE), lambda i: (i, 0)),
        compiler_params=_cparams(("arbitrary",)),
        name="w_in_layout",
    )(wt)


def _in_proj_kernel(rope, x_ref, mods_ref, g1_ref, w_ref, qg_ref, kvg_ref, *rest):
    if rope:
        cos_ref, slo_ref, shi_ref = rest[:3]
        rest = rest[3:]
    q_ref, pool_ref, gates_ref, ckv_ref, kr_ref, krp_ref, h_scr = rest
    j = pl.program_id(1)

    @pl.when(j == 0)
    def _():
        x = x_ref[...]
        y = x * lax.rsqrt(jnp.mean(x * x, axis=-1, keepdims=True) + RMS_EPS) * g1_ref[...]
        h = y * (1.0 + mods_ref[1:2, :]) + mods_ref[0:1, :]
        h_scr[...] = h.astype(BF16)

    def proj(c0, c1):
        return _dot(h_scr[...], w_ref[:, c0:c1])

    @pl.when(j < N_Q_TILES)
    def _():
        for hh in range(HEADS_PER_TILE):
            a = proj(hh * HEAD_PAD, (hh + 1) * HEAD_PAD)
            r = lax.rsqrt(jnp.sum(a * a, axis=-1, keepdims=True) / QK + RMS_EPS)
            y = a * r * qg_ref[...]
            if rope:
                yr = _rope_lanes(y[:, LANES:], cos_ref[...], slo_ref[...], shi_ref[...])
                q_ref[:, hh * HEAD_PAD:hh * HEAD_PAD + LANES] = y[:, :LANES].astype(BF16)
                q_ref[:, hh * HEAD_PAD + LANES:(hh + 1) * HEAD_PAD] = yr.astype(BF16)
            else:
                q_ref[:, hh * HEAD_PAD:(hh + 1) * HEAD_PAD] = y.astype(BF16)

    @pl.when(j == POOL_TILE)
    def _():
        pool_ref[...] = proj(0, IN_TILE)

    @pl.when(jnp.logical_and(j >= GATE_TILE0, j < KV_TILE))
    def _():
        gates_ref[...] = proj(0, IN_TILE)

    @pl.when(j == KV_TILE)
    def _():
        a = proj(0, KV_RANK)
        r = lax.rsqrt(jnp.mean(a * a, axis=-1, keepdims=True) + RMS_EPS)
        ckv_ref[...] = a * r * kvg_ref[...]
        krp = proj(KV_RANK, KV_RANK + LANES)
        krp_ref[...] = krp
        kr_ref[...] = krp[:, :ROPE]


def _in_proj(x, mods, mod_row, g1, w_cat, qg, kvg, rope_tabs, tm):
    t = x.shape[0]
    rope = rope_tabs is not None
    in_specs = [
        pl.BlockSpec((tm, D_MODEL), lambda i, j: (i, 0)),
        pl.BlockSpec((None, 6, D_MODEL), lambda i, j: (mod_row(i), 0, 0)),
        pl.BlockSpec((1, D_MODEL), lambda i, j: (0, 0)),
        pl.BlockSpec((D_MODEL, IN_TILE), lambda i, j: (0, j)),
        pl.BlockSpec((1, HEAD_PAD), lambda i, j: (0, 0)),
        pl.BlockSpec((1, KV_RANK), lambda i, j: (0, 0)),
    ]
    args = [x, mods, g1, w_cat, qg, kvg]
    if rope:
        seq_tiles = rope_tabs[0].shape[0] // tm
        in_specs += [pl.BlockSpec((tm, LANES), lambda i, j: (i % seq_tiles, 0))] * 3
        args += list(rope_tabs)
    out_shape = (
        jax.ShapeDtypeStruct((t, Q_COLS), BF16),
        jax.ShapeDtypeStruct((t, POOL_WIDTH), F32),
        jax.ShapeDtypeStruct((t, 2 * D_MODEL), F32),
        jax.ShapeDtypeStruct((t, KV_RANK), F32),
        jax.ShapeDtypeStruct((t, ROPE), F32),
        jax.ShapeDtypeStruct((t, LANES), F32),
    )
    out_specs = (
        pl.BlockSpec((tm, IN_TILE), lambda i, j: (i, jnp.minimum(j, N_Q_TILES - 1))),
        pl.BlockSpec((tm, POOL_WIDTH), lambda i, j: (i, 0)),
        pl.BlockSpec((tm, IN_TILE), lambda i, j: (i, jnp.clip(j - GATE_TILE0, 0, N_GATE_TILES - 1))),
        pl.BlockSpec((tm, KV_RANK), lambda i, j: (i, 0)),
        pl.BlockSpec((tm, ROPE), lambda i, j: (i, 0)),
        pl.BlockSpec((tm, LANES), lambda i, j: (i, 0)),
    )
    return pl.pallas_call(
        functools.partial(_in_proj_kernel, rope),
        out_shape=out_shape,
        grid=(t // tm, N_IN_TILES),
        in_specs=in_specs,
        out_specs=out_specs,
        scratch_shapes=[pltpu.VMEM((tm, D_MODEL), BF16)],
        compiler_params=_cparams(("arbitrary", "arbitrary")),
        name="in_proj_rope" if rope else "in_proj",
    )(*args)


def _kv_expand_kernel(rope, ckv_ref, krp_ref, w_ref, kgn_ref, kgr_ref, *rest):
    if rope:
        cos_ref, slo_ref, shi_ref = rest[:3]
        rest = rest[3:]
    k_ref, v_ref = rest
    kv = _dot(ckv_ref[...].astype(BF16), w_ref[...])
    kr = krp_ref[...]
    ssq_r = jnp.sum(kr * kr, axis=-1, keepdims=True)
    krg = kr * kgr_ref[...]
    if rope:
        krg = _rope_lanes(krg, cos_ref[...], slo_ref[...], shi_ref[...])
    for h in range(N_HEADS):
        kn = kv[:, h * HEAD_PAD:h * HEAD_PAD + NOPE]
        r = lax.rsqrt((jnp.sum(kn * kn, axis=-1, keepdims=True) + ssq_r) / QK + RMS_EPS)
        k_ref[:, h * HEAD_PAD:h * HEAD_PAD + NOPE] = (kn * r * kgn_ref[...]).astype(BF16)
        k_ref[:, h * HEAD_PAD + NOPE:(h + 1) * HEAD_PAD] = (krg * r).astype(BF16)
        v_ref[:, h * V_DIM:(h + 1) * V_DIM] = kv[:, h * HEAD_PAD + NOPE:(h + 1) * HEAD_PAD].astype(BF16)


def _kv_expand(ckv_n, krp, w_ukv, kgn, kgr, rope_tabs, tr):
    r = ckv_n.shape[0]
    rope = rope_tabs is not None
    in_specs = [
        pl.BlockSpec((tr, KV_RANK), lambda i: (i, 0)),
        pl.BlockSpec((tr, LANES), lambda i: (i, 0)),
        pl.BlockSpec((KV_RANK, N_HEADS * HEAD_PAD), lambda i: (0, 0)),
        pl.BlockSpec((1, LANES), lambda i: (0, 0)),
        pl.BlockSpec((1, LANES), lambda i: (0, 0)),
    ]
    args = [ckv_n, krp, w_ukv, kgn, kgr]
    if rope:
        seq_tiles = rope_tabs[0].shape[0] // tr
        in_specs += [pl.BlockSpec((tr, LANES), lambda i: (i % seq_tiles, 0))] * 3
        args += list(rope_tabs)
    return pl.pallas_call(
        functools.partial(_kv_expand_kernel, rope),
        out_shape=(jax.ShapeDtypeStruct((r, N_HEADS * HEAD_PAD), BF16),
                   jax.ShapeDtypeStruct((r, N_HEADS * V_DIM), BF16)),
        grid=(r // tr,),
        in_specs=in_specs,
        out_specs=(pl.BlockSpec((tr, N_HEADS * HEAD_PAD), lambda i: (i, 0)),
                   pl.BlockSpec((tr, N_HEADS * V_DIM), lambda i: (i, 0))),
        compiler_params=_cparams(("arbitrary",)),
        name="kv_expand_rope" if rope else "kv_expand",
    )(*args)


def _qk(q, k):
    return lax.dot_general(q, k, (((1,), (1,)), ((), ())), preferred_element_type=F32)


def _attn_ctx_kernel(q_ref, k_ref, v_ref, o_ref):
    for h in range(N_HEADS):
        s = _qk(q_ref[:, h * HEAD_PAD:(h + 1) * HEAD_PAD], k_ref[:, h * HEAD_PAD:(h + 1) * HEAD_PAD])
        p = jnp.exp(s - jnp.max(s, axis=-1, keepdims=True))
        l = jnp.sum(p, axis=-1, keepdims=True)
        o = _dot(p.astype(BF16), v_ref[:, h * V_DIM:(h + 1) * V_DIM])
        o_ref[:, h * V_DIM:(h + 1) * V_DIM] = (o / l).astype(BF16)


def _attn_ctx(q, k, v, seq):
    t = q.shape[0]
    return pl.pallas_call(
        _attn_ctx_kernel,
        out_shape=jax.ShapeDtypeStruct((t, N_HEADS * V_DIM), BF16),
        grid=(t // seq,),
        in_specs=[pl.BlockSpec((seq, Q_COLS), lambda b: (b, 0)),
                  pl.BlockSpec((seq, Q_COLS), lambda b: (b, 0)),
                  pl.BlockSpec((seq, N_HEADS * V_DIM), lambda b: (b, 0))],
        out_specs=pl.BlockSpec((seq, N_HEADS * V_DIM), lambda b: (b, 0)),
        compiler_params=_cparams(("arbitrary",)),
        name="attn_ctx",
    )(q, k, v)


def _attn_lat_kernel(q_ref, k_ref, v_ref, kc_ref, vc_ref, o_ref):
    q = q_ref[...]
    s1 = _qk(q, k_ref[...])
    s2 = _qk(q, kc_ref[...])
    m = jnp.maximum(jnp.max(s1, axis=-1, keepdims=True), jnp.max(s2, axis=-1, keepdims=True))
    p1 = jnp.exp(s1 - m)
    p2 = jnp.exp(s2 - m)
    l = jnp.sum(p1, axis=-1, keepdims=True) + jnp.sum(p2, axis=-1, keepdims=True)
    o = _dot(p1.astype(BF16), v_ref[...]) + _dot(p2.astype(BF16), vc_ref[...])
    o_ref[...] = (o / l).astype(BF16)


def _attn_lat(q, k, v, kc, vc, seq, past, tq):
    t = q.shape[0]
    nq = seq // tq
    return pl.pallas_call(
        _attn_lat_kernel,
        out_shape=jax.ShapeDtypeStruct((t, N_HEADS * V_DIM), BF16),
        grid=(t // seq, N_HEADS, nq),
        in_specs=[pl.BlockSpec((tq, HEAD_PAD), lambda b, h, i: (b * nq + i, h)),
                  pl.BlockSpec((seq, HEAD_PAD), lambda b, h, i: (b, h)),
                  pl.BlockSpec((seq, V_DIM), lambda b, h, i: (b, h)),
                  pl.BlockSpec((past, HEAD_PAD), lambda b, h, i: (b, h)),
                  pl.BlockSpec((past, V_DIM), lambda b, h, i: (b, h))],
        out_specs=pl.BlockSpec((tq, V_DIM), lambda b, h, i: (b * nq + i, h)),
        compiler_params=_cparams(("arbitrary", "arbitrary", "arbitrary")),
        name="attn_lat",
    )(q, k, v, kc, vc)


POOL_HALO = 8


def _pool_kernel(seq, u_ref, w_ref, sc_ref, o_ref, pad_scr):
    zeros = jnp.zeros((POOL_HALO, POOL_WIDTH), F32)
    pad_scr[0:POOL_HALO, :] = zeros
    pad_scr[POOL_HALO + seq:2 * POOL_HALO + seq, :] = zeros
    pad_scr[POOL_HALO:POOL_HALO + seq, :] = u_ref[...]
    t = lax.broadcasted_iota(I32, (seq, 1), 0)
    for g, w in enumerate(POOL_WINDOWS):
        cols = slice(g * POOL_GROUP, (g + 1) * POOL_GROUP)
        tot = None
        for d in range(-(w // 2), w - w // 2):
            piece = pad_scr[POOL_HALO + d:POOL_HALO + d + seq, cols]
            tot = piece if tot is None else tot + piece
        cnt = (jnp.minimum(t + (w - w // 2), seq) - jnp.maximum(t - w // 2, 0)).astype(F32)
        mixed = tot / cnt - u_ref[:, cols]
        o_ref[:, cols] = (_dot(mixed.astype(BF16), w_ref[g]) * sc_ref[:, cols]).astype(BF16)


def _pool(u, w_pool, pool_scale, seq):
    t = u.shape[0]
    n_groups = len(POOL_WINDOWS)
    return pl.pallas_call(
        functools.partial(_pool_kernel, seq),
        out_shape=jax.ShapeDtypeStruct((t, POOL_WIDTH), BF16),
        grid=(t // seq,),
        in_specs=[pl.BlockSpec((seq, POOL_WIDTH), lambda b: (b, 0)),
                  pl.BlockSpec((n_groups, POOL_GROUP, POOL_GROUP), lambda b: (0, 0, 0)),
                  pl.BlockSpec((1, POOL_WIDTH), lambda b: (0, 0))],
        out_specs=pl.BlockSpec((seq, POOL_WIDTH), lambda b: (b, 0)),
        scratch_shapes=[pltpu.VMEM((seq + 2 * POOL_HALO, POOL_WIDTH), F32)],
        compiler_params=_cparams(("arbitrary",)),
        name="pool",
    )(u, w_pool, pool_scale)


def _merge_kernel(attn_ref, pool_ref, gates_ref, x_ref, mods_ref, bbg_ref, woa_ref, wop_ref, wout_ref,
                  g2_ref, rw_ref, rb_ref, x1_ref, h2_ref, tidx_ref, tw_ref):
    a = _dot(attn_ref[...], woa_ref[...])
    p = _dot(pool_ref[...], wop_ref[...])
    ga = _sigmoid(gates_ref[:, :D_MODEL] + bbg_ref[:, :D_MODEL])
    gp = _sigmoid(gates_ref[:, D_MODEL:] + bbg_ref[:, D_MODEL:])
    merged = (ga * a + gp * p).astype(BF16)
    x1 = x_ref[...] + mods_ref[2:3, :] * _dot(merged, wout_ref[...])
    x1_ref[...] = x1
    y = x1 * lax.rsqrt(jnp.mean(x1 * x1, axis=-1, keepdims=True) + RMS_EPS) * g2_ref[...]
    h2 = y * (1.0 + mods_ref[4:5, :]) + mods_ref[3:4, :]
    _tok_store(h2_ref, 0, h2.shape[0], _pack_halves(h2))
    logits = _dot(h2.astype(BF16), rw_ref[...]) + rb_ref[...]
    lane = lax.broadcasted_iota(I32, logits.shape, 1).astype(F32)
    vals, idxs = [], []
    for _ in range(TOP_K):
        m = jnp.max(logits, axis=-1, keepdims=True)
        ix = jnp.min(jnp.where(logits == m, lane, float(LANES)), axis=-1, keepdims=True)
        vals.append(m)
        idxs.append(ix)
        logits = jnp.where(lane == ix, -jnp.inf, logits)
    es = [jnp.exp(v - vals[0]) for v in vals]
    tot = es[0] + es[1] + es[2] + es[3]
    tidx = jnp.zeros(logits.shape, F32)
    tw = jnp.zeros(logits.shape, F32)
    for k in range(TOP_K):
        tidx = jnp.where(lane == k, idxs[k], tidx)
        tw = jnp.where(lane == k, es[k] / tot, tw)
    tidx_ref[...] = tidx.astype(I32)
    tw_ref[...] = tw


def _merge(attn_o, pool_o, gates, x, mods, mod_row, bbg, woa, wop, wout, g2, rw, rb, tm):
    t = x.shape[0]
    const = lambda shape: pl.BlockSpec(shape, lambda i: (0, 0), pipeline_mode=pl.Buffered(1))
    return pl.pallas_call(
        _merge_kernel,
        out_shape=(jax.ShapeDtypeStruct((t, D_MODEL), F32),
                   jax.ShapeDtypeStruct((t * TOK_SUB, LANES), I32),
                   jax.ShapeDtypeStruct((t, LANES), I32),
                   jax.ShapeDtypeStruct((t, LANES), F32)),
        grid=(t // tm,),
        in_specs=[pl.BlockSpec((tm, D_MODEL), lambda i: (i, 0)),
                  pl.BlockSpec((tm, POOL_WIDTH), lambda i: (i, 0)),
                  pl.BlockSpec((tm, 2 * D_MODEL), lambda i: (i, 0)),
                  pl.BlockSpec((tm, D_MODEL), lambda i: (i, 0)),
                  pl.BlockSpec((None, 6, D_MODEL), lambda i: (mod_row(i), 0, 0)),
                  const((1, 2 * D_MODEL)),
                  const((D_MODEL, D_MODEL)),
                  const((POOL_WIDTH, D_MODEL)),
                  const((D_MODEL, D_MODEL)),
                  const((1, D_MODEL)),
                  const((D_MODEL, LANES)),
                  const((1, LANES))],
        out_specs=(pl.BlockSpec((tm, D_MODEL), lambda i: (i, 0)),
                   pl.BlockSpec((tm * TOK_SUB, LANES), lambda i: (i, 0)),
                   pl.BlockSpec((tm, LANES), lambda i: (i, 0)),
                   pl.BlockSpec((tm, LANES), lambda i: (i, 0))),
        compiler_params=_cparams(("arbitrary",)),
        name="merge",
    )(attn_o, pool_o, gates, x, mods, bbg, woa, wop, wout, g2, rw, rb)


DISPATCH_ROWS = 256
DMA_UNROLL = 8


N_FILL = N_EXPERTS + 1


def _dispatch_kernel(n_first, dest_ref, fill_ref, ha_ref, hb_ref, xs_ref, zero_scr, sem):
    base = pl.program_id(0) * (DISPATCH_ROWS * TOP_K)

    @pl.when(pl.program_id(0) == 0)
    def _():
        zero_scr[...] = jnp.zeros(zero_scr.shape, I32)

        def fill_copy(slot):
            return pltpu.make_async_copy(zero_scr, xs_ref.at[_tok_rows(slot, 1), :], sem)

        def fill(e, carry):
            def one(r, c):
                fill_copy(fill_ref[e] + r).start()
                return c

            return lax.fori_loop(0, fill_ref[N_FILL + e], one, carry)

        lax.fori_loop(0, N_FILL, fill, 0)

        def drain(e, carry):
            def one(r, c):
                fill_copy(0).wait()
                return c

            return lax.fori_loop(0, fill_ref[N_FILL + e], one, carry)

        lax.fori_loop(0, N_FILL, drain, 0)

    def scatter(h_ref):
        def issue(t, carry):
            for k in range(TOP_K):
                pltpu.make_async_copy(h_ref.at[_tok_rows(t, 1), :],
                                      xs_ref.at[_tok_rows(dest_ref[base + t * TOP_K + k], 1), :],
                                      sem).start(priority=k % 2)
            return carry

        lax.fori_loop(0, DISPATCH_ROWS, issue, 0, unroll=DMA_UNROLL // TOP_K)
        for _ in range(TOP_K):
            pltpu.make_async_copy(h_ref, xs_ref.at[_tok_rows(0, DISPATCH_ROWS), :], sem).wait()

    @pl.when(pl.program_id(0) < n_first)
    def _():
        scatter(ha_ref)

    @pl.when(pl.program_id(0) >= n_first)
    def _():
        scatter(hb_ref)


def _dispatch(dest, fill, h2_a, h2_b, cap):
    n_a = h2_a.shape[0] // TOK_SUB // DISPATCH_ROWS
    n_b = h2_b.shape[0] // TOK_SUB // DISPATCH_ROWS
    blk = (DISPATCH_ROWS * TOK_SUB, LANES)
    return pl.pallas_call(
        functools.partial(_dispatch_kernel, n_a),
        out_shape=jax.ShapeDtypeStruct((cap * TOK_SUB, LANES), I32),
        grid_spec=pltpu.PrefetchScalarGridSpec(
            num_scalar_prefetch=2,
            grid=(n_a + n_b,),
            in_specs=[pl.BlockSpec(blk, lambda i, d, z: (jnp.minimum(i, n_a - 1), 0)),
                      pl.BlockSpec(blk, lambda i, d, z: (jnp.maximum(i - n_a, 0), 0))],
            out_specs=pl.BlockSpec(memory_space=pl.ANY),
            scratch_shapes=[pltpu.VMEM((TOK_SUB, LANES), I32), pltpu.SemaphoreType.DMA]),
        compiler_params=_cparams(("arbitrary",)),
        name="moe_dispatch",
    )(dest, fill, h2_a, h2_b)


MOE_SPAN = 8


def _moe_kernel(ie_ref, ir_ref, in_ref, xs_ref, wg_hbm, bg_ref, wu_hbm, bu_ref, wd_hbm, bd_ref, ys_ref,
                xin_scr, acc_scr, wgf_scr, wuf_scr, wdf_scr, wg_scr, wu_scr, wd_scr, sem_x, sem_y, sem_w):
    i = pl.program_id(0)
    n_items = pl.num_programs(0)
    nch = in_ref[i]
    row0 = ir_ref[i]
    expert = ie_ref[i]
    slot = i % 2
    down_row = pl.ds(expert, 1)

    def w_copies(e, f, ws):
        col0 = f * FF_TILE
        cols = pl.ds(col0 if isinstance(col0, int) else pl.multiple_of(col0, FF_TILE), FF_TILE)
        return (pltpu.make_async_copy(wg_hbm.at[e, :, cols], wgf_scr.at[ws], sem_w.at[ws]),
                pltpu.make_async_copy(wu_hbm.at[e, :, cols], wuf_scr.at[ws], sem_w.at[ws]),
                pltpu.make_async_copy(wd_hbm.at[e, cols, :], wdf_scr.at[ws], sem_w.at[ws]))

    def rows_of(c, k=1):
        return pl.ds(pl.multiple_of(c * MOE_CHUNK, MOE_CHUNK), k * MOE_CHUNK)

    def x_copy(item_row0, c, s):
        return pltpu.make_async_copy(xs_ref.at[_tok_rows(item_row0 + c * MOE_CHUNK, MOE_CHUNK), :],
                                     xin_scr.at[s, _tok_rows(c * MOE_CHUNK, MOE_CHUNK), :], sem_x.at[s])

    def y_copy(item_row0, c, s):
        return pltpu.make_async_copy(xin_scr.at[s, _tok_rows(c * MOE_CHUNK, MOE_CHUNK), :],
                                     ys_ref.at[_tok_rows(item_row0 + c * MOE_CHUNK, MOE_CHUNK), :], sem_y)

    def for_chunks(n, body):
        def step(c, carry):
            body(c)
            return carry

        lax.fori_loop(0, n, step, 0)

    def for_spans(n, body, first_span):
        def span(g, carry):
            body(g * MOE_SPAN, MOE_SPAN)
            return carry

        lax.fori_loop(first_span, n // MOE_SPAN, span, 0)
        k = MOE_SPAN // 2
        while k >= 1:
            start = n // (2 * k) * (2 * k)

            @pl.when(n % (2 * k) >= k)
            def _(start=start, k=k):
                body(start, k)

            k //= 2

    @pl.when(i > 0)
    def _():
        prev = jnp.maximum(i - 1, 0)
        for_chunks(in_ref[prev], lambda c: y_copy(ir_ref[prev], c, 1 - slot).wait())

    @pl.when(i == 0)
    def _():
        for_chunks(nch, lambda c: x_copy(row0, c, slot).start())
        for cp in w_copies(expert, 0, 0):
            cp.start()

    @pl.when(i + 1 < n_items)
    def _():
        nxt = jnp.minimum(i + 1, n_items - 1)
        for_chunks(in_ref[nxt], lambda c: x_copy(ir_ref[nxt], c, 1 - slot).start())

    for_chunks(nch, lambda c: x_copy(row0, c, slot).wait())

    @pl.when(nch > 0)
    def _():
        def cast_weights(ws):
            w = (wgf_scr[ws].astype(BF16), wuf_scr[ws].astype(BF16), wdf_scr[ws].astype(BF16))
            wg_scr[...], wu_scr[...], wd_scr[...] = w
            return w

        def contribution(f, c, k, w):
            wg, wu, wd = (wg_scr[...], wu_scr[...], wd_scr[...]) if w is None else w
            bias_row = pl.ds(expert * N_FF_TILES + f, 1)
            lo, hi = _unpack_halves(_tok_load(xin_scr.at[slot], c * MOE_CHUNK, k * MOE_CHUNK))
            x = jnp.concatenate([lo.astype(BF16), hi.astype(BF16)], axis=1)
            gt = jnp.minimum(_dot(x, wg) + bg_ref[bias_row, :], SWIGLU_LIMIT)
            up = jnp.clip(_dot(x, wu) + bu_ref[bias_row, :], -SWIGLU_LIMIT, SWIGLU_LIMIT)
            act = (up + 1.0) * (gt * _sigmoid(SWIGLU_ALPHA * gt))
            return _dot(act.astype(BF16), wd)

        def first(f, c, k, w=None):
            acc_scr[rows_of(c, k), :] = contribution(f, c, k, w) + bd_ref[down_row, :]

        def middle(f, c, k, w=None):
            acc_scr[rows_of(c, k), :] += contribution(f, c, k, w)

        def last(f, c, k, w=None):
            y = acc_scr[rows_of(c, k), :] + contribution(f, c, k, w)
            _tok_store(xin_scr.at[slot], c * MOE_CHUNK, k * MOE_CHUNK, _pack_halves(y))
            for u in range(k):
                y_copy(row0, c + u, slot).start()

        def ff_step(f, phase, request_next):
            ws = f % 2
            for cp in w_copies(expert, f, ws):
                cp.wait()
            request_next()
            body = functools.partial(phase, f)

            @pl.when(nch >= MOE_SPAN)
            def _():
                body(0, MOE_SPAN, cast_weights(ws))

            @pl.when(nch < MOE_SPAN)
            def _():
                cast_weights(ws)

            for_spans(nch, body, 1)

        def request_tile(f):
            def go():
                for cp in w_copies(expert, f, f % 2):
                    cp.start()
            return go

        def request_next_item():
            nxt = jnp.minimum(i + 1, n_items - 1)

            @pl.when(jnp.logical_and(i + 1 < n_items, in_ref[nxt] > 0))
            def _():
                for cp in w_copies(ie_ref[nxt], 0, 0):
                    cp.start()

        ff_step(0, first, request_tile(1))

        def mid(f, carry):
            ff_step(f, middle, request_tile(f + 1))
            return carry

        lax.fori_loop(1, N_FF_TILES - 1, mid, 0)
        ff_step(N_FF_TILES - 1, last, request_next_item)

        @pl.when(i == n_items - 1)
        def _():
            for_chunks(nch, lambda c: y_copy(row0, c, slot).wait())


def _moe_experts(item_e, item_row0, item_nch, xs, w_g, b_g, w_u, b_u, w_d, b_d):
    n_items = item_e.shape[0]
    whole = lambda a: pl.BlockSpec(a.shape, lambda i, ie, ir, inch: (0, 0))
    return pl.pallas_call(
        _moe_kernel,
        out_shape=jax.ShapeDtypeStruct(xs.shape, I32),
        grid_spec=pltpu.PrefetchScalarGridSpec(
            num_scalar_prefetch=3,
            grid=(n_items,),
            in_specs=[
                pl.BlockSpec(memory_space=pl.ANY),
                pl.BlockSpec(memory_space=pl.ANY), whole(b_g),
                pl.BlockSpec(memory_space=pl.ANY), whole(b_u),
                pl.BlockSpec(memory_space=pl.ANY), whole(b_d),
            ],
            out_specs=pl.BlockSpec(memory_space=pl.ANY),
            scratch_shapes=[
                pltpu.VMEM((2, MOE_ITEM_ROWS * TOK_SUB, LANES), I32),
                pltpu.VMEM((MOE_ITEM_ROWS, D_MODEL), F32),
                pltpu.VMEM((2, D_MODEL, FF_TILE), F32),
                pltpu.VMEM((2, D_MODEL, FF_TILE), F32),
                pltpu.VMEM((2, FF_TILE, D_MODEL), F32),
                pltpu.VMEM((D_MODEL, FF_TILE), BF16),
                pltpu.VMEM((D_MODEL, FF_TILE), BF16),
                pltpu.VMEM((FF_TILE, D_MODEL), BF16),
                pltpu.SemaphoreType.DMA((2,)),
                pltpu.SemaphoreType.DMA,
                pltpu.SemaphoreType.DMA((2,)),
            ]),
        input_output_aliases={3: 0},
        compiler_params=_cparams(("arbitrary",)),
        name="moe_experts",
    )(item_e, item_row0, item_nch, xs, w_g, b_g, w_u, b_u, w_d, b_d)


COMBINE_ROWS = 256


def _combine_kernel(dest_ref, ys_ref, x1_ref, tw_ref, mods_ref, o_ref, ybuf, sem):
    step = pl.program_id(0)
    slot = step % 2

    def gather(blk, s):
        base = blk * (COMBINE_ROWS * TOP_K)

        def issue(t, carry):
            for k in range(TOP_K):
                pltpu.make_async_copy(ys_ref.at[_tok_rows(dest_ref[base + t * TOP_K + k], 1), :],
                                      ybuf.at[s, k, _tok_rows(t, 1), :], sem.at[s]).start(priority=k % 2)
            return carry

        lax.fori_loop(0, COMBINE_ROWS, issue, 0, unroll=DMA_UNROLL // TOP_K)

    @pl.when(step == 0)
    def _():
        gather(step, slot)

    @pl.when(step + 1 < pl.num_programs(0))
    def _():
        gather(step + 1, 1 - slot)

    for k in range(TOP_K):
        pltpu.make_async_copy(ys_ref.at[_tok_rows(0, COMBINE_ROWS), :], ybuf.at[slot, k], sem.at[slot]).wait()

    y_lo = y_hi = None
    for k in range(TOP_K):
        lo, hi = _unpack_halves(_tok_load(ybuf.at[slot, k], 0, COMBINE_ROWS))
        w = tw_ref[:, k:k + 1]
        y_lo = w * lo if y_lo is None else y_lo + w * lo
        y_hi = w * hi if y_hi is None else y_hi + w * hi
    o_ref[:, :HALF] = x1_ref[:, :HALF] + mods_ref[5:6, :HALF] * y_lo
    o_ref[:, HALF:] = x1_ref[:, HALF:] + mods_ref[5:6, HALF:] * y_hi


def _combine(dest, ys, x1, tw, mods, mod_row):
    t = x1.shape[0]
    return pl.pallas_call(
        _combine_kernel,
        out_shape=jax.ShapeDtypeStruct((t, D_MODEL), F32),
        grid_spec=pltpu.PrefetchScalarGridSpec(
            num_scalar_prefetch=1,
            grid=(t // COMBINE_ROWS,),
            in_specs=[pl.BlockSpec(memory_space=pl.ANY),
                      pl.BlockSpec((COMBINE_ROWS, D_MODEL), lambda i, d: (i, 0)),
                      pl.BlockSpec((COMBINE_ROWS, LANES), lambda i, d: (i, 0)),
                      pl.BlockSpec((None, 6, D_MODEL), lambda i, d: (mod_row(i * COMBINE_ROWS), 0, 0))],
            out_specs=pl.BlockSpec((COMBINE_ROWS, D_MODEL), lambda i, d: (i, 0)),
            scratch_shapes=[pltpu.VMEM((2, TOP_K, COMBINE_ROWS * TOK_SUB, LANES), I32),
                            pltpu.SemaphoreType.DMA((2,))]),
        compiler_params=_cparams(("arbitrary",)),
        name="moe_combine",
    )(dest, ys, x1, tw, mods)


def _routing_tables(top_idx, n_items, cap):
    flat_e = top_idx.reshape(-1)
    onehot = (flat_e[:, None] == jnp.arange(N_EXPERTS, dtype=I32)[None, :]).astype(I32)
    csum = jnp.cumsum(onehot, axis=0)
    rank = jnp.take_along_axis(csum, flat_e[:, None], axis=1)[:, 0] - 1
    counts = csum[-1]
    nch = (counts + MOE_CHUNK - 1) // MOE_CHUNK
    pad_rows = nch * MOE_CHUNK
    pad_end = jnp.cumsum(pad_rows)
    pad_start = pad_end - pad_rows
    dest = (pad_start[flat_e] + rank).astype(I32)
    items_e = (nch + MOE_ITEM_CHUNKS - 1) // MOE_ITEM_CHUNKS
    item_end = jnp.cumsum(items_e)
    item_start = item_end - items_e
    total = item_end[-1]
    i = jnp.arange(n_items, dtype=I32)
    ii = jnp.minimum(i, total - 1)
    e_i = jnp.minimum(jnp.searchsorted(item_end, ii, side="right"), N_EXPERTS - 1).astype(I32)
    local = ii - item_start[e_i]
    row0 = (pad_start[e_i] + local * MOE_ITEM_ROWS).astype(I32)
    n_i = jnp.where(i < total, jnp.minimum(MOE_ITEM_CHUNKS, nch[e_i] - local * MOE_ITEM_CHUNKS), 0).astype(I32)
    fill = jnp.concatenate([pad_start + counts, pad_end[-1:], pad_rows - counts, cap - pad_end[-1:]]).astype(I32)
    return dest, fill, e_i, row0, n_i


def _rope_tables(n_lat):
    nf = ROPE // 4
    inv = ROPE_BASE ** (-jnp.arange(nf, dtype=F32) / nf)
    t = jnp.arange(n_lat)
    row = (t // GRID_W).astype(F32)
    col = (t % GRID_W).astype(F32)
    ang_r = row[:, None] * inv[None, :]
    ang_c = col[:, None] * inv[None, :]
    z = jnp.zeros((n_lat, nf), F32)
    tail = jnp.zeros((n_lat, LANES - ROPE), F32)
    cos = jnp.concatenate([jnp.cos(ang_r), jnp.cos(ang_r), jnp.cos(ang_c), jnp.cos(ang_c), tail], axis=1)
    sin_lo = jnp.concatenate([-jnp.sin(ang_r), z, -jnp.sin(ang_c), z, tail], axis=1)
    sin_hi = jnp.concatenate([z, jnp.sin(ang_r), z, jnp.sin(ang_c), tail], axis=1)
    return cos, sin_lo, sin_hi


def kernel(x_prompt, x_sample, cache_ckv, cache_krope, c, c_ctx, norm1_g, norm2_g, w_ada, b_ada, w_in,
           b_branch_gate, kv_norm_g, w_ukv, q_norm_g, k_norm_g, w_o_attn, w_pool, pool_scale, w_o_pool, w_out,
           router_w, router_b, w_exp_gate, b_exp_gate, w_exp_up, b_exp_up, w_exp_down, b_exp_down):
    assert w_in.shape[0] == 1, "single-layer trunk"
    batch, seq, _ = x_prompt.shape
    dec_batch, n_lat, _ = x_sample.shape
    past = cache_ckv.shape[2]
    n_ctx = batch * seq
    n_dec = dec_batch * n_lat

    cond = jnp.concatenate([c_ctx[None, :], c, jnp.zeros((8 - 1 - dec_batch, D_MODEL), F32)], axis=0)
    mods = _adaln(cond, w_ada[0], b_ada).reshape(8, 6, D_MODEL)

    w_cat = _w_in_layout(w_in[0].T)
    qg = jnp.pad(q_norm_g * ATTN_SCALE, ((0, 0), (0, HEAD_PAD - QK)))
    kgn = k_norm_g[:, :NOPE]
    kgr = jnp.pad(k_norm_g[:, NOPE:], ((0, 0), (0, LANES - ROPE)))
    w_ukv_b = w_ukv[0].astype(BF16)
    rope_tabs = _rope_tables(n_lat)

    ctx_row = lambda i: 0
    tm_in = 512
    lat_row_in = lambda i: 1 + (i * tm_in) // n_lat
    q_c, pool_c, gates_c, ckv_c, kr_c, krp_c = _in_proj(
        x_prompt.reshape(n_ctx, D_MODEL), mods, ctx_row, norm1_g, w_cat, qg, kv_norm_g, None, tm_in)
    q_l, pool_l, gates_l, ckv_l, _, krp_l = _in_proj(
        x_sample.reshape(n_dec, D_MODEL), mods, lat_row_in, norm1_g, w_cat, qg, kv_norm_g, rope_tabs, tm_in)

    tr = 256
    k_c, v_c = _kv_expand(ckv_c, krp_c, w_ukv_b, kgn, kgr, None, tr)
    k_l, v_l = _kv_expand(ckv_l, krp_l, w_ukv_b, kgn, kgr, rope_tabs, tr)
    cache_krp = jnp.pad(cache_krope.reshape(dec_batch * past, ROPE), ((0, 0), (0, LANES - ROPE)))
    k_p, v_p = _kv_expand(cache_ckv.reshape(dec_batch * past, KV_RANK), cache_krp, w_ukv_b, kgn, kgr, None, tr)

    attn_c = _attn_ctx(q_c, k_c, v_c, seq)
    attn_l = _attn_lat(q_l, k_l, v_l, k_p, v_p, n_lat, past, 512)

    w_pool_b = w_pool[0].astype(BF16)
    poolo_c = _pool(pool_c, w_pool_b, pool_scale, seq)
    poolo_l = _pool(pool_l, w_pool_b, pool_scale, n_lat)

    woa = w_o_attn[0].astype(BF16)
    wop = w_o_pool[0].astype(BF16)
    wout = w_out[0].astype(BF16)
    rw = jnp.pad(router_w[0], ((0, 0), (0, LANES - N_EXPERTS))).astype(BF16)
    rb = jnp.pad(router_b, ((0, 0), (0, LANES - N_EXPERTS)), constant_values=NEG_BIG)
    tm_mg = 256
    lat_row_mg = lambda i: 1 + (i * tm_mg) // n_lat
    x1_c, h2_c, tidx_c, tw_c = _merge(attn_c, poolo_c, gates_c, x_prompt.reshape(n_ctx, D_MODEL), mods, ctx_row,
                                      b_branch_gate, woa, wop, wout, norm2_g, rw, rb, tm_mg)
    x1_l, h2_l, tidx_l, tw_l = _merge(attn_l, poolo_l, gates_l, x_sample.reshape(n_dec, D_MODEL), mods, lat_row_mg,
                                      b_branch_gate, woa, wop, wout, norm2_g, rw, rb, tm_mg)

    n_assign = (n_ctx + n_dec) * TOP_K
    max_chunks = (n_assign + N_EXPERTS * (MOE_CHUNK - 1)) // MOE_CHUNK
    cap = max_chunks * MOE_CHUNK
    n_items = (max_chunks + N_EXPERTS * (MOE_ITEM_CHUNKS - 1)) // MOE_ITEM_CHUNKS
    top_idx = jnp.concatenate([tidx_c[:, :TOP_K], tidx_l[:, :TOP_K]], axis=0)
    dest, fill, item_e, item_row0, item_nch = _routing_tables(top_idx, n_items, cap)

    xs = _dispatch(dest, fill, h2_c, h2_l, cap)
    ys = _moe_experts(item_e, item_row0, item_nch, xs,
                      w_exp_gate[0], b_exp_gate[0].reshape(N_EXPERTS * N_FF_TILES, FF_TILE),
                      w_exp_up[0], b_exp_up[0].reshape(N_EXPERTS * N_FF_TILES, FF_TILE),
                      w_exp_down[0], b_exp_down[0])

    y_c = _combine(dest[:n_ctx * TOP_K], ys, x1_c, tw_c, mods, lambda r: 0)
    y_l = _combine(dest[n_ctx * TOP_K:], ys, x1_l, tw_l, mods, lambda r: 1 + r // n_lat)

    return (y_c.reshape(batch, seq, D_MODEL),
            y_l.reshape(dec_batch, n_lat, D_MODEL),
            ckv_c.reshape(batch, 1, seq, KV_RANK),
            kr_c.reshape(batch, 1, seq, ROPE))
```

```python
import functools

import jax
import jax.numpy as jnp
from jax import lax
from jax.experimental import pallas as pl
from jax.experimental.pallas import tpu as pltpu

F32 = jnp.float32
BF16 = jnp.bfloat16
I32 = jnp.int32

D_MODEL = 2048
N_HEADS = 16
NOPE = 128
ROPE = 64
QK = NOPE + ROPE
V_DIM = 128
KV_RANK = 512
POOL_WINDOWS = (2, 4, 8, 16)
POOL_WIDTH = 1024
POOL_GROUP = POOL_WIDTH // len(POOL_WINDOWS)
N_EXPERTS = 32
TOP_K = 4
D_FF = 2048
SWIGLU_LIMIT = 7.0
SWIGLU_ALPHA = 1.702
ROPE_BASE = 10000.0
RMS_EPS = 1e-6
GRID_W = 64
ATTN_SCALE = QK ** -0.5

LANES = 128
HEAD_PAD = 2 * LANES
Q_COLS = N_HEADS * HEAD_PAD
IN_TILE = 1024
N_Q_TILES = Q_COLS // IN_TILE
POOL_TILE = N_Q_TILES
GATE_TILE0 = POOL_TILE + 1
N_GATE_TILES = 2 * D_MODEL // IN_TILE
KV_TILE = GATE_TILE0 + N_GATE_TILES
N_IN_TILES = KV_TILE + 1
HEADS_PER_TILE = IN_TILE // HEAD_PAD

MOE_CHUNK = 128
MOE_ITEM_CHUNKS = 16
MOE_ITEM_ROWS = MOE_CHUNK * MOE_ITEM_CHUNKS
FF_TILE = 256
N_FF_TILES = D_FF // FF_TILE
NEG_BIG = -1e30

VMEM_LIMIT = 56 * 1024 * 1024


def _cparams(sem, vmem=VMEM_LIMIT):
    return pltpu.CompilerParams(dimension_semantics=sem, vmem_limit_bytes=vmem)


def _dot(a, b):
    return jnp.dot(a, b, preferred_element_type=F32)


def _sigmoid(z):
    return 1.0 / (1.0 + jnp.exp(-z))


HALF = D_MODEL // 2
HI_MASK = -65536


def _pack_halves(v):
    lo = lax.bitcast_convert_type(v[:, :HALF].astype(BF16).astype(F32), I32)
    hi = lax.bitcast_convert_type(v[:, HALF:].astype(BF16).astype(F32), I32)
    return jnp.bitwise_or(jnp.bitwise_and(hi, HI_MASK), lax.shift_right_logical(lo, 16))


TOK_SUB = HALF // LANES


def _tok_rows(tok0, n):
    row0 = tok0 * TOK_SUB
    return pl.ds(row0 if isinstance(row0, int) else pl.multiple_of(row0, TOK_SUB), n * TOK_SUB)


def _tok_store(ref, tok0, n, words):
    for s in range(TOK_SUB):
        ref[pl.ds(tok0 * TOK_SUB + s, n, stride=TOK_SUB), :] = words[:, s * LANES:(s + 1) * LANES]


def _tok_load(ref, tok0, n):
    return jnp.concatenate([ref[pl.ds(tok0 * TOK_SUB + s, n, stride=TOK_SUB), :] for s in range(TOK_SUB)], axis=1)


def _unpack_halves(w):
    lo = lax.bitcast_convert_type(lax.shift_left(w, 16), F32)
    hi = lax.bitcast_convert_type(jnp.bitwise_and(w, HI_MASK), F32)
    return lo, hi


def _rope_lanes(y, cos, sin_lo, sin_hi):
    return y * cos + pltpu.roll(y, LANES - 16, axis=1) * sin_lo + pltpu.roll(y, 16, axis=1) * sin_hi


def _adaln_kernel(c_ref, w_ref, b_ref, o_ref):
    c = c_ref[...]
    s = (c * _sigmoid(c)).astype(BF16)
    o_ref[...] = _dot(s, w_ref[...].astype(BF16)) + b_ref[...]


def _adaln(cond, w_ada, b_ada):
    rows = cond.shape[0]
    n = w_ada.shape[1]
    tn = 1024
    return pl.pallas_call(
        _adaln_kernel,
        out_shape=jax.ShapeDtypeStruct((rows, n), F32),
        grid=(n // tn,),
        in_specs=[pl.BlockSpec((rows, D_MODEL), lambda j: (0, 0)),
                  pl.BlockSpec((D_MODEL, tn), lambda j: (0, j)),
                  pl.BlockSpec((1, tn), lambda j: (0, j))],
        out_specs=pl.BlockSpec((rows, tn), lambda j: (0, j)),
        compiler_params=_cparams(("arbitrary",)),
        name="adaln",
    )(cond, w_ada, b_ada)


W_Q_END = N_HEADS * QK
W_CKV_END = W_Q_END + KV_RANK
W_KR_END = W_CKV_END + ROPE
N_IN_COLS = W_KR_END + POOL_WIDTH + 2 * D_MODEL
W_PREP = 256


def _w_in_layout_kernel(wt_ref, o_ref):
    lane = lax.broadcasted_iota(I32, (W_PREP, W_PREP), 1)

    def move(src_row, dst_col, keep):
        sq = wt_ref[src_row:src_row + W_PREP, :].T
        if keep < W_PREP:
            sq = jnp.where(lane < keep, sq, 0.0)
        o_ref[:, dst_col:dst_col + W_PREP] = sq.astype(BF16)

    for h in range(N_HEADS):
        move(h * QK, h * HEAD_PAD, QK)
    for p in range((N_IN_COLS - W_KR_END) // W_PREP):
        move(W_KR_END + p * W_PREP, Q_COLS + p * W_PREP, W_PREP)
    kv0 = Q_COLS + N_IN_COLS - W_KR_END
    kv_cols = KV_RANK + ROPE
    for p in range(IN_TILE // W_PREP):
        keep = min(max(kv_cols - p * W_PREP, 0), W_PREP)
        if keep:
            move(W_Q_END + p * W_PREP, kv0 + p * W_PREP, keep)
        else:
            o_ref[:, kv0 + p * W_PREP:kv0 + (p + 1) * W_PREP] = jnp.zeros((W_PREP, W_PREP), BF16)


def _w_in_layout(wt):
    return pl.pallas_call(
        _w_in_layout_kernel,
        out_shape=jax.ShapeDtypeStruct((D_MODEL, N_IN_TILES * IN_TILE), BF16),
        grid=(D_MODEL // W_PREP,),
        in_specs=[pl.BlockSpec((N_IN_COLS, W_PREP), lambda i: (0, i))],
        out_specs=pl.BlockSpec((W_PREP, N_IN_TILES * IN_TILE), lambda i: (i, 0)),
        compiler_params=_cparams(("arbitrary",)),
        name="w_in_layout",
    )(wt)


def _in_proj_kernel(rope, x_ref, mods_ref, g1_ref, w_ref, qg_ref, kvg_ref, *rest):
    if rope:
        cos_ref, slo_ref, shi_ref = rest[:3]
        rest = rest[3:]
    q_ref, pool_ref, gates_ref, ckv_ref, kr_ref, krp_ref, h_scr = rest
    j = pl.program_id(1)

    def proj(c0, c1, h=None):
        return _dot(h_scr[...] if h is None else h, w_ref[:, c0:c1])

    def q_heads(h=None):
        for hh in range(HEADS_PER_TILE):
            a = proj(hh * HEAD_PAD, (hh + 1) * HEAD_PAD, h)
            r = lax.rsqrt(jnp.sum(a * a, axis=-1, keepdims=True) / QK + RMS_EPS)
            y = a * r * qg_ref[...]
            if rope:
                yr = _rope_lanes(y[:, LANES:], cos_ref[...], slo_ref[...], shi_ref[...])
                q_ref[:, hh * HEAD_PAD:hh * HEAD_PAD + LANES] = y[:, :LANES].astype(BF16)
                q_ref[:, hh * HEAD_PAD + LANES:(hh + 1) * HEAD_PAD] = yr.astype(BF16)
            else:
                q_ref[:, hh * HEAD_PAD:(hh + 1) * HEAD_PAD] = y.astype(BF16)

    @pl.when(j == 0)
    def _():
        x = x_ref[...]
        y = x * lax.rsqrt(jnp.mean(x * x, axis=-1, keepdims=True) + RMS_EPS) * g1_ref[...]
        h = (y * (1.0 + mods_ref[1:2, :]) + mods_ref[0:1, :]).astype(BF16)
        h_scr[...] = h
        q_heads(h)

    @pl.when(jnp.logical_and(j > 0, j < N_Q_TILES))
    def _():
        q_heads()

    @pl.when(j == POOL_TILE)
    def _():
        pool_ref[...] = proj(0, IN_TILE)

    @pl.when(jnp.logical_and(j >= GATE_TILE0, j < KV_TILE))
    def _():
        gates_ref[...] = proj(0, IN_TILE)

    @pl.when(j == KV_TILE)
    def _():
        a = proj(0, KV_RANK)
        r = lax.rsqrt(jnp.mean(a * a, axis=-1, keepdims=True) + RMS_EPS)
        ckv_ref[...] = a * r * kvg_ref[...]
        krp = proj(KV_RANK, KV_RANK + LANES)
        krp_ref[...] = krp
        kr_ref[...] = krp[:, :ROPE]


def _in_proj(x, mods, mod_row, g1, w_cat, qg, kvg, rope_tabs, tm):
    t = x.shape[0]
    rope = rope_tabs is not None
    in_specs = [
        pl.BlockSpec((tm, D_MODEL), lambda i, j: (i, 0)),
        pl.BlockSpec((None, 6, D_MODEL), lambda i, j: (mod_row(i), 0, 0)),
        pl.BlockSpec((1, D_MODEL), lambda i, j: (0, 0)),
        pl.BlockSpec((D_MODEL, IN_TILE), lambda i, j: (0, j)),
        pl.BlockSpec((1, HEAD_PAD), lambda i, j: (0, 0)),
        pl.BlockSpec((1, KV_RANK), lambda i, j: (0, 0)),
    ]
    args = [x, mods, g1, w_cat, qg, kvg]
    if rope:
        seq_tiles = rope_tabs[0].shape[0] // tm
        in_specs += [pl.BlockSpec((tm, LANES), lambda i, j: (i % seq_tiles, 0))] * 3
        args += list(rope_tabs)
    out_shape = (
        jax.ShapeDtypeStruct((t, Q_COLS), BF16),
        jax.ShapeDtypeStruct((t, POOL_WIDTH), F32),
        jax.ShapeDtypeStruct((t, 2 * D_MODEL), F32),
        jax.ShapeDtypeStruct((t, KV_RANK), F32),
        jax.ShapeDtypeStruct((t, ROPE), F32),
        jax.ShapeDtypeStruct((t, LANES), F32),
    )
    out_specs = (
        pl.BlockSpec((tm, IN_TILE), lambda i, j: (i, jnp.minimum(j, N_Q_TILES - 1))),
        pl.BlockSpec((tm, POOL_WIDTH), lambda i, j: (i, 0)),
        pl.BlockSpec((tm, IN_TILE), lambda i, j: (i, jnp.clip(j - GATE_TILE0, 0, N_GATE_TILES - 1))),
        pl.BlockSpec((tm, KV_RANK), lambda i, j: (i, 0)),
        pl.BlockSpec((tm, ROPE), lambda i, j: (i, 0)),
        pl.BlockSpec((tm, LANES), lambda i, j: (i, 0)),
    )
    return pl.pallas_call(
        functools.partial(_in_proj_kernel, rope),
        out_shape=out_shape,
        grid=(t // tm, N_IN_TILES),
        in_specs=in_specs,
        out_specs=out_specs,
        scratch_shapes=[pltpu.VMEM((tm, D_MODEL), BF16)],
        compiler_params=_cparams(("arbitrary", "arbitrary")),
        name="in_proj_rope" if rope else "in_proj",
    )(*args)


def _kv_expand_kernel(rope, ckv_ref, krp_ref, w_ref, kgn_ref, kgr_ref, *rest):
    if rope:
        cos_ref, slo_ref, shi_ref = rest[:3]
        rest = rest[3:]
    k_ref, v_ref = rest
    kv = _dot(ckv_ref[...].astype(BF16), w_ref[...])
    kr = krp_ref[...]
    ssq_r = jnp.sum(kr * kr, axis=-1, keepdims=True)
    krg = kr * kgr_ref[...]
    if rope:
        krg = _rope_lanes(krg, cos_ref[...], slo_ref[...], shi_ref[...])
    for h in range(N_HEADS):
        kn = kv[:, h * HEAD_PAD:h * HEAD_PAD + NOPE]
        r = lax.rsqrt((jnp.sum(kn * kn, axis=-1, keepdims=True) + ssq_r) / QK + RMS_EPS)
        k_ref[:, h * HEAD_PAD:h * HEAD_PAD + NOPE] = (kn * r * kgn_ref[...]).astype(BF16)
        k_ref[:, h * HEAD_PAD + NOPE:(h + 1) * HEAD_PAD] = (krg * r).astype(BF16)
        v_ref[:, h * V_DIM:(h + 1) * V_DIM] = kv[:, h * HEAD_PAD + NOPE:(h + 1) * HEAD_PAD].astype(BF16)


def _kv_expand(ckv_n, krp, w_ukv, kgn, kgr, rope_tabs, tr):
    r = ckv_n.shape[0]
    rope = rope_tabs is not None
    in_specs = [
        pl.BlockSpec((tr, KV_RANK), lambda i: (i, 0)),
        pl.BlockSpec((tr, LANES), lambda i: (i, 0)),
        pl.BlockSpec((KV_RANK, N_HEADS * HEAD_PAD), lambda i: (0, 0)),
        pl.BlockSpec((1, LANES), lambda i: (0, 0)),
        pl.BlockSpec((1, LANES), lambda i: (0, 0)),
    ]
    args = [ckv_n, krp, w_ukv, kgn, kgr]
    if rope:
        seq_tiles = rope_tabs[0].shape[0] // tr
        in_specs += [pl.BlockSpec((tr, LANES), lambda i: (i % seq_tiles, 0))] * 3
        args += list(rope_tabs)
    return pl.pallas_call(
        functools.partial(_kv_expand_kernel, rope),
        out_shape=(jax.ShapeDtypeStruct((r, N_HEADS * HEAD_PAD), BF16),
                   jax.ShapeDtypeStruct((r, N_HEADS * V_DIM), BF16)),
        grid=(r // tr,),
        in_specs=in_specs,
        out_specs=(pl.BlockSpec((tr, N_HEADS * HEAD_PAD), lambda i: (i, 0)),
                   pl.BlockSpec((tr, N_HEADS * V_DIM), lambda i: (i, 0))),
        compiler_params=_cparams(("arbitrary",)),
        name="kv_expand_rope" if rope else "kv_expand",
    )(*args)


def _qk(q, k):
    return lax.dot_general(q, k, (((1,), (1,)), ((), ())), preferred_element_type=F32)


def _attn_ctx_kernel(q_ref, k_ref, v_ref, o_ref):
    for h in range(N_HEADS):
        s = _qk(q_ref[:, h * HEAD_PAD:(h + 1) * HEAD_PAD], k_ref[:, h * HEAD_PAD:(h + 1) * HEAD_PAD])
        p = jnp.exp(s - jnp.max(s, axis=-1, keepdims=True))
        l = jnp.sum(p, axis=-1, keepdims=True)
        o = _dot(p.astype(BF16), v_ref[:, h * V_DIM:(h + 1) * V_DIM])
        o_ref[:, h * V_DIM:(h + 1) * V_DIM] = (o / l).astype(BF16)


def _attn_ctx(q, k, v, seq):
    t = q.shape[0]
    return pl.pallas_call(
        _attn_ctx_kernel,
        out_shape=jax.ShapeDtypeStruct((t, N_HEADS * V_DIM), BF16),
        grid=(t // seq,),
        in_specs=[pl.BlockSpec((seq, Q_COLS), lambda b: (b, 0)),
                  pl.BlockSpec((seq, Q_COLS), lambda b: (b, 0)),
                  pl.BlockSpec((seq, N_HEADS * V_DIM), lambda b: (b, 0))],
        out_specs=pl.BlockSpec((seq, N_HEADS * V_DIM), lambda b: (b, 0)),
        compiler_params=_cparams(("arbitrary",)),
        name="attn_ctx",
    )(q, k, v)


def _attn_lat_kernel(q_ref, k_ref, v_ref, kc_ref, vc_ref, o_ref):
    q = q_ref[...]
    s1 = _qk(q, k_ref[...])
    s2 = _qk(q, kc_ref[...])
    m = jnp.maximum(jnp.max(s1, axis=-1, keepdims=True), jnp.max(s2, axis=-1, keepdims=True))
    p1 = jnp.exp(s1 - m)
    p2 = jnp.exp(s2 - m)
    l = jnp.sum(p1, axis=-1, keepdims=True) + jnp.sum(p2, axis=-1, keepdims=True)
    o = _dot(p1.astype(BF16), v_ref[...]) + _dot(p2.astype(BF16), vc_ref[...])
    o_ref[...] = (o / l).astype(BF16)


def _attn_lat(q, k, v, kc, vc, seq, past, tq):
    t = q.shape[0]
    nq = seq // tq
    return pl.pallas_call(
        _attn_lat_kernel,
        out_shape=jax.ShapeDtypeStruct((t, N_HEADS * V_DIM), BF16),
        grid=(t // seq, N_HEADS, nq),
        in_specs=[pl.BlockSpec((tq, HEAD_PAD), lambda b, h, i: (b * nq + i, h)),
                  pl.BlockSpec((seq, HEAD_PAD), lambda b, h, i: (b, h)),
                  pl.BlockSpec((seq, V_DIM), lambda b, h, i: (b, h)),
                  pl.BlockSpec((past, HEAD_PAD), lambda b, h, i: (b, h)),
                  pl.BlockSpec((past, V_DIM), lambda b, h, i: (b, h))],
        out_specs=pl.BlockSpec((tq, V_DIM), lambda b, h, i: (b * nq + i, h)),
        compiler_params=_cparams(("arbitrary", "arbitrary", "arbitrary")),
        name="attn_lat",
    )(q, k, v, kc, vc)


POOL_HALO = 8


def _pool_kernel(seq, u_ref, w_ref, sc_ref, o_ref, pad_scr):
    zeros = jnp.zeros((POOL_HALO, POOL_WIDTH), F32)
    pad_scr[0:POOL_HALO, :] = zeros
    pad_scr[POOL_HALO + seq:2 * POOL_HALO + seq, :] = zeros
    pad_scr[POOL_HALO:POOL_HALO + seq, :] = u_ref[...]
    t = lax.broadcasted_iota(I32, (seq, 1), 0)
    for g, w in enumerate(POOL_WINDOWS):
        cols = slice(g * POOL_GROUP, (g + 1) * POOL_GROUP)
        tot = None
        for d in range(-(w // 2), w - w // 2):
            piece = pad_scr[POOL_HALO + d:POOL_HALO + d + seq, cols]
            tot = piece if tot is None else tot + piece
        cnt = (jnp.minimum(t + (w - w // 2), seq) - jnp.maximum(t - w // 2, 0)).astype(F32)
        mixed = tot / cnt - u_ref[:, cols]
        o_ref[:, cols] = (_dot(mixed.astype(BF16), w_ref[g]) * sc_ref[:, cols]).astype(BF16)


def _pool(u, w_pool, pool_scale, seq):
    t = u.shape[0]
    n_groups = len(POOL_WINDOWS)
    return pl.pallas_call(
        functools.partial(_pool_kernel, seq),
        out_shape=jax.ShapeDtypeStruct((t, POOL_WIDTH), BF16),
        grid=(t // seq,),
        in_specs=[pl.BlockSpec((seq, POOL_WIDTH), lambda b: (b, 0)),
                  pl.BlockSpec((n_groups, POOL_GROUP, POOL_GROUP), lambda b: (0, 0, 0)),
                  pl.BlockSpec((1, POOL_WIDTH), lambda b: (0, 0))],
        out_specs=pl.BlockSpec((seq, POOL_WIDTH), lambda b: (b, 0)),
        scratch_shapes=[pltpu.VMEM((seq + 2 * POOL_HALO, POOL_WIDTH), F32)],
        compiler_params=_cparams(("arbitrary",)),
        name="pool",
    )(u, w_pool, pool_scale)


def _merge_kernel(attn_ref, pool_ref, gates_ref, x_ref, mods_ref, bbg_ref, woa_ref, wop_ref, wout_ref,
                  g2_ref, rw_ref, rb_ref, x1_ref, h2_ref, tidx_ref, tw_ref):
    a = _dot(attn_ref[...], woa_ref[...])
    p = _dot(pool_ref[...], wop_ref[...])
    ga = _sigmoid(gates_ref[:, :D_MODEL] + bbg_ref[:, :D_MODEL])
    gp = _sigmoid(gates_ref[:, D_MODEL:] + bbg_ref[:, D_MODEL:])
    merged = (ga * a + gp * p).astype(BF16)
    x1 = x_ref[...] + mods_ref[2:3, :] * _dot(merged, wout_ref[...])
    x1_ref[...] = x1
    y = x1 * lax.rsqrt(jnp.mean(x1 * x1, axis=-1, keepdims=True) + RMS_EPS) * g2_ref[...]
    h2 = y * (1.0 + mods_ref[4:5, :]) + mods_ref[3:4, :]
    _tok_store(h2_ref, 0, h2.shape[0], _pack_halves(h2))
    logits = _dot(h2.astype(BF16), rw_ref[...]) + rb_ref[...]
    lane = lax.broadcasted_iota(I32, logits.shape, 1).astype(F32)
    vals, idxs = [], []
    for _ in range(TOP_K):
        m = jnp.max(logits, axis=-1, keepdims=True)
        ix = jnp.min(jnp.where(logits == m, lane, float(LANES)), axis=-1, keepdims=True)
        vals.append(m)
        idxs.append(ix)
        logits = jnp.where(lane == ix, -jnp.inf, logits)
    es = [jnp.exp(v - vals[0]) for v in vals]
    tot = es[0] + es[1] + es[2] + es[3]
    tidx = jnp.zeros(logits.shape, F32)
    tw = jnp.zeros(logits.shape, F32)
    for k in range(TOP_K):
        tidx = jnp.where(lane == k, idxs[k], tidx)
        tw = jnp.where(lane == k, es[k] / tot, tw)
    tidx_ref[...] = tidx.astype(I32)
    tw_ref[...] = tw


def _merge(attn_o, pool_o, gates, x, mods, mod_row, bbg, woa, wop, wout, g2, rw, rb, tm):
    t = x.shape[0]
    const = lambda shape: pl.BlockSpec(shape, lambda i: (0, 0), pipeline_mode=pl.Buffered(1))
    return pl.pallas_call(
        _merge_kernel,
        out_shape=(jax.ShapeDtypeStruct((t, D_MODEL), F32),
                   jax.ShapeDtypeStruct((t * TOK_SUB, LANES), I32),
                   jax.ShapeDtypeStruct((t, LANES), I32),
                   jax.ShapeDtypeStruct((t, LANES), F32)),
        grid=(t // tm,),
        in_specs=[pl.BlockSpec((tm, D_MODEL), lambda i: (i, 0)),
                  pl.BlockSpec((tm, POOL_WIDTH), lambda i: (i, 0)),
                  pl.BlockSpec((tm, 2 * D_MODEL), lambda i: (i, 0)),
                  pl.BlockSpec((tm, D_MODEL), lambda i: (i, 0)),
                  pl.BlockSpec((None, 6, D_MODEL), lambda i: (mod_row(i), 0, 0)),
                  const((1, 2 * D_MODEL)),
                  const((D_MODEL, D_MODEL)),
                  const((POOL_WIDTH, D_MODEL)),
                  const((D_MODEL, D_MODEL)),
                  const((1, D_MODEL)),
                  const((D_MODEL, LANES)),
                  const((1, LANES))],
        out_specs=(pl.BlockSpec((tm, D_MODEL), lambda i: (i, 0)),
                   pl.BlockSpec((tm * TOK_SUB, LANES), lambda i: (i, 0)),
                   pl.BlockSpec((tm, LANES), lambda i: (i, 0)),
                   pl.BlockSpec((tm, LANES), lambda i: (i, 0))),
        compiler_params=_cparams(("arbitrary",)),
        name="merge",
    )(attn_o, pool_o, gates, x, mods, bbg, woa, wop, wout, g2, rw, rb)


DISPATCH_ROWS = 256
DMA_UNROLL = 8


N_FILL = N_EXPERTS + 1


def _dispatch_kernel(n_first, dest_ref, fill_ref, ha_ref, hb_ref, xs_ref, zero_scr, sem):
    base = pl.program_id(0) * (DISPATCH_ROWS * TOP_K)

    @pl.when(pl.program_id(0) == 0)
    def _():
        zero_scr[...] = jnp.zeros(zero_scr.shape, I32)

        def fill_copy(slot, n):
            return pltpu.make_async_copy(zero_scr.at[_tok_rows(0, n), :], xs_ref.at[_tok_rows(slot, n), :], sem)

        def for_fill(e, n, op):
            def one(r, c):
                op(fill_copy(fill_ref[e] + r * n, n))
                return c

            lax.fori_loop(0, fill_ref[N_FILL + e], one, 0)

        def each_range(op):
            def expert(e, carry):
                for_fill(e, 1, op)
                return carry

            lax.fori_loop(0, N_EXPERTS, expert, 0)
            for_fill(N_EXPERTS, MOE_CHUNK, op)

        each_range(lambda cp: cp.start())
        each_range(lambda cp: cp.wait())

    def scatter(h_ref):
        def issue(t, carry):
            for k in range(TOP_K):
                pltpu.make_async_copy(h_ref.at[_tok_rows(t, 1), :],
                                      xs_ref.at[_tok_rows(dest_ref[base + t * TOP_K + k], 1), :],
                                      sem).start(priority=k % 2)
            return carry

        lax.fori_loop(0, DISPATCH_ROWS, issue, 0, unroll=DMA_UNROLL // TOP_K)
        for _ in range(TOP_K):
            pltpu.make_async_copy(h_ref, xs_ref.at[_tok_rows(0, DISPATCH_ROWS), :], sem).wait()

    @pl.when(pl.program_id(0) < n_first)
    def _():
        scatter(ha_ref)

    @pl.when(pl.program_id(0) >= n_first)
    def _():
        scatter(hb_ref)


def _dispatch(dest, fill, h2_a, h2_b, cap):
    n_a = h2_a.shape[0] // TOK_SUB // DISPATCH_ROWS
    n_b = h2_b.shape[0] // TOK_SUB // DISPATCH_ROWS
    blk = (DISPATCH_ROWS * TOK_SUB, LANES)
    return pl.pallas_call(
        functools.partial(_dispatch_kernel, n_a),
        out_shape=jax.ShapeDtypeStruct((cap * TOK_SUB, LANES), I32),
        grid_spec=pltpu.PrefetchScalarGridSpec(
            num_scalar_prefetch=2,
            grid=(n_a + n_b,),
            in_specs=[pl.BlockSpec(blk, lambda i, d, z: (jnp.minimum(i, n_a - 1), 0)),
                      pl.BlockSpec(blk, lambda i, d, z: (jnp.maximum(i - n_a, 0), 0))],
            out_specs=pl.BlockSpec(memory_space=pl.ANY),
            scratch_shapes=[pltpu.VMEM((MOE_CHUNK * TOK_SUB, LANES), I32), pltpu.SemaphoreType.DMA]),
        compiler_params=_cparams(("arbitrary",)),
        name="moe_dispatch",
    )(dest, fill, h2_a, h2_b)


MOE_SPAN = 8


def _moe_kernel(ie_ref, ir_ref, in_ref, xs_ref, wg_hbm, bg_ref, wu_hbm, bu_ref, wd_hbm, bd_ref, ys_ref,
                xin_scr, acc_scr, wgf_scr, wuf_scr, wdf_scr, wg_scr, wu_scr, wd_scr, sem_x, sem_y, sem_w):
    i = pl.program_id(0)
    n_items = pl.num_programs(0)
    nch = in_ref[i]
    row0 = ir_ref[i]
    expert = ie_ref[i]
    slot = i % 2
    down_row = pl.ds(expert, 1)

    def w_copies(e, f, ws):
        col0 = f * FF_TILE
        cols = pl.ds(col0 if isinstance(col0, int) else pl.multiple_of(col0, FF_TILE), FF_TILE)
        return (pltpu.make_async_copy(wg_hbm.at[e, :, cols], wgf_scr.at[ws], sem_w.at[ws]),
                pltpu.make_async_copy(wu_hbm.at[e, :, cols], wuf_scr.at[ws], sem_w.at[ws]),
                pltpu.make_async_copy(wd_hbm.at[e, cols, :], wdf_scr.at[ws], sem_w.at[ws]))

    def rows_of(c, k=1):
        return pl.ds(pl.multiple_of(c * MOE_CHUNK, MOE_CHUNK), k * MOE_CHUNK)

    def x_copy(item_row0, c, s):
        return pltpu.make_async_copy(xs_ref.at[_tok_rows(item_row0 + c * MOE_CHUNK, MOE_CHUNK), :],
                                     xin_scr.at[s, _tok_rows(c * MOE_CHUNK, MOE_CHUNK), :], sem_x.at[s])

    def y_copy(item_row0, c, s):
        return pltpu.make_async_copy(xin_scr.at[s, _tok_rows(c * MOE_CHUNK, MOE_CHUNK), :],
                                     ys_ref.at[_tok_rows(item_row0 + c * MOE_CHUNK, MOE_CHUNK), :], sem_y)

    def for_chunks(n, body):
        def step(c, carry):
            body(c)
            return carry

        lax.fori_loop(0, n, step, 0)

    def for_spans(n, body, first_span):
        def span(g, carry):
            body(g * MOE_SPAN, MOE_SPAN)
            return carry

        lax.fori_loop(first_span, n // MOE_SPAN, span, 0)
        k = MOE_SPAN // 2
        while k >= 1:
            start = n // (2 * k) * (2 * k)

            @pl.when(n % (2 * k) >= k)
            def _(start=start, k=k):
                body(start, k)

            k //= 2

    @pl.when(i > 0)
    def _():
        prev = jnp.maximum(i - 1, 0)
        for_chunks(in_ref[prev], lambda c: y_copy(ir_ref[prev], c, 1 - slot).wait())

    @pl.when(i == 0)
    def _():
        for_chunks(nch, lambda c: x_copy(row0, c, slot).start())
        for cp in w_copies(expert, 0, 0):
            cp.start()

    @pl.when(i + 1 < n_items)
    def _():
        nxt = jnp.minimum(i + 1, n_items - 1)
        for_chunks(in_ref[nxt], lambda c: x_copy(ir_ref[nxt], c, 1 - slot).start())

    for_chunks(nch, lambda c: x_copy(row0, c, slot).wait())

    @pl.when(nch > 0)
    def _():
        def cast_weights(ws):
            w = (wgf_scr[ws].astype(BF16), wuf_scr[ws].astype(BF16), wdf_scr[ws].astype(BF16))
            wg_scr[...], wu_scr[...], wd_scr[...] = w
            return w

        def contribution(f, c, k, w):
            wg, wu, wd = (wg_scr[...], wu_scr[...], wd_scr[...]) if w is None else w
            bias_row = pl.ds(expert * N_FF_TILES + f, 1)
            lo, hi = _unpack_halves(_tok_load(xin_scr.at[slot], c * MOE_CHUNK, k * MOE_CHUNK))
            x = jnp.concatenate([lo.astype(BF16), hi.astype(BF16)], axis=1)
            gt = jnp.minimum(_dot(x, wg) + bg_ref[bias_row, :], SWIGLU_LIMIT)
            up = jnp.clip(_dot(x, wu) + bu_ref[bias_row, :], -SWIGLU_LIMIT, SWIGLU_LIMIT)
            act = (up + 1.0) * (gt * _sigmoid(SWIGLU_ALPHA * gt))
            return _dot(act.astype(BF16), wd)

        def first(f, c, k, w=None):
            acc_scr[rows_of(c, k), :] = contribution(f, c, k, w) + bd_ref[down_row, :]

        def middle(f, c, k, w=None):
            acc_scr[rows_of(c, k), :] += contribution(f, c, k, w)

        def last(f, c, k, w=None):
            y = acc_scr[rows_of(c, k), :] + contribution(f, c, k, w)
            _tok_store(xin_scr.at[slot], c * MOE_CHUNK, k * MOE_CHUNK, _pack_halves(y))
            for u in range(k):
                y_copy(row0, c + u, slot).start()

        def ff_step(f, phase, request_next):
            ws = f % 2
            for cp in w_copies(expert, f, ws):
                cp.wait()
            request_next()
            body = functools.partial(phase, f)

            @pl.when(nch >= MOE_SPAN)
            def _():
                body(0, MOE_SPAN, cast_weights(ws))

            @pl.when(nch < MOE_SPAN)
            def _():
                cast_weights(ws)

            for_spans(nch, body, 1)

        def request_tile(f):
            def go():
                for cp in w_copies(expert, f, f % 2):
                    cp.start()
            return go

        def request_next_item():
            nxt = jnp.minimum(i + 1, n_items - 1)

            @pl.when(jnp.logical_and(i + 1 < n_items, in_ref[nxt] > 0))
            def _():
                for cp in w_copies(ie_ref[nxt], 0, 0):
                    cp.start()

        ff_step(0, first, request_tile(1))

        def mid(f, carry):
            ff_step(f, middle, request_tile(f + 1))
            return carry

        lax.fori_loop(1, N_FF_TILES - 1, mid, 0)
        ff_step(N_FF_TILES - 1, last, request_next_item)

        @pl.when(i == n_items - 1)
        def _():
            for_chunks(nch, lambda c: y_copy(row0, c, slot).wait())


def _moe_experts(item_e, item_row0, item_nch, xs, w_g, b_g, w_u, b_u, w_d, b_d):
    n_items = item_e.shape[0]
    whole = lambda a: pl.BlockSpec(a.shape, lambda i, ie, ir, inch: (0, 0))
    return pl.pallas_call(
        _moe_kernel,
        out_shape=jax.ShapeDtypeStruct(xs.shape, I32),
        grid_spec=pltpu.PrefetchScalarGridSpec(
            num_scalar_prefetch=3,
            grid=(n_items,),
            in_specs=[
                pl.BlockSpec(memory_space=pl.ANY),
                pl.BlockSpec(memory_space=pl.ANY), whole(b_g),
                pl.BlockSpec(memory_space=pl.ANY), whole(b_u),
                pl.BlockSpec(memory_space=pl.ANY), whole(b_d),
            ],
            out_specs=pl.BlockSpec(memory_space=pl.ANY),
            scratch_shapes=[
                pltpu.VMEM((2, MOE_ITEM_ROWS * TOK_SUB, LANES), I32),
                pltpu.VMEM((MOE_ITEM_ROWS, D_MODEL), F32),
                pltpu.VMEM((2, D_MODEL, FF_TILE), F32),
                pltpu.VMEM((2, D_MODEL, FF_TILE), F32),
                pltpu.VMEM((2, FF_TILE, D_MODEL), F32),
                pltpu.VMEM((D_MODEL, FF_TILE), BF16),
                pltpu.VMEM((D_MODEL, FF_TILE), BF16),
                pltpu.VMEM((FF_TILE, D_MODEL), BF16),
                pltpu.SemaphoreType.DMA((2,)),
                pltpu.SemaphoreType.DMA,
                pltpu.SemaphoreType.DMA((2,)),
            ]),
        input_output_aliases={3: 0},
        compiler_params=_cparams(("arbitrary",)),
        name="moe_experts",
    )(item_e, item_row0, item_nch, xs, w_g, b_g, w_u, b_u, w_d, b_d)


COMBINE_ROWS = 256


def _combine_kernel(dest_ref, ys_ref, x1_ref, tw_ref, mods_ref, o_ref, ybuf, sem):
    step = pl.program_id(0)
    slot = step % 2

    def gather(blk, s):
        base = blk * (COMBINE_ROWS * TOP_K)

        def issue(t, carry):
            for k in range(TOP_K):
                pltpu.make_async_copy(ys_ref.at[_tok_rows(dest_ref[base + t * TOP_K + k], 1), :],
                                      ybuf.at[s, k, _tok_rows(t, 1), :], sem.at[s]).start(priority=k % 2)
            return carry

        lax.fori_loop(0, COMBINE_ROWS, issue, 0, unroll=DMA_UNROLL // TOP_K)

    @pl.when(step == 0)
    def _():
        gather(step, slot)

    @pl.when(step + 1 < pl.num_programs(0))
    def _():
        gather(step + 1, 1 - slot)

    for k in range(TOP_K):
        pltpu.make_async_copy(ys_ref.at[_tok_rows(0, COMBINE_ROWS), :], ybuf.at[slot, k], sem.at[slot]).wait()

    y_lo = y_hi = None
    for k in range(TOP_K):
        lo, hi = _unpack_halves(_tok_load(ybuf.at[slot, k], 0, COMBINE_ROWS))
        w = tw_ref[:, k:k + 1]
        y_lo = w * lo if y_lo is None else y_lo + w * lo
        y_hi = w * hi if y_hi is None else y_hi + w * hi
    o_ref[:, :HALF] = x1_ref[:, :HALF] + mods_ref[5:6, :HALF] * y_lo
    o_ref[:, HALF:] = x1_ref[:, HALF:] + mods_ref[5:6, HALF:] * y_hi


def _combine(dest, ys, x1, tw, mods, mod_row):
    t = x1.shape[0]
    return pl.pallas_call(
        _combine_kernel,
        out_shape=jax.ShapeDtypeStruct((t, D_MODEL), F32),
        grid_spec=pltpu.PrefetchScalarGridSpec(
            num_scalar_prefetch=1,
            grid=(t // COMBINE_ROWS,),
            in_specs=[pl.BlockSpec(memory_space=pl.ANY),
                      pl.BlockSpec((COMBINE_ROWS, D_MODEL), lambda i, d: (i, 0)),
                      pl.BlockSpec((COMBINE_ROWS, LANES), lambda i, d: (i, 0)),
                      pl.BlockSpec((None, 6, D_MODEL), lambda i, d: (mod_row(i * COMBINE_ROWS), 0, 0))],
            out_specs=pl.BlockSpec((COMBINE_ROWS, D_MODEL), lambda i, d: (i, 0)),
            scratch_shapes=[pltpu.VMEM((2, TOP_K, COMBINE_ROWS * TOK_SUB, LANES), I32),
                            pltpu.SemaphoreType.DMA((2,))]),
        compiler_params=_cparams(("arbitrary",)),
        name="moe_combine",
    )(dest, ys, x1, tw, mods)


def _routing_tables(top_idx, n_items, cap):
    flat_e = top_idx.reshape(-1)
    onehot = (flat_e[:, None] == jnp.arange(N_EXPERTS, dtype=I32)[None, :]).astype(I32)
    csum = jnp.cumsum(onehot, axis=0)
    rank = jnp.take_along_axis(csum, flat_e[:, None], axis=1)[:, 0] - 1
    counts = csum[-1]
    nch = (counts + MOE_CHUNK - 1) // MOE_CHUNK
    pad_rows = nch * MOE_CHUNK
    pad_end = jnp.cumsum(pad_rows)
    pad_start = pad_end - pad_rows
    dest = (pad_start[flat_e] + rank).astype(I32)
    items_e = (nch + MOE_ITEM_CHUNKS - 1) // MOE_ITEM_CHUNKS
    item_end = jnp.cumsum(items_e)
    item_start = item_end - items_e
    total = item_end[-1]
    i = jnp.arange(n_items, dtype=I32)
    ii = jnp.minimum(i, total - 1)
    e_i = jnp.minimum(jnp.searchsorted(item_end, ii, side="right"), N_EXPERTS - 1).astype(I32)
    local = ii - item_start[e_i]
    row0 = (pad_start[e_i] + local * MOE_ITEM_ROWS).astype(I32)
    n_i = jnp.where(i < total, jnp.minimum(MOE_ITEM_CHUNKS, nch[e_i] - local * MOE_ITEM_CHUNKS), 0).astype(I32)
    fill = jnp.concatenate([pad_start + counts, pad_end[-1:], pad_rows - counts, (cap - pad_end[-1:]) // MOE_CHUNK]).astype(I32)
    return dest, fill, e_i, row0, n_i


def _rope_tables(n_lat):
    nf = ROPE // 4
    inv = ROPE_BASE ** (-jnp.arange(nf, dtype=F32) / nf)
    t = jnp.arange(n_lat)
    row = (t // GRID_W).astype(F32)
    col = (t % GRID_W).astype(F32)
    ang_r = row[:, None] * inv[None, :]
    ang_c = col[:, None] * inv[None, :]
    z = jnp.zeros((n_lat, nf), F32)
    tail = jnp.zeros((n_lat, LANES - ROPE), F32)
    cos = jnp.concatenate([jnp.cos(ang_r), jnp.cos(ang_r), jnp.cos(ang_c), jnp.cos(ang_c), tail], axis=1)
    sin_lo = jnp.concatenate([-jnp.sin(ang_r), z, -jnp.sin(ang_c), z, tail], axis=1)
    sin_hi = jnp.concatenate([z, jnp.sin(ang_r), z, jnp.sin(ang_c), tail], axis=1)
    return cos, sin_lo, sin_hi


def kernel(x_prompt, x_sample, cache_ckv, cache_krope, c, c_ctx, norm1_g, norm2_g, w_ada, b_ada, w_in,
           b_branch_gate, kv_norm_g, w_ukv, q_norm_g, k_norm_g, w_o_attn, w_pool, pool_scale, w_o_pool, w_out,
           router_w, router_b, w_exp_gate, b_exp_gate, w_exp_up, b_exp_up, w_exp_down, b_exp_down):
    assert w_in.shape[0] == 1, "single-layer trunk"
    batch, seq, _ = x_prompt.shape
    dec_batch, n_lat, _ = x_sample.shape
    past = cache_ckv.shape[2]
    n_ctx = batch * seq
    n_dec = dec_batch * n_lat

    cond = jnp.concatenate([c_ctx[None, :], c, jnp.zeros((8 - 1 - dec_batch, D_MODEL), F32)], axis=0)
    mods = _adaln(cond, w_ada[0], b_ada).reshape(8, 6, D_MODEL)

    w_cat = _w_in_layout(w_in[0].T)
    qg = jnp.pad(q_norm_g * ATTN_SCALE, ((0, 0), (0, HEAD_PAD - QK)))
    kgn = k_norm_g[:, :NOPE]
    kgr = jnp.pad(k_norm_g[:, NOPE:], ((0, 0), (0, LANES - ROPE)))
    w_ukv_b = w_ukv[0].astype(BF16)
    rope_tabs = _rope_tables(n_lat)

    ctx_row = lambda i: 0
    tm_in = 512
    lat_row_in = lambda i: 1 + (i * tm_in) // n_lat
    q_c, pool_c, gates_c, ckv_c, kr_c, krp_c = _in_proj(
        x_prompt.reshape(n_ctx, D_MODEL), mods, ctx_row, norm1_g, w_cat, qg, kv_norm_g, None, tm_in)
    q_l, pool_l, gates_l, ckv_l, _, krp_l = _in_proj(
        x_sample.reshape(n_dec, D_MODEL), mods, lat_row_in, norm1_g, w_cat, qg, kv_norm_g, rope_tabs, tm_in)

    tr = 256
    k_c, v_c = _kv_expand(ckv_c, krp_c, w_ukv_b, kgn, kgr, None, tr)
    k_l, v_l = _kv_expand(ckv_l, krp_l, w_ukv_b, kgn, kgr, rope_tabs, tr)
    cache_krp = jnp.pad(cache_krope.reshape(dec_batch * past, ROPE), ((0, 0), (0, LANES - ROPE)))
    k_p, v_p = _kv_expand(cache_ckv.reshape(dec_batch * past, KV_RANK), cache_krp, w_ukv_b, kgn, kgr, None, tr)

    attn_c = _attn_ctx(q_c, k_c, v_c, seq)
    attn_l = _attn_lat(q_l, k_l, v_l, k_p, v_p, n_lat, past, 512)

    w_pool_b = w_pool[0].astype(BF16)
    poolo_c = _pool(pool_c, w_pool_b, pool_scale, seq)
    poolo_l = _pool(pool_l, w_pool_b, pool_scale, n_lat)

    woa = w_o_attn[0].astype(BF16)
    wop = w_o_pool[0].astype(BF16)
    wout = w_out[0].astype(BF16)
    rw = jnp.pad(router_w[0], ((0, 0), (0, LANES - N_EXPERTS))).astype(BF16)
    rb = jnp.pad(router_b, ((0, 0), (0, LANES - N_EXPERTS)), constant_values=NEG_BIG)
    tm_mg = 256
    lat_row_mg = lambda i: 1 + (i * tm_mg) // n_lat
    x1_c, h2_c, tidx_c, tw_c = _merge(attn_c, poolo_c, gates_c, x_prompt.reshape(n_ctx, D_MODEL), mods, ctx_row,
                                      b_branch_gate, woa, wop, wout, norm2_g, rw, rb, tm_mg)
    x1_l, h2_l, tidx_l, tw_l = _merge(attn_l, poolo_l, gates_l, x_sample.reshape(n_dec, D_MODEL), mods, lat_row_mg,
                                      b_branch_gate, woa, wop, wout, norm2_g, rw, rb, tm_mg)

    n_assign = (n_ctx + n_dec) * TOP_K
    max_chunks = (n_assign + N_EXPERTS * (MOE_CHUNK - 1)) // MOE_CHUNK
    cap = max_chunks * MOE_CHUNK
    n_items = (max_chunks + N_EXPERTS * (MOE_ITEM_CHUNKS - 1)) // MOE_ITEM_CHUNKS
    top_idx = jnp.concatenate([tidx_c[:, :TOP_K], tidx_l[:, :TOP_K]], axis=0)
    dest, fill, item_e, item_row0, item_nch = _routing_tables(top_idx, n_items, cap)

    xs = _dispatch(dest, fill, h2_c, h2_l, cap)
    ys = _moe_experts(item_e, item_row0, item_nch, xs,
                      w_exp_gate[0], b_exp_gate[0].reshape(N_EXPERTS * N_FF_TILES, FF_TILE),
                      w_exp_up[0], b_exp_up[0].reshape(N_EXPERTS * N_FF_TILES, FF_TILE),
                      w_exp_down[0], b_exp_down[0])

    y_c = _combine(dest[:n_ctx * TOP_K], ys, x1_c, tw_c, mods, lambda r: 0)
    y_l = _combine(dest[n_ctx * TOP_K:], ys, x1_l, tw_l, mods, lambda r: 1 + r // n_lat)

    return (y_c.reshape(batch, seq, D_MODEL),
            y_l.reshape(dec_batch, n_lat, D_MODEL),
            ckv_c.reshape(batch, 1, seq, KV_RANK),
            kr_c.reshape(batch, 1, seq, ROPE))
```

```python
import functools

import jax
import jax.numpy as jnp
from jax import lax
from jax.experimental import pallas as pl
from jax.experimental.pallas import tpu as pltpu

F32 = jnp.float32
BF16 = jnp.bfloat16
I32 = jnp.int32

D_MODEL = 2048
N_HEADS = 16
NOPE = 128
ROPE = 64
QK = NOPE + ROPE
V_DIM = 128
KV_RANK = 512
POOL_WINDOWS = (2, 4, 8, 16)
POOL_WIDTH = 1024
POOL_GROUP = POOL_WIDTH // len(POOL_WINDOWS)
N_EXPERTS = 32
TOP_K = 4
D_FF = 2048
SWIGLU_LIMIT = 7.0
SWIGLU_ALPHA = 1.702
ROPE_BASE = 10000.0
RMS_EPS = 1e-6
GRID_W = 64
ATTN_SCALE = QK ** -0.5

LANES = 128
HEAD_PAD = 2 * LANES
Q_COLS = N_HEADS * HEAD_PAD
IN_TILE = KV_RANK
N_Q_TILES = Q_COLS // IN_TILE
POOL_TILE0 = N_Q_TILES
N_POOL_TILES = POOL_WIDTH // IN_TILE
GATE_TILE0 = POOL_TILE0 + N_POOL_TILES
N_GATE_TILES = 2 * D_MODEL // IN_TILE
CKV_TILE = GATE_TILE0 + N_GATE_TILES
KR_TILE = CKV_TILE + 1
N_IN_TILES = KR_TILE + 1
W_CAT_COLS = N_IN_TILES * IN_TILE
HEADS_PER_TILE = IN_TILE // HEAD_PAD

MOE_CHUNK = 128
MOE_ITEM_CHUNKS = 16
MOE_ITEM_ROWS = MOE_CHUNK * MOE_ITEM_CHUNKS
FF_TILE = 256
N_FF_TILES = D_FF // FF_TILE
NEG_BIG = -1e30

VMEM_LIMIT = 56 * 1024 * 1024


def _cparams(sem, vmem=VMEM_LIMIT):
    return pltpu.CompilerParams(dimension_semantics=sem, vmem_limit_bytes=vmem)


def _dot(a, b):
    return jnp.dot(a, b, preferred_element_type=F32)


def _sigmoid(z):
    return 1.0 / (1.0 + jnp.exp(-z))


HALF = D_MODEL // 2
HI_MASK = -65536


def _pack_halves(v):
    lo = lax.bitcast_convert_type(v[:, :HALF].astype(BF16).astype(F32), I32)
    hi = lax.bitcast_convert_type(v[:, HALF:].astype(BF16).astype(F32), I32)
    return jnp.bitwise_or(jnp.bitwise_and(hi, HI_MASK), lax.shift_right_logical(lo, 16))


TOK_SUB = HALF // LANES


def _tok_rows(tok0, n):
    row0 = tok0 * TOK_SUB
    return pl.ds(row0 if isinstance(row0, int) else pl.multiple_of(row0, TOK_SUB), n * TOK_SUB)


def _tok_store(ref, tok0, n, words):
    for s in range(TOK_SUB):
        ref[pl.ds(tok0 * TOK_SUB + s, n, stride=TOK_SUB), :] = words[:, s * LANES:(s + 1) * LANES]


def _tok_load(ref, tok0, n):
    return jnp.concatenate([ref[pl.ds(tok0 * TOK_SUB + s, n, stride=TOK_SUB), :] for s in range(TOK_SUB)], axis=1)


def _unpack_halves(w):
    lo = lax.bitcast_convert_type(lax.shift_left(w, 16), F32)
    hi = lax.bitcast_convert_type(jnp.bitwise_and(w, HI_MASK), F32)
    return lo, hi


def _rope_lanes(y, cos, sin_lo, sin_hi):
    return y * cos + pltpu.roll(y, LANES - 16, axis=1) * sin_lo + pltpu.roll(y, 16, axis=1) * sin_hi


def _adaln_kernel(c_ref, w_ref, b_ref, o_ref):
    c = c_ref[...]
    s = (c * _sigmoid(c)).astype(BF16)
    o_ref[...] = _dot(s, w_ref[...].astype(BF16)) + b_ref[...]


def _adaln(cond, w_ada, b_ada):
    rows = cond.shape[0]
    n = w_ada.shape[1]
    tn = 1024
    return pl.pallas_call(
        _adaln_kernel,
        out_shape=jax.ShapeDtypeStruct((rows, n), F32),
        grid=(n // tn,),
        in_specs=[pl.BlockSpec((rows, D_MODEL), lambda j: (0, 0)),
                  pl.BlockSpec((D_MODEL, tn), lambda j: (0, j)),
                  pl.BlockSpec((1, tn), lambda j: (0, j))],
        out_specs=pl.BlockSpec((rows, tn), lambda j: (0, j)),
        compiler_params=_cparams(("arbitrary",)),
        name="adaln",
    )(cond, w_ada, b_ada)


W_Q_END = N_HEADS * QK
W_CKV_END = W_Q_END + KV_RANK
W_KR_END = W_CKV_END + ROPE
N_IN_COLS = W_KR_END + POOL_WIDTH + 2 * D_MODEL
W_PREP = 256


def _w_in_layout_kernel(wt_ref, o_ref):
    lane = lax.broadcasted_iota(I32, (W_PREP, W_PREP), 1)

    def move(src_row, dst_col, keep):
        sq = wt_ref[src_row:src_row + W_PREP, :].T
        if keep < W_PREP:
            sq = jnp.where(lane < keep, sq, 0.0)
        o_ref[:, dst_col:dst_col + W_PREP] = sq.astype(BF16)

    for h in range(N_HEADS):
        move(h * QK, h * HEAD_PAD, QK)
    for p in range((N_IN_COLS - W_KR_END) // W_PREP):
        move(W_KR_END + p * W_PREP, Q_COLS + p * W_PREP, W_PREP)
    kv0 = Q_COLS + N_IN_COLS - W_KR_END
    kv_cols = KV_RANK + ROPE
    for p in range((W_CAT_COLS - kv0) // W_PREP):
        keep = min(max(kv_cols - p * W_PREP, 0), W_PREP)
        if keep:
            move(W_Q_END + p * W_PREP, kv0 + p * W_PREP, keep)
        else:
            o_ref[:, kv0 + p * W_PREP:kv0 + (p + 1) * W_PREP] = jnp.zeros((W_PREP, W_PREP), BF16)


def _w_in_layout(wt):
    return pl.pallas_call(
        _w_in_layout_kernel,
        out_shape=jax.ShapeDtypeStruct((D_MODEL, W_CAT_COLS), BF16),
        grid=(D_MODEL // W_PREP,),
        in_specs=[pl.BlockSpec((N_IN_COLS, W_PREP), lambda i: (0, i))],
        out_specs=pl.BlockSpec((W_PREP, W_CAT_COLS), lambda i: (i, 0)),
        compiler_params=_cparams(("arbitrary",)),
        name="w_in_layout",
    )(wt)


def _in_proj_kernel(rope, x_ref, mods_ref, g1_ref, w_ref, qg_ref, kvg_ref, *rest):
    if rope:
        cos_ref, slo_ref, shi_ref = rest[:3]
        rest = rest[3:]
    q_ref, pool_ref, gates_ref, ckv_ref, kr_ref, krp_ref, h_scr = rest
    j = pl.program_id(1)

    def proj(c0, c1, h=None):
        return _dot(h_scr[...] if h is None else h, w_ref[:, c0:c1])

    def q_heads(h=None):
        for hh in range(HEADS_PER_TILE):
            a = proj(hh * HEAD_PAD, (hh + 1) * HEAD_PAD, h)
            r = lax.rsqrt(jnp.sum(a * a, axis=-1, keepdims=True) / QK + RMS_EPS)
            y = a * r * qg_ref[...]
            if rope:
                yr = _rope_lanes(y[:, LANES:], cos_ref[...], slo_ref[...], shi_ref[...])
                q_ref[:, hh * HEAD_PAD:hh * HEAD_PAD + LANES] = y[:, :LANES].astype(BF16)
                q_ref[:, hh * HEAD_PAD + LANES:(hh + 1) * HEAD_PAD] = yr.astype(BF16)
            else:
                q_ref[:, hh * HEAD_PAD:(hh + 1) * HEAD_PAD] = y.astype(BF16)

    @pl.when(j == 0)
    def _():
        x = x_ref[...]
        y = x * lax.rsqrt(jnp.mean(x * x, axis=-1, keepdims=True) + RMS_EPS) * g1_ref[...]
        h = (y * (1.0 + mods_ref[1:2, :]) + mods_ref[0:1, :]).astype(BF16)
        h_scr[...] = h
        q_heads(h)

    @pl.when(jnp.logical_and(j > 0, j < N_Q_TILES))
    def _():
        q_heads()

    @pl.when(jnp.logical_and(j >= POOL_TILE0, j < GATE_TILE0))
    def _():
        pool_ref[...] = proj(0, IN_TILE)

    @pl.when(jnp.logical_and(j >= GATE_TILE0, j < CKV_TILE))
    def _():
        gates_ref[...] = proj(0, IN_TILE)

    @pl.when(j == CKV_TILE)
    def _():
        a = proj(0, KV_RANK)
        r = lax.rsqrt(jnp.mean(a * a, axis=-1, keepdims=True) + RMS_EPS)
        ckv_ref[...] = a * r * kvg_ref[...]

    @pl.when(j == KR_TILE)
    def _():
        krp = proj(0, LANES)
        krp_ref[...] = krp
        kr_ref[...] = krp[:, :ROPE]


def _in_proj(x, mods, mod_row, g1, w_cat, qg, kvg, rope_tabs, tm):
    t = x.shape[0]
    rope = rope_tabs is not None
    in_specs = [
        pl.BlockSpec((tm, D_MODEL), lambda i, j: (i, 0)),
        pl.BlockSpec((None, 6, D_MODEL), lambda i, j: (mod_row(i), 0, 0)),
        pl.BlockSpec((1, D_MODEL), lambda i, j: (0, 0)),
        pl.BlockSpec((D_MODEL, IN_TILE), lambda i, j: (0, j)),
        pl.BlockSpec((1, HEAD_PAD), lambda i, j: (0, 0)),
        pl.BlockSpec((1, KV_RANK), lambda i, j: (0, 0)),
    ]
    args = [x, mods, g1, w_cat, qg, kvg]
    if rope:
        seq_tiles = rope_tabs[0].shape[0] // tm
        in_specs += [pl.BlockSpec((tm, LANES), lambda i, j: (i % seq_tiles, 0))] * 3
        args += list(rope_tabs)
    out_shape = (
        jax.ShapeDtypeStruct((t, Q_COLS), BF16),
        jax.ShapeDtypeStruct((t, POOL_WIDTH), F32),
        jax.ShapeDtypeStruct((t, 2 * D_MODEL), F32),
        jax.ShapeDtypeStruct((t, KV_RANK), F32),
        jax.ShapeDtypeStruct((t, ROPE), F32),
        jax.ShapeDtypeStruct((t, LANES), F32),
    )
    out_specs = (
        pl.BlockSpec((tm, IN_TILE), lambda i, j: (i, jnp.minimum(j, N_Q_TILES - 1))),
        pl.BlockSpec((tm, IN_TILE), lambda i, j: (i, jnp.clip(j - POOL_TILE0, 0, N_POOL_TILES - 1))),
        pl.BlockSpec((tm, IN_TILE), lambda i, j: (i, jnp.clip(j - GATE_TILE0, 0, N_GATE_TILES - 1))),
        pl.BlockSpec((tm, KV_RANK), lambda i, j: (i, 0)),
        pl.BlockSpec((tm, ROPE), lambda i, j: (i, 0)),
        pl.BlockSpec((tm, LANES), lambda i, j: (i, 0)),
    )
    return pl.pallas_call(
        functools.partial(_in_proj_kernel, rope),
        out_shape=out_shape,
        grid=(t // tm, N_IN_TILES),
        in_specs=in_specs,
        out_specs=out_specs,
        scratch_shapes=[pltpu.VMEM((tm, D_MODEL), BF16)],
        compiler_params=_cparams(("arbitrary", "arbitrary")),
        name="in_proj_rope" if rope else "in_proj",
    )(*args)


def _kv_expand_kernel(rope, ckv_ref, krp_ref, w_ref, kgn_ref, kgr_ref, *rest):
    if rope:
        cos_ref, slo_ref, shi_ref = rest[:3]
        rest = rest[3:]
    k_ref, v_ref = rest
    kv = _dot(ckv_ref[...].astype(BF16), w_ref[...])
    kr = krp_ref[...]
    ssq_r = jnp.sum(kr * kr, axis=-1, keepdims=True)
    krg = kr * kgr_ref[...]
    if rope:
        krg = _rope_lanes(krg, cos_ref[...], slo_ref[...], shi_ref[...])
    for h in range(N_HEADS):
        kn = kv[:, h * HEAD_PAD:h * HEAD_PAD + NOPE]
        r = lax.rsqrt((jnp.sum(kn * kn, axis=-1, keepdims=True) + ssq_r) / QK + RMS_EPS)
        k_ref[:, h * HEAD_PAD:h * HEAD_PAD + NOPE] = (kn * r * kgn_ref[...]).astype(BF16)
        k_ref[:, h * HEAD_PAD + NOPE:(h + 1) * HEAD_PAD] = (krg * r).astype(BF16)
        v_ref[:, h * V_DIM:(h + 1) * V_DIM] = kv[:, h * HEAD_PAD + NOPE:(h + 1) * HEAD_PAD].astype(BF16)


def _kv_expand(ckv_n, krp, w_ukv, kgn, kgr, rope_tabs, tr):
    r = ckv_n.shape[0]
    rope = rope_tabs is not None
    in_specs = [
        pl.BlockSpec((tr, KV_RANK), lambda i: (i, 0)),
        pl.BlockSpec((tr, LANES), lambda i: (i, 0)),
        pl.BlockSpec((KV_RANK, N_HEADS * HEAD_PAD), lambda i: (0, 0)),
        pl.BlockSpec((1, LANES), lambda i: (0, 0)),
        pl.BlockSpec((1, LANES), lambda i: (0, 0)),
    ]
    args = [ckv_n, krp, w_ukv, kgn, kgr]
    if rope:
        seq_tiles = rope_tabs[0].shape[0] // tr
        in_specs += [pl.BlockSpec((tr, LANES), lambda i: (i % seq_tiles, 0))] * 3
        args += list(rope_tabs)
    return pl.pallas_call(
        functools.partial(_kv_expand_kernel, rope),
        out_shape=(jax.ShapeDtypeStruct((r, N_HEADS * HEAD_PAD), BF16),
                   jax.ShapeDtypeStruct((r, N_HEADS * V_DIM), BF16)),
        grid=(r // tr,),
        in_specs=in_specs,
        out_specs=(pl.BlockSpec((tr, N_HEADS * HEAD_PAD), lambda i: (i, 0)),
                   pl.BlockSpec((tr, N_HEADS * V_DIM), lambda i: (i, 0))),
        compiler_params=_cparams(("arbitrary",)),
        name="kv_expand_rope" if rope else "kv_expand",
    )(*args)


def _qk(q, k):
    return lax.dot_general(q, k, (((1,), (1,)), ((), ())), preferred_element_type=F32)


def _attn_ctx_kernel(q_ref, k_ref, v_ref, o_ref):
    for h in range(N_HEADS):
        s = _qk(q_ref[:, h * HEAD_PAD:(h + 1) * HEAD_PAD], k_ref[:, h * HEAD_PAD:(h + 1) * HEAD_PAD])
        p = jnp.exp(s - jnp.max(s, axis=-1, keepdims=True))
        l = jnp.sum(p, axis=-1, keepdims=True)
        o = _dot(p.astype(BF16), v_ref[:, h * V_DIM:(h + 1) * V_DIM])
        o_ref[:, h * V_DIM:(h + 1) * V_DIM] = (o / l).astype(BF16)


def _attn_ctx(q, k, v, seq):
    t = q.shape[0]
    return pl.pallas_call(
        _attn_ctx_kernel,
        out_shape=jax.ShapeDtypeStruct((t, N_HEADS * V_DIM), BF16),
        grid=(t // seq,),
        in_specs=[pl.BlockSpec((seq, Q_COLS), lambda b: (b, 0)),
                  pl.BlockSpec((seq, Q_COLS), lambda b: (b, 0)),
                  pl.BlockSpec((seq, N_HEADS * V_DIM), lambda b: (b, 0))],
        out_specs=pl.BlockSpec((seq, N_HEADS * V_DIM), lambda b: (b, 0)),
        compiler_params=_cparams(("arbitrary",)),
        name="attn_ctx",
    )(q, k, v)


def _attn_lat_kernel(q_ref, k_ref, v_ref, kc_ref, vc_ref, o_ref):
    q = q_ref[...]
    s1 = _qk(q, k_ref[...])
    s2 = _qk(q, kc_ref[...])
    m = jnp.maximum(jnp.max(s1, axis=-1, keepdims=True), jnp.max(s2, axis=-1, keepdims=True))
    p1 = jnp.exp(s1 - m)
    p2 = jnp.exp(s2 - m)
    l = jnp.sum(p1, axis=-1, keepdims=True) + jnp.sum(p2, axis=-1, keepdims=True)
    o = _dot(p1.astype(BF16), v_ref[...]) + _dot(p2.astype(BF16), vc_ref[...])
    o_ref[...] = (o / l).astype(BF16)


def _attn_lat(q, k, v, kc, vc, seq, past, tq):
    t = q.shape[0]
    nq = seq // tq
    return pl.pallas_call(
        _attn_lat_kernel,
        out_shape=jax.ShapeDtypeStruct((t, N_HEADS * V_DIM), BF16),
        grid=(t // seq, N_HEADS, nq),
        in_specs=[pl.BlockSpec((tq, HEAD_PAD), lambda b, h, i: (b * nq + i, h)),
                  pl.BlockSpec((seq, HEAD_PAD), lambda b, h, i: (b, h)),
                  pl.BlockSpec((seq, V_DIM), lambda b, h, i: (b, h)),
                  pl.BlockSpec((past, HEAD_PAD), lambda b, h, i: (b, h)),
                  pl.BlockSpec((past, V_DIM), lambda b, h, i: (b, h))],
        out_specs=pl.BlockSpec((tq, V_DIM), lambda b, h, i: (b * nq + i, h)),
        compiler_params=_cparams(("arbitrary", "arbitrary", "arbitrary")),
        name="attn_lat",
    )(q, k, v, kc, vc)


POOL_HALO = 8


def _pool_kernel(seq, u_ref, w_ref, sc_ref, o_ref, pad_scr):
    zeros = jnp.zeros((POOL_HALO, POOL_WIDTH), F32)
    pad_scr[0:POOL_HALO, :] = zeros
    pad_scr[POOL_HALO + seq:2 * POOL_HALO + seq, :] = zeros
    pad_scr[POOL_HALO:POOL_HALO + seq, :] = u_ref[...]
    t = lax.broadcasted_iota(I32, (seq, 1), 0)
    for g, w in enumerate(POOL_WINDOWS):
        cols = slice(g * POOL_GROUP, (g + 1) * POOL_GROUP)
        tot = None
        for d in range(-(w // 2), w - w // 2):
            piece = pad_scr[POOL_HALO + d:POOL_HALO + d + seq, cols]
            tot = piece if tot is None else tot + piece
        cnt = (jnp.minimum(t + (w - w // 2), seq) - jnp.maximum(t - w // 2, 0)).astype(F32)
        mixed = tot / cnt - u_ref[:, cols]
        o_ref[:, cols] = (_dot(mixed.astype(BF16), w_ref[g]) * sc_ref[:, cols]).astype(BF16)


def _pool(u, w_pool, pool_scale, seq):
    t = u.shape[0]
    n_groups = len(POOL_WINDOWS)
    return pl.pallas_call(
        functools.partial(_pool_kernel, seq),
        out_shape=jax.ShapeDtypeStruct((t, POOL_WIDTH), BF16),
        grid=(t // seq,),
        in_specs=[pl.BlockSpec((seq, POOL_WIDTH), lambda b: (b, 0)),
                  pl.BlockSpec((n_groups, POOL_GROUP, POOL_GROUP), lambda b: (0, 0, 0)),
                  pl.BlockSpec((1, POOL_WIDTH), lambda b: (0, 0))],
        out_specs=pl.BlockSpec((seq, POOL_WIDTH), lambda b: (b, 0)),
        scratch_shapes=[pltpu.VMEM((seq + 2 * POOL_HALO, POOL_WIDTH), F32)],
        compiler_params=_cparams(("arbitrary",)),
        name="pool",
    )(u, w_pool, pool_scale)


def _merge_kernel(attn_ref, pool_ref, gates_ref, x_ref, mods_ref, bbg_ref, woa_ref, wop_ref, wout_ref,
                  g2_ref, rw_ref, rb_ref, x1_ref, h2_ref, tidx_ref, tw_ref):
    a = _dot(attn_ref[...], woa_ref[...])
    p = _dot(pool_ref[...], wop_ref[...])
    ga = _sigmoid(gates_ref[:, :D_MODEL] + bbg_ref[:, :D_MODEL])
    gp = _sigmoid(gates_ref[:, D_MODEL:] + bbg_ref[:, D_MODEL:])
    merged = (ga * a + gp * p).astype(BF16)
    x1 = x_ref[...] + mods_ref[2:3, :] * _dot(merged, wout_ref[...])
    x1_ref[...] = x1
    y = x1 * lax.rsqrt(jnp.mean(x1 * x1, axis=-1, keepdims=True) + RMS_EPS) * g2_ref[...]
    h2 = y * (1.0 + mods_ref[4:5, :]) + mods_ref[3:4, :]
    _tok_store(h2_ref, 0, h2.shape[0], _pack_halves(h2))
    logits = _dot(h2.astype(BF16), rw_ref[...]) + rb_ref[...]
    lane = lax.broadcasted_iota(I32, logits.shape, 1).astype(F32)
    vals, idxs = [], []
    for _ in range(TOP_K):
        m = jnp.max(logits, axis=-1, keepdims=True)
        ix = jnp.min(jnp.where(logits == m, lane, float(LANES)), axis=-1, keepdims=True)
        vals.append(m)
        idxs.append(ix)
        logits = jnp.where(lane == ix, -jnp.inf, logits)
    es = [jnp.exp(v - vals[0]) for v in vals]
    tot = es[0] + es[1] + es[2] + es[3]
    tidx = jnp.zeros(logits.shape, F32)
    tw = jnp.zeros(logits.shape, F32)
    for k in range(TOP_K):
        tidx = jnp.where(lane == k, idxs[k], tidx)
        tw = jnp.where(lane == k, es[k] / tot, tw)
    tidx_ref[...] = tidx.astype(I32)
    tw_ref[...] = tw


def _merge(attn_o, pool_o, gates, x, mods, mod_row, bbg, woa, wop, wout, g2, rw, rb, tm):
    t = x.shape[0]
    const = lambda shape: pl.BlockSpec(shape, lambda i: (0, 0), pipeline_mode=pl.Buffered(1))
    return pl.pallas_call(
        _merge_kernel,
        out_shape=(jax.ShapeDtypeStruct((t, D_MODEL), F32),
                   jax.ShapeDtypeStruct((t * TOK_SUB, LANES), I32),
                   jax.ShapeDtypeStruct((t, LANES), I32),
                   jax.ShapeDtypeStruct((t, LANES), F32)),
        grid=(t // tm,),
        in_specs=[pl.BlockSpec((tm, D_MODEL), lambda i: (i, 0)),
                  pl.BlockSpec((tm, POOL_WIDTH), lambda i: (i, 0)),
                  pl.BlockSpec((tm, 2 * D_MODEL), lambda i: (i, 0)),
                  pl.BlockSpec((tm, D_MODEL), lambda i: (i, 0)),
                  pl.BlockSpec((None, 6, D_MODEL), lambda i: (mod_row(i), 0, 0)),
                  const((1, 2 * D_MODEL)),
                  const((D_MODEL, D_MODEL)),
                  const((POOL_WIDTH, D_MODEL)),
                  const((D_MODEL, D_MODEL)),
                  const((1, D_MODEL)),
                  const((D_MODEL, LANES)),
                  const((1, LANES))],
        out_specs=(pl.BlockSpec((tm, D_MODEL), lambda i: (i, 0)),
                   pl.BlockSpec((tm * TOK_SUB, LANES), lambda i: (i, 0)),
                   pl.BlockSpec((tm, LANES), lambda i: (i, 0)),
                   pl.BlockSpec((tm, LANES), lambda i: (i, 0))),
        compiler_params=_cparams(("arbitrary",)),
        name="merge",
    )(attn_o, pool_o, gates, x, mods, bbg, woa, wop, wout, g2, rw, rb)


DISPATCH_ROWS = 256
DMA_UNROLL = 8


N_FILL = N_EXPERTS + 1


def _dispatch_kernel(n_first, dest_ref, fill_ref, ha_ref, hb_ref, xs_ref, zero_scr, sem):
    base = pl.program_id(0) * (DISPATCH_ROWS * TOP_K)

    @pl.when(pl.program_id(0) == 0)
    def _():
        zero_scr[...] = jnp.zeros(zero_scr.shape, I32)

        def fill_copy(slot, n):
            return pltpu.make_async_copy(zero_scr.at[_tok_rows(0, n), :], xs_ref.at[_tok_rows(slot, n), :], sem)

        def for_fill(e, n, op):
            def one(r, c):
                op(fill_copy(fill_ref[e] + r * n, n))
                return c

            lax.fori_loop(0, fill_ref[N_FILL + e], one, 0)

        def each_range(op):
            def expert(e, carry):
                for_fill(e, 1, op)
                return carry

            lax.fori_loop(0, N_EXPERTS, expert, 0)
            for_fill(N_EXPERTS, MOE_CHUNK, op)

        each_range(lambda cp: cp.start())
        each_range(lambda cp: cp.wait())

    def scatter(h_ref):
        def issue(t, carry):
            for k in range(TOP_K):
                pltpu.make_async_copy(h_ref.at[_tok_rows(t, 1), :],
                                      xs_ref.at[_tok_rows(dest_ref[base + t * TOP_K + k], 1), :],
                                      sem).start(priority=k % 2)
            return carry

        lax.fori_loop(0, DISPATCH_ROWS, issue, 0, unroll=DMA_UNROLL // TOP_K)
        for _ in range(TOP_K):
            pltpu.make_async_copy(h_ref, xs_ref.at[_tok_rows(0, DISPATCH_ROWS), :], sem).wait()

    @pl.when(pl.program_id(0) < n_first)
    def _():
        scatter(ha_ref)

    @pl.when(pl.program_id(0) >= n_first)
    def _():
        scatter(hb_ref)


def _dispatch(dest, fill, h2_a, h2_b, cap):
    n_a = h2_a.shape[0] // TOK_SUB // DISPATCH_ROWS
    n_b = h2_b.shape[0] // TOK_SUB // DISPATCH_ROWS
    blk = (DISPATCH_ROWS * TOK_SUB, LANES)
    return pl.pallas_call(
        functools.partial(_dispatch_kernel, n_a),
        out_shape=jax.ShapeDtypeStruct((cap * TOK_SUB, LANES), I32),
        grid_spec=pltpu.PrefetchScalarGridSpec(
            num_scalar_prefetch=2,
            grid=(n_a + n_b,),
            in_specs=[pl.BlockSpec(blk, lambda i, d, z: (jnp.minimum(i, n_a - 1), 0)),
                      pl.BlockSpec(blk, lambda i, d, z: (jnp.maximum(i - n_a, 0), 0))],
            out_specs=pl.BlockSpec(memory_space=pl.ANY),
            scratch_shapes=[pltpu.VMEM((MOE_CHUNK * TOK_SUB, LANES), I32), pltpu.SemaphoreType.DMA]),
        compiler_params=_cparams(("arbitrary",)),
        name="moe_dispatch",
    )(dest, fill, h2_a, h2_b)


MOE_SPAN = 8


def _moe_kernel(ie_ref, ir_ref, in_ref, xs_ref, wg_hbm, bg_ref, wu_hbm, bu_ref, wd_hbm, bd_ref, ys_ref,
                xin_scr, acc_scr, wgf_scr, wuf_scr, wdf_scr, wg_scr, wu_scr, wd_scr, sem_x, sem_y, sem_w):
    i = pl.program_id(0)
    n_items = pl.num_programs(0)
    nch = in_ref[i]
    row0 = ir_ref[i]
    expert = ie_ref[i]
    slot = i % 2
    down_row = pl.ds(expert, 1)

    def w_copies(e, f, ws):
        col0 = f * FF_TILE
        cols = pl.ds(col0 if isinstance(col0, int) else pl.multiple_of(col0, FF_TILE), FF_TILE)
        return (pltpu.make_async_copy(wg_hbm.at[e, :, cols], wgf_scr.at[ws], sem_w.at[ws]),
                pltpu.make_async_copy(wu_hbm.at[e, :, cols], wuf_scr.at[ws], sem_w.at[ws]),
                pltpu.make_async_copy(wd_hbm.at[e, cols, :], wdf_scr.at[ws], sem_w.at[ws]))

    def rows_of(c, k=1):
        return pl.ds(pl.multiple_of(c * MOE_CHUNK, MOE_CHUNK), k * MOE_CHUNK)

    def x_copy(item_row0, c, s):
        return pltpu.make_async_copy(xs_ref.at[_tok_rows(item_row0 + c * MOE_CHUNK, MOE_CHUNK), :],
                                     xin_scr.at[s, _tok_rows(c * MOE_CHUNK, MOE_CHUNK), :], sem_x.at[s])

    def y_copy(item_row0, c, s):
        return pltpu.make_async_copy(xin_scr.at[s, _tok_rows(c * MOE_CHUNK, MOE_CHUNK), :],
                                     ys_ref.at[_tok_rows(item_row0 + c * MOE_CHUNK, MOE_CHUNK), :], sem_y)

    def for_chunks(n, body):
        def step(c, carry):
            body(c)
            return carry

        lax.fori_loop(0, n, step, 0)

    def for_spans(n, body, first_span):
        def span(g, carry):
            body(g * MOE_SPAN, MOE_SPAN)
            return carry

        lax.fori_loop(first_span, n // MOE_SPAN, span, 0)
        k = MOE_SPAN // 2
        while k >= 1:
            start = n // (2 * k) * (2 * k)

            @pl.when(n % (2 * k) >= k)
            def _(start=start, k=k):
                body(start, k)

            k //= 2

    @pl.when(i > 0)
    def _():
        prev = jnp.maximum(i - 1, 0)
        for_chunks(in_ref[prev], lambda c: y_copy(ir_ref[prev], c, 1 - slot).wait())

    @pl.when(i == 0)
    def _():
        for_chunks(nch, lambda c: x_copy(row0, c, slot).start())
        for cp in w_copies(expert, 0, 0):
            cp.start()

    @pl.when(i + 1 < n_items)
    def _():
        nxt = jnp.minimum(i + 1, n_items - 1)
        for_chunks(in_ref[nxt], lambda c: x_copy(ir_ref[nxt], c, 1 - slot).start())

    for_chunks(nch, lambda c: x_copy(row0, c, slot).wait())

    @pl.when(nch > 0)
    def _():
        def cast_weights(ws):
            w = (wgf_scr[ws].astype(BF16), wuf_scr[ws].astype(BF16), wdf_scr[ws].astype(BF16))
            wg_scr[...], wu_scr[...], wd_scr[...] = w
            return w

        def contribution(f, c, k, w):
            wg, wu, wd = (wg_scr[...], wu_scr[...], wd_scr[...]) if w is None else w
            bias_row = pl.ds(expert * N_FF_TILES + f, 1)
            lo, hi = _unpack_halves(_tok_load(xin_scr.at[slot], c * MOE_CHUNK, k * MOE_CHUNK))
            x = jnp.concatenate([lo.astype(BF16), hi.astype(BF16)], axis=1)
            gt = jnp.minimum(_dot(x, wg) + bg_ref[bias_row, :], SWIGLU_LIMIT)
            up = jnp.clip(_dot(x, wu) + bu_ref[bias_row, :], -SWIGLU_LIMIT, SWIGLU_LIMIT)
            act = (up + 1.0) * (gt * _sigmoid(SWIGLU_ALPHA * gt))
            return _dot(act.astype(BF16), wd)

        def first(f, c, k, w=None):
            acc_scr[rows_of(c, k), :] = contribution(f, c, k, w) + bd_ref[down_row, :]

        def middle(f, c, k, w=None):
            acc_scr[rows_of(c, k), :] += contribution(f, c, k, w)

        def last(f, c, k, w=None):
            y = acc_scr[rows_of(c, k), :] + contribution(f, c, k, w)
            _tok_store(xin_scr.at[slot], c * MOE_CHUNK, k * MOE_CHUNK, _pack_halves(y))
            for u in range(k):
                y_copy(row0, c + u, slot).start()

        def ff_step(f, phase, request_next):
            ws = f % 2
            for cp in w_copies(expert, f, ws):
                cp.wait()
            request_next()
            body = functools.partial(phase, f)

            @pl.when(nch >= MOE_SPAN)
            def _():
                body(0, MOE_SPAN, cast_weights(ws))

            @pl.when(nch < MOE_SPAN)
            def _():
                cast_weights(ws)

            for_spans(nch, body, 1)

        def request_tile(f):
            def go():
                for cp in w_copies(expert, f, f % 2):
                    cp.start()
            return go

        def request_next_item():
            nxt = jnp.minimum(i + 1, n_items - 1)

            @pl.when(jnp.logical_and(i + 1 < n_items, in_ref[nxt] > 0))
            def _():
                for cp in w_copies(ie_ref[nxt], 0, 0):
                    cp.start()

        ff_step(0, first, request_tile(1))

        def mid(f, carry):
            ff_step(f, middle, request_tile(f + 1))
            return carry

        lax.fori_loop(1, N_FF_TILES - 1, mid, 0)
        ff_step(N_FF_TILES - 1, last, request_next_item)

        @pl.when(i == n_items - 1)
        def _():
            for_chunks(nch, lambda c: y_copy(row0, c, slot).wait())


def _moe_experts(item_e, item_row0, item_nch, xs, w_g, b_g, w_u, b_u, w_d, b_d):
    n_items = item_e.shape[0]
    whole = lambda a: pl.BlockSpec(a.shape, lambda i, ie, ir, inch: (0, 0))
    return pl.pallas_call(
        _moe_kernel,
        out_shape=jax.ShapeDtypeStruct(xs.shape, I32),
        grid_spec=pltpu.PrefetchScalarGridSpec(
            num_scalar_prefetch=3,
            grid=(n_items,),
            in_specs=[
                pl.BlockSpec(memory_space=pl.ANY),
                pl.BlockSpec(memory_space=pl.ANY), whole(b_g),
                pl.BlockSpec(memory_space=pl.ANY), whole(b_u),
                pl.BlockSpec(memory_space=pl.ANY), whole(b_d),
            ],
            out_specs=pl.BlockSpec(memory_space=pl.ANY),
            scratch_shapes=[
                pltpu.VMEM((2, MOE_ITEM_ROWS * TOK_SUB, LANES), I32),
                pltpu.VMEM((MOE_ITEM_ROWS, D_MODEL), F32),
                pltpu.VMEM((2, D_MODEL, FF_TILE), F32),
                pltpu.VMEM((2, D_MODEL, FF_TILE), F32),
                pltpu.VMEM((2, FF_TILE, D_MODEL), F32),
                pltpu.VMEM((D_MODEL, FF_TILE), BF16),
                pltpu.VMEM((D_MODEL, FF_TILE), BF16),
                pltpu.VMEM((FF_TILE, D_MODEL), BF16),
                pltpu.SemaphoreType.DMA((2,)),
                pltpu.SemaphoreType.DMA,
                pltpu.SemaphoreType.DMA((2,)),
            ]),
        input_output_aliases={3: 0},
        compiler_params=_cparams(("arbitrary",)),
        name="moe_experts",
    )(item_e, item_row0, item_nch, xs, w_g, b_g, w_u, b_u, w_d, b_d)


COMBINE_ROWS = 256


def _combine_kernel(dest_ref, ys_ref, x1_ref, tw_ref, mods_ref, o_ref, ybuf, sem):
    step = pl.program_id(0)
    slot = step % 2

    def gather(blk, s):
        base = blk * (COMBINE_ROWS * TOP_K)

        def issue(t, carry):
            for k in range(TOP_K):
                pltpu.make_async_copy(ys_ref.at[_tok_rows(dest_ref[base + t * TOP_K + k], 1), :],
                                      ybuf.at[s, k, _tok_rows(t, 1), :], sem.at[s]).start(priority=k % 2)
            return carry

        lax.fori_loop(0, COMBINE_ROWS, issue, 0, unroll=DMA_UNROLL // TOP_K)

    @pl.when(step == 0)
    def _():
        gather(step, slot)

    @pl.when(step + 1 < pl.num_programs(0))
    def _():
        gather(step + 1, 1 - slot)

    for k in range(TOP_K):
        pltpu.make_async_copy(ys_ref.at[_tok_rows(0, COMBINE_ROWS), :], ybuf.at[slot, k], sem.at[slot]).wait()

    y_lo = y_hi = None
    for k in range(TOP_K):
        lo, hi = _unpack_halves(_tok_load(ybuf.at[slot, k], 0, COMBINE_ROWS))
        w = tw_ref[:, k:k + 1]
        y_lo = w * lo if y_lo is None else y_lo + w * lo
        y_hi = w * hi if y_hi is None else y_hi + w * hi
    o_ref[:, :HALF] = x1_ref[:, :HALF] + mods_ref[5:6, :HALF] * y_lo
    o_ref[:, HALF:] = x1_ref[:, HALF:] + mods_ref[5:6, HALF:] * y_hi


def _combine(dest, ys, x1, tw, mods, mod_row):
    t = x1.shape[0]
    return pl.pallas_call(
        _combine_kernel,
        out_shape=jax.ShapeDtypeStruct((t, D_MODEL), F32),
        grid_spec=pltpu.PrefetchScalarGridSpec(
            num_scalar_prefetch=1,
            grid=(t // COMBINE_ROWS,),
            in_specs=[pl.BlockSpec(memory_space=pl.ANY),
                      pl.BlockSpec((COMBINE_ROWS, D_MODEL), lambda i, d: (i, 0)),
                      pl.BlockSpec((COMBINE_ROWS, LANES), lambda i, d: (i, 0)),
                      pl.BlockSpec((None, 6, D_MODEL), lambda i, d: (mod_row(i * COMBINE_ROWS), 0, 0))],
            out_specs=pl.BlockSpec((COMBINE_ROWS, D_MODEL), lambda i, d: (i, 0)),
            scratch_shapes=[pltpu.VMEM((2, TOP_K, COMBINE_ROWS * TOK_SUB, LANES), I32),
                            pltpu.SemaphoreType.DMA((2,))]),
        compiler_params=_cparams(("arbitrary",)),
        name="moe_combine",
    )(dest, ys, x1, tw, mods)


def _routing_tables(top_idx, n_items, cap):
    flat_e = top_idx.reshape(-1)
    onehot = (flat_e[:, None] == jnp.arange(N_EXPERTS, dtype=I32)[None, :]).astype(I32)
    csum = jnp.cumsum(onehot, axis=0)
    rank = jnp.take_along_axis(csum, flat_e[:, None], axis=1)[:, 0] - 1
    counts = csum[-1]
    nch = (counts + MOE_CHUNK - 1) // MOE_CHUNK
    pad_rows = nch * MOE_CHUNK
    pad_end = jnp.cumsum(pad_rows)
    pad_start = pad_end - pad_rows
    dest = (pad_start[flat_e] + rank).astype(I32)
    items_e = (nch + MOE_ITEM_CHUNKS - 1) // MOE_ITEM_CHUNKS
    item_end = jnp.cumsum(items_e)
    item_start = item_end - items_e
    total = item_end[-1]
    i = jnp.arange(n_items, dtype=I32)
    ii = jnp.minimum(i, total - 1)
    e_i = jnp.minimum(jnp.searchsorted(item_end, ii, side="right"), N_EXPERTS - 1).astype(I32)
    local = ii - item_start[e_i]
    row0 = (pad_start[e_i] + local * MOE_ITEM_ROWS).astype(I32)
    n_i = jnp.where(i < total, jnp.minimum(MOE_ITEM_CHUNKS, nch[e_i] - local * MOE_ITEM_CHUNKS), 0).astype(I32)
    fill = jnp.concatenate([pad_start + counts, pad_end[-1:], pad_rows - counts, (cap - pad_end[-1:]) // MOE_CHUNK]).astype(I32)
    return dest, fill, e_i, row0, n_i


def _rope_tables(n_lat):
    nf = ROPE // 4
    inv = ROPE_BASE ** (-jnp.arange(nf, dtype=F32) / nf)
    t = jnp.arange(n_lat)
    row = (t // GRID_W).astype(F32)
    col = (t % GRID_W).astype(F32)
    ang_r = row[:, None] * inv[None, :]
    ang_c = col[:, None] * inv[None, :]
    z = jnp.zeros((n_lat, nf), F32)
    tail = jnp.zeros((n_lat, LANES - ROPE), F32)
    cos = jnp.concatenate([jnp.cos(ang_r), jnp.cos(ang_r), jnp.cos(ang_c), jnp.cos(ang_c), tail], axis=1)
    sin_lo = jnp.concatenate([-jnp.sin(ang_r), z, -jnp.sin(ang_c), z, tail], axis=1)
    sin_hi = jnp.concatenate([z, jnp.sin(ang_r), z, jnp.sin(ang_c), tail], axis=1)
    return cos, sin_lo, sin_hi


def kernel(x_prompt, x_sample, cache_ckv, cache_krope, c, c_ctx, norm1_g, norm2_g, w_ada, b_ada, w_in,
           b_branch_gate, kv_norm_g, w_ukv, q_norm_g, k_norm_g, w_o_attn, w_pool, pool_scale, w_o_pool, w_out,
           router_w, router_b, w_exp_gate, b_exp_gate, w_exp_up, b_exp_up, w_exp_down, b_exp_down):
    assert w_in.shape[0] == 1, "single-layer trunk"
    batch, seq, _ = x_prompt.shape
    dec_batch, n_lat, _ = x_sample.shape
    past = cache_ckv.shape[2]
    n_ctx = batch * seq
    n_dec = dec_batch * n_lat

    cond = jnp.concatenate([c_ctx[None, :], c, jnp.zeros((8 - 1 - dec_batch, D_MODEL), F32)], axis=0)
    mods = _adaln(cond, w_ada[0], b_ada).reshape(8, 6, D_MODEL)

    w_cat = _w_in_layout(w_in[0].T)
    qg = jnp.pad(q_norm_g * ATTN_SCALE, ((0, 0), (0, HEAD_PAD - QK)))
    kgn = k_norm_g[:, :NOPE]
    kgr = jnp.pad(k_norm_g[:, NOPE:], ((0, 0), (0, LANES - ROPE)))
    w_ukv_b = w_ukv[0].astype(BF16)
    rope_tabs = _rope_tables(n_lat)

    ctx_row = lambda i: 0
    tm_in = 1024
    lat_row_in = lambda i: 1 + (i * tm_in) // n_lat
    q_c, pool_c, gates_c, ckv_c, kr_c, krp_c = _in_proj(
        x_prompt.reshape(n_ctx, D_MODEL), mods, ctx_row, norm1_g, w_cat, qg, kv_norm_g, None, tm_in)
    q_l, pool_l, gates_l, ckv_l, _, krp_l = _in_proj(
        x_sample.reshape(n_dec, D_MODEL), mods, lat_row_in, norm1_g, w_cat, qg, kv_norm_g, rope_tabs, tm_in)

    tr = 256
    k_c, v_c = _kv_expand(ckv_c, krp_c, w_ukv_b, kgn, kgr, None, tr)
    k_l, v_l = _kv_expand(ckv_l, krp_l, w_ukv_b, kgn, kgr, rope_tabs, tr)
    cache_krp = jnp.pad(cache_krope.reshape(dec_batch * past, ROPE), ((0, 0), (0, LANES - ROPE)))
    k_p, v_p = _kv_expand(cache_ckv.reshape(dec_batch * past, KV_RANK), cache_krp, w_ukv_b, kgn, kgr, None, tr)

    attn_c = _attn_ctx(q_c, k_c, v_c, seq)
    attn_l = _attn_lat(q_l, k_l, v_l, k_p, v_p, n_lat, past, 512)

    w_pool_b = w_pool[0].astype(BF16)
    poolo_c = _pool(pool_c, w_pool_b, pool_scale, seq)
    poolo_l = _pool(pool_l, w_pool_b, pool_scale, n_lat)

    woa = w_o_attn[0].astype(BF16)
    wop = w_o_pool[0].astype(BF16)
    wout = w_out[0].astype(BF16)
    rw = jnp.pad(router_w[0], ((0, 0), (0, LANES - N_EXPERTS))).astype(BF16)
    rb = jnp.pad(router_b, ((0, 0), (0, LANES - N_EXPERTS)), constant_values=NEG_BIG)
    tm_mg = 256
    lat_row_mg = lambda i: 1 + (i * tm_mg) // n_lat
    x1_c, h2_c, tidx_c, tw_c = _merge(attn_c, poolo_c, gates_c, x_prompt.reshape(n_ctx, D_MODEL), mods, ctx_row,
                                      b_branch_gate, woa, wop, wout, norm2_g, rw, rb, tm_mg)
    x1_l, h2_l, tidx_l, tw_l = _merge(attn_l, poolo_l, gates_l, x_sample.reshape(n_dec, D_MODEL), mods, lat_row_mg,
                                      b_branch_gate, woa, wop, wout, norm2_g, rw, rb, tm_mg)

    n_assign = (n_ctx + n_dec) * TOP_K
    max_chunks = (n_assign + N_EXPERTS * (MOE_CHUNK - 1)) // MOE_CHUNK
    cap = max_chunks * MOE_CHUNK
    n_items = (max_chunks + N_EXPERTS * (MOE_ITEM_CHUNKS - 1)) // MOE_ITEM_CHUNKS
    top_idx = jnp.concatenate([tidx_c[:, :TOP_K], tidx_l[:, :TOP_K]], axis=0)
    dest, fill, item_e, item_row0, item_nch = _routing_tables(top_idx, n_items, cap)

    xs = _dispatch(dest, fill, h2_c, h2_l, cap)
    ys = _moe_experts(item_e, item_row0, item_nch, xs,
                      w_exp_gate[0], b_exp_gate[0].reshape(N_EXPERTS * N_FF_TILES, FF_TILE),
                      w_exp_up[0], b_exp_up[0].reshape(N_EXPERTS * N_FF_TILES, FF_TILE),
                      w_exp_down[0], b_exp_down[0])

    y_c = _combine(dest[:n_ctx * TOP_K], ys, x1_c, tw_c, mods, lambda r: 0)
    y_l = _combine(dest[n_ctx * TOP_K:], ys, x1_l, tw_l, mods, lambda r: 1 + r // n_lat)

    return (y_c.reshape(batch, seq, D_MODEL),
            y_l.reshape(dec_batch, n_lat, D_MODEL),
            ckv_c.reshape(batch, 1, seq, KV_RANK),
            kr_c.reshape(batch, 1, seq, ROPE))
```

```python
import functools

import jax
import jax.numpy as jnp
from jax import lax
from jax.experimental import pallas as pl
from jax.experimental.pallas import tpu as pltpu

F32 = jnp.float32
BF16 = jnp.bfloat16
I32 = jnp.int32

D_MODEL = 2048
N_HEADS = 16
NOPE = 128
ROPE = 64
QK = NOPE + ROPE
V_DIM = 128
KV_RANK = 512
POOL_WINDOWS = (2, 4, 8, 16)
POOL_WIDTH = 1024
POOL_GROUP = POOL_WIDTH // len(POOL_WINDOWS)
N_EXPERTS = 32
TOP_K = 4
D_FF = 2048
SWIGLU_LIMIT = 7.0
SWIGLU_ALPHA = 1.702
ROPE_BASE = 10000.0
RMS_EPS = 1e-6
GRID_W = 64
ATTN_SCALE = QK ** -0.5

LANES = 128
HEAD_PAD = 2 * LANES
Q_COLS = N_HEADS * HEAD_PAD
IN_TILE = 1024
N_Q_TILES = Q_COLS // IN_TILE
POOL_TILE = N_Q_TILES
GATE_TILE0 = POOL_TILE + 1
N_GATE_TILES = 2 * D_MODEL // IN_TILE
KV_TILE = GATE_TILE0 + N_GATE_TILES
N_IN_TILES = KV_TILE + 1
HEADS_PER_TILE = IN_TILE // HEAD_PAD

MOE_CHUNK = 128
MOE_ITEM_CHUNKS = 16
MOE_ITEM_ROWS = MOE_CHUNK * MOE_ITEM_CHUNKS
FF_TILE = 256
N_FF_TILES = D_FF // FF_TILE
NEG_BIG = -1e30

VMEM_LIMIT = 56 * 1024 * 1024


def _cparams(sem, vmem=VMEM_LIMIT):
    return pltpu.CompilerParams(dimension_semantics=sem, vmem_limit_bytes=vmem)


def _dot(a, b):
    return jnp.dot(a, b, preferred_element_type=F32)


def _sigmoid(z):
    return 1.0 / (1.0 + jnp.exp(-z))


HALF = D_MODEL // 2
HI_MASK = -65536


def _pack_halves(v):
    lo = lax.bitcast_convert_type(v[:, :HALF].astype(BF16).astype(F32), I32)
    hi = lax.bitcast_convert_type(v[:, HALF:].astype(BF16).astype(F32), I32)
    return jnp.bitwise_or(jnp.bitwise_and(hi, HI_MASK), lax.shift_right_logical(lo, 16))


TOK_SUB = HALF // LANES


def _tok_rows(tok0, n):
    row0 = tok0 * TOK_SUB
    return pl.ds(row0 if isinstance(row0, int) else pl.multiple_of(row0, TOK_SUB), n * TOK_SUB)


def _tok_store(ref, tok0, n, words):
    for s in range(TOK_SUB):
        ref[pl.ds(tok0 * TOK_SUB + s, n, stride=TOK_SUB), :] = words[:, s * LANES:(s + 1) * LANES]


def _tok_load(ref, tok0, n):
    return jnp.concatenate([ref[pl.ds(tok0 * TOK_SUB + s, n, stride=TOK_SUB), :] for s in range(TOK_SUB)], axis=1)


def _unpack_halves(w):
    lo = lax.bitcast_convert_type(lax.shift_left(w, 16), F32)
    hi = lax.bitcast_convert_type(jnp.bitwise_and(w, HI_MASK), F32)
    return lo, hi


def _rope_lanes(y, cos, sin_lo, sin_hi):
    return y * cos + pltpu.roll(y, LANES - 16, axis=1) * sin_lo + pltpu.roll(y, 16, axis=1) * sin_hi


def _adaln_kernel(c_ref, w_ref, b_ref, o_ref):
    c = c_ref[...]
    s = (c * _sigmoid(c)).astype(BF16)
    o_ref[...] = _dot(s, w_ref[...].astype(BF16)) + b_ref[...]


def _adaln(cond, w_ada, b_ada):
    rows = cond.shape[0]
    n = w_ada.shape[1]
    tn = 1024
    return pl.pallas_call(
        _adaln_kernel,
        out_shape=jax.ShapeDtypeStruct((rows, n), F32),
        grid=(n // tn,),
        in_specs=[pl.BlockSpec((rows, D_MODEL), lambda j: (0, 0)),
                  pl.BlockSpec((D_MODEL, tn), lambda j: (0, j)),
                  pl.BlockSpec((1, tn), lambda j: (0, j))],
        out_specs=pl.BlockSpec((rows, tn), lambda j: (0, j)),
        compiler_params=_cparams(("arbitrary",)),
        name="adaln",
    )(cond, w_ada, b_ada)


W_Q_END = N_HEADS * QK
W_CKV_END = W_Q_END + KV_RANK
W_KR_END = W_CKV_END + ROPE
N_IN_COLS = W_KR_END + POOL_WIDTH + 2 * D_MODEL
W_PREP = 256


def _w_in_layout_kernel(wt_ref, o_ref):
    lane = lax.broadcasted_iota(I32, (W_PREP, W_PREP), 1)

    def move(src_row, dst_col, keep):
        sq = wt_ref[src_row:src_row + W_PREP, :].T
        if keep < W_PREP:
            sq = jnp.where(lane < keep, sq, 0.0)
        o_ref[:, dst_col:dst_col + W_PREP] = sq.astype(BF16)

    for h in range(N_HEADS):
        move(h * QK, h * HEAD_PAD, QK)
    for p in range((N_IN_COLS - W_KR_END) // W_PREP):
        move(W_KR_END + p * W_PREP, Q_COLS + p * W_PREP, W_PREP)
    kv0 = Q_COLS + N_IN_COLS - W_KR_END
    kv_cols = KV_RANK + ROPE
    for p in range(IN_TILE // W_PREP):
        keep = min(max(kv_cols - p * W_PREP, 0), W_PREP)
        if keep:
            move(W_Q_END + p * W_PREP, kv0 + p * W_PREP, keep)
        else:
            o_ref[:, kv0 + p * W_PREP:kv0 + (p + 1) * W_PREP] = jnp.zeros((W_PREP, W_PREP), BF16)


def _w_in_layout(wt):
    return pl.pallas_call(
        _w_in_layout_kernel,
        out_shape=jax.ShapeDtypeStruct((D_MODEL, N_IN_TILES * IN_TILE), BF16),
        grid=(D_MODEL // W_PREP,),
        in_specs=[pl.BlockSpec((N_IN_COLS, W_PREP), lambda i: (0, i))],
        out_specs=pl.BlockSpec((W_PREP, N_IN_TILES * IN_TILE), lambda i: (i, 0)),
        compiler_params=_cparams(("arbitrary",)),
        name="w_in_layout",
    )(wt)


def _in_proj_kernel(rope, x_ref, mods_ref, g1_ref, w_ref, qg_ref, kvg_ref, *rest):
    if rope:
        cos_ref, slo_ref, shi_ref = rest[:3]
        rest = rest[3:]
    q_ref, pool_ref, gates_ref, ckv_ref, kr_ref, krp_ref, h_scr = rest
    j = pl.program_id(1)

    def proj(c0, c1, h=None):
        return _dot(h_scr[...] if h is None else h, w_ref[:, c0:c1])

    def q_heads(h=None):
        for hh in range(HEADS_PER_TILE):
            a = proj(hh * HEAD_PAD, (hh + 1) * HEAD_PAD, h)
            r = lax.rsqrt(jnp.sum(a * a, axis=-1, keepdims=True) / QK + RMS_EPS)
            y = a * r * qg_ref[...]
            if rope:
                yr = _rope_lanes(y[:, LANES:], cos_ref[...], slo_ref[...], shi_ref[...])
                q_ref[:, hh * HEAD_PAD:hh * HEAD_PAD + LANES] = y[:, :LANES].astype(BF16)
                q_ref[:, hh * HEAD_PAD + LANES:(hh + 1) * HEAD_PAD] = yr.astype(BF16)
            else:
                q_ref[:, hh * HEAD_PAD:(hh + 1) * HEAD_PAD] = y.astype(BF16)

    @pl.when(j == 0)
    def _():
        x = x_ref[...]
        y = x * lax.rsqrt(jnp.mean(x * x, axis=-1, keepdims=True) + RMS_EPS) * g1_ref[...]
        h = (y * (1.0 + mods_ref[1:2, :]) + mods_ref[0:1, :]).astype(BF16)
        h_scr[...] = h
        q_heads(h)

    @pl.when(jnp.logical_and(j > 0, j < N_Q_TILES))
    def _():
        q_heads()

    @pl.when(j == POOL_TILE)
    def _():
        pool_ref[...] = proj(0, IN_TILE)

    @pl.when(jnp.logical_and(j >= GATE_TILE0, j < KV_TILE))
    def _():
        gates_ref[...] = proj(0, IN_TILE)

    @pl.when(j == KV_TILE)
    def _():
        a = proj(0, KV_RANK)
        r = lax.rsqrt(jnp.mean(a * a, axis=-1, keepdims=True) + RMS_EPS)
        ckv_ref[...] = a * r * kvg_ref[...]
        krp = proj(KV_RANK, KV_RANK + LANES)
        krp_ref[...] = krp
        kr_ref[...] = krp[:, :ROPE]


def _in_proj(x, mods, mod_row, g1, w_cat, qg, kvg, rope_tabs, tm):
    t = x.shape[0]
    rope = rope_tabs is not None
    in_specs = [
        pl.BlockSpec((tm, D_MODEL), lambda i, j: (i, 0)),
        pl.BlockSpec((None, 6, D_MODEL), lambda i, j: (mod_row(i), 0, 0)),
        pl.BlockSpec((1, D_MODEL), lambda i, j: (0, 0)),
        pl.BlockSpec((D_MODEL, IN_TILE), lambda i, j: (0, j)),
        pl.BlockSpec((1, HEAD_PAD), lambda i, j: (0, 0)),
        pl.BlockSpec((1, KV_RANK), lambda i, j: (0, 0)),
    ]
    args = [x, mods, g1, w_cat, qg, kvg]
    if rope:
        seq_tiles = rope_tabs[0].shape[0] // tm
        in_specs += [pl.BlockSpec((tm, LANES), lambda i, j: (i % seq_tiles, 0))] * 3
        args += list(rope_tabs)
    out_shape = (
        jax.ShapeDtypeStruct((t, Q_COLS), BF16),
        jax.ShapeDtypeStruct((t, POOL_WIDTH), F32),
        jax.ShapeDtypeStruct((t, 2 * D_MODEL), F32),
        jax.ShapeDtypeStruct((t, KV_RANK), F32),
        jax.ShapeDtypeStruct((t, ROPE), F32),
        jax.ShapeDtypeStruct((t, LANES), F32),
    )
    out_specs = (
        pl.BlockSpec((tm, IN_TILE), lambda i, j: (i, jnp.minimum(j, N_Q_TILES - 1))),
        pl.BlockSpec((tm, POOL_WIDTH), lambda i, j: (i, 0)),
        pl.BlockSpec((tm, IN_TILE), lambda i, j: (i, jnp.clip(j - GATE_TILE0, 0, N_GATE_TILES - 1))),
        pl.BlockSpec((tm, KV_RANK), lambda i, j: (i, 0)),
        pl.BlockSpec((tm, ROPE), lambda i, j: (i, 0)),
        pl.BlockSpec((tm, LANES), lambda i, j: (i, 0)),
    )
    return pl.pallas_call(
        functools.partial(_in_proj_kernel, rope),
        out_shape=out_shape,
        grid=(t // tm, N_IN_TILES),
        in_specs=in_specs,
        out_specs=out_specs,
        scratch_shapes=[pltpu.VMEM((tm, D_MODEL), BF16)],
        compiler_params=_cparams(("arbitrary", "arbitrary")),
        name="in_proj_rope" if rope else "in_proj",
    )(*args)


def _kv_expand_kernel(rope, ckv_ref, krp_ref, w_ref, kgn_ref, kgr_ref, *rest):
    if rope:
        cos_ref, slo_ref, shi_ref = rest[:3]
        rest = rest[3:]
    k_ref, v_ref = rest
    kv = _dot(ckv_ref[...].astype(BF16), w_ref[...])
    kr = krp_ref[...]
    ssq_r = jnp.sum(kr * kr, axis=-1, keepdims=True)
    krg = kr * kgr_ref[...]
    if rope:
        krg = _rope_lanes(krg, cos_ref[...], slo_ref[...], shi_ref[...])
    for h in range(N_HEADS):
        kn = kv[:, h * HEAD_PAD:h * HEAD_PAD + NOPE]
        r = lax.rsqrt((jnp.sum(kn * kn, axis=-1, keepdims=True) + ssq_r) / QK + RMS_EPS)
        k_ref[:, h * HEAD_PAD:h * HEAD_PAD + NOPE] = (kn * r * kgn_ref[...]).astype(BF16)
        k_ref[:, h * HEAD_PAD + NOPE:(h + 1) * HEAD_PAD] = (krg * r).astype(BF16)
        v_ref[:, h * V_DIM:(h + 1) * V_DIM] = kv[:, h * HEAD_PAD + NOPE:(h + 1) * HEAD_PAD].astype(BF16)


def _kv_expand(ckv_n, krp, w_ukv, kgn, kgr, rope_tabs, tr):
    r = ckv_n.shape[0]
    rope = rope_tabs is not None
    in_specs = [
        pl.BlockSpec((tr, KV_RANK), lambda i: (i, 0)),
        pl.BlockSpec((tr, LANES), lambda i: (i, 0)),
        pl.BlockSpec((KV_RANK, N_HEADS * HEAD_PAD), lambda i: (0, 0)),
        pl.BlockSpec((1, LANES), lambda i: (0, 0)),
        pl.BlockSpec((1, LANES), lambda i: (0, 0)),
    ]
    args = [ckv_n, krp, w_ukv, kgn, kgr]
    if rope:
        seq_tiles = rope_tabs[0].shape[0] // tr
        in_specs += [pl.BlockSpec((tr, LANES), lambda i: (i % seq_tiles, 0))] * 3
        args += list(rope_tabs)
    return pl.pallas_call(
        functools.partial(_kv_expand_kernel, rope),
        out_shape=(jax.ShapeDtypeStruct((r, N_HEADS * HEAD_PAD), BF16),
                   jax.ShapeDtypeStruct((r, N_HEADS * V_DIM), BF16)),
        grid=(r // tr,),
        in_specs=in_specs,
        out_specs=(pl.BlockSpec((tr, N_HEADS * HEAD_PAD), lambda i: (i, 0)),
                   pl.BlockSpec((tr, N_HEADS * V_DIM), lambda i: (i, 0))),
        compiler_params=_cparams(("arbitrary",)),
        name="kv_expand_rope" if rope else "kv_expand",
    )(*args)


def _qk(q, k):
    return lax.dot_general(q, k, (((1,), (1,)), ((), ())), preferred_element_type=F32)


def _attn_ctx_kernel(q_ref, k_ref, v_ref, o_ref):
    for h in range(N_HEADS):
        s = _qk(q_ref[:, h * HEAD_PAD:(h + 1) * HEAD_PAD], k_ref[:, h * HEAD_PAD:(h + 1) * HEAD_PAD])
        p = jnp.exp(s - jnp.max(s, axis=-1, keepdims=True))
        l = jnp.sum(p, axis=-1, keepdims=True)
        o = _dot(p.astype(BF16), v_ref[:, h * V_DIM:(h + 1) * V_DIM])
        o_ref[:, h * V_DIM:(h + 1) * V_DIM] = (o / l).astype(BF16)


def _attn_ctx(q, k, v, seq):
    t = q.shape[0]
    return pl.pallas_call(
        _attn_ctx_kernel,
        out_shape=jax.ShapeDtypeStruct((t, N_HEADS * V_DIM), BF16),
        grid=(t // seq,),
        in_specs=[pl.BlockSpec((seq, Q_COLS), lambda b: (b, 0)),
                  pl.BlockSpec((seq, Q_COLS), lambda b: (b, 0)),
                  pl.BlockSpec((seq, N_HEADS * V_DIM), lambda b: (b, 0))],
        out_specs=pl.BlockSpec((seq, N_HEADS * V_DIM), lambda b: (b, 0)),
        compiler_params=_cparams(("arbitrary",)),
        name="attn_ctx",
    )(q, k, v)


def _attn_lat_kernel(q_ref, k_ref, v_ref, kc_ref, vc_ref, o_ref):
    q = q_ref[...]
    s1 = _qk(q, k_ref[...])
    s2 = _qk(q, kc_ref[...])
    m = jnp.maximum(jnp.max(s1, axis=-1, keepdims=True), jnp.max(s2, axis=-1, keepdims=True))
    p1 = jnp.exp(s1 - m)
    p2 = jnp.exp(s2 - m)
    l = jnp.sum(p1, axis=-1, keepdims=True) + jnp.sum(p2, axis=-1, keepdims=True)
    o = _dot(p1.astype(BF16), v_ref[...]) + _dot(p2.astype(BF16), vc_ref[...])
    o_ref[...] = (o / l).astype(BF16)


def _attn_lat(q, k, v, kc, vc, seq, past, tq):
    t = q.shape[0]
    nq = seq // tq
    return pl.pallas_call(
        _attn_lat_kernel,
        out_shape=jax.ShapeDtypeStruct((t, N_HEADS * V_DIM), BF16),
        grid=(t // seq, N_HEADS, nq),
        in_specs=[pl.BlockSpec((tq, HEAD_PAD), lambda b, h, i: (b * nq + i, h)),
                  pl.BlockSpec((seq, HEAD_PAD), lambda b, h, i: (b, h)),
                  pl.BlockSpec((seq, V_DIM), lambda b, h, i: (b, h)),
                  pl.BlockSpec((past, HEAD_PAD), lambda b, h, i: (b, h)),
                  pl.BlockSpec((past, V_DIM), lambda b, h, i: (b, h))],
        out_specs=pl.BlockSpec((tq, V_DIM), lambda b, h, i: (b * nq + i, h)),
        compiler_params=_cparams(("arbitrary", "arbitrary", "arbitrary")),
        name="attn_lat",
    )(q, k, v, kc, vc)


POOL_HALO = 8


def _pool_kernel(seq, u_ref, w_ref, sc_ref, o_ref, pad_scr):
    zeros = jnp.zeros((POOL_HALO, POOL_WIDTH), F32)
    pad_scr[0:POOL_HALO, :] = zeros
    pad_scr[POOL_HALO + seq:2 * POOL_HALO + seq, :] = zeros
    pad_scr[POOL_HALO:POOL_HALO + seq, :] = u_ref[...]
    t = lax.broadcasted_iota(I32, (seq, 1), 0)
    for g, w in enumerate(POOL_WINDOWS):
        cols = slice(g * POOL_GROUP, (g + 1) * POOL_GROUP)
        tot = None
        for d in range(-(w // 2), w - w // 2):
            piece = pad_scr[POOL_HALO + d:POOL_HALO + d + seq, cols]
            tot = piece if tot is None else tot + piece
        cnt = (jnp.minimum(t + (w - w // 2), seq) - jnp.maximum(t - w // 2, 0)).astype(F32)
        mixed = tot / cnt - u_ref[:, cols]
        o_ref[:, cols] = (_dot(mixed.astype(BF16), w_ref[g]) * sc_ref[:, cols]).astype(BF16)


def _pool(u, w_pool, pool_scale, seq):
    t = u.shape[0]
    n_groups = len(POOL_WINDOWS)
    return pl.pallas_call(
        functools.partial(_pool_kernel, seq),
        out_shape=jax.ShapeDtypeStruct((t, POOL_WIDTH), BF16),
        grid=(t // seq,),
        in_specs=[pl.BlockSpec((seq, POOL_WIDTH), lambda b: (b, 0)),
                  pl.BlockSpec((n_groups, POOL_GROUP, POOL_GROUP), lambda b: (0, 0, 0)),
                  pl.BlockSpec((1, POOL_WIDTH), lambda b: (0, 0))],
        out_specs=pl.BlockSpec((seq, POOL_WIDTH), lambda b: (b, 0)),
        scratch_shapes=[pltpu.VMEM((seq + 2 * POOL_HALO, POOL_WIDTH), F32)],
        compiler_params=_cparams(("arbitrary",)),
        name="pool",
    )(u, w_pool, pool_scale)


def _merge_kernel(attn_ref, pool_ref, gates_ref, x_ref, mods_ref, bbg_ref, woa_ref, wop_ref, wout_ref,
                  g2_ref, rw_ref, rb_ref, cnt_in_ref, x1_ref, h2_ref, tidx_ref, tw_ref, cnt_out_ref, cnt_scr):
    @pl.when(pl.program_id(0) == 0)
    def _():
        cnt_scr[...] = cnt_in_ref[...]

    a = _dot(attn_ref[...], woa_ref[...])
    p = _dot(pool_ref[...], wop_ref[...])
    ga = _sigmoid(gates_ref[:, :D_MODEL] + bbg_ref[:, :D_MODEL])
    gp = _sigmoid(gates_ref[:, D_MODEL:] + bbg_ref[:, D_MODEL:])
    merged = (ga * a + gp * p).astype(BF16)
    x1 = x_ref[...] + mods_ref[2:3, :] * _dot(merged, wout_ref[...])
    x1_ref[...] = x1
    y = x1 * lax.rsqrt(jnp.mean(x1 * x1, axis=-1, keepdims=True) + RMS_EPS) * g2_ref[...]
    h2 = y * (1.0 + mods_ref[4:5, :]) + mods_ref[3:4, :]
    _tok_store(h2_ref, 0, h2.shape[0], _pack_halves(h2))
    logits = _dot(h2.astype(BF16), rw_ref[...]) + rb_ref[...]
    lane = lax.broadcasted_iota(I32, logits.shape, 1).astype(F32)
    vals, idxs = [], []
    for _ in range(TOP_K):
        m = jnp.max(logits, axis=-1, keepdims=True)
        ix = jnp.min(jnp.where(logits == m, lane, float(LANES)), axis=-1, keepdims=True)
        vals.append(m)
        idxs.append(ix)
        logits = jnp.where(lane == ix, -jnp.inf, logits)
    es = [jnp.exp(v - vals[0]) for v in vals]
    tot = es[0] + es[1] + es[2] + es[3]
    tidx = jnp.zeros(logits.shape, F32)
    tw = jnp.zeros(logits.shape, F32)
    for k in range(TOP_K):
        tidx = jnp.where(lane == k, idxs[k], tidx)
        tw = jnp.where(lane == k, es[k] / tot, tw)
    rows = logits.shape[0]
    chosen = sum((lane == ix).astype(F32) for ix in idxs)
    earlier = (lax.broadcasted_iota(I32, (rows, rows), 1) < lax.broadcasted_iota(I32, (rows, rows), 0)).astype(BF16)
    before = _dot(earlier, chosen.astype(BF16)) + cnt_scr[...]
    for k in range(TOP_K):
        rank = jnp.sum(jnp.where(lane == idxs[k], before, 0.0), axis=-1, keepdims=True)
        tidx = jnp.where(lane == TOP_K + k, rank, tidx)
    cnt_scr[...] += jnp.sum(chosen, axis=0, keepdims=True)
    cnt_out_ref[...] = cnt_scr[...]
    tidx_ref[...] = tidx.astype(I32)
    tw_ref[...] = tw


def _merge(attn_o, pool_o, gates, x, mods, mod_row, bbg, woa, wop, wout, g2, rw, rb, cnt_in, tm):
    t = x.shape[0]
    const = lambda shape: pl.BlockSpec(shape, lambda i: (0, 0), pipeline_mode=pl.Buffered(1))
    return pl.pallas_call(
        _merge_kernel,
        out_shape=(jax.ShapeDtypeStruct((t, D_MODEL), F32),
                   jax.ShapeDtypeStruct((t * TOK_SUB, LANES), I32),
                   jax.ShapeDtypeStruct((t, LANES), I32),
                   jax.ShapeDtypeStruct((t, LANES), F32),
                   jax.ShapeDtypeStruct((1, LANES), F32)),
        grid=(t // tm,),
        in_specs=[pl.BlockSpec((tm, D_MODEL), lambda i: (i, 0)),
                  pl.BlockSpec((tm, POOL_WIDTH), lambda i: (i, 0)),
                  pl.BlockSpec((tm, 2 * D_MODEL), lambda i: (i, 0)),
                  pl.BlockSpec((tm, D_MODEL), lambda i: (i, 0)),
                  pl.BlockSpec((None, 6, D_MODEL), lambda i: (mod_row(i), 0, 0)),
                  const((1, 2 * D_MODEL)),
                  const((D_MODEL, D_MODEL)),
                  const((POOL_WIDTH, D_MODEL)),
                  const((D_MODEL, D_MODEL)),
                  const((1, D_MODEL)),
                  const((D_MODEL, LANES)),
                  const((1, LANES)),
                  const((1, LANES))],
        out_specs=(pl.BlockSpec((tm, D_MODEL), lambda i: (i, 0)),
                   pl.BlockSpec((tm * TOK_SUB, LANES), lambda i: (i, 0)),
                   pl.BlockSpec((tm, LANES), lambda i: (i, 0)),
                   pl.BlockSpec((tm, LANES), lambda i: (i, 0)),
                   pl.BlockSpec((1, LANES), lambda i: (0, 0))),
        scratch_shapes=[pltpu.VMEM((1, LANES), F32)],
        compiler_params=_cparams(("arbitrary",)),
        name="merge",
    )(attn_o, pool_o, gates, x, mods, bbg, woa, wop, wout, g2, rw, rb, cnt_in)


DISPATCH_ROWS = 256
DMA_UNROLL = 8


N_FILL = N_EXPERTS + 1


def _dispatch_kernel(n_first, dest_ref, fill_ref, ha_ref, hb_ref, xs_ref, zero_scr, sem):
    base = pl.program_id(0) * (DISPATCH_ROWS * TOP_K)

    @pl.when(pl.program_id(0) == 0)
    def _():
        zero_scr[...] = jnp.zeros(zero_scr.shape, I32)

        def fill_copy(slot, n):
            return pltpu.make_async_copy(zero_scr.at[_tok_rows(0, n), :], xs_ref.at[_tok_rows(slot, n), :], sem)

        def for_fill(e, n, op):
            def one(r, c):
                op(fill_copy(fill_ref[e] + r * n, n))
                return c

            lax.fori_loop(0, fill_ref[N_FILL + e], one, 0)

        def each_range(op):
            def expert(e, carry):
                for_fill(e, 1, op)
                return carry

            lax.fori_loop(0, N_EXPERTS, expert, 0)
            for_fill(N_EXPERTS, MOE_CHUNK, op)

        each_range(lambda cp: cp.start())
        each_range(lambda cp: cp.wait())

    def scatter(h_ref):
        def issue(t, carry):
            for k in range(TOP_K):
                pltpu.make_async_copy(h_ref.at[_tok_rows(t, 1), :],
                                      xs_ref.at[_tok_rows(dest_ref[base + t * TOP_K + k], 1), :],
                                      sem).start(priority=k % 2)
            return carry

        lax.fori_loop(0, DISPATCH_ROWS, issue, 0, unroll=DMA_UNROLL // TOP_K)
        for _ in range(TOP_K):
            pltpu.make_async_copy(h_ref, xs_ref.at[_tok_rows(0, DISPATCH_ROWS), :], sem).wait()

    @pl.when(pl.program_id(0) < n_first)
    def _():
        scatter(ha_ref)

    @pl.when(pl.program_id(0) >= n_first)
    def _():
        scatter(hb_ref)


def _dispatch(dest, fill, h2_a, h2_b, cap):
    n_a = h2_a.shape[0] // TOK_SUB // DISPATCH_ROWS
    n_b = h2_b.shape[0] // TOK_SUB // DISPATCH_ROWS
    blk = (DISPATCH_ROWS * TOK_SUB, LANES)
    return pl.pallas_call(
        functools.partial(_dispatch_kernel, n_a),
        out_shape=jax.ShapeDtypeStruct((cap * TOK_SUB, LANES), I32),
        grid_spec=pltpu.PrefetchScalarGridSpec(
            num_scalar_prefetch=2,
            grid=(n_a + n_b,),
            in_specs=[pl.BlockSpec(blk, lambda i, d, z: (jnp.minimum(i, n_a - 1), 0)),
                      pl.BlockSpec(blk, lambda i, d, z: (jnp.maximum(i - n_a, 0), 0))],
            out_specs=pl.BlockSpec(memory_space=pl.ANY),
            scratch_shapes=[pltpu.VMEM((MOE_CHUNK * TOK_SUB, LANES), I32), pltpu.SemaphoreType.DMA]),
        compiler_params=_cparams(("arbitrary",)),
        name="moe_dispatch",
    )(dest, fill, h2_a, h2_b)


MOE_SPAN = 8


def _moe_kernel(ie_ref, ir_ref, in_ref, xs_ref, wg_hbm, bg_ref, wu_hbm, bu_ref, wd_hbm, bd_ref, ys_ref,
                xin_scr, acc_scr, wgf_scr, wuf_scr, wdf_scr, wg_scr, wu_scr, wd_scr, sem_x, sem_y, sem_w):
    i = pl.program_id(0)
    n_items = pl.num_programs(0)
    nch = in_ref[i]
    row0 = ir_ref[i]
    expert = ie_ref[i]
    slot = i % 2
    down_row = pl.ds(expert, 1)

    def w_copies(e, f, ws):
        col0 = f * FF_TILE
        cols = pl.ds(col0 if isinstance(col0, int) else pl.multiple_of(col0, FF_TILE), FF_TILE)
        return (pltpu.make_async_copy(wg_hbm.at[e, :, cols], wgf_scr.at[ws], sem_w.at[ws]),
                pltpu.make_async_copy(wu_hbm.at[e, :, cols], wuf_scr.at[ws], sem_w.at[ws]),
                pltpu.make_async_copy(wd_hbm.at[e, cols, :], wdf_scr.at[ws], sem_w.at[ws]))

    def rows_of(c, k=1):
        return pl.ds(pl.multiple_of(c * MOE_CHUNK, MOE_CHUNK), k * MOE_CHUNK)

    def x_copy(item_row0, c, s):
        return pltpu.make_async_copy(xs_ref.at[_tok_rows(item_row0 + c * MOE_CHUNK, MOE_CHUNK), :],
                                     xin_scr.at[s, _tok_rows(c * MOE_CHUNK, MOE_CHUNK), :], sem_x.at[s])

    def y_copy(item_row0, c, s):
        return pltpu.make_async_copy(xin_scr.at[s, _tok_rows(c * MOE_CHUNK, MOE_CHUNK), :],
                                     ys_ref.at[_tok_rows(item_row0 + c * MOE_CHUNK, MOE_CHUNK), :], sem_y)

    def for_chunks(n, body):
        def step(c, carry):
            body(c)
            return carry

        lax.fori_loop(0, n, step, 0)

    def for_spans(n, body, first_span):
        def span(g, carry):
            body(g * MOE_SPAN, MOE_SPAN)
            return carry

        lax.fori_loop(first_span, n // MOE_SPAN, span, 0)
        k = MOE_SPAN // 2
        while k >= 1:
            start = n // (2 * k) * (2 * k)

            @pl.when(n % (2 * k) >= k)
            def _(start=start, k=k):
                body(start, k)

            k //= 2

    @pl.when(i > 0)
    def _():
        prev = jnp.maximum(i - 1, 0)
        for_chunks(in_ref[prev], lambda c: y_copy(ir_ref[prev], c, 1 - slot).wait())

    @pl.when(i == 0)
    def _():
        for_chunks(nch, lambda c: x_copy(row0, c, slot).start())
        for cp in w_copies(expert, 0, 0):
            cp.start()

    @pl.when(i + 1 < n_items)
    def _():
        nxt = jnp.minimum(i + 1, n_items - 1)
        for_chunks(in_ref[nxt], lambda c: x_copy(ir_ref[nxt], c, 1 - slot).start())

    for_chunks(nch, lambda c: x_copy(row0, c, slot).wait())

    @pl.when(nch > 0)
    def _():
        def cast_weights(ws):
            w = (wgf_scr[ws].astype(BF16), wuf_scr[ws].astype(BF16), wdf_scr[ws].astype(BF16))
            wg_scr[...], wu_scr[...], wd_scr[...] = w
            return w

        def contribution(f, c, k, w):
            wg, wu, wd = (wg_scr[...], wu_scr[...], wd_scr[...]) if w is None else w
            bias_row = pl.ds(expert * N_FF_TILES + f, 1)
            lo, hi = _unpack_halves(_tok_load(xin_scr.at[slot], c * MOE_CHUNK, k * MOE_CHUNK))
            x = jnp.concatenate([lo.astype(BF16), hi.astype(BF16)], axis=1)
            gt = jnp.minimum(_dot(x, wg) + bg_ref[bias_row, :], SWIGLU_LIMIT)
            up = jnp.clip(_dot(x, wu) + bu_ref[bias_row, :], -SWIGLU_LIMIT, SWIGLU_LIMIT)
            act = (up + 1.0) * (gt * _sigmoid(SWIGLU_ALPHA * gt))
            return _dot(act.astype(BF16), wd)

        def first(f, c, k, w=None):
            acc_scr[rows_of(c, k), :] = contribution(f, c, k, w) + bd_ref[down_row, :]

        def middle(f, c, k, w=None):
            acc_scr[rows_of(c, k), :] += contribution(f, c, k, w)

        def last(f, c, k, w=None):
            y = acc_scr[rows_of(c, k), :] + contribution(f, c, k, w)
            _tok_store(xin_scr.at[slot], c * MOE_CHUNK, k * MOE_CHUNK, _pack_halves(y))
            for u in range(k):
                y_copy(row0, c + u, slot).start()

        def ff_step(f, phase, request_next):
            ws = f % 2
            for cp in w_copies(expert, f, ws):
                cp.wait()
            request_next()
            body = functools.partial(phase, f)

            @pl.when(nch >= MOE_SPAN)
            def _():
                body(0, MOE_SPAN, cast_weights(ws))

            @pl.when(nch < MOE_SPAN)
            def _():
                cast_weights(ws)

            for_spans(nch, body, 1)

        def request_tile(f):
            def go():
                for cp in w_copies(expert, f, f % 2):
                    cp.start()
            return go

        def request_next_item():
            nxt = jnp.minimum(i + 1, n_items - 1)

            @pl.when(jnp.logical_and(i + 1 < n_items, in_ref[nxt] > 0))
            def _():
                for cp in w_copies(ie_ref[nxt], 0, 0):
                    cp.start()

        ff_step(0, first, request_tile(1))

        def mid(f, carry):
            ff_step(f, middle, request_tile(f + 1))
            return carry

        lax.fori_loop(1, N_FF_TILES - 1, mid, 0)
        ff_step(N_FF_TILES - 1, last, request_next_item)

        @pl.when(i == n_items - 1)
        def _():
            for_chunks(nch, lambda c: y_copy(row0, c, slot).wait())


def _moe_experts(item_e, item_row0, item_nch, xs, w_g, b_g, w_u, b_u, w_d, b_d):
    n_items = item_e.shape[0]
    whole = lambda a: pl.BlockSpec(a.shape, lambda i, ie, ir, inch: (0, 0))
    return pl.pallas_call(
        _moe_kernel,
        out_shape=jax.ShapeDtypeStruct(xs.shape, I32),
        grid_spec=pltpu.PrefetchScalarGridSpec(
            num_scalar_prefetch=3,
            grid=(n_items,),
            in_specs=[
                pl.BlockSpec(memory_space=pl.ANY),
                pl.BlockSpec(memory_space=pl.ANY), whole(b_g),
                pl.BlockSpec(memory_space=pl.ANY), whole(b_u),
                pl.BlockSpec(memory_space=pl.ANY), whole(b_d),
            ],
            out_specs=pl.BlockSpec(memory_space=pl.ANY),
            scratch_shapes=[
                pltpu.VMEM((2, MOE_ITEM_ROWS * TOK_SUB, LANES), I32),
                pltpu.VMEM((MOE_ITEM_ROWS, D_MODEL), F32),
                pltpu.VMEM((2, D_MODEL, FF_TILE), F32),
                pltpu.VMEM((2, D_MODEL, FF_TILE), F32),
                pltpu.VMEM((2, FF_TILE, D_MODEL), F32),
                pltpu.VMEM((D_MODEL, FF_TILE), BF16),
                pltpu.VMEM((D_MODEL, FF_TILE), BF16),
                pltpu.VMEM((FF_TILE, D_MODEL), BF16),
                pltpu.SemaphoreType.DMA((2,)),
                pltpu.SemaphoreType.DMA,
                pltpu.SemaphoreType.DMA((2,)),
            ]),
        input_output_aliases={3: 0},
        compiler_params=_cparams(("arbitrary",)),
        name="moe_experts",
    )(item_e, item_row0, item_nch, xs, w_g, b_g, w_u, b_u, w_d, b_d)


COMBINE_ROWS = 256


def _combine_kernel(dest_ref, ys_ref, x1_ref, tw_ref, mods_ref, o_ref, ybuf, sem):
    step = pl.program_id(0)
    slot = step % 2

    def gather(blk, s):
        base = blk * (COMBINE_ROWS * TOP_K)

        def issue(t, carry):
            for k in range(TOP_K):
                pltpu.make_async_copy(ys_ref.at[_tok_rows(dest_ref[base + t * TOP_K + k], 1), :],
                                      ybuf.at[s, k, _tok_rows(t, 1), :], sem.at[s]).start(priority=k % 2)
            return carry

        lax.fori_loop(0, COMBINE_ROWS, issue, 0, unroll=DMA_UNROLL // TOP_K)

    @pl.when(step == 0)
    def _():
        gather(step, slot)

    @pl.when(step + 1 < pl.num_programs(0))
    def _():
        gather(step + 1, 1 - slot)

    for k in range(TOP_K):
        pltpu.make_async_copy(ys_ref.at[_tok_rows(0, COMBINE_ROWS), :], ybuf.at[slot, k], sem.at[slot]).wait()

    y_lo = y_hi = None
    for k in range(TOP_K):
        lo, hi = _unpack_halves(_tok_load(ybuf.at[slot, k], 0, COMBINE_ROWS))
        w = tw_ref[:, k:k + 1]
        y_lo = w * lo if y_lo is None else y_lo + w * lo
        y_hi = w * hi if y_hi is None else y_hi + w * hi
    o_ref[:, :HALF] = x1_ref[:, :HALF] + mods_ref[5:6, :HALF] * y_lo
    o_ref[:, HALF:] = x1_ref[:, HALF:] + mods_ref[5:6, HALF:] * y_hi


def _combine(dest, ys, x1, tw, mods, mod_row):
    t = x1.shape[0]
    return pl.pallas_call(
        _combine_kernel,
        out_shape=jax.ShapeDtypeStruct((t, D_MODEL), F32),
        grid_spec=pltpu.PrefetchScalarGridSpec(
            num_scalar_prefetch=1,
            grid=(t // COMBINE_ROWS,),
            in_specs=[pl.BlockSpec(memory_space=pl.ANY),
                      pl.BlockSpec((COMBINE_ROWS, D_MODEL), lambda i, d: (i, 0)),
                      pl.BlockSpec((COMBINE_ROWS, LANES), lambda i, d: (i, 0)),
                      pl.BlockSpec((None, 6, D_MODEL), lambda i, d: (mod_row(i * COMBINE_ROWS), 0, 0))],
            out_specs=pl.BlockSpec((COMBINE_ROWS, D_MODEL), lambda i, d: (i, 0)),
            scratch_shapes=[pltpu.VMEM((2, TOP_K, COMBINE_ROWS * TOK_SUB, LANES), I32),
                            pltpu.SemaphoreType.DMA((2,))]),
        compiler_params=_cparams(("arbitrary",)),
        name="moe_combine",
    )(dest, ys, x1, tw, mods)


def _routing_tables(top_idx, rank, counts, n_items, cap):
    nch = (counts + MOE_CHUNK - 1) // MOE_CHUNK
    pad_rows = nch * MOE_CHUNK
    pad_end = jnp.cumsum(pad_rows)
    pad_start = pad_end - pad_rows
    dest = (pad_start[top_idx] + rank).reshape(-1).astype(I32)
    items_e = (nch + MOE_ITEM_CHUNKS - 1) // MOE_ITEM_CHUNKS
    item_end = jnp.cumsum(items_e)
    item_start = item_end - items_e
    total = item_end[-1]
    i = jnp.arange(n_items, dtype=I32)
    ii = jnp.minimum(i, total - 1)
    e_i = jnp.minimum(jnp.searchsorted(item_end, ii, side="right"), N_EXPERTS - 1).astype(I32)
    local = ii - item_start[e_i]
    row0 = (pad_start[e_i] + local * MOE_ITEM_ROWS).astype(I32)
    n_i = jnp.where(i < total, jnp.minimum(MOE_ITEM_CHUNKS, nch[e_i] - local * MOE_ITEM_CHUNKS), 0).astype(I32)
    fill = jnp.concatenate([pad_start + counts, pad_end[-1:], pad_rows - counts, (cap - pad_end[-1:]) // MOE_CHUNK]).astype(I32)
    return dest, fill, e_i, row0, n_i


def _rope_tables(n_lat):
    nf = ROPE // 4
    inv = ROPE_BASE ** (-jnp.arange(nf, dtype=F32) / nf)
    t = jnp.arange(n_lat)
    row = (t // GRID_W).astype(F32)
    col = (t % GRID_W).astype(F32)
    ang_r = row[:, None] * inv[None, :]
    ang_c = col[:, None] * inv[None, :]
    z = jnp.zeros((n_lat, nf), F32)
    tail = jnp.zeros((n_lat, LANES - ROPE), F32)
    cos = jnp.concatenate([jnp.cos(ang_r), jnp.cos(ang_r), jnp.cos(ang_c), jnp.cos(ang_c), tail], axis=1)
    sin_lo = jnp.concatenate([-jnp.sin(ang_r), z, -jnp.sin(ang_c), z, tail], axis=1)
    sin_hi = jnp.concatenate([z, jnp.sin(ang_r), z, jnp.sin(ang_c), tail], axis=1)
    return cos, sin_lo, sin_hi


def kernel(x_prompt, x_sample, cache_ckv, cache_krope, c, c_ctx, norm1_g, norm2_g, w_ada, b_ada, w_in,
           b_branch_gate, kv_norm_g, w_ukv, q_norm_g, k_norm_g, w_o_attn, w_pool, pool_scale, w_o_pool, w_out,
           router_w, router_b, w_exp_gate, b_exp_gate, w_exp_up, b_exp_up, w_exp_down, b_exp_down):
    assert w_in.shape[0] == 1, "single-layer trunk"
    batch, seq, _ = x_prompt.shape
    dec_batch, n_lat, _ = x_sample.shape
    past = cache_ckv.shape[2]
    n_ctx = batch * seq
    n_dec = dec_batch * n_lat

    cond = jnp.concatenate([c_ctx[None, :], c, jnp.zeros((8 - 1 - dec_batch, D_MODEL), F32)], axis=0)
    mods = _adaln(cond, w_ada[0], b_ada).reshape(8, 6, D_MODEL)

    w_cat = _w_in_layout(w_in[0].T)
    qg = jnp.pad(q_norm_g * ATTN_SCALE, ((0, 0), (0, HEAD_PAD - QK)))
    kgn = k_norm_g[:, :NOPE]
    kgr = jnp.pad(k_norm_g[:, NOPE:], ((0, 0), (0, LANES - ROPE)))
    w_ukv_b = w_ukv[0].astype(BF16)
    rope_tabs = _rope_tables(n_lat)

    ctx_row = lambda i: 0
    tm_in = 512
    lat_row_in = lambda i: 1 + (i * tm_in) // n_lat
    q_c, pool_c, gates_c, ckv_c, kr_c, krp_c = _in_proj(
        x_prompt.reshape(n_ctx, D_MODEL), mods, ctx_row, norm1_g, w_cat, qg, kv_norm_g, None, tm_in)
    q_l, pool_l, gates_l, ckv_l, _, krp_l = _in_proj(
        x_sample.reshape(n_dec, D_MODEL), mods, lat_row_in, norm1_g, w_cat, qg, kv_norm_g, rope_tabs, tm_in)

    tr = 256
    k_c, v_c = _kv_expand(ckv_c, krp_c, w_ukv_b, kgn, kgr, None, tr)
    k_l, v_l = _kv_expand(ckv_l, krp_l, w_ukv_b, kgn, kgr, rope_tabs, tr)
    cache_krp = jnp.pad(cache_krope.reshape(dec_batch * past, ROPE), ((0, 0), (0, LANES - ROPE)))
    k_p, v_p = _kv_expand(cache_ckv.reshape(dec_batch * past, KV_RANK), cache_krp, w_ukv_b, kgn, kgr, None, tr)

    attn_c = _attn_ctx(q_c, k_c, v_c, seq)
    attn_l = _attn_lat(q_l, k_l, v_l, k_p, v_p, n_lat, past, 512)

    w_pool_b = w_pool[0].astype(BF16)
    poolo_c = _pool(pool_c, w_pool_b, pool_scale, seq)
    poolo_l = _pool(pool_l, w_pool_b, pool_scale, n_lat)

    woa = w_o_attn[0].astype(BF16)
    wop = w_o_pool[0].astype(BF16)
    wout = w_out[0].astype(BF16)
    rw = jnp.pad(router_w[0], ((0, 0), (0, LANES - N_EXPERTS))).astype(BF16)
    rb = jnp.pad(router_b, ((0, 0), (0, LANES - N_EXPERTS)), constant_values=NEG_BIG)
    tm_mg = 256
    lat_row_mg = lambda i: 1 + (i * tm_mg) // n_lat
    x1_c, h2_c, tidx_c, tw_c, cnt_c = _merge(
        attn_c, poolo_c, gates_c, x_prompt.reshape(n_ctx, D_MODEL), mods, ctx_row,
        b_branch_gate, woa, wop, wout, norm2_g, rw, rb, jnp.zeros((1, LANES), F32), tm_mg)
    x1_l, h2_l, tidx_l, tw_l, cnt_all = _merge(
        attn_l, poolo_l, gates_l, x_sample.reshape(n_dec, D_MODEL), mods, lat_row_mg,
        b_branch_gate, woa, wop, wout, norm2_g, rw, rb, cnt_c, tm_mg)

    n_assign = (n_ctx + n_dec) * TOP_K
    max_chunks = (n_assign + N_EXPERTS * (MOE_CHUNK - 1)) // MOE_CHUNK
    cap = max_chunks * MOE_CHUNK
    n_items = (max_chunks + N_EXPERTS * (MOE_ITEM_CHUNKS - 1)) // MOE_ITEM_CHUNKS
    routed = jnp.concatenate([tidx_c[:, :2 * TOP_K], tidx_l[:, :2 * TOP_K]], axis=0)
    dest, fill, item_e, item_row0, item_nch = _routing_tables(
        routed[:, :TOP_K], routed[:, TOP_K:], cnt_all[0, :N_EXPERTS].astype(I32), n_items, cap)

    xs = _dispatch(dest, fill, h2_c, h2_l, cap)
    ys = _moe_experts(item_e, item_row0, item_nch, xs,
                      w_exp_gate[0], b_exp_gate[0].reshape(N_EXPERTS * N_FF_TILES, FF_TILE),
                      w_exp_up[0], b_exp_up[0].reshape(N_EXPERTS * N_FF_TILES, FF_TILE),
                      w_exp_down[0], b_exp_down[0])

    y_c = _combine(dest[:n_ctx * TOP_K], ys, x1_c, tw_c, mods, lambda r: 0)
    y_l = _combine(dest[n_ctx * TOP_K:], ys, x1_l, tw_l, mods, lambda r: 1 + r // n_lat)

    return (y_c.reshape(batch, seq, D_MODEL),
            y_l.reshape(dec_batch, n_lat, D_MODEL),
            ckv_c.reshape(batch, 1, seq, KV_RANK),
            kr_c.reshape(batch, 1, seq, ROPE))
```

```python
import functools

import jax
import jax.numpy as jnp
from jax import lax
from jax.experimental import pallas as pl
from jax.experimental.pallas import tpu as pltpu

F32 = jnp.float32
BF16 = jnp.bfloat16
I32 = jnp.int32

D_MODEL = 2048
N_HEADS = 16
NOPE = 128
ROPE = 64
QK = NOPE + ROPE
V_DIM = 128
KV_RANK = 512
POOL_WINDOWS = (2, 4, 8, 16)
POOL_WIDTH = 1024
POOL_GROUP = POOL_WIDTH // len(POOL_WINDOWS)
N_EXPERTS = 32
TOP_K = 4
D_FF = 2048
SWIGLU_LIMIT = 7.0
SWIGLU_ALPHA = 1.702
ROPE_BASE = 10000.0
RMS_EPS = 1e-6
GRID_W = 64
ATTN_SCALE = QK ** -0.5

LANES = 128
HEAD_PAD = 2 * LANES
Q_COLS = N_HEADS * HEAD_PAD
IN_TILE = 1024
N_Q_TILES = Q_COLS // IN_TILE
POOL_TILE = N_Q_TILES
GATE_TILE0 = POOL_TILE + 1
N_GATE_TILES = 2 * D_MODEL // IN_TILE
KV_TILE = GATE_TILE0 + N_GATE_TILES
N_IN_TILES = KV_TILE + 1
HEADS_PER_TILE = IN_TILE // HEAD_PAD

MOE_CHUNK = 128
MOE_ITEM_CHUNKS = 16
MOE_ITEM_ROWS = MOE_CHUNK * MOE_ITEM_CHUNKS
FF_TILE = 256
N_FF_TILES = D_FF // FF_TILE
NEG_BIG = -1e30

VMEM_LIMIT = 56 * 1024 * 1024


def _cparams(sem, vmem=VMEM_LIMIT):
    return pltpu.CompilerParams(dimension_semantics=sem, vmem_limit_bytes=vmem)


def _dot(a, b):
    return jnp.dot(a, b, preferred_element_type=F32)


def _sigmoid(z):
    return 1.0 / (1.0 + jnp.exp(-z))


HALF = D_MODEL // 2
HI_MASK = -65536


def _pack_halves(v):
    lo = lax.bitcast_convert_type(v[:, :HALF].astype(BF16).astype(F32), I32)
    hi = lax.bitcast_convert_type(v[:, HALF:].astype(BF16).astype(F32), I32)
    return jnp.bitwise_or(jnp.bitwise_and(hi, HI_MASK), lax.shift_right_logical(lo, 16))


TOK_SUB = HALF // LANES


def _tok_rows(tok0, n):
    row0 = tok0 * TOK_SUB
    return pl.ds(row0 if isinstance(row0, int) else pl.multiple_of(row0, TOK_SUB), n * TOK_SUB)


def _tok_store(ref, tok0, n, words):
    for s in range(TOK_SUB):
        ref[pl.ds(tok0 * TOK_SUB + s, n, stride=TOK_SUB), :] = words[:, s * LANES:(s + 1) * LANES]


def _tok_load(ref, tok0, n):
    return jnp.concatenate([ref[pl.ds(tok0 * TOK_SUB + s, n, stride=TOK_SUB), :] for s in range(TOK_SUB)], axis=1)


def _unpack_halves(w):
    lo = lax.bitcast_convert_type(lax.shift_left(w, 16), F32)
    hi = lax.bitcast_convert_type(jnp.bitwise_and(w, HI_MASK), F32)
    return lo, hi


def _rope_lanes(y, cos, sin_lo, sin_hi):
    return y * cos + pltpu.roll(y, LANES - 16, axis=1) * sin_lo + pltpu.roll(y, 16, axis=1) * sin_hi


def _adaln_kernel(c_ref, w_ref, b_ref, o_ref):
    c = c_ref[...]
    s = (c * _sigmoid(c)).astype(BF16)
    o_ref[...] = _dot(s, w_ref[...].astype(BF16)) + b_ref[...]


def _adaln(cond, w_ada, b_ada):
    rows = cond.shape[0]
    n = w_ada.shape[1]
    tn = 1024
    return pl.pallas_call(
        _adaln_kernel,
        out_shape=jax.ShapeDtypeStruct((rows, n), F32),
        grid=(n // tn,),
        in_specs=[pl.BlockSpec((rows, D_MODEL), lambda j: (0, 0)),
                  pl.BlockSpec((D_MODEL, tn), lambda j: (0, j)),
                  pl.BlockSpec((1, tn), lambda j: (0, j))],
        out_specs=pl.BlockSpec((rows, tn), lambda j: (0, j)),
        compiler_params=_cparams(("arbitrary",)),
        name="adaln",
    )(cond, w_ada, b_ada)


W_Q_END = N_HEADS * QK
W_CKV_END = W_Q_END + KV_RANK
W_KR_END = W_CKV_END + ROPE
N_IN_COLS = W_KR_END + POOL_WIDTH + 2 * D_MODEL
W_PREP = 256


def _w_in_layout_kernel(wt_ref, o_ref):
    lane = lax.broadcasted_iota(I32, (W_PREP, W_PREP), 1)

    def move(src_row, dst_col, keep):
        sq = wt_ref[src_row:src_row + W_PREP, :].T
        if keep < W_PREP:
            sq = jnp.where(lane < keep, sq, 0.0)
        o_ref[:, dst_col:dst_col + W_PREP] = sq.astype(BF16)

    for h in range(N_HEADS):
        move(h * QK, h * HEAD_PAD, QK)
    for p in range((N_IN_COLS - W_KR_END) // W_PREP):
        move(W_KR_END + p * W_PREP, Q_COLS + p * W_PREP, W_PREP)
    kv0 = Q_COLS + N_IN_COLS - W_KR_END
    kv_cols = KV_RANK + ROPE
    for p in range(IN_TILE // W_PREP):
        keep = min(max(kv_cols - p * W_PREP, 0), W_PREP)
        if keep:
            move(W_Q_END + p * W_PREP, kv0 + p * W_PREP, keep)
        else:
            o_ref[:, kv0 + p * W_PREP:kv0 + (p + 1) * W_PREP] = jnp.zeros((W_PREP, W_PREP), BF16)


def _w_in_layout(wt):
    return pl.pallas_call(
        _w_in_layout_kernel,
        out_shape=jax.ShapeDtypeStruct((D_MODEL, N_IN_TILES * IN_TILE), BF16),
        grid=(D_MODEL // W_PREP,),
        in_specs=[pl.BlockSpec((N_IN_COLS, W_PREP), lambda i: (0, i))],
        out_specs=pl.BlockSpec((W_PREP, N_IN_TILES * IN_TILE), lambda i: (i, 0)),
        compiler_params=_cparams(("arbitrary",)),
        name="w_in_layout",
    )(wt)


def _in_proj_kernel(rope, x_ref, mods_ref, g1_ref, w_ref, qg_ref, kvg_ref, *rest):
    if rope:
        cos_ref, slo_ref, shi_ref = rest[:3]
        rest = rest[3:]
    q_ref, pool_ref, gates_ref, ckv_ref, kr_ref, krp_ref, h_scr = rest
    j = pl.program_id(1)

    def proj(c0, c1, h=None):
        return _dot(h_scr[...] if h is None else h, w_ref[:, c0:c1])

    def q_heads(h=None):
        for hh in range(HEADS_PER_TILE):
            a = proj(hh * HEAD_PAD, (hh + 1) * HEAD_PAD, h)
            r = lax.rsqrt(jnp.sum(a * a, axis=-1, keepdims=True) / QK + RMS_EPS)
            y = a * r * qg_ref[...]
            if rope:
                yr = _rope_lanes(y[:, LANES:], cos_ref[...], slo_ref[...], shi_ref[...])
                q_ref[:, hh * HEAD_PAD:hh * HEAD_PAD + LANES] = y[:, :LANES].astype(BF16)
                q_ref[:, hh * HEAD_PAD + LANES:(hh + 1) * HEAD_PAD] = yr.astype(BF16)
            else:
                q_ref[:, hh * HEAD_PAD:(hh + 1) * HEAD_PAD] = y.astype(BF16)

    @pl.when(j == 0)
    def _():
        x = x_ref[...]
        y = x * lax.rsqrt(jnp.mean(x * x, axis=-1, keepdims=True) + RMS_EPS) * g1_ref[...]
        h = (y * (1.0 + mods_ref[1:2, :]) + mods_ref[0:1, :]).astype(BF16)
        h_scr[...] = h
        q_heads(h)

    @pl.when(jnp.logical_and(j > 0, j < N_Q_TILES))
    def _():
        q_heads()

    @pl.when(j == POOL_TILE)
    def _():
        pool_ref[...] = proj(0, IN_TILE)

    @pl.when(jnp.logical_and(j >= GATE_TILE0, j < KV_TILE))
    def _():
        gates_ref[...] = proj(0, IN_TILE)

    @pl.when(j == KV_TILE)
    def _():
        a = proj(0, KV_RANK)
        r = lax.rsqrt(jnp.mean(a * a, axis=-1, keepdims=True) + RMS_EPS)
        ckv_ref[...] = a * r * kvg_ref[...]
        krp = proj(KV_RANK, KV_RANK + LANES)
        krp_ref[...] = krp
        kr_ref[...] = krp[:, :ROPE]


def _in_proj(x, mods, mod_row, g1, w_cat, qg, kvg, rope_tabs, tm):
    t = x.shape[0]
    rope = rope_tabs is not None
    in_specs = [
        pl.BlockSpec((tm, D_MODEL), lambda i, j: (i, 0)),
        pl.BlockSpec((None, 6, D_MODEL), lambda i, j: (mod_row(i), 0, 0)),
        pl.BlockSpec((1, D_MODEL), lambda i, j: (0, 0)),
        pl.BlockSpec((D_MODEL, IN_TILE), lambda i, j: (0, j)),
        pl.BlockSpec((1, HEAD_PAD), lambda i, j: (0, 0)),
        pl.BlockSpec((1, KV_RANK), lambda i, j: (0, 0)),
    ]
    args = [x, mods, g1, w_cat, qg, kvg]
    if rope:
        seq_tiles = rope_tabs[0].shape[0] // tm
        in_specs += [pl.BlockSpec((tm, LANES), lambda i, j: (i % seq_tiles, 0))] * 3
        args += list(rope_tabs)
    out_shape = (
        jax.ShapeDtypeStruct((t, Q_COLS), BF16),
        jax.ShapeDtypeStruct((t, POOL_WIDTH), F32),
        jax.ShapeDtypeStruct((t, 2 * D_MODEL), F32),
        jax.ShapeDtypeStruct((t, KV_RANK), F32),
        jax.ShapeDtypeStruct((t, ROPE), F32),
        jax.ShapeDtypeStruct((t, LANES), F32),
    )
    out_specs = (
        pl.BlockSpec((tm, IN_TILE), lambda i, j: (i, jnp.minimum(j, N_Q_TILES - 1))),
        pl.BlockSpec((tm, POOL_WIDTH), lambda i, j: (i, 0)),
        pl.BlockSpec((tm, IN_TILE), lambda i, j: (i, jnp.clip(j - GATE_TILE0, 0, N_GATE_TILES - 1))),
        pl.BlockSpec((tm, KV_RANK), lambda i, j: (i, 0)),
        pl.BlockSpec((tm, ROPE), lambda i, j: (i, 0)),
        pl.BlockSpec((tm, LANES), lambda i, j: (i, 0)),
    )
    return pl.pallas_call(
        functools.partial(_in_proj_kernel, rope),
        out_shape=out_shape,
        grid=(t // tm, N_IN_TILES),
        in_specs=in_specs,
        out_specs=out_specs,
        scratch_shapes=[pltpu.VMEM((tm, D_MODEL), BF16)],
        compiler_params=_cparams(("arbitrary", "arbitrary")),
        name="in_proj_rope" if rope else "in_proj",
    )(*args)


def _kv_expand_kernel(rope, ckv_ref, krp_ref, w_ref, kgn_ref, kgr_ref, *rest):
    if rope:
        cos_ref, slo_ref, shi_ref = rest[:3]
        rest = rest[3:]
    k_ref, v_ref = rest
    kv = _dot(ckv_ref[...].astype(BF16), w_ref[...])
    kr = krp_ref[...]
    ssq_r = jnp.sum(kr * kr, axis=-1, keepdims=True)
    krg = kr * kgr_ref[...]
    if rope:
        krg = _rope_lanes(krg, cos_ref[...], slo_ref[...], shi_ref[...])
    for h in range(N_HEADS):
        kn = kv[:, h * HEAD_PAD:h * HEAD_PAD + NOPE]
        r = lax.rsqrt((jnp.sum(kn * kn, axis=-1, keepdims=True) + ssq_r) / QK + RMS_EPS)
        k_ref[:, h * HEAD_PAD:h * HEAD_PAD + NOPE] = (kn * r * kgn_ref[...]).astype(BF16)
        k_ref[:, h * HEAD_PAD + NOPE:(h + 1) * HEAD_PAD] = (krg * r).astype(BF16)
        v_ref[:, h * V_DIM:(h + 1) * V_DIM] = kv[:, h * HEAD_PAD + NOPE:(h + 1) * HEAD_PAD].astype(BF16)


def _kv_expand(ckv_n, krp, w_ukv, kgn, kgr, rope_tabs, tr):
    r = ckv_n.shape[0]
    rope = rope_tabs is not None
    in_specs = [
        pl.BlockSpec((tr, KV_RANK), lambda i: (i, 0)),
        pl.BlockSpec((tr, LANES), lambda i: (i, 0)),
        pl.BlockSpec((KV_RANK, N_HEADS * HEAD_PAD), lambda i: (0, 0)),
        pl.BlockSpec((1, LANES), lambda i: (0, 0)),
        pl.BlockSpec((1, LANES), lambda i: (0, 0)),
    ]
    args = [ckv_n, krp, w_ukv, kgn, kgr]
    if rope:
        seq_tiles = rope_tabs[0].shape[0] // tr
        in_specs += [pl.BlockSpec((tr, LANES), lambda i: (i % seq_tiles, 0))] * 3
        args += list(rope_tabs)
    return pl.pallas_call(
        functools.partial(_kv_expand_kernel, rope),
        out_shape=(jax.ShapeDtypeStruct((r, N_HEADS * HEAD_PAD), BF16),
                   jax.ShapeDtypeStruct((r, N_HEADS * V_DIM), BF16)),
        grid=(r // tr,),
        in_specs=in_specs,
        out_specs=(pl.BlockSpec((tr, N_HEADS * HEAD_PAD), lambda i: (i, 0)),
                   pl.BlockSpec((tr, N_HEADS * V_DIM), lambda i: (i, 0))),
        compiler_params=_cparams(("arbitrary",)),
        name="kv_expand_rope" if rope else "kv_expand",
    )(*args)


def _qk(q, k):
    return lax.dot_general(q, k, (((1,), (1,)), ((), ())), preferred_element_type=F32)


def _attn_ctx_kernel(q_ref, k_ref, v_ref, o_ref):
    for h in range(N_HEADS):
        s = _qk(q_ref[:, h * HEAD_PAD:(h + 1) * HEAD_PAD], k_ref[:, h * HEAD_PAD:(h + 1) * HEAD_PAD])
        p = jnp.exp(s - jnp.max(s, axis=-1, keepdims=True))
        l = jnp.sum(p, axis=-1, keepdims=True)
        o = _dot(p.astype(BF16), v_ref[:, h * V_DIM:(h + 1) * V_DIM])
        o_ref[:, h * V_DIM:(h + 1) * V_DIM] = (o / l).astype(BF16)


def _attn_ctx(q, k, v, seq):
    t = q.shape[0]
    return pl.pallas_call(
        _attn_ctx_kernel,
        out_shape=jax.ShapeDtypeStruct((t, N_HEADS * V_DIM), BF16),
        grid=(t // seq,),
        in_specs=[pl.BlockSpec((seq, Q_COLS), lambda b: (b, 0)),
                  pl.BlockSpec((seq, Q_COLS), lambda b: (b, 0)),
                  pl.BlockSpec((seq, N_HEADS * V_DIM), lambda b: (b, 0))],
        out_specs=pl.BlockSpec((seq, N_HEADS * V_DIM), lambda b: (b, 0)),
        compiler_params=_cparams(("arbitrary",)),
        name="attn_ctx",
    )(q, k, v)


def _attn_lat_kernel(q_ref, k_ref, v_ref, kc_ref, vc_ref, o_ref):
    q = q_ref[...]
    s1 = _qk(q, k_ref[...])
    s2 = _qk(q, kc_ref[...])
    m = jnp.maximum(jnp.max(s1, axis=-1, keepdims=True), jnp.max(s2, axis=-1, keepdims=True))
    p1 = jnp.exp(s1 - m)
    p2 = jnp.exp(s2 - m)
    l = jnp.sum(p1, axis=-1, keepdims=True) + jnp.sum(p2, axis=-1, keepdims=True)
    o = _dot(p1.astype(BF16), v_ref[...]) + _dot(p2.astype(BF16), vc_ref[...])
    o_ref[...] = (o / l).astype(BF16)


def _attn_lat(q, k, v, kc, vc, seq, past, tq):
    t = q.shape[0]
    nq = seq // tq
    return pl.pallas_call(
        _attn_lat_kernel,
        out_shape=jax.ShapeDtypeStruct((t, N_HEADS * V_DIM), BF16),
        grid=(t // seq, N_HEADS, nq),
        in_specs=[pl.BlockSpec((tq, HEAD_PAD), lambda b, h, i: (b * nq + i, h)),
                  pl.BlockSpec((seq, HEAD_PAD), lambda b, h, i: (b, h)),
                  pl.BlockSpec((seq, V_DIM), lambda b, h, i: (b, h)),
                  pl.BlockSpec((past, HEAD_PAD), lambda b, h, i: (b, h)),
                  pl.BlockSpec((past, V_DIM), lambda b, h, i: (b, h))],
        out_specs=pl.BlockSpec((tq, V_DIM), lambda b, h, i: (b * nq + i, h)),
        compiler_params=_cparams(("arbitrary", "arbitrary", "arbitrary")),
        name="attn_lat",
    )(q, k, v, kc, vc)


POOL_HALO = 8


def _pool_kernel(seq, u_ref, w_ref, sc_ref, o_ref, pad_scr):
    zeros = jnp.zeros((POOL_HALO, POOL_WIDTH), F32)
    pad_scr[0:POOL_HALO, :] = zeros
    pad_scr[POOL_HALO + seq:2 * POOL_HALO + seq, :] = zeros
    pad_scr[POOL_HALO:POOL_HALO + seq, :] = u_ref[...]
    t = lax.broadcasted_iota(I32, (seq, 1), 0)
    for g, w in enumerate(POOL_WINDOWS):
        cols = slice(g * POOL_GROUP, (g + 1) * POOL_GROUP)
        tot = None
        for d in range(-(w // 2), w - w // 2):
            piece = pad_scr[POOL_HALO + d:POOL_HALO + d + seq, cols]
            tot = piece if tot is None else tot + piece
        cnt = (jnp.minimum(t + (w - w // 2), seq) - jnp.maximum(t - w // 2, 0)).astype(F32)
        mixed = tot / cnt - u_ref[:, cols]
        o_ref[:, cols] = (_dot(mixed.astype(BF16), w_ref[g]) * sc_ref[:, cols]).astype(BF16)


def _pool(u, w_pool, pool_scale, seq):
    t = u.shape[0]
    n_groups = len(POOL_WINDOWS)
    return pl.pallas_call(
        functools.partial(_pool_kernel, seq),
        out_shape=jax.ShapeDtypeStruct((t, POOL_WIDTH), BF16),
        grid=(t // seq,),
        in_specs=[pl.BlockSpec((seq, POOL_WIDTH), lambda b: (b, 0)),
                  pl.BlockSpec((n_groups, POOL_GROUP, POOL_GROUP), lambda b: (0, 0, 0)),
                  pl.BlockSpec((1, POOL_WIDTH), lambda b: (0, 0))],
        out_specs=pl.BlockSpec((seq, POOL_WIDTH), lambda b: (b, 0)),
        scratch_shapes=[pltpu.VMEM((seq + 2 * POOL_HALO, POOL_WIDTH), F32)],
        compiler_params=_cparams(("arbitrary",)),
        name="pool",
    )(u, w_pool, pool_scale)


def _merge_kernel(attn_ref, pool_ref, gates_ref, x_ref, mods_ref, bbg_ref, woa_ref, wop_ref, wout_ref,
                  g2_ref, rw_ref, rb_ref, cnt_in_ref, x1_ref, h2_ref, tidx_ref, tw_ref, cnt_out_ref, cnt_scr):
    @pl.when(pl.program_id(0) == 0)
    def _():
        cnt_scr[...] = cnt_in_ref[...]

    a = _dot(attn_ref[...], woa_ref[...])
    p = _dot(pool_ref[...], wop_ref[...])
    ga = _sigmoid(gates_ref[:, :D_MODEL] + bbg_ref[:, :D_MODEL])
    gp = _sigmoid(gates_ref[:, D_MODEL:] + bbg_ref[:, D_MODEL:])
    merged = (ga * a + gp * p).astype(BF16)
    x1 = x_ref[...] + mods_ref[2:3, :] * _dot(merged, wout_ref[...])
    x1_ref[...] = x1
    y = x1 * lax.rsqrt(jnp.mean(x1 * x1, axis=-1, keepdims=True) + RMS_EPS) * g2_ref[...]
    h2 = y * (1.0 + mods_ref[4:5, :]) + mods_ref[3:4, :]
    _tok_store(h2_ref, 0, h2.shape[0], _pack_halves(h2))
    logits = _dot(h2.astype(BF16), rw_ref[...]) + rb_ref[...]
    lane = lax.broadcasted_iota(I32, logits.shape, 1).astype(F32)
    vals, idxs = [], []
    for _ in range(TOP_K):
        m = jnp.max(logits, axis=-1, keepdims=True)
        ix = jnp.min(jnp.where(logits == m, lane, float(LANES)), axis=-1, keepdims=True)
        vals.append(m)
        idxs.append(ix)
        logits = jnp.where(lane == ix, -jnp.inf, logits)
    es = [jnp.exp(v - vals[0]) for v in vals]
    tot = es[0] + es[1] + es[2] + es[3]
    tidx = jnp.zeros(logits.shape, F32)
    tw = jnp.zeros(logits.shape, F32)
    for k in range(TOP_K):
        tidx = jnp.where(lane == k, idxs[k], tidx)
        tw = jnp.where(lane == k, es[k] / tot, tw)
    rows = logits.shape[0]
    chosen = sum((lane == ix).astype(F32) for ix in idxs)
    earlier = (lax.broadcasted_iota(I32, (rows, rows), 1) < lax.broadcasted_iota(I32, (rows, rows), 0)).astype(BF16)
    before = _dot(earlier, chosen.astype(BF16)) + cnt_scr[...]
    for k in range(TOP_K):
        rank = jnp.sum(jnp.where(lane == idxs[k], before, 0.0), axis=-1, keepdims=True)
        tidx = jnp.where(lane == TOP_K + k, rank, tidx)
    cnt_scr[...] += jnp.sum(chosen, axis=0, keepdims=True)
    cnt_out_ref[...] = cnt_scr[...]
    tidx_ref[...] = tidx.astype(I32)
    tw_ref[...] = tw


def _merge(attn_o, pool_o, gates, x, mods, mod_row, bbg, woa, wop, wout, g2, rw, rb, cnt_in, tm):
    t = x.shape[0]
    const = lambda shape: pl.BlockSpec(shape, lambda i: (0, 0), pipeline_mode=pl.Buffered(1))
    return pl.pallas_call(
        _merge_kernel,
        out_shape=(jax.ShapeDtypeStruct((t, D_MODEL), F32),
                   jax.ShapeDtypeStruct((t * TOK_SUB, LANES), I32),
                   jax.ShapeDtypeStruct((t, LANES), I32),
                   jax.ShapeDtypeStruct((t, LANES), F32),
                   jax.ShapeDtypeStruct((1, LANES), F32)),
        grid=(t // tm,),
        in_specs=[pl.BlockSpec((tm, D_MODEL), lambda i: (i, 0)),
                  pl.BlockSpec((tm, POOL_WIDTH), lambda i: (i, 0)),
                  pl.BlockSpec((tm, 2 * D_MODEL), lambda i: (i, 0)),
                  pl.BlockSpec((tm, D_MODEL), lambda i: (i, 0)),
                  pl.BlockSpec((None, 6, D_MODEL), lambda i: (mod_row(i), 0, 0)),
                  const((1, 2 * D_MODEL)),
                  const((D_MODEL, D_MODEL)),
                  const((POOL_WIDTH, D_MODEL)),
                  const((D_MODEL, D_MODEL)),
                  const((1, D_MODEL)),
                  const((D_MODEL, LANES)),
                  const((1, LANES)),
                  const((1, LANES))],
        out_specs=(pl.BlockSpec((tm, D_MODEL), lambda i: (i, 0)),
                   pl.BlockSpec((tm * TOK_SUB, LANES), lambda i: (i, 0)),
                   pl.BlockSpec((tm, LANES), lambda i: (i, 0)),
                   pl.BlockSpec((tm, LANES), lambda i: (i, 0)),
                   pl.BlockSpec((1, LANES), lambda i: (0, 0))),
        scratch_shapes=[pltpu.VMEM((1, LANES), F32)],
        compiler_params=_cparams(("arbitrary",)),
        name="merge",
    )(attn_o, pool_o, gates, x, mods, bbg, woa, wop, wout, g2, rw, rb, cnt_in)


DISPATCH_ROWS = 512
DMA_UNROLL = 8


N_FILL = N_EXPERTS + 1


def _dispatch_kernel(n_first, dest_ref, fill_ref, ha_ref, hb_ref, xs_ref, zero_scr, sem):
    base = pl.program_id(0) * (DISPATCH_ROWS * TOP_K)

    @pl.when(pl.program_id(0) == 0)
    def _():
        zero_scr[...] = jnp.zeros(zero_scr.shape, I32)

        def fill_copy(slot, n):
            return pltpu.make_async_copy(zero_scr.at[_tok_rows(0, n), :], xs_ref.at[_tok_rows(slot, n), :], sem)

        def for_fill(e, n, op):
            def one(r, c):
                op(fill_copy(fill_ref[e] + r * n, n))
                return c

            lax.fori_loop(0, fill_ref[N_FILL + e], one, 0)

        def each_range(op):
            def expert(e, carry):
                for_fill(e, 1, op)
                return carry

            lax.fori_loop(0, N_EXPERTS, expert, 0)
            for_fill(N_EXPERTS, MOE_CHUNK, op)

        each_range(lambda cp: cp.start())
        each_range(lambda cp: cp.wait())

    def scatter(h_ref):
        def issue(t, carry):
            for k in range(TOP_K):
                pltpu.make_async_copy(h_ref.at[_tok_rows(t, 1), :],
                                      xs_ref.at[_tok_rows(dest_ref[base + t * TOP_K + k], 1), :],
                                      sem).start(priority=k % 2)
            return carry

        lax.fori_loop(0, DISPATCH_ROWS, issue, 0, unroll=DMA_UNROLL // TOP_K)
        for _ in range(TOP_K):
            pltpu.make_async_copy(h_ref, xs_ref.at[_tok_rows(0, DISPATCH_ROWS), :], sem).wait()

    @pl.when(pl.program_id(0) < n_first)
    def _():
        scatter(ha_ref)

    @pl.when(pl.program_id(0) >= n_first)
    def _():
        scatter(hb_ref)


def _dispatch(dest, fill, h2_a, h2_b, cap):
    n_a = h2_a.shape[0] // TOK_SUB // DISPATCH_ROWS
    n_b = h2_b.shape[0] // TOK_SUB // DISPATCH_ROWS
    blk = (DISPATCH_ROWS * TOK_SUB, LANES)
    return pl.pallas_call(
        functools.partial(_dispatch_kernel, n_a),
        out_shape=jax.ShapeDtypeStruct((cap * TOK_SUB, LANES), I32),
        grid_spec=pltpu.PrefetchScalarGridSpec(
            num_scalar_prefetch=2,
            grid=(n_a + n_b,),
            in_specs=[pl.BlockSpec(blk, lambda i, d, z: (jnp.minimum(i, n_a - 1), 0)),
                      pl.BlockSpec(blk, lambda i, d, z: (jnp.maximum(i - n_a, 0), 0))],
            out_specs=pl.BlockSpec(memory_space=pl.ANY),
            scratch_shapes=[pltpu.VMEM((MOE_CHUNK * TOK_SUB, LANES), I32), pltpu.SemaphoreType.DMA]),
        compiler_params=_cparams(("arbitrary",)),
        name="moe_dispatch",
    )(dest, fill, h2_a, h2_b)


MOE_SPAN = 8


def _moe_kernel(ie_ref, ir_ref, in_ref, xs_ref, wg_hbm, bg_ref, wu_hbm, bu_ref, wd_hbm, bd_ref, ys_ref,
                xin_scr, acc_scr, wgf_scr, wuf_scr, wdf_scr, wg_scr, wu_scr, wd_scr, sem_x, sem_y, sem_w):
    i = pl.program_id(0)
    n_items = pl.num_programs(0)
    nch = in_ref[i]
    row0 = ir_ref[i]
    expert = ie_ref[i]
    slot = i % 2
    down_row = pl.ds(expert, 1)

    def w_copies(e, f, ws):
        col0 = f * FF_TILE
        cols = pl.ds(col0 if isinstance(col0, int) else pl.multiple_of(col0, FF_TILE), FF_TILE)
        return (pltpu.make_async_copy(wg_hbm.at[e, :, cols], wgf_scr.at[ws], sem_w.at[ws]),
                pltpu.make_async_copy(wu_hbm.at[e, :, cols], wuf_scr.at[ws], sem_w.at[ws]),
                pltpu.make_async_copy(wd_hbm.at[e, cols, :], wdf_scr.at[ws], sem_w.at[ws]))

    def rows_of(c, k=1):
        return pl.ds(pl.multiple_of(c * MOE_CHUNK, MOE_CHUNK), k * MOE_CHUNK)

    def x_copy(item_row0, c, s):
        return pltpu.make_async_copy(xs_ref.at[_tok_rows(item_row0 + c * MOE_CHUNK, MOE_CHUNK), :],
                                     xin_scr.at[s, _tok_rows(c * MOE_CHUNK, MOE_CHUNK), :], sem_x.at[s])

    def y_copy(item_row0, c, s):
        return pltpu.make_async_copy(xin_scr.at[s, _tok_rows(c * MOE_CHUNK, MOE_CHUNK), :],
                                     ys_ref.at[_tok_rows(item_row0 + c * MOE_CHUNK, MOE_CHUNK), :], sem_y)

    def for_chunks(n, body):
        def step(c, carry):
            body(c)
            return carry

        lax.fori_loop(0, n, step, 0)

    def for_spans(n, body, first_span):
        def span(g, carry):
            body(g * MOE_SPAN, MOE_SPAN)
            return carry

        lax.fori_loop(first_span, n // MOE_SPAN, span, 0)
        k = MOE_SPAN // 2
        while k >= 1:
            start = n // (2 * k) * (2 * k)

            @pl.when(n % (2 * k) >= k)
            def _(start=start, k=k):
                body(start, k)

            k //= 2

    @pl.when(i > 0)
    def _():
        prev = jnp.maximum(i - 1, 0)
        for_chunks(in_ref[prev], lambda c: y_copy(ir_ref[prev], c, 1 - slot).wait())

    @pl.when(i == 0)
    def _():
        for_chunks(nch, lambda c: x_copy(row0, c, slot).start())
        for cp in w_copies(expert, 0, 0):
            cp.start()

    @pl.when(i + 1 < n_items)
    def _():
        nxt = jnp.minimum(i + 1, n_items - 1)
        for_chunks(in_ref[nxt], lambda c: x_copy(ir_ref[nxt], c, 1 - slot).start())

    for_chunks(nch, lambda c: x_copy(row0, c, slot).wait())

    @pl.when(nch > 0)
    def _():
        def cast_weights(ws):
            w = (wgf_scr[ws].astype(BF16), wuf_scr[ws].astype(BF16), wdf_scr[ws].astype(BF16))
            wg_scr[...], wu_scr[...], wd_scr[...] = w
            return w

        def contribution(f, c, k, w):
            wg, wu, wd = (wg_scr[...], wu_scr[...], wd_scr[...]) if w is None else w
            bias_row = pl.ds(expert * N_FF_TILES + f, 1)
            lo, hi = _unpack_halves(_tok_load(xin_scr.at[slot], c * MOE_CHUNK, k * MOE_CHUNK))
            x = jnp.concatenate([lo.astype(BF16), hi.astype(BF16)], axis=1)
            gt = jnp.minimum(_dot(x, wg) + bg_ref[bias_row, :], SWIGLU_LIMIT)
            up = jnp.clip(_dot(x, wu) + bu_ref[bias_row, :], -SWIGLU_LIMIT, SWIGLU_LIMIT)
            act = (up + 1.0) * (gt * _sigmoid(SWIGLU_ALPHA * gt))
            return _dot(act.astype(BF16), wd)

        def first(f, c, k, w=None):
            acc_scr[rows_of(c, k), :] = contribution(f, c, k, w) + bd_ref[down_row, :]

        def middle(f, c, k, w=None):
            acc_scr[rows_of(c, k), :] += contribution(f, c, k, w)

        def last(f, c, k, w=None):
            y = acc_scr[rows_of(c, k), :] + contribution(f, c, k, w)
            _tok_store(xin_scr.at[slot], c * MOE_CHUNK, k * MOE_CHUNK, _pack_halves(y))
            for u in range(k):
                y_copy(row0, c + u, slot).start()

        def ff_step(f, phase, request_next):
            ws = f % 2
            for cp in w_copies(expert, f, ws):
                cp.wait()
            request_next()
            body = functools.partial(phase, f)

            @pl.when(nch >= MOE_SPAN)
            def _():
                body(0, MOE_SPAN, cast_weights(ws))

            @pl.when(nch < MOE_SPAN)
            def _():
                cast_weights(ws)

            for_spans(nch, body, 1)

        def request_tile(f):
            def go():
                for cp in w_copies(expert, f, f % 2):
                    cp.start()
            return go

        def request_next_item():
            nxt = jnp.minimum(i + 1, n_items - 1)

            @pl.when(jnp.logical_and(i + 1 < n_items, in_ref[nxt] > 0))
            def _():
                for cp in w_copies(ie_ref[nxt], 0, 0):
                    cp.start()

        ff_step(0, first, request_tile(1))

        def mid(f, carry):
            ff_step(f, middle, request_tile(f + 1))
            return carry

        lax.fori_loop(1, N_FF_TILES - 1, mid, 0)
        ff_step(N_FF_TILES - 1, last, request_next_item)

        @pl.when(i == n_items - 1)
        def _():
            for_chunks(nch, lambda c: y_copy(row0, c, slot).wait())


def _moe_experts(item_e, item_row0, item_nch, xs, w_g, b_g, w_u, b_u, w_d, b_d):
    n_items = item_e.shape[0]
    whole = lambda a: pl.BlockSpec(a.shape, lambda i, ie, ir, inch: (0, 0))
    return pl.pallas_call(
        _moe_kernel,
        out_shape=jax.ShapeDtypeStruct(xs.shape, I32),
        grid_spec=pltpu.PrefetchScalarGridSpec(
            num_scalar_prefetch=3,
            grid=(n_items,),
            in_specs=[
                pl.BlockSpec(memory_space=pl.ANY),
                pl.BlockSpec(memory_space=pl.ANY), whole(b_g),
                pl.BlockSpec(memory_space=pl.ANY), whole(b_u),
                pl.BlockSpec(memory_space=pl.ANY), whole(b_d),
            ],
            out_specs=pl.BlockSpec(memory_space=pl.ANY),
            scratch_shapes=[
                pltpu.VMEM((2, MOE_ITEM_ROWS * TOK_SUB, LANES), I32),
                pltpu.VMEM((MOE_ITEM_ROWS, D_MODEL), F32),
                pltpu.VMEM((2, D_MODEL, FF_TILE), F32),
                pltpu.VMEM((2, D_MODEL, FF_TILE), F32),
                pltpu.VMEM((2, FF_TILE, D_MODEL), F32),
                pltpu.VMEM((D_MODEL, FF_TILE), BF16),
                pltpu.VMEM((D_MODEL, FF_TILE), BF16),
                pltpu.VMEM((FF_TILE, D_MODEL), BF16),
                pltpu.SemaphoreType.DMA((2,)),
                pltpu.SemaphoreType.DMA,
                pltpu.SemaphoreType.DMA((2,)),
            ]),
        input_output_aliases={3: 0},
        compiler_params=_cparams(("arbitrary",)),
        name="moe_experts",
    )(item_e, item_row0, item_nch, xs, w_g, b_g, w_u, b_u, w_d, b_d)


COMBINE_ROWS = 512


def _combine_kernel(dest_ref, ys_ref, x1_ref, tw_ref, mods_ref, o_ref, ybuf, sem):
    step = pl.program_id(0)
    slot = step % 2

    def gather(blk, s):
        base = blk * (COMBINE_ROWS * TOP_K)

        def issue(t, carry):
            for k in range(TOP_K):
                pltpu.make_async_copy(ys_ref.at[_tok_rows(dest_ref[base + t * TOP_K + k], 1), :],
                                      ybuf.at[s, k, _tok_rows(t, 1), :], sem.at[s]).start(priority=k % 2)
            return carry

        lax.fori_loop(0, COMBINE_ROWS, issue, 0, unroll=DMA_UNROLL // TOP_K)

    @pl.when(step == 0)
    def _():
        gather(step, slot)

    @pl.when(step + 1 < pl.num_programs(0))
    def _():
        gather(step + 1, 1 - slot)

    for k in range(TOP_K):
        pltpu.make_async_copy(ys_ref.at[_tok_rows(0, COMBINE_ROWS), :], ybuf.at[slot, k], sem.at[slot]).wait()

    y_lo = y_hi = None
    for k in range(TOP_K):
        lo, hi = _unpack_halves(_tok_load(ybuf.at[slot, k], 0, COMBINE_ROWS))
        w = tw_ref[:, k:k + 1]
        y_lo = w * lo if y_lo is None else y_lo + w * lo
        y_hi = w * hi if y_hi is None else y_hi + w * hi
    o_ref[:, :HALF] = x1_ref[:, :HALF] + mods_ref[5:6, :HALF] * y_lo
    o_ref[:, HALF:] = x1_ref[:, HALF:] + mods_ref[5:6, HALF:] * y_hi


def _combine(dest, ys, x1, tw, mods, mod_row):
    t = x1.shape[0]
    return pl.pallas_call(
        _combine_kernel,
        out_shape=jax.ShapeDtypeStruct((t, D_MODEL), F32),
        grid_spec=pltpu.PrefetchScalarGridSpec(
            num_scalar_prefetch=1,
            grid=(t // COMBINE_ROWS,),
            in_specs=[pl.BlockSpec(memory_space=pl.ANY),
                      pl.BlockSpec((COMBINE_ROWS, D_MODEL), lambda i, d: (i, 0)),
                      pl.BlockSpec((COMBINE_ROWS, LANES), lambda i, d: (i, 0)),
                      pl.BlockSpec((None, 6, D_MODEL), lambda i, d: (mod_row(i * COMBINE_ROWS), 0, 0))],
            out_specs=pl.BlockSpec((COMBINE_ROWS, D_MODEL), lambda i, d: (i, 0)),
            scratch_shapes=[pltpu.VMEM((2, TOP_K, COMBINE_ROWS * TOK_SUB, LANES), I32),
                            pltpu.SemaphoreType.DMA((2,))]),
        compiler_params=_cparams(("arbitrary",)),
        name="moe_combine",
    )(dest, ys, x1, tw, mods)


def _routing_tables(top_idx, rank, counts, n_items, cap):
    nch = (counts + MOE_CHUNK - 1) // MOE_CHUNK
    pad_rows = nch * MOE_CHUNK
    pad_end = jnp.cumsum(pad_rows)
    pad_start = pad_end - pad_rows
    dest = (pad_start[top_idx] + rank).reshape(-1).astype(I32)
    items_e = (nch + MOE_ITEM_CHUNKS - 1) // MOE_ITEM_CHUNKS
    item_end = jnp.cumsum(items_e)
    item_start = item_end - items_e
    total = item_end[-1]
    i = jnp.arange(n_items, dtype=I32)
    ii = jnp.minimum(i, total - 1)
    e_i = jnp.minimum(jnp.searchsorted(item_end, ii, side="right"), N_EXPERTS - 1).astype(I32)
    local = ii - item_start[e_i]
    row0 = (pad_start[e_i] + local * MOE_ITEM_ROWS).astype(I32)
    n_i = jnp.where(i < total, jnp.minimum(MOE_ITEM_CHUNKS, nch[e_i] - local * MOE_ITEM_CHUNKS), 0).astype(I32)
    fill = jnp.concatenate([pad_start + counts, pad_end[-1:], pad_rows - counts, (cap - pad_end[-1:]) // MOE_CHUNK]).astype(I32)
    return dest, fill, e_i, row0, n_i


def _rope_tables(n_lat):
    nf = ROPE // 4
    inv = ROPE_BASE ** (-jnp.arange(nf, dtype=F32) / nf)
    t = jnp.arange(n_lat)
    row = (t // GRID_W).astype(F32)
    col = (t % GRID_W).astype(F32)
    ang_r = row[:, None] * inv[None, :]
    ang_c = col[:, None] * inv[None, :]
    z = jnp.zeros((n_lat, nf), F32)
    tail = jnp.zeros((n_lat, LANES - ROPE), F32)
    cos = jnp.concatenate([jnp.cos(ang_r), jnp.cos(ang_r), jnp.cos(ang_c), jnp.cos(ang_c), tail], axis=1)
    sin_lo = jnp.concatenate([-jnp.sin(ang_r), z, -jnp.sin(ang_c), z, tail], axis=1)
    sin_hi = jnp.concatenate([z, jnp.sin(ang_r), z, jnp.sin(ang_c), tail], axis=1)
    return cos, sin_lo, sin_hi


def kernel(x_prompt, x_sample, cache_ckv, cache_krope, c, c_ctx, norm1_g, norm2_g, w_ada, b_ada, w_in,
           b_branch_gate, kv_norm_g, w_ukv, q_norm_g, k_norm_g, w_o_attn, w_pool, pool_scale, w_o_pool, w_out,
           router_w, router_b, w_exp_gate, b_exp_gate, w_exp_up, b_exp_up, w_exp_down, b_exp_down):
    assert w_in.shape[0] == 1, "single-layer trunk"
    batch, seq, _ = x_prompt.shape
    dec_batch, n_lat, _ = x_sample.shape
    past = cache_ckv.shape[2]
    n_ctx = batch * seq
    n_dec = dec_batch * n_lat

    cond = jnp.concatenate([c_ctx[None, :], c, jnp.zeros((8 - 1 - dec_batch, D_MODEL), F32)], axis=0)
    mods = _adaln(cond, w_ada[0], b_ada).reshape(8, 6, D_MODEL)

    w_cat = _w_in_layout(w_in[0].T)
    qg = jnp.pad(q_norm_g * ATTN_SCALE, ((0, 0), (0, HEAD_PAD - QK)))
    kgn = k_norm_g[:, :NOPE]
    kgr = jnp.pad(k_norm_g[:, NOPE:], ((0, 0), (0, LANES - ROPE)))
    w_ukv_b = w_ukv[0].astype(BF16)
    rope_tabs = _rope_tables(n_lat)

    ctx_row = lambda i: 0
    tm_in = 512
    lat_row_in = lambda i: 1 + (i * tm_in) // n_lat
    q_c, pool_c, gates_c, ckv_c, kr_c, krp_c = _in_proj(
        x_prompt.reshape(n_ctx, D_MODEL), mods, ctx_row, norm1_g, w_cat, qg, kv_norm_g, None, tm_in)
    q_l, pool_l, gates_l, ckv_l, _, krp_l = _in_proj(
        x_sample.reshape(n_dec, D_MODEL), mods, lat_row_in, norm1_g, w_cat, qg, kv_norm_g, rope_tabs, tm_in)

    tr = 512
    k_c, v_c = _kv_expand(ckv_c, krp_c, w_ukv_b, kgn, kgr, None, tr)
    k_l, v_l = _kv_expand(ckv_l, krp_l, w_ukv_b, kgn, kgr, rope_tabs, tr)
    cache_krp = jnp.pad(cache_krope.reshape(dec_batch * past, ROPE), ((0, 0), (0, LANES - ROPE)))
    k_p, v_p = _kv_expand(cache_ckv.reshape(dec_batch * past, KV_RANK), cache_krp, w_ukv_b, kgn, kgr, None, tr)

    attn_c = _attn_ctx(q_c, k_c, v_c, seq)
    attn_l = _attn_lat(q_l, k_l, v_l, k_p, v_p, n_lat, past, 512)

    w_pool_b = w_pool[0].astype(BF16)
    poolo_c = _pool(pool_c, w_pool_b, pool_scale, seq)
    poolo_l = _pool(pool_l, w_pool_b, pool_scale, n_lat)

    woa = w_o_attn[0].astype(BF16)
    wop = w_o_pool[0].astype(BF16)
    wout = w_out[0].astype(BF16)
    rw = jnp.pad(router_w[0], ((0, 0), (0, LANES - N_EXPERTS))).astype(BF16)
    rb = jnp.pad(router_b, ((0, 0), (0, LANES - N_EXPERTS)), constant_values=NEG_BIG)
    tm_mg = 256
    lat_row_mg = lambda i: 1 + (i * tm_mg) // n_lat
    x1_c, h2_c, tidx_c, tw_c, cnt_c = _merge(
        attn_c, poolo_c, gates_c, x_prompt.reshape(n_ctx, D_MODEL), mods, ctx_row,
        b_branch_gate, woa, wop, wout, norm2_g, rw, rb, jnp.zeros((1, LANES), F32), tm_mg)
    x1_l, h2_l, tidx_l, tw_l, cnt_all = _merge(
        attn_l, poolo_l, gates_l, x_sample.reshape(n_dec, D_MODEL), mods, lat_row_mg,
        b_branch_gate, woa, wop, wout, norm2_g, rw, rb, cnt_c, tm_mg)

    n_assign = (n_ctx + n_dec) * TOP_K
    max_chunks = (n_assign + N_EXPERTS * (MOE_CHUNK - 1)) // MOE_CHUNK
    cap = max_chunks * MOE_CHUNK
    n_items = (max_chunks + N_EXPERTS * (MOE_ITEM_CHUNKS - 1)) // MOE_ITEM_CHUNKS
    routed = jnp.concatenate([tidx_c[:, :2 * TOP_K], tidx_l[:, :2 * TOP_K]], axis=0)
    dest, fill, item_e, item_row0, item_nch = _routing_tables(
        routed[:, :TOP_K], routed[:, TOP_K:], cnt_all[0, :N_EXPERTS].astype(I32), n_items, cap)

    xs = _dispatch(dest, fill, h2_c, h2_l, cap)
    ys = _moe_experts(item_e, item_row0, item_nch, xs,
                      w_exp_gate[0], b_exp_gate[0].reshape(N_EXPERTS * N_FF_TILES, FF_TILE),
                      w_exp_up[0], b_exp_up[0].reshape(N_EXPERTS * N_FF_TILES, FF_TILE),
                      w_exp_down[0], b_exp_down[0])

    y_c = _combine(dest[:n_ctx * TOP_K], ys, x1_c, tw_c, mods, lambda r: 0)
    y_l = _combine(dest[n_ctx * TOP_K:], ys, x1_l, tw_l, mods, lambda r: 1 + r // n_lat)

    return (y_c.reshape(batch, seq, D_MODEL),
            y_l.reshape(dec_batch, n_lat, D_MODEL),
            ckv_c.reshape(batch, 1, seq, KV_RANK),
            kr_c.reshape(batch, 1, seq, ROPE))
```

```python
import functools

import jax
import jax.numpy as jnp
from jax import lax
from jax.experimental import pallas as pl
from jax.experimental.pallas import tpu as pltpu

F32 = jnp.float32
BF16 = jnp.bfloat16
I32 = jnp.int32

D_MODEL = 2048
N_HEADS = 16
NOPE = 128
ROPE = 64
QK = NOPE + ROPE
V_DIM = 128
KV_RANK = 512
POOL_WINDOWS = (2, 4, 8, 16)
POOL_WIDTH = 1024
POOL_GROUP = POOL_WIDTH // len(POOL_WINDOWS)
N_EXPERTS = 32
TOP_K = 4
D_FF = 2048
SWIGLU_LIMIT = 7.0
SWIGLU_ALPHA = 1.702
ROPE_BASE = 10000.0
RMS_EPS = 1e-6
GRID_W = 64
ATTN_SCALE = QK ** -0.5

LANES = 128
HEAD_PAD = 2 * LANES
Q_COLS = N_HEADS * HEAD_PAD
IN_TILE = 1024
N_Q_TILES = Q_COLS // IN_TILE
POOL_TILE = N_Q_TILES
GATE_TILE0 = POOL_TILE + 1
N_GATE_TILES = 2 * D_MODEL // IN_TILE
KV_TILE = GATE_TILE0 + N_GATE_TILES
N_IN_TILES = KV_TILE + 1
HEADS_PER_TILE = IN_TILE // HEAD_PAD

MOE_CHUNK = 128
MOE_ITEM_CHUNKS = 16
MOE_ITEM_ROWS = MOE_CHUNK * MOE_ITEM_CHUNKS
FF_TILE = 256
N_FF_TILES = D_FF // FF_TILE
NEG_BIG = -1e30

VMEM_LIMIT = 56 * 1024 * 1024


def _cparams(sem, vmem=VMEM_LIMIT):
    return pltpu.CompilerParams(dimension_semantics=sem, vmem_limit_bytes=vmem)


def _dot(a, b):
    return jnp.dot(a, b, preferred_element_type=F32)


def _sigmoid(z):
    return 1.0 / (1.0 + jnp.exp(-z))


HALF = D_MODEL // 2
HI_MASK = -65536


def _pack_halves(v):
    lo = lax.bitcast_convert_type(v[:, :HALF].astype(BF16).astype(F32), I32)
    hi = lax.bitcast_convert_type(v[:, HALF:].astype(BF16).astype(F32), I32)
    return jnp.bitwise_or(jnp.bitwise_and(hi, HI_MASK), lax.shift_right_logical(lo, 16))


TOK_SUB = HALF // LANES


def _tok_rows(tok0, n):
    row0 = tok0 * TOK_SUB
    return pl.ds(row0 if isinstance(row0, int) else pl.multiple_of(row0, TOK_SUB), n * TOK_SUB)


def _tok_store(ref, tok0, n, words):
    for s in range(TOK_SUB):
        ref[pl.ds(tok0 * TOK_SUB + s, n, stride=TOK_SUB), :] = words[:, s * LANES:(s + 1) * LANES]


def _tok_load(ref, tok0, n):
    return jnp.concatenate([ref[pl.ds(tok0 * TOK_SUB + s, n, stride=TOK_SUB), :] for s in range(TOK_SUB)], axis=1)


def _unpack_halves(w):
    lo = lax.bitcast_convert_type(lax.shift_left(w, 16), F32)
    hi = lax.bitcast_convert_type(jnp.bitwise_and(w, HI_MASK), F32)
    return lo, hi


def _rope_lanes(y, cos, sin_lo, sin_hi):
    return y * cos + pltpu.roll(y, LANES - 16, axis=1) * sin_lo + pltpu.roll(y, 16, axis=1) * sin_hi


def _adaln_kernel(c_ref, w_ref, b_ref, o_ref):
    c = c_ref[...]
    s = (c * _sigmoid(c)).astype(BF16)
    o_ref[...] = _dot(s, w_ref[...].astype(BF16)) + b_ref[...]


def _adaln(cond, w_ada, b_ada):
    rows = cond.shape[0]
    n = w_ada.shape[1]
    tn = 1024
    return pl.pallas_call(
        _adaln_kernel,
        out_shape=jax.ShapeDtypeStruct((rows, n), F32),
        grid=(n // tn,),
        in_specs=[pl.BlockSpec((rows, D_MODEL), lambda j: (0, 0)),
                  pl.BlockSpec((D_MODEL, tn), lambda j: (0, j)),
                  pl.BlockSpec((1, tn), lambda j: (0, j))],
        out_specs=pl.BlockSpec((rows, tn), lambda j: (0, j)),
        compiler_params=_cparams(("arbitrary",)),
        name="adaln",
    )(cond, w_ada, b_ada)


W_Q_END = N_HEADS * QK
W_CKV_END = W_Q_END + KV_RANK
W_KR_END = W_CKV_END + ROPE
N_IN_COLS = W_KR_END + POOL_WIDTH + 2 * D_MODEL
W_PREP = 256


def _w_in_layout_kernel(wt_ref, o_ref):
    lane = lax.broadcasted_iota(I32, (W_PREP, W_PREP), 1)

    def move(src_row, dst_col, keep):
        sq = wt_ref[src_row:src_row + W_PREP, :].T
        if keep < W_PREP:
            sq = jnp.where(lane < keep, sq, 0.0)
        o_ref[:, dst_col:dst_col + W_PREP] = sq.astype(BF16)

    for h in range(N_HEADS):
        move(h * QK, h * HEAD_PAD, QK)
    for p in range((N_IN_COLS - W_KR_END) // W_PREP):
        move(W_KR_END + p * W_PREP, Q_COLS + p * W_PREP, W_PREP)
    kv0 = Q_COLS + N_IN_COLS - W_KR_END
    kv_cols = KV_RANK + ROPE
    for p in range(IN_TILE // W_PREP):
        keep = min(max(kv_cols - p * W_PREP, 0), W_PREP)
        if keep:
            move(W_Q_END + p * W_PREP, kv0 + p * W_PREP, keep)
        else:
            o_ref[:, kv0 + p * W_PREP:kv0 + (p + 1) * W_PREP] = jnp.zeros((W_PREP, W_PREP), BF16)


def _w_in_layout(wt):
    return pl.pallas_call(
        _w_in_layout_kernel,
        out_shape=jax.ShapeDtypeStruct((D_MODEL, N_IN_TILES * IN_TILE), BF16),
        grid=(D_MODEL // W_PREP,),
        in_specs=[pl.BlockSpec((N_IN_COLS, W_PREP), lambda i: (0, i))],
        out_specs=pl.BlockSpec((W_PREP, N_IN_TILES * IN_TILE), lambda i: (i, 0)),
        compiler_params=_cparams(("arbitrary",)),
        name="w_in_layout",
    )(wt)


def _in_proj_kernel(rope, x_ref, mods_ref, g1_ref, w_ref, qg_ref, kvg_ref, *rest):
    if rope:
        cos_ref, slo_ref, shi_ref = rest[:3]
        rest = rest[3:]
    q_ref, pool_ref, gates_ref, ckv_ref, kr_ref, krp_ref, h_scr = rest
    j = pl.program_id(1)

    def proj(c0, c1, h=None):
        return _dot(h_scr[...] if h is None else h, w_ref[:, c0:c1])

    def q_heads(h=None):
        for hh in range(HEADS_PER_TILE):
            a = proj(hh * HEAD_PAD, (hh + 1) * HEAD_PAD, h)
            r = lax.rsqrt(jnp.sum(a * a, axis=-1, keepdims=True) / QK + RMS_EPS)
            y = a * r * qg_ref[...]
            if rope:
                yr = _rope_lanes(y[:, LANES:], cos_ref[...], slo_ref[...], shi_ref[...])
                q_ref[:, hh * HEAD_PAD:hh * HEAD_PAD + LANES] = y[:, :LANES].astype(BF16)
                q_ref[:, hh * HEAD_PAD + LANES:(hh + 1) * HEAD_PAD] = yr.astype(BF16)
            else:
                q_ref[:, hh * HEAD_PAD:(hh + 1) * HEAD_PAD] = y.astype(BF16)

    @pl.when(j == 0)
    def _():
        x = x_ref[...]
        y = x * lax.rsqrt(jnp.mean(x * x, axis=-1, keepdims=True) + RMS_EPS) * g1_ref[...]
        h = (y * (1.0 + mods_ref[1:2, :]) + mods_ref[0:1, :]).astype(BF16)
        h_scr[...] = h
        q_heads(h)

    @pl.when(jnp.logical_and(j > 0, j < N_Q_TILES))
    def _():
        q_heads()

    @pl.when(j == POOL_TILE)
    def _():
        pool_ref[...] = proj(0, IN_TILE)

    @pl.when(jnp.logical_and(j >= GATE_TILE0, j < KV_TILE))
    def _():
        gates_ref[...] = proj(0, IN_TILE)

    @pl.when(j == KV_TILE)
    def _():
        a = proj(0, KV_RANK)
        r = lax.rsqrt(jnp.mean(a * a, axis=-1, keepdims=True) + RMS_EPS)
        ckv_ref[...] = a * r * kvg_ref[...]
        krp = proj(KV_RANK, KV_RANK + LANES)
        krp_ref[...] = krp
        kr_ref[...] = krp[:, :ROPE]


def _in_proj(x, mods, mod_row, g1, w_cat, qg, kvg, rope_tabs, tm):
    t = x.shape[0]
    rope = rope_tabs is not None
    in_specs = [
        pl.BlockSpec((tm, D_MODEL), lambda i, j: (i, 0)),
        pl.BlockSpec((None, 6, D_MODEL), lambda i, j: (mod_row(i), 0, 0)),
        pl.BlockSpec((1, D_MODEL), lambda i, j: (0, 0)),
        pl.BlockSpec((D_MODEL, IN_TILE), lambda i, j: (0, j)),
        pl.BlockSpec((1, HEAD_PAD), lambda i, j: (0, 0)),
        pl.BlockSpec((1, KV_RANK), lambda i, j: (0, 0)),
    ]
    args = [x, mods, g1, w_cat, qg, kvg]
    if rope:
        seq_tiles = rope_tabs[0].shape[0] // tm
        in_specs += [pl.BlockSpec((tm, LANES), lambda i, j: (i % seq_tiles, 0))] * 3
        args += list(rope_tabs)
    out_shape = (
        jax.ShapeDtypeStruct((t, Q_COLS), BF16),
        jax.ShapeDtypeStruct((t, POOL_WIDTH), F32),
        jax.ShapeDtypeStruct((t, 2 * D_MODEL), F32),
        jax.ShapeDtypeStruct((t, KV_RANK), F32),
        jax.ShapeDtypeStruct((t, ROPE), F32),
        jax.ShapeDtypeStruct((t, LANES), F32),
    )
    out_specs = (
        pl.BlockSpec((tm, IN_TILE), lambda i, j: (i, jnp.minimum(j, N_Q_TILES - 1))),
        pl.BlockSpec((tm, POOL_WIDTH), lambda i, j: (i, 0)),
        pl.BlockSpec((tm, IN_TILE), lambda i, j: (i, jnp.clip(j - GATE_TILE0, 0, N_GATE_TILES - 1))),
        pl.BlockSpec((tm, KV_RANK), lambda i, j: (i, 0)),
        pl.BlockSpec((tm, ROPE), lambda i, j: (i, 0)),
        pl.BlockSpec((tm, LANES), lambda i, j: (i, 0)),
    )
    return pl.pallas_call(
        functools.partial(_in_proj_kernel, rope),
        out_shape=out_shape,
        grid=(t // tm, N_IN_TILES),
        in_specs=in_specs,
        out_specs=out_specs,
        scratch_shapes=[pltpu.VMEM((tm, D_MODEL), BF16)],
        compiler_params=_cparams(("arbitrary", "arbitrary")),
        name="in_proj_rope" if rope else "in_proj",
    )(*args)


def _kv_expand_kernel(rope, ckv_ref, krp_ref, w_ref, kgn_ref, kgr_ref, *rest):
    if rope:
        cos_ref, slo_ref, shi_ref = rest[:3]
        rest = rest[3:]
    k_ref, v_ref = rest
    kv = _dot(ckv_ref[...].astype(BF16), w_ref[...])
    kr = krp_ref[...]
    ssq_r = jnp.sum(kr * kr, axis=-1, keepdims=True)
    krg = kr * kgr_ref[...]
    if rope:
        krg = _rope_lanes(krg, cos_ref[...], slo_ref[...], shi_ref[...])
    for h in range(N_HEADS):
        kn = kv[:, h * HEAD_PAD:h * HEAD_PAD + NOPE]
        r = lax.rsqrt((jnp.sum(kn * kn, axis=-1, keepdims=True) + ssq_r) / QK + RMS_EPS)
        k_ref[:, h * HEAD_PAD:h * HEAD_PAD + NOPE] = (kn * r * kgn_ref[...]).astype(BF16)
        k_ref[:, h * HEAD_PAD + NOPE:(h + 1) * HEAD_PAD] = (krg * r).astype(BF16)
        v_ref[:, h * V_DIM:(h + 1) * V_DIM] = kv[:, h * HEAD_PAD + NOPE:(h + 1) * HEAD_PAD].astype(BF16)


def _kv_expand(ckv_n, krp, w_ukv, kgn, kgr, rope_tabs, tr):
    r = ckv_n.shape[0]
    rope = rope_tabs is not None
    in_specs = [
        pl.BlockSpec((tr, KV_RANK), lambda i: (i, 0)),
        pl.BlockSpec((tr, LANES), lambda i: (i, 0)),
        pl.BlockSpec((KV_RANK, N_HEADS * HEAD_PAD), lambda i: (0, 0)),
        pl.BlockSpec((1, LANES), lambda i: (0, 0)),
        pl.BlockSpec((1, LANES), lambda i: (0, 0)),
    ]
    args = [ckv_n, krp, w_ukv, kgn, kgr]
    if rope:
        seq_tiles = rope_tabs[0].shape[0] // tr
        in_specs += [pl.BlockSpec((tr, LANES), lambda i: (i % seq_tiles, 0))] * 3
        args += list(rope_tabs)
    return pl.pallas_call(
        functools.partial(_kv_expand_kernel, rope),
        out_shape=(jax.ShapeDtypeStruct((r, N_HEADS * HEAD_PAD), BF16),
                   jax.ShapeDtypeStruct((r, N_HEADS * V_DIM), BF16)),
        grid=(r // tr,),
        in_specs=in_specs,
        out_specs=(pl.BlockSpec((tr, N_HEADS * HEAD_PAD), lambda i: (i, 0)),
                   pl.BlockSpec((tr, N_HEADS * V_DIM), lambda i: (i, 0))),
        compiler_params=_cparams(("arbitrary",)),
        name="kv_expand_rope" if rope else "kv_expand",
    )(*args)


def _qk(q, k):
    return lax.dot_general(q, k, (((1,), (1,)), ((), ())), preferred_element_type=F32)


def _attn_ctx_kernel(q_ref, k_ref, v_ref, o_ref):
    for h in range(N_HEADS):
        s = _qk(q_ref[:, h * HEAD_PAD:(h + 1) * HEAD_PAD], k_ref[:, h * HEAD_PAD:(h + 1) * HEAD_PAD])
        p = jnp.exp(s - jnp.max(s, axis=-1, keepdims=True))
        l = jnp.sum(p, axis=-1, keepdims=True)
        o = _dot(p.astype(BF16), v_ref[:, h * V_DIM:(h + 1) * V_DIM])
        o_ref[:, h * V_DIM:(h + 1) * V_DIM] = (o / l).astype(BF16)


def _attn_ctx(q, k, v, seq):
    t = q.shape[0]
    return pl.pallas_call(
        _attn_ctx_kernel,
        out_shape=jax.ShapeDtypeStruct((t, N_HEADS * V_DIM), BF16),
        grid=(t // seq,),
        in_specs=[pl.BlockSpec((seq, Q_COLS), lambda b: (b, 0)),
                  pl.BlockSpec((seq, Q_COLS), lambda b: (b, 0)),
                  pl.BlockSpec((seq, N_HEADS * V_DIM), lambda b: (b, 0))],
        out_specs=pl.BlockSpec((seq, N_HEADS * V_DIM), lambda b: (b, 0)),
        compiler_params=_cparams(("arbitrary",)),
        name="attn_ctx",
    )(q, k, v)


def _attn_lat_kernel(q_ref, k_ref, v_ref, kc_ref, vc_ref, o_ref):
    q = q_ref[...]
    s1 = _qk(q, k_ref[...])
    s2 = _qk(q, kc_ref[...])
    m = jnp.maximum(jnp.max(s1, axis=-1, keepdims=True), jnp.max(s2, axis=-1, keepdims=True))
    p1 = jnp.exp(s1 - m)
    p2 = jnp.exp(s2 - m)
    l = jnp.sum(p1, axis=-1, keepdims=True) + jnp.sum(p2, axis=-1, keepdims=True)
    o = _dot(p1.astype(BF16), v_ref[...]) + _dot(p2.astype(BF16), vc_ref[...])
    o_ref[...] = (o / l).astype(BF16)


def _attn_lat(q, k, v, kc, vc, seq, past, tq):
    t = q.shape[0]
    nq = seq // tq
    return pl.pallas_call(
        _attn_lat_kernel,
        out_shape=jax.ShapeDtypeStruct((t, N_HEADS * V_DIM), BF16),
        grid=(t // seq, N_HEADS, nq),
        in_specs=[pl.BlockSpec((tq, HEAD_PAD), lambda b, h, i: (b * nq + i, h)),
                  pl.BlockSpec((seq, HEAD_PAD), lambda b, h, i: (b, h)),
                  pl.BlockSpec((seq, V_DIM), lambda b, h, i: (b, h)),
                  pl.BlockSpec((past, HEAD_PAD), lambda b, h, i: (b, h)),
                  pl.BlockSpec((past, V_DIM), lambda b, h, i: (b, h))],
        out_specs=pl.BlockSpec((tq, V_DIM), lambda b, h, i: (b * nq + i, h)),
        compiler_params=_cparams(("arbitrary", "arbitrary", "arbitrary")),
        name="attn_lat",
    )(q, k, v, kc, vc)


POOL_HALO = 8


def _pool_kernel(seq, u_ref, w_ref, sc_ref, o_ref, pad_scr):
    zeros = jnp.zeros((POOL_HALO, POOL_WIDTH), F32)
    pad_scr[0:POOL_HALO, :] = zeros
    pad_scr[POOL_HALO + seq:2 * POOL_HALO + seq, :] = zeros
    pad_scr[POOL_HALO:POOL_HALO + seq, :] = u_ref[...]
    t = lax.broadcasted_iota(I32, (seq, 1), 0)
    for g, w in enumerate(POOL_WINDOWS):
        cols = slice(g * POOL_GROUP, (g + 1) * POOL_GROUP)
        tot = None
        for d in range(-(w // 2), w - w // 2):
            piece = pad_scr[POOL_HALO + d:POOL_HALO + d + seq, cols]
            tot = piece if tot is None else tot + piece
        cnt = (jnp.minimum(t + (w - w // 2), seq) - jnp.maximum(t - w // 2, 0)).astype(F32)
        mixed = tot / cnt - u_ref[:, cols]
        o_ref[:, cols] = (_dot(mixed.astype(BF16), w_ref[g]) * sc_ref[:, cols]).astype(BF16)


def _pool(u, w_pool, pool_scale, seq):
    t = u.shape[0]
    n_groups = len(POOL_WINDOWS)
    return pl.pallas_call(
        functools.partial(_pool_kernel, seq),
        out_shape=jax.ShapeDtypeStruct((t, POOL_WIDTH), BF16),
        grid=(t // seq,),
        in_specs=[pl.BlockSpec((seq, POOL_WIDTH), lambda b: (b, 0)),
                  pl.BlockSpec((n_groups, POOL_GROUP, POOL_GROUP), lambda b: (0, 0, 0)),
                  pl.BlockSpec((1, POOL_WIDTH), lambda b: (0, 0))],
        out_specs=pl.BlockSpec((seq, POOL_WIDTH), lambda b: (b, 0)),
        scratch_shapes=[pltpu.VMEM((seq + 2 * POOL_HALO, POOL_WIDTH), F32)],
        compiler_params=_cparams(("arbitrary",)),
        name="pool",
    )(u, w_pool, pool_scale)


def _merge_kernel(attn_ref, pool_ref, gates_ref, x_ref, mods_ref, bbg_ref, woa_ref, wop_ref, wout_ref,
                  g2_ref, rw_ref, rb_ref, cnt_in_ref, x1_ref, h2_ref, tidx_ref, tw_ref, cnt_out_ref, cnt_scr):
    @pl.when(pl.program_id(0) == 0)
    def _():
        cnt_scr[...] = cnt_in_ref[...]

    a = _dot(attn_ref[...], woa_ref[...])
    p = _dot(pool_ref[...], wop_ref[...])
    ga = _sigmoid(gates_ref[:, :D_MODEL] + bbg_ref[:, :D_MODEL])
    gp = _sigmoid(gates_ref[:, D_MODEL:] + bbg_ref[:, D_MODEL:])
    merged = (ga * a + gp * p).astype(BF16)
    x1 = x_ref[...] + mods_ref[2:3, :] * _dot(merged, wout_ref[...])
    x1_ref[...] = x1
    y = x1 * lax.rsqrt(jnp.mean(x1 * x1, axis=-1, keepdims=True) + RMS_EPS) * g2_ref[...]
    h2 = y * (1.0 + mods_ref[4:5, :]) + mods_ref[3:4, :]
    _tok_store(h2_ref, 0, h2.shape[0], _pack_halves(h2))
    logits = _dot(h2.astype(BF16), rw_ref[...]) + rb_ref[...]
    lane = lax.broadcasted_iota(I32, logits.shape, 1).astype(F32)
    vals, idxs = [], []
    for _ in range(TOP_K):
        m = jnp.max(logits, axis=-1, keepdims=True)
        ix = jnp.min(jnp.where(logits == m, lane, float(LANES)), axis=-1, keepdims=True)
        vals.append(m)
        idxs.append(ix)
        logits = jnp.where(lane == ix, -jnp.inf, logits)
    es = [jnp.exp(v - vals[0]) for v in vals]
    tot = es[0] + es[1] + es[2] + es[3]
    tidx = jnp.zeros(logits.shape, F32)
    tw = jnp.zeros(logits.shape, F32)
    for k in range(TOP_K):
        tidx = jnp.where(lane == k, idxs[k], tidx)
        tw = jnp.where(lane == k, es[k] / tot, tw)
    rows = logits.shape[0]
    chosen = sum((lane == ix).astype(F32) for ix in idxs)
    earlier = (lax.broadcasted_iota(I32, (rows, rows), 1) < lax.broadcasted_iota(I32, (rows, rows), 0)).astype(BF16)
    before = _dot(earlier, chosen.astype(BF16)) + cnt_scr[...]
    for k in range(TOP_K):
        rank = jnp.sum(jnp.where(lane == idxs[k], before, 0.0), axis=-1, keepdims=True)
        tidx = jnp.where(lane == TOP_K + k, rank, tidx)
    cnt_scr[...] += jnp.sum(chosen, axis=0, keepdims=True)
    cnt_out_ref[...] = cnt_scr[...]
    tidx_ref[...] = tidx.astype(I32)
    tw_ref[...] = tw


def _merge(attn_o, pool_o, gates, x, mods, mod_row, bbg, woa, wop, wout, g2, rw, rb, cnt_in, tm):
    t = x.shape[0]
    const = lambda shape: pl.BlockSpec(shape, lambda i: (0, 0), pipeline_mode=pl.Buffered(1))
    return pl.pallas_call(
        _merge_kernel,
        out_shape=(jax.ShapeDtypeStruct((t, D_MODEL), F32),
                   jax.ShapeDtypeStruct((t * TOK_SUB, LANES), I32),
                   jax.ShapeDtypeStruct((t, LANES), I32),
                   jax.ShapeDtypeStruct((t, LANES), F32),
                   jax.ShapeDtypeStruct((1, LANES), F32)),
        grid=(t // tm,),
        in_specs=[pl.BlockSpec((tm, D_MODEL), lambda i: (i, 0)),
                  pl.BlockSpec((tm, POOL_WIDTH), lambda i: (i, 0)),
                  pl.BlockSpec((tm, 2 * D_MODEL), lambda i: (i, 0)),
                  pl.BlockSpec((tm, D_MODEL), lambda i: (i, 0)),
                  pl.BlockSpec((None, 6, D_MODEL), lambda i: (mod_row(i), 0, 0)),
                  const((1, 2 * D_MODEL)),
                  const((D_MODEL, D_MODEL)),
                  const((POOL_WIDTH, D_MODEL)),
                  const((D_MODEL, D_MODEL)),
                  const((1, D_MODEL)),
                  const((D_MODEL, LANES)),
                  const((1, LANES)),
                  const((1, LANES))],
        out_specs=(pl.BlockSpec((tm, D_MODEL), lambda i: (i, 0)),
                   pl.BlockSpec((tm * TOK_SUB, LANES), lambda i: (i, 0)),
                   pl.BlockSpec((tm, LANES), lambda i: (i, 0)),
                   pl.BlockSpec((tm, LANES), lambda i: (i, 0)),
                   pl.BlockSpec((1, LANES), lambda i: (0, 0))),
        scratch_shapes=[pltpu.VMEM((1, LANES), F32)],
        compiler_params=_cparams(("arbitrary",)),
        name="merge",
    )(attn_o, pool_o, gates, x, mods, bbg, woa, wop, wout, g2, rw, rb, cnt_in)


DISPATCH_ROWS = 512
DMA_UNROLL = 8


N_FILL = N_EXPERTS + 1


def _dispatch_kernel(n_first, dest_ref, fill_ref, ha_ref, hb_ref, xs_ref, zero_scr, sem):
    base = pl.program_id(0) * (DISPATCH_ROWS * TOP_K)

    @pl.when(pl.program_id(0) == 0)
    def _():
        zero_scr[...] = jnp.zeros(zero_scr.shape, I32)

        def fill_copy(slot, n):
            return pltpu.make_async_copy(zero_scr.at[_tok_rows(0, n), :], xs_ref.at[_tok_rows(slot, n), :], sem)

        def for_fill(e, n, op):
            def one(r, c):
                op(fill_copy(fill_ref[e] + r * n, n))
                return c

            lax.fori_loop(0, fill_ref[N_FILL + e], one, 0)

        def each_range(op):
            def expert(e, carry):
                for_fill(e, 1, op)
                return carry

            lax.fori_loop(0, N_EXPERTS, expert, 0)
            for_fill(N_EXPERTS, MOE_CHUNK, op)

        each_range(lambda cp: cp.start())
        each_range(lambda cp: cp.wait())

    def scatter(h_ref):
        def issue(t, carry):
            for k in range(TOP_K):
                pltpu.make_async_copy(h_ref.at[_tok_rows(t, 1), :],
                                      xs_ref.at[_tok_rows(dest_ref[base + t * TOP_K + k], 1), :],
                                      sem).start(priority=k % 2)
            return carry

        lax.fori_loop(0, DISPATCH_ROWS, issue, 0, unroll=DMA_UNROLL // TOP_K)
        for _ in range(TOP_K):
            pltpu.make_async_copy(h_ref, xs_ref.at[_tok_rows(0, DISPATCH_ROWS), :], sem).wait()

    @pl.when(pl.program_id(0) < n_first)
    def _():
        scatter(ha_ref)

    @pl.when(pl.program_id(0) >= n_first)
    def _():
        scatter(hb_ref)


def _dispatch(dest, fill, h2_a, h2_b, cap):
    n_a = h2_a.shape[0] // TOK_SUB // DISPATCH_ROWS
    n_b = h2_b.shape[0] // TOK_SUB // DISPATCH_ROWS
    blk = (DISPATCH_ROWS * TOK_SUB, LANES)
    return pl.pallas_call(
        functools.partial(_dispatch_kernel, n_a),
        out_shape=jax.ShapeDtypeStruct((cap * TOK_SUB, LANES), I32),
        grid_spec=pltpu.PrefetchScalarGridSpec(
            num_scalar_prefetch=2,
            grid=(n_a + n_b,),
            in_specs=[pl.BlockSpec(blk, lambda i, d, z: (jnp.minimum(i, n_a - 1), 0)),
                      pl.BlockSpec(blk, lambda i, d, z: (jnp.maximum(i - n_a, 0), 0))],
            out_specs=pl.BlockSpec(memory_space=pl.ANY),
            scratch_shapes=[pltpu.VMEM((MOE_CHUNK * TOK_SUB, LANES), I32), pltpu.SemaphoreType.DMA]),
        compiler_params=_cparams(("arbitrary",)),
        name="moe_dispatch",
    )(dest, fill, h2_a, h2_b)


MOE_SPAN = 8


def _moe_kernel(ie_ref, ir_ref, in_ref, xs_ref, wg_hbm, bg_ref, wu_hbm, bu_ref, wd_hbm, bd_ref, ys_ref,
                xin_scr, acc_scr, wgf_scr, wuf_scr, wdf_scr, wg_scr, wu_scr, wd_scr, sem_x, sem_y, sem_w):
    i = pl.program_id(0)
    n_items = pl.num_programs(0)
    nch = in_ref[i]
    row0 = ir_ref[i]
    expert = ie_ref[i]
    slot = i % 2
    down_row = pl.ds(expert, 1)

    def w_copies(e, f, ws):
        col0 = f * FF_TILE
        cols = pl.ds(col0 if isinstance(col0, int) else pl.multiple_of(col0, FF_TILE), FF_TILE)
        return (pltpu.make_async_copy(wg_hbm.at[e, :, cols], wgf_scr.at[ws], sem_w.at[ws]),
                pltpu.make_async_copy(wu_hbm.at[e, :, cols], wuf_scr.at[ws], sem_w.at[ws]),
                pltpu.make_async_copy(wd_hbm.at[e, cols, :], wdf_scr.at[ws], sem_w.at[ws]))

    def rows_of(c, k=1):
        return pl.ds(pl.multiple_of(c * MOE_CHUNK, MOE_CHUNK), k * MOE_CHUNK)

    def x_copy(item_row0, c, s):
        return pltpu.make_async_copy(xs_ref.at[_tok_rows(item_row0 + c * MOE_CHUNK, MOE_CHUNK), :],
                                     xin_scr.at[s, _tok_rows(c * MOE_CHUNK, MOE_CHUNK), :], sem_x.at[s])

    def y_copy(item_row0, c, s):
        return pltpu.make_async_copy(xin_scr.at[s, _tok_rows(c * MOE_CHUNK, MOE_CHUNK), :],
                                     ys_ref.at[_tok_rows(item_row0 + c * MOE_CHUNK, MOE_CHUNK), :], sem_y)

    def for_chunks(n, body):
        def step(c, carry):
            body(c)
            return carry

        lax.fori_loop(0, n, step, 0)

    def for_spans(n, body, first_span):
        def span(g, carry):
            body(g * MOE_SPAN, MOE_SPAN)
            return carry

        lax.fori_loop(first_span, n // MOE_SPAN, span, 0)
        k = MOE_SPAN // 2
        while k >= 1:
            start = n // (2 * k) * (2 * k)

            @pl.when(n % (2 * k) >= k)
            def _(start=start, k=k):
                body(start, k)

            k //= 2

    @pl.when(i > 0)
    def _():
        prev = jnp.maximum(i - 1, 0)
        for_chunks(in_ref[prev], lambda c: y_copy(ir_ref[prev], c, 1 - slot).wait())

    @pl.when(i == 0)
    def _():
        for_chunks(nch, lambda c: x_copy(row0, c, slot).start())
        for cp in w_copies(expert, 0, 0):
            cp.start()

    @pl.when(i + 1 < n_items)
    def _():
        nxt = jnp.minimum(i + 1, n_items - 1)
        for_chunks(in_ref[nxt], lambda c: x_copy(ir_ref[nxt], c, 1 - slot).start())

    for_chunks(nch, lambda c: x_copy(row0, c, slot).wait())

    @pl.when(nch > 0)
    def _():
        def cast_weights(ws):
            w = (wgf_scr[ws].astype(BF16), wuf_scr[ws].astype(BF16), wdf_scr[ws].astype(BF16))
            wg_scr[...], wu_scr[...], wd_scr[...] = w
            return w

        def contribution(f, c, k, w):
            wg, wu, wd = (wg_scr[...], wu_scr[...], wd_scr[...]) if w is None else w
            bias_row = pl.ds(expert * N_FF_TILES + f, 1)
            lo, hi = _unpack_halves(_tok_load(xin_scr.at[slot], c * MOE_CHUNK, k * MOE_CHUNK))
            x = jnp.concatenate([lo.astype(BF16), hi.astype(BF16)], axis=1)
            gt = jnp.minimum(_dot(x, wg) + bg_ref[bias_row, :], SWIGLU_LIMIT)
            up = jnp.clip(_dot(x, wu) + bu_ref[bias_row, :], -SWIGLU_LIMIT, SWIGLU_LIMIT)
            act = (up + 1.0) * (gt * _sigmoid(SWIGLU_ALPHA * gt))
            return _dot(act.astype(BF16), wd)

        def first(f, c, k, w=None):
            acc_scr[rows_of(c, k), :] = contribution(f, c, k, w) + bd_ref[down_row, :]

        def middle(f, c, k, w=None):
            acc_scr[rows_of(c, k), :] += contribution(f, c, k, w)

        def last(f, c, k, w=None):
            y = acc_scr[rows_of(c, k), :] + contribution(f, c, k, w)
            _tok_store(xin_scr.at[slot], c * MOE_CHUNK, k * MOE_CHUNK, _pack_halves(y))
            for u in range(k):
                y_copy(row0, c + u, slot).start()

        def ff_step(f, phase, request_next):
            ws = f % 2
            for cp in w_copies(expert, f, ws):
                cp.wait()
            request_next()
            body = functools.partial(phase, f)

            @pl.when(nch >= MOE_SPAN)
            def _():
                body(0, MOE_SPAN, cast_weights(ws))

            @pl.when(nch < MOE_SPAN)
            def _():
                cast_weights(ws)

            for_spans(nch, body, 1)

        def request_tile(f):
            def go():
                for cp in w_copies(expert, f, f % 2):
                    cp.start()
            return go

        def request_next_item():
            nxt = jnp.minimum(i + 1, n_items - 1)

            @pl.when(jnp.logical_and(i + 1 < n_items, in_ref[nxt] > 0))
            def _():
                for cp in w_copies(ie_ref[nxt], 0, 0):
                    cp.start()

        ff_step(0, first, request_tile(1))

        def mid(f, carry):
            ff_step(f, middle, request_tile(f + 1))
            return carry

        lax.fori_loop(1, N_FF_TILES - 1, mid, 0)
        ff_step(N_FF_TILES - 1, last, request_next_item)

        @pl.when(i == n_items - 1)
        def _():
            for_chunks(nch, lambda c: y_copy(row0, c, slot).wait())


def _moe_experts(item_e, item_row0, item_nch, xs, w_g, b_g, w_u, b_u, w_d, b_d):
    n_items = item_e.shape[0]
    whole = lambda a: pl.BlockSpec(a.shape, lambda i, ie, ir, inch: (0, 0))
    return pl.pallas_call(
        _moe_kernel,
        out_shape=jax.ShapeDtypeStruct(xs.shape, I32),
        grid_spec=pltpu.PrefetchScalarGridSpec(
            num_scalar_prefetch=3,
            grid=(n_items,),
            in_specs=[
                pl.BlockSpec(memory_space=pl.ANY),
                pl.BlockSpec(memory_space=pl.ANY), whole(b_g),
                pl.BlockSpec(memory_space=pl.ANY), whole(b_u),
                pl.BlockSpec(memory_space=pl.ANY), whole(b_d),
            ],
            out_specs=pl.BlockSpec(memory_space=pl.ANY),
            scratch_shapes=[
                pltpu.VMEM((2, MOE_ITEM_ROWS * TOK_SUB, LANES), I32),
                pltpu.VMEM((MOE_ITEM_ROWS, D_MODEL), F32),
                pltpu.VMEM((2, D_MODEL, FF_TILE), F32),
                pltpu.VMEM((2, D_MODEL, FF_TILE), F32),
                pltpu.VMEM((2, FF_TILE, D_MODEL), F32),
                pltpu.VMEM((D_MODEL, FF_TILE), BF16),
                pltpu.VMEM((D_MODEL, FF_TILE), BF16),
                pltpu.VMEM((FF_TILE, D_MODEL), BF16),
                pltpu.SemaphoreType.DMA((2,)),
                pltpu.SemaphoreType.DMA,
                pltpu.SemaphoreType.DMA((2,)),
            ]),
        input_output_aliases={3: 0},
        compiler_params=_cparams(("arbitrary",)),
        name="moe_experts",
    )(item_e, item_row0, item_nch, xs, w_g, b_g, w_u, b_u, w_d, b_d)


COMBINE_ROWS = 256


def _combine_kernel(dest_ref, ys_ref, x1_ref, tw_ref, mods_ref, o_ref, ybuf, sem):
    step = pl.program_id(0)
    slot = step % 2

    def gather(blk, s):
        base = blk * (COMBINE_ROWS * TOP_K)

        def issue(t, carry):
            for k in range(TOP_K):
                pltpu.make_async_copy(ys_ref.at[_tok_rows(dest_ref[base + t * TOP_K + k], 1), :],
                                      ybuf.at[s, k, _tok_rows(t, 1), :], sem.at[s]).start(priority=k % 2)
            return carry

        lax.fori_loop(0, COMBINE_ROWS, issue, 0, unroll=DMA_UNROLL // TOP_K)

    @pl.when(step == 0)
    def _():
        gather(step, slot)

    @pl.when(step + 1 < pl.num_programs(0))
    def _():
        gather(step + 1, 1 - slot)

    for k in range(TOP_K):
        pltpu.make_async_copy(ys_ref.at[_tok_rows(0, COMBINE_ROWS), :], ybuf.at[slot, k], sem.at[slot]).wait()

    y_lo = y_hi = None
    for k in range(TOP_K):
        lo, hi = _unpack_halves(_tok_load(ybuf.at[slot, k], 0, COMBINE_ROWS))
        w = tw_ref[:, k:k + 1]
        y_lo = w * lo if y_lo is None else y_lo + w * lo
        y_hi = w * hi if y_hi is None else y_hi + w * hi
    o_ref[:, :HALF] = x1_ref[:, :HALF] + mods_ref[5:6, :HALF] * y_lo
    o_ref[:, HALF:] = x1_ref[:, HALF:] + mods_ref[5:6, HALF:] * y_hi


def _combine(dest, ys, x1, tw, mods, mod_row):
    t = x1.shape[0]
    return pl.pallas_call(
        _combine_kernel,
        out_shape=jax.ShapeDtypeStruct((t, D_MODEL), F32),
        grid_spec=pltpu.PrefetchScalarGridSpec(
            num_scalar_prefetch=1,
            grid=(t // COMBINE_ROWS,),
            in_specs=[pl.BlockSpec(memory_space=pl.ANY),
                      pl.BlockSpec((COMBINE_ROWS, D_MODEL), lambda i, d: (i, 0)),
                      pl.BlockSpec((COMBINE_ROWS, LANES), lambda i, d: (i, 0)),
                      pl.BlockSpec((None, 6, D_MODEL), lambda i, d: (mod_row(i * COMBINE_ROWS), 0, 0))],
            out_specs=pl.BlockSpec((COMBINE_ROWS, D_MODEL), lambda i, d: (i, 0)),
            scratch_shapes=[pltpu.VMEM((2, TOP_K, COMBINE_ROWS * TOK_SUB, LANES), I32),
                            pltpu.SemaphoreType.DMA((2,))]),
        compiler_params=_cparams(("arbitrary",)),
        name="moe_combine",
    )(dest, ys, x1, tw, mods)


def _routing_tables(top_idx, rank, counts, n_items, cap):
    nch = (counts + MOE_CHUNK - 1) // MOE_CHUNK
    pad_rows = nch * MOE_CHUNK
    pad_end = jnp.cumsum(pad_rows)
    pad_start = pad_end - pad_rows
    dest = (pad_start[top_idx] + rank).reshape(-1).astype(I32)
    items_e = (nch + MOE_ITEM_CHUNKS - 1) // MOE_ITEM_CHUNKS
    item_end = jnp.cumsum(items_e)
    item_start = item_end - items_e
    total = item_end[-1]
    i = jnp.arange(n_items, dtype=I32)
    ii = jnp.minimum(i, total - 1)
    e_i = jnp.minimum(jnp.searchsorted(item_end, ii, side="right"), N_EXPERTS - 1).astype(I32)
    local = ii - item_start[e_i]
    row0 = (pad_start[e_i] + local * MOE_ITEM_ROWS).astype(I32)
    n_i = jnp.where(i < total, jnp.minimum(MOE_ITEM_CHUNKS, nch[e_i] - local * MOE_ITEM_CHUNKS), 0).astype(I32)
    fill = jnp.concatenate([pad_start + counts, pad_end[-1:], pad_rows - counts, (cap - pad_end[-1:]) // MOE_CHUNK]).astype(I32)
    return dest, fill, e_i, row0, n_i


def _rope_tables(n_lat):
    nf = ROPE // 4
    inv = ROPE_BASE ** (-jnp.arange(nf, dtype=F32) / nf)
    t = jnp.arange(n_lat)
    row = (t // GRID_W).astype(F32)
    col = (t % GRID_W).astype(F32)
    ang_r = row[:, None] * inv[None, :]
    ang_c = col[:, None] * inv[None, :]
    z = jnp.zeros((n_lat, nf), F32)
    tail = jnp.zeros((n_lat, LANES - ROPE), F32)
    cos = jnp.concatenate([jnp.cos(ang_r), jnp.cos(ang_r), jnp.cos(ang_c), jnp.cos(ang_c), tail], axis=1)
    sin_lo = jnp.concatenate([-jnp.sin(ang_r), z, -jnp.sin(ang_c), z, tail], axis=1)
    sin_hi = jnp.concatenate([z, jnp.sin(ang_r), z, jnp.sin(ang_c), tail], axis=1)
    return cos, sin_lo, sin_hi


def kernel(x_prompt, x_sample, cache_ckv, cache_krope, c, c_ctx, norm1_g, norm2_g, w_ada, b_ada, w_in,
           b_branch_gate, kv_norm_g, w_ukv, q_norm_g, k_norm_g, w_o_attn, w_pool, pool_scale, w_o_pool, w_out,
           router_w, router_b, w_exp_gate, b_exp_gate, w_exp_up, b_exp_up, w_exp_down, b_exp_down):
    assert w_in.shape[0] == 1, "single-layer trunk"
    batch, seq, _ = x_prompt.shape
    dec_batch, n_lat, _ = x_sample.shape
    past = cache_ckv.shape[2]
    n_ctx = batch * seq
    n_dec = dec_batch * n_lat

    cond = jnp.concatenate([c_ctx[None, :], c, jnp.zeros((8 - 1 - dec_batch, D_MODEL), F32)], axis=0)
    mods = _adaln(cond, w_ada[0], b_ada).reshape(8, 6, D_MODEL)

    w_cat = _w_in_layout(w_in[0].T)
    qg = jnp.pad(q_norm_g * ATTN_SCALE, ((0, 0), (0, HEAD_PAD - QK)))
    kgn = k_norm_g[:, :NOPE]
    kgr = jnp.pad(k_norm_g[:, NOPE:], ((0, 0), (0, LANES - ROPE)))
    w_ukv_b = w_ukv[0].astype(BF16)
    rope_tabs = _rope_tables(n_lat)

    ctx_row = lambda i: 0
    tm_in = 512
    lat_row_in = lambda i: 1 + (i * tm_in) // n_lat
    q_c, pool_c, gates_c, ckv_c, kr_c, krp_c = _in_proj(
        x_prompt.reshape(n_ctx, D_MODEL), mods, ctx_row, norm1_g, w_cat, qg, kv_norm_g, None, tm_in)
    q_l, pool_l, gates_l, ckv_l, _, krp_l = _in_proj(
        x_sample.reshape(n_dec, D_MODEL), mods, lat_row_in, norm1_g, w_cat, qg, kv_norm_g, rope_tabs, tm_in)

    tr = 512
    k_c, v_c = _kv_expand(ckv_c, krp_c, w_ukv_b, kgn, kgr, None, tr)
    k_l, v_l = _kv_expand(ckv_l, krp_l, w_ukv_b, kgn, kgr, rope_tabs, tr)
    cache_krp = jnp.pad(cache_krope.reshape(dec_batch * past, ROPE), ((0, 0), (0, LANES - ROPE)))
    k_p, v_p = _kv_expand(cache_ckv.reshape(dec_batch * past, KV_RANK), cache_krp, w_ukv_b, kgn, kgr, None, tr)

    attn_c = _attn_ctx(q_c, k_c, v_c, seq)
    attn_l = _attn_lat(q_l, k_l, v_l, k_p, v_p, n_lat, past, 512)

    w_pool_b = w_pool[0].astype(BF16)
    poolo_c = _pool(pool_c, w_pool_b, pool_scale, seq)
    poolo_l = _pool(pool_l, w_pool_b, pool_scale, n_lat)

    woa = w_o_attn[0].astype(BF16)
    wop = w_o_pool[0].astype(BF16)
    wout = w_out[0].astype(BF16)
    rw = jnp.pad(router_w[0], ((0, 0), (0, LANES - N_EXPERTS))).astype(BF16)
    rb = jnp.pad(router_b, ((0, 0), (0, LANES - N_EXPERTS)), constant_values=NEG_BIG)
    tm_mg = 256
    lat_row_mg = lambda i: 1 + (i * tm_mg) // n_lat
    x1_c, h2_c, tidx_c, tw_c, cnt_c = _merge(
        attn_c, poolo_c, gates_c, x_prompt.reshape(n_ctx, D_MODEL), mods, ctx_row,
        b_branch_gate, woa, wop, wout, norm2_g, rw, rb, jnp.zeros((1, LANES), F32), tm_mg)
    x1_l, h2_l, tidx_l, tw_l, cnt_all = _merge(
        attn_l, poolo_l, gates_l, x_sample.reshape(n_dec, D_MODEL), mods, lat_row_mg,
        b_branch_gate, woa, wop, wout, norm2_g, rw, rb, cnt_c, tm_mg)

    n_assign = (n_ctx + n_dec) * TOP_K
    max_chunks = (n_assign + N_EXPERTS * (MOE_CHUNK - 1)) // MOE_CHUNK
    cap = max_chunks * MOE_CHUNK
    n_items = (max_chunks + N_EXPERTS * (MOE_ITEM_CHUNKS - 1)) // MOE_ITEM_CHUNKS
    routed = jnp.concatenate([tidx_c[:, :2 * TOP_K], tidx_l[:, :2 * TOP_K]], axis=0)
    dest, fill, item_e, item_row0, item_nch = _routing_tables(
        routed[:, :TOP_K], routed[:, TOP_K:], cnt_all[0, :N_EXPERTS].astype(I32), n_items, cap)

    xs = _dispatch(dest, fill, h2_c, h2_l, cap)
    ys = _moe_experts(item_e, item_row0, item_nch, xs,
                      w_exp_gate[0], b_exp_gate[0].reshape(N_EXPERTS * N_FF_TILES, FF_TILE),
                      w_exp_up[0], b_exp_up[0].reshape(N_EXPERTS * N_FF_TILES, FF_TILE),
                      w_exp_down[0], b_exp_down[0])

    y_c = _combine(dest[:n_ctx * TOP_K], ys, x1_c, tw_c, mods, lambda r: 0)
    y_l = _combine(dest[n_ctx * TOP_K:], ys, x1_l, tw_l, mods, lambda r: 1 + r // n_lat)

    return (y_c.reshape(batch, seq, D_MODEL),
            y_l.reshape(dec_batch, n_lat, D_MODEL),
            ckv_c.reshape(batch, 1, seq, KV_RANK),
            kr_c.reshape(batch, 1, seq, ROPE))
```

```python
import functools

import jax
import jax.numpy as jnp
from jax import lax
from jax.experimental import pallas as pl
from jax.experimental.pallas import tpu as pltpu

F32 = jnp.float32
BF16 = jnp.bfloat16
I32 = jnp.int32

D_MODEL = 2048
N_HEADS = 16
NOPE = 128
ROPE = 64
QK = NOPE + ROPE
V_DIM = 128
KV_RANK = 512
POOL_WINDOWS = (2, 4, 8, 16)
POOL_WIDTH = 1024
POOL_GROUP = POOL_WIDTH // len(POOL_WINDOWS)
N_EXPERTS = 32
TOP_K = 4
D_FF = 2048
SWIGLU_LIMIT = 7.0
SWIGLU_ALPHA = 1.702
ROPE_BASE = 10000.0
RMS_EPS = 1e-6
GRID_W = 64
ATTN_SCALE = QK ** -0.5

LANES = 128
HEAD_PAD = 2 * LANES
Q_COLS = N_HEADS * HEAD_PAD
IN_TILE = 1024
N_Q_TILES = Q_COLS // IN_TILE
POOL_TILE = N_Q_TILES
GATE_TILE0 = POOL_TILE + 1
N_GATE_TILES = 2 * D_MODEL // IN_TILE
KV_TILE = GATE_TILE0 + N_GATE_TILES
N_IN_TILES = KV_TILE + 1
HEADS_PER_TILE = IN_TILE // HEAD_PAD

MOE_CHUNK = 128
MOE_ITEM_CHUNKS = 16
MOE_ITEM_ROWS = MOE_CHUNK * MOE_ITEM_CHUNKS
FF_TILE = 256
N_FF_TILES = D_FF // FF_TILE
NEG_BIG = -1e30

VMEM_LIMIT = 56 * 1024 * 1024


def _cparams(sem, vmem=VMEM_LIMIT):
    return pltpu.CompilerParams(dimension_semantics=sem, vmem_limit_bytes=vmem)


def _dot(a, b):
    return jnp.dot(a, b, preferred_element_type=F32)


def _sigmoid(z):
    return 1.0 / (1.0 + jnp.exp(-z))


HALF = D_MODEL // 2
HI_MASK = -65536


def _pack_halves(v):
    lo = lax.bitcast_convert_type(v[:, :HALF].astype(BF16).astype(F32), I32)
    hi = lax.bitcast_convert_type(v[:, HALF:].astype(BF16).astype(F32), I32)
    return jnp.bitwise_or(jnp.bitwise_and(hi, HI_MASK), lax.shift_right_logical(lo, 16))


TOK_SUB = HALF // LANES


def _tok_rows(tok0, n):
    row0 = tok0 * TOK_SUB
    return pl.ds(row0 if isinstance(row0, int) else pl.multiple_of(row0, TOK_SUB), n * TOK_SUB)


def _tok_store(ref, tok0, n, words):
    for s in range(TOK_SUB):
        ref[pl.ds(tok0 * TOK_SUB + s, n, stride=TOK_SUB), :] = words[:, s * LANES:(s + 1) * LANES]


def _tok_load(ref, tok0, n):
    return jnp.concatenate([ref[pl.ds(tok0 * TOK_SUB + s, n, stride=TOK_SUB), :] for s in range(TOK_SUB)], axis=1)


def _unpack_halves(w):
    lo = lax.bitcast_convert_type(lax.shift_left(w, 16), F32)
    hi = lax.bitcast_convert_type(jnp.bitwise_and(w, HI_MASK), F32)
    return lo, hi


def _rope_lanes(y, cos, sin_lo, sin_hi):
    return y * cos + pltpu.roll(y, LANES - 16, axis=1) * sin_lo + pltpu.roll(y, 16, axis=1) * sin_hi


def _adaln_kernel(c_ref, w_ref, b_ref, o_ref):
    c = c_ref[...]
    s = (c * _sigmoid(c)).astype(BF16)
    o_ref[...] = _dot(s, w_ref[...].astype(BF16)) + b_ref[...]


def _adaln(cond, w_ada, b_ada):
    rows = cond.shape[0]
    n = w_ada.shape[1]
    tn = 1024
    return pl.pallas_call(
        _adaln_kernel,
        out_shape=jax.ShapeDtypeStruct((rows, n), F32),
        grid=(n // tn,),
        in_specs=[pl.BlockSpec((rows, D_MODEL), lambda j: (0, 0)),
                  pl.BlockSpec((D_MODEL, tn), lambda j: (0, j)),
                  pl.BlockSpec((1, tn), lambda j: (0, j))],
        out_specs=pl.BlockSpec((rows, tn), lambda j: (0, j)),
        compiler_params=_cparams(("arbitrary",)),
        name="adaln",
    )(cond, w_ada, b_ada)


W_Q_END = N_HEADS * QK
W_CKV_END = W_Q_END + KV_RANK
W_KR_END = W_CKV_END + ROPE
N_IN_COLS = W_KR_END + POOL_WIDTH + 2 * D_MODEL
W_PREP = 256


def _w_in_layout_kernel(wt_ref, o_ref):
    lane = lax.broadcasted_iota(I32, (W_PREP, W_PREP), 1)

    def move(src_row, dst_col, keep):
        sq = wt_ref[src_row:src_row + W_PREP, :].T
        if keep < W_PREP:
            sq = jnp.where(lane < keep, sq, 0.0)
        o_ref[:, dst_col:dst_col + W_PREP] = sq.astype(BF16)

    for h in range(N_HEADS):
        move(h * QK, h * HEAD_PAD, QK)
    for p in range((N_IN_COLS - W_KR_END) // W_PREP):
        move(W_KR_END + p * W_PREP, Q_COLS + p * W_PREP, W_PREP)
    kv0 = Q_COLS + N_IN_COLS - W_KR_END
    kv_cols = KV_RANK + ROPE
    for p in range(IN_TILE // W_PREP):
        keep = min(max(kv_cols - p * W_PREP, 0), W_PREP)
        if keep:
            move(W_Q_END + p * W_PREP, kv0 + p * W_PREP, keep)
        else:
            o_ref[:, kv0 + p * W_PREP:kv0 + (p + 1) * W_PREP] = jnp.zeros((W_PREP, W_PREP), BF16)


def _w_in_layout(wt):
    return pl.pallas_call(
        _w_in_layout_kernel,
        out_shape=jax.ShapeDtypeStruct((D_MODEL, N_IN_TILES * IN_TILE), BF16),
        grid=(D_MODEL // W_PREP,),
        in_specs=[pl.BlockSpec((N_IN_COLS, W_PREP), lambda i: (0, i))],
        out_specs=pl.BlockSpec((W_PREP, N_IN_TILES * IN_TILE), lambda i: (i, 0)),
        compiler_params=_cparams(("arbitrary",)),
        name="w_in_layout",
    )(wt)


def _in_proj_kernel(rope, x_ref, mods_ref, g1_ref, w_ref, qg_ref, kvg_ref, *rest):
    if rope:
        cos_ref, slo_ref, shi_ref = rest[:3]
        rest = rest[3:]
    q_ref, pool_ref, gates_ref, ckv_ref, kr_ref, krp_ref, h_scr = rest
    j = pl.program_id(1)

    def proj(c0, c1, h=None):
        return _dot(h_scr[...] if h is None else h, w_ref[:, c0:c1])

    def q_heads(h=None):
        for hh in range(HEADS_PER_TILE):
            a = proj(hh * HEAD_PAD, (hh + 1) * HEAD_PAD, h)
            r = lax.rsqrt(jnp.sum(a * a, axis=-1, keepdims=True) / QK + RMS_EPS)
            y = a * r * qg_ref[...]
            if rope:
                yr = _rope_lanes(y[:, LANES:], cos_ref[...], slo_ref[...], shi_ref[...])
                q_ref[:, hh * HEAD_PAD:hh * HEAD_PAD + LANES] = y[:, :LANES].astype(BF16)
                q_ref[:, hh * HEAD_PAD + LANES:(hh + 1) * HEAD_PAD] = yr.astype(BF16)
            else:
                q_ref[:, hh * HEAD_PAD:(hh + 1) * HEAD_PAD] = y.astype(BF16)

    @pl.when(j == 0)
    def _():
        x = x_ref[...]
        y = x * lax.rsqrt(jnp.mean(x * x, axis=-1, keepdims=True) + RMS_EPS) * g1_ref[...]
        h = (y * (1.0 + mods_ref[1:2, :]) + mods_ref[0:1, :]).astype(BF16)
        h_scr[...] = h
        q_heads(h)

    @pl.when(jnp.logical_and(j > 0, j < N_Q_TILES))
    def _():
        q_heads()

    @pl.when(j == POOL_TILE)
    def _():
        pool_ref[...] = proj(0, IN_TILE)

    @pl.when(jnp.logical_and(j >= GATE_TILE0, j < KV_TILE))
    def _():
        gates_ref[...] = proj(0, IN_TILE)

    @pl.when(j == KV_TILE)
    def _():
        a = proj(0, KV_RANK)
        r = lax.rsqrt(jnp.mean(a * a, axis=-1, keepdims=True) + RMS_EPS)
        ckv_ref[...] = a * r * kvg_ref[...]
        krp = proj(KV_RANK, KV_RANK + LANES)
        krp_ref[...] = krp
        kr_ref[...] = krp[:, :ROPE]


def _in_proj(x, mods, mod_row, g1, w_cat, qg, kvg, rope_tabs, tm):
    t = x.shape[0]
    rope = rope_tabs is not None
    in_specs = [
        pl.BlockSpec((tm, D_MODEL), lambda i, j: (i, 0)),
        pl.BlockSpec((None, 6, D_MODEL), lambda i, j: (mod_row(i), 0, 0)),
        pl.BlockSpec((1, D_MODEL), lambda i, j: (0, 0)),
        pl.BlockSpec((D_MODEL, IN_TILE), lambda i, j: (0, j)),
        pl.BlockSpec((1, HEAD_PAD), lambda i, j: (0, 0)),
        pl.BlockSpec((1, KV_RANK), lambda i, j: (0, 0)),
    ]
    args = [x, mods, g1, w_cat, qg, kvg]
    if rope:
        seq_tiles = rope_tabs[0].shape[0] // tm
        in_specs += [pl.BlockSpec((tm, LANES), lambda i, j: (i % seq_tiles, 0))] * 3
        args += list(rope_tabs)
    out_shape = (
        jax.ShapeDtypeStruct((t, Q_COLS), BF16),
        jax.ShapeDtypeStruct((t, POOL_WIDTH), F32),
        jax.ShapeDtypeStruct((t, 2 * D_MODEL), F32),
        jax.ShapeDtypeStruct((t, KV_RANK), F32),
        jax.ShapeDtypeStruct((t, ROPE), F32),
        jax.ShapeDtypeStruct((t, LANES), F32),
    )
    out_specs = (
        pl.BlockSpec((tm, IN_TILE), lambda i, j: (i, jnp.minimum(j, N_Q_TILES - 1))),
        pl.BlockSpec((tm, POOL_WIDTH), lambda i, j: (i, 0)),
        pl.BlockSpec((tm, IN_TILE), lambda i, j: (i, jnp.clip(j - GATE_TILE0, 0, N_GATE_TILES - 1))),
        pl.BlockSpec((tm, KV_RANK), lambda i, j: (i, 0)),
        pl.BlockSpec((tm, ROPE), lambda i, j: (i, 0)),
        pl.BlockSpec((tm, LANES), lambda i, j: (i, 0)),
    )
    return pl.pallas_call(
        functools.partial(_in_proj_kernel, rope),
        out_shape=out_shape,
        grid=(t // tm, N_IN_TILES),
        in_specs=in_specs,
        out_specs=out_specs,
        scratch_shapes=[pltpu.VMEM((tm, D_MODEL), BF16)],
        compiler_params=_cparams(("arbitrary", "arbitrary")),
        name="in_proj_rope" if rope else "in_proj",
    )(*args)


def _kv_expand_kernel(rope, ckv_ref, krp_ref, w_ref, kgn_ref, kgr_ref, *rest):
    if rope:
        cos_ref, slo_ref, shi_ref = rest[:3]
        rest = rest[3:]
    k_ref, v_ref = rest
    kv = _dot(ckv_ref[...].astype(BF16), w_ref[...])
    kr = krp_ref[...]
    ssq_r = jnp.sum(kr * kr, axis=-1, keepdims=True)
    krg = kr * kgr_ref[...]
    if rope:
        krg = _rope_lanes(krg, cos_ref[...], slo_ref[...], shi_ref[...])
    for h in range(N_HEADS):
        kn = kv[:, h * HEAD_PAD:h * HEAD_PAD + NOPE]
        r = lax.rsqrt((jnp.sum(kn * kn, axis=-1, keepdims=True) + ssq_r) / QK + RMS_EPS)
        k_ref[:, h * HEAD_PAD:h * HEAD_PAD + NOPE] = (kn * r * kgn_ref[...]).astype(BF16)
        k_ref[:, h * HEAD_PAD + NOPE:(h + 1) * HEAD_PAD] = (krg * r).astype(BF16)
        v_ref[:, h * V_DIM:(h + 1) * V_DIM] = kv[:, h * HEAD_PAD + NOPE:(h + 1) * HEAD_PAD].astype(BF16)


def _kv_expand(ckv_n, krp, w_ukv, kgn, kgr, rope_tabs, tr):
    r = ckv_n.shape[0]
    rope = rope_tabs is not None
    in_specs = [
        pl.BlockSpec((tr, KV_RANK), lambda i: (i, 0)),
        pl.BlockSpec((tr, LANES), lambda i: (i, 0)),
        pl.BlockSpec((KV_RANK, N_HEADS * HEAD_PAD), lambda i: (0, 0)),
        pl.BlockSpec((1, LANES), lambda i: (0, 0)),
        pl.BlockSpec((1, LANES), lambda i: (0, 0)),
    ]
    args = [ckv_n, krp, w_ukv, kgn, kgr]
    if rope:
        seq_tiles = rope_tabs[0].shape[0] // tr
        in_specs += [pl.BlockSpec((tr, LANES), lambda i: (i % seq_tiles, 0))] * 3
        args += list(rope_tabs)
    return pl.pallas_call(
        functools.partial(_kv_expand_kernel, rope),
        out_shape=(jax.ShapeDtypeStruct((r, N_HEADS * HEAD_PAD), BF16),
                   jax.ShapeDtypeStruct((r, N_HEADS * V_DIM), BF16)),
        grid=(r // tr,),
        in_specs=in_specs,
        out_specs=(pl.BlockSpec((tr, N_HEADS * HEAD_PAD), lambda i: (i, 0)),
                   pl.BlockSpec((tr, N_HEADS * V_DIM), lambda i: (i, 0))),
        compiler_params=_cparams(("arbitrary",)),
        name="kv_expand_rope" if rope else "kv_expand",
    )(*args)


def _qk(q, k):
    return lax.dot_general(q, k, (((1,), (1,)), ((), ())), preferred_element_type=F32)


def _attn_ctx_kernel(q_ref, k_ref, v_ref, o_ref):
    for h in range(N_HEADS):
        s = _qk(q_ref[:, h * HEAD_PAD:(h + 1) * HEAD_PAD], k_ref[:, h * HEAD_PAD:(h + 1) * HEAD_PAD])
        p = jnp.exp(s - jnp.max(s, axis=-1, keepdims=True))
        l = jnp.sum(p, axis=-1, keepdims=True)
        o = _dot(p.astype(BF16), v_ref[:, h * V_DIM:(h + 1) * V_DIM])
        o_ref[:, h * V_DIM:(h + 1) * V_DIM] = (o / l).astype(BF16)


def _attn_ctx(q, k, v, seq):
    t = q.shape[0]
    return pl.pallas_call(
        _attn_ctx_kernel,
        out_shape=jax.ShapeDtypeStruct((t, N_HEADS * V_DIM), BF16),
        grid=(t // seq,),
        in_specs=[pl.BlockSpec((seq, Q_COLS), lambda b: (b, 0)),
                  pl.BlockSpec((seq, Q_COLS), lambda b: (b, 0)),
                  pl.BlockSpec((seq, N_HEADS * V_DIM), lambda b: (b, 0))],
        out_specs=pl.BlockSpec((seq, N_HEADS * V_DIM), lambda b: (b, 0)),
        compiler_params=_cparams(("arbitrary",)),
        name="attn_ctx",
    )(q, k, v)


LAT_HEADS = 4


def _attn_lat_kernel(q_ref, k_ref, v_ref, kc_ref, vc_ref, o_ref):
    for hh in range(LAT_HEADS):
        qk_cols = slice(hh * HEAD_PAD, (hh + 1) * HEAD_PAD)
        v_cols = slice(hh * V_DIM, (hh + 1) * V_DIM)
        q = q_ref[:, qk_cols]
        s1 = _qk(q, k_ref[:, qk_cols])
        s2 = _qk(q, kc_ref[:, qk_cols])
        m = jnp.maximum(jnp.max(s1, axis=-1, keepdims=True), jnp.max(s2, axis=-1, keepdims=True))
        p1 = jnp.exp(s1 - m)
        p2 = jnp.exp(s2 - m)
        l = jnp.sum(p1, axis=-1, keepdims=True) + jnp.sum(p2, axis=-1, keepdims=True)
        o = _dot(p1.astype(BF16), v_ref[:, v_cols]) + _dot(p2.astype(BF16), vc_ref[:, v_cols])
        o_ref[:, v_cols] = (o / l).astype(BF16)


def _attn_lat(q, k, v, kc, vc, seq, past, tq):
    t = q.shape[0]
    nq = seq // tq
    qk_w = LAT_HEADS * HEAD_PAD
    v_w = LAT_HEADS * V_DIM
    return pl.pallas_call(
        _attn_lat_kernel,
        out_shape=jax.ShapeDtypeStruct((t, N_HEADS * V_DIM), BF16),
        grid=(t // seq, N_HEADS // LAT_HEADS, nq),
        in_specs=[pl.BlockSpec((tq, qk_w), lambda b, h, i: (b * nq + i, h)),
                  pl.BlockSpec((seq, qk_w), lambda b, h, i: (b, h)),
                  pl.BlockSpec((seq, v_w), lambda b, h, i: (b, h)),
                  pl.BlockSpec((past, qk_w), lambda b, h, i: (b, h)),
                  pl.BlockSpec((past, v_w), lambda b, h, i: (b, h))],
        out_specs=pl.BlockSpec((tq, v_w), lambda b, h, i: (b * nq + i, h)),
        compiler_params=_cparams(("arbitrary", "arbitrary", "arbitrary")),
        name="attn_lat",
    )(q, k, v, kc, vc)


POOL_HALO = 8


def _pool_kernel(seq, u_ref, w_ref, sc_ref, o_ref, pad_scr):
    zeros = jnp.zeros((POOL_HALO, POOL_WIDTH), F32)
    pad_scr[0:POOL_HALO, :] = zeros
    pad_scr[POOL_HALO + seq:2 * POOL_HALO + seq, :] = zeros
    pad_scr[POOL_HALO:POOL_HALO + seq, :] = u_ref[...]
    t = lax.broadcasted_iota(I32, (seq, 1), 0)
    for g, w in enumerate(POOL_WINDOWS):
        cols = slice(g * POOL_GROUP, (g + 1) * POOL_GROUP)
        tot = None
        for d in range(-(w // 2), w - w // 2):
            piece = pad_scr[POOL_HALO + d:POOL_HALO + d + seq, cols]
            tot = piece if tot is None else tot + piece
        cnt = (jnp.minimum(t + (w - w // 2), seq) - jnp.maximum(t - w // 2, 0)).astype(F32)
        mixed = tot / cnt - u_ref[:, cols]
        o_ref[:, cols] = (_dot(mixed.astype(BF16), w_ref[g]) * sc_ref[:, cols]).astype(BF16)


def _pool(u, w_pool, pool_scale, seq):
    t = u.shape[0]
    n_groups = len(POOL_WINDOWS)
    return pl.pallas_call(
        functools.partial(_pool_kernel, seq),
        out_shape=jax.ShapeDtypeStruct((t, POOL_WIDTH), BF16),
        grid=(t // seq,),
        in_specs=[pl.BlockSpec((seq, POOL_WIDTH), lambda b: (b, 0)),
                  pl.BlockSpec((n_groups, POOL_GROUP, POOL_GROUP), lambda b: (0, 0, 0)),
                  pl.BlockSpec((1, POOL_WIDTH), lambda b: (0, 0))],
        out_specs=pl.BlockSpec((seq, POOL_WIDTH), lambda b: (b, 0)),
        scratch_shapes=[pltpu.VMEM((seq + 2 * POOL_HALO, POOL_WIDTH), F32)],
        compiler_params=_cparams(("arbitrary",)),
        name="pool",
    )(u, w_pool, pool_scale)


def _merge_kernel(attn_ref, pool_ref, gates_ref, x_ref, mods_ref, bbg_ref, woa_ref, wop_ref, wout_ref,
                  g2_ref, rw_ref, rb_ref, cnt_in_ref, x1_ref, h2_ref, tidx_ref, tw_ref, cnt_out_ref, cnt_scr):
    @pl.when(pl.program_id(0) == 0)
    def _():
        cnt_scr[...] = cnt_in_ref[...]

    a = _dot(attn_ref[...], woa_ref[...])
    p = _dot(pool_ref[...], wop_ref[...])
    ga = _sigmoid(gates_ref[:, :D_MODEL] + bbg_ref[:, :D_MODEL])
    gp = _sigmoid(gates_ref[:, D_MODEL:] + bbg_ref[:, D_MODEL:])
    merged = (ga * a + gp * p).astype(BF16)
    x1 = x_ref[...] + mods_ref[2:3, :] * _dot(merged, wout_ref[...])
    x1_ref[...] = x1
    y = x1 * lax.rsqrt(jnp.mean(x1 * x1, axis=-1, keepdims=True) + RMS_EPS) * g2_ref[...]
    h2 = y * (1.0 + mods_ref[4:5, :]) + mods_ref[3:4, :]
    _tok_store(h2_ref, 0, h2.shape[0], _pack_halves(h2))
    logits = _dot(h2.astype(BF16), rw_ref[...]) + rb_ref[...]
    lane = lax.broadcasted_iota(I32, logits.shape, 1).astype(F32)
    vals, idxs = [], []
    for _ in range(TOP_K):
        m = jnp.max(logits, axis=-1, keepdims=True)
        ix = jnp.min(jnp.where(logits == m, lane, float(LANES)), axis=-1, keepdims=True)
        vals.append(m)
        idxs.append(ix)
        logits = jnp.where(lane == ix, -jnp.inf, logits)
    es = [jnp.exp(v - vals[0]) for v in vals]
    tot = es[0] + es[1] + es[2] + es[3]
    tidx = jnp.zeros(logits.shape, F32)
    tw = jnp.zeros(logits.shape, F32)
    for k in range(TOP_K):
        tidx = jnp.where(lane == k, idxs[k], tidx)
        tw = jnp.where(lane == k, es[k] / tot, tw)
    rows = logits.shape[0]
    chosen = sum((lane == ix).astype(F32) for ix in idxs)
    earlier = (lax.broadcasted_iota(I32, (rows, rows), 1) < lax.broadcasted_iota(I32, (rows, rows), 0)).astype(BF16)
    before = _dot(earlier, chosen.astype(BF16)) + cnt_scr[...]
    for k in range(TOP_K):
        rank = jnp.sum(jnp.where(lane == idxs[k], before, 0.0), axis=-1, keepdims=True)
        tidx = jnp.where(lane == TOP_K + k, rank, tidx)
    cnt_scr[...] += jnp.sum(chosen, axis=0, keepdims=True)
    cnt_out_ref[...] = cnt_scr[...]
    tidx_ref[...] = tidx.astype(I32)
    tw_ref[...] = tw


def _merge(attn_o, pool_o, gates, x, mods, mod_row, bbg, woa, wop, wout, g2, rw, rb, cnt_in, tm):
    t = x.shape[0]
    const = lambda shape: pl.BlockSpec(shape, lambda i: (0, 0), pipeline_mode=pl.Buffered(1))
    return pl.pallas_call(
        _merge_kernel,
        out_shape=(jax.ShapeDtypeStruct((t, D_MODEL), F32),
                   jax.ShapeDtypeStruct((t * TOK_SUB, LANES), I32),
                   jax.ShapeDtypeStruct((t, LANES), I32),
                   jax.ShapeDtypeStruct((t, LANES), F32),
                   jax.ShapeDtypeStruct((1, LANES), F32)),
        grid=(t // tm,),
        in_specs=[pl.BlockSpec((tm, D_MODEL), lambda i: (i, 0)),
                  pl.BlockSpec((tm, POOL_WIDTH), lambda i: (i, 0)),
                  pl.BlockSpec((tm, 2 * D_MODEL), lambda i: (i, 0)),
                  pl.BlockSpec((tm, D_MODEL), lambda i: (i, 0)),
                  pl.BlockSpec((None, 6, D_MODEL), lambda i: (mod_row(i), 0, 0)),
                  const((1, 2 * D_MODEL)),
                  const((D_MODEL, D_MODEL)),
                  const((POOL_WIDTH, D_MODEL)),
                  const((D_MODEL, D_MODEL)),
                  const((1, D_MODEL)),
                  const((D_MODEL, LANES)),
                  const((1, LANES)),
                  const((1, LANES))],
        out_specs=(pl.BlockSpec((tm, D_MODEL), lambda i: (i, 0)),
                   pl.BlockSpec((tm * TOK_SUB, LANES), lambda i: (i, 0)),
                   pl.BlockSpec((tm, LANES), lambda i: (i, 0)),
                   pl.BlockSpec((tm, LANES), lambda i: (i, 0)),
                   pl.BlockSpec((1, LANES), lambda i: (0, 0))),
        scratch_shapes=[pltpu.VMEM((1, LANES), F32)],
        compiler_params=_cparams(("arbitrary",)),
        name="merge",
    )(attn_o, pool_o, gates, x, mods, bbg, woa, wop, wout, g2, rw, rb, cnt_in)


DISPATCH_ROWS = 512
DMA_UNROLL = 8


N_FILL = N_EXPERTS + 1


def _dispatch_kernel(n_first, dest_ref, fill_ref, ha_ref, hb_ref, xs_ref, zero_scr, sem):
    base = pl.program_id(0) * (DISPATCH_ROWS * TOP_K)

    @pl.when(pl.program_id(0) == 0)
    def _():
        zero_scr[...] = jnp.zeros(zero_scr.shape, I32)

        def fill_copy(slot, n):
            return pltpu.make_async_copy(zero_scr.at[_tok_rows(0, n), :], xs_ref.at[_tok_rows(slot, n), :], sem)

        def for_fill(e, n, op):
            def one(r, c):
                op(fill_copy(fill_ref[e] + r * n, n))
                return c

            lax.fori_loop(0, fill_ref[N_FILL + e], one, 0)

        def each_range(op):
            def expert(e, carry):
                for_fill(e, 1, op)
                return carry

            lax.fori_loop(0, N_EXPERTS, expert, 0)
            for_fill(N_EXPERTS, MOE_CHUNK, op)

        each_range(lambda cp: cp.start())
        each_range(lambda cp: cp.wait())

    def scatter(h_ref):
        def issue(t, carry):
            for k in range(TOP_K):
                pltpu.make_async_copy(h_ref.at[_tok_rows(t, 1), :],
                                      xs_ref.at[_tok_rows(dest_ref[base + t * TOP_K + k], 1), :],
                                      sem).start(priority=k % 2)
            return carry

        lax.fori_loop(0, DISPATCH_ROWS, issue, 0, unroll=DMA_UNROLL // TOP_K)
        for _ in range(TOP_K):
            pltpu.make_async_copy(h_ref, xs_ref.at[_tok_rows(0, DISPATCH_ROWS), :], sem).wait()

    @pl.when(pl.program_id(0) < n_first)
    def _():
        scatter(ha_ref)

    @pl.when(pl.program_id(0) >= n_first)
    def _():
        scatter(hb_ref)


def _dispatch(dest, fill, h2_a, h2_b, cap):
    n_a = h2_a.shape[0] // TOK_SUB // DISPATCH_ROWS
    n_b = h2_b.shape[0] // TOK_SUB // DISPATCH_ROWS
    blk = (DISPATCH_ROWS * TOK_SUB, LANES)
    return pl.pallas_call(
        functools.partial(_dispatch_kernel, n_a),
        out_shape=jax.ShapeDtypeStruct((cap * TOK_SUB, LANES), I32),
        grid_spec=pltpu.PrefetchScalarGridSpec(
            num_scalar_prefetch=2,
            grid=(n_a + n_b,),
            in_specs=[pl.BlockSpec(blk, lambda i, d, z: (jnp.minimum(i, n_a - 1), 0)),
                      pl.BlockSpec(blk, lambda i, d, z: (jnp.maximum(i - n_a, 0), 0))],
            out_specs=pl.BlockSpec(memory_space=pl.ANY),
            scratch_shapes=[pltpu.VMEM((MOE_CHUNK * TOK_SUB, LANES), I32), pltpu.SemaphoreType.DMA]),
        compiler_params=_cparams(("arbitrary",)),
        name="moe_dispatch",
    )(dest, fill, h2_a, h2_b)


MOE_SPAN = 8


def _moe_kernel(ie_ref, ir_ref, in_ref, xs_ref, wg_hbm, bg_ref, wu_hbm, bu_ref, wd_hbm, bd_ref, ys_ref,
                xin_scr, acc_scr, wgf_scr, wuf_scr, wdf_scr, wg_scr, wu_scr, wd_scr, sem_x, sem_y, sem_w):
    i = pl.program_id(0)
    n_items = pl.num_programs(0)
    nch = in_ref[i]
    row0 = ir_ref[i]
    expert = ie_ref[i]
    slot = i % 2
    down_row = pl.ds(expert, 1)

    def w_copies(e, f, ws):
        col0 = f * FF_TILE
        cols = pl.ds(col0 if isinstance(col0, int) else pl.multiple_of(col0, FF_TILE), FF_TILE)
        return (pltpu.make_async_copy(wg_hbm.at[e, :, cols], wgf_scr.at[ws], sem_w.at[ws]),
                pltpu.make_async_copy(wu_hbm.at[e, :, cols], wuf_scr.at[ws], sem_w.at[ws]),
                pltpu.make_async_copy(wd_hbm.at[e, cols, :], wdf_scr.at[ws], sem_w.at[ws]))

    def rows_of(c, k=1):
        return pl.ds(pl.multiple_of(c * MOE_CHUNK, MOE_CHUNK), k * MOE_CHUNK)

    def x_copy(item_row0, c, s):
        return pltpu.make_async_copy(xs_ref.at[_tok_rows(item_row0 + c * MOE_CHUNK, MOE_CHUNK), :],
                                     xin_scr.at[s, _tok_rows(c * MOE_CHUNK, MOE_CHUNK), :], sem_x.at[s])

    def y_copy(item_row0, c, s):
        return pltpu.make_async_copy(xin_scr.at[s, _tok_rows(c * MOE_CHUNK, MOE_CHUNK), :],
                                     ys_ref.at[_tok_rows(item_row0 + c * MOE_CHUNK, MOE_CHUNK), :], sem_y)

    def for_chunks(n, body):
        def step(c, carry):
            body(c)
            return carry

        lax.fori_loop(0, n, step, 0)

    def for_spans(n, body, first_span):
        def span(g, carry):
            body(g * MOE_SPAN, MOE_SPAN)
            return carry

        lax.fori_loop(first_span, n // MOE_SPAN, span, 0)
        k = MOE_SPAN // 2
        while k >= 1:
            start = n // (2 * k) * (2 * k)

            @pl.when(n % (2 * k) >= k)
            def _(start=start, k=k):
                body(start, k)

            k //= 2

    @pl.when(i > 0)
    def _():
        prev = jnp.maximum(i - 1, 0)
        for_chunks(in_ref[prev], lambda c: y_copy(ir_ref[prev], c, 1 - slot).wait())

    @pl.when(i == 0)
    def _():
        for_chunks(nch, lambda c: x_copy(row0, c, slot).start())
        for cp in w_copies(expert, 0, 0):
            cp.start()

    @pl.when(i + 1 < n_items)
    def _():
        nxt = jnp.minimum(i + 1, n_items - 1)
        for_chunks(in_ref[nxt], lambda c: x_copy(ir_ref[nxt], c, 1 - slot).start())

    for_chunks(nch, lambda c: x_copy(row0, c, slot).wait())

    @pl.when(nch > 0)
    def _():
        def cast_weights(ws):
            w = (wgf_scr[ws].astype(BF16), wuf_scr[ws].astype(BF16), wdf_scr[ws].astype(BF16))
            wg_scr[...], wu_scr[...], wd_scr[...] = w
            return w

        def contribution(f, c, k, w):
            wg, wu, wd = (wg_scr[...], wu_scr[...], wd_scr[...]) if w is None else w
            bias_row = pl.ds(expert * N_FF_TILES + f, 1)
            lo, hi = _unpack_halves(_tok_load(xin_scr.at[slot], c * MOE_CHUNK, k * MOE_CHUNK))
            x = jnp.concatenate([lo.astype(BF16), hi.astype(BF16)], axis=1)
            gt = jnp.minimum(_dot(x, wg) + bg_ref[bias_row, :], SWIGLU_LIMIT)
            up = jnp.clip(_dot(x, wu) + bu_ref[bias_row, :], -SWIGLU_LIMIT, SWIGLU_LIMIT)
            act = (up + 1.0) * (gt * _sigmoid(SWIGLU_ALPHA * gt))
            return _dot(act.astype(BF16), wd)

        def first(f, c, k, w=None):
            acc_scr[rows_of(c, k), :] = contribution(f, c, k, w) + bd_ref[down_row, :]

        def middle(f, c, k, w=None):
            acc_scr[rows_of(c, k), :] += contribution(f, c, k, w)

        def last(f, c, k, w=None):
            y = acc_scr[rows_of(c, k), :] + contribution(f, c, k, w)
            _tok_store(xin_scr.at[slot], c * MOE_CHUNK, k * MOE_CHUNK, _pack_halves(y))
            for u in range(k):
                y_copy(row0, c + u, slot).start()

        def ff_step(f, phase, request_next):
            ws = f % 2
            for cp in w_copies(expert, f, ws):
                cp.wait()
            request_next()
            body = functools.partial(phase, f)

            @pl.when(nch >= MOE_SPAN)
            def _():
                body(0, MOE_SPAN, cast_weights(ws))

            @pl.when(nch < MOE_SPAN)
            def _():
                cast_weights(ws)

            for_spans(nch, body, 1)

        def request_tile(f):
            def go():
                for cp in w_copies(expert, f, f % 2):
                    cp.start()
            return go

        def request_next_item():
            nxt = jnp.minimum(i + 1, n_items - 1)

            @pl.when(jnp.logical_and(i + 1 < n_items, in_ref[nxt] > 0))
            def _():
                for cp in w_copies(ie_ref[nxt], 0, 0):
                    cp.start()

        ff_step(0, first, request_tile(1))

        def mid(f, carry):
            ff_step(f, middle, request_tile(f + 1))
            return carry

        lax.fori_loop(1, N_FF_TILES - 1, mid, 0)
        ff_step(N_FF_TILES - 1, last, request_next_item)

        @pl.when(i == n_items - 1)
        def _():
            for_chunks(nch, lambda c: y_copy(row0, c, slot).wait())


def _moe_experts(item_e, item_row0, item_nch, xs, w_g, b_g, w_u, b_u, w_d, b_d):
    n_items = item_e.shape[0]
    whole = lambda a: pl.BlockSpec(a.shape, lambda i, ie, ir, inch: (0, 0))
    return pl.pallas_call(
        _moe_kernel,
        out_shape=jax.ShapeDtypeStruct(xs.shape, I32),
        grid_spec=pltpu.PrefetchScalarGridSpec(
            num_scalar_prefetch=3,
            grid=(n_items,),
            in_specs=[
                pl.BlockSpec(memory_space=pl.ANY),
                pl.BlockSpec(memory_space=pl.ANY), whole(b_g),
                pl.BlockSpec(memory_space=pl.ANY), whole(b_u),
                pl.BlockSpec(memory_space=pl.ANY), whole(b_d),
            ],
            out_specs=pl.BlockSpec(memory_space=pl.ANY),
            scratch_shapes=[
                pltpu.VMEM((2, MOE_ITEM_ROWS * TOK_SUB, LANES), I32),
                pltpu.VMEM((MOE_ITEM_ROWS, D_MODEL), F32),
                pltpu.VMEM((2, D_MODEL, FF_TILE), F32),
                pltpu.VMEM((2, D_MODEL, FF_TILE), F32),
                pltpu.VMEM((2, FF_TILE, D_MODEL), F32),
                pltpu.VMEM((D_MODEL, FF_TILE), BF16),
                pltpu.VMEM((D_MODEL, FF_TILE), BF16),
                pltpu.VMEM((FF_TILE, D_MODEL), BF16),
                pltpu.SemaphoreType.DMA((2,)),
                pltpu.SemaphoreType.DMA,
                pltpu.SemaphoreType.DMA((2,)),
            ]),
        input_output_aliases={3: 0},
        compiler_params=_cparams(("arbitrary",)),
        name="moe_experts",
    )(item_e, item_row0, item_nch, xs, w_g, b_g, w_u, b_u, w_d, b_d)


COMBINE_ROWS = 256


def _combine_kernel(dest_ref, ys_ref, x1_ref, tw_ref, mods_ref, o_ref, ybuf, sem):
    step = pl.program_id(0)
    slot = step % 2

    def gather(blk, s):
        base = blk * (COMBINE_ROWS * TOP_K)

        def issue(t, carry):
            for k in range(TOP_K):
                pltpu.make_async_copy(ys_ref.at[_tok_rows(dest_ref[base + t * TOP_K + k], 1), :],
                                      ybuf.at[s, k, _tok_rows(t, 1), :], sem.at[s]).start(priority=k % 2)
            return carry

        lax.fori_loop(0, COMBINE_ROWS, issue, 0, unroll=DMA_UNROLL // TOP_K)

    @pl.when(step == 0)
    def _():
        gather(step, slot)

    @pl.when(step + 1 < pl.num_programs(0))
    def _():
        gather(step + 1, 1 - slot)

    for k in range(TOP_K):
        pltpu.make_async_copy(ys_ref.at[_tok_rows(0, COMBINE_ROWS), :], ybuf.at[slot, k], sem.at[slot]).wait()

    y_lo = y_hi = None
    for k in range(TOP_K):
        lo, hi = _unpack_halves(_tok_load(ybuf.at[slot, k], 0, COMBINE_ROWS))
        w = tw_ref[:, k:k + 1]
        y_lo = w * lo if y_lo is None else y_lo + w * lo
        y_hi = w * hi if y_hi is None else y_hi + w * hi
    o_ref[:, :HALF] = x1_ref[:, :HALF] + mods_ref[5:6, :HALF] * y_lo
    o_ref[:, HALF:] = x1_ref[:, HALF:] + mods_ref[5:6, HALF:] * y_hi


def _combine(dest, ys, x1, tw, mods, mod_row):
    t = x1.shape[0]
    return pl.pallas_call(
        _combine_kernel,
        out_shape=jax.ShapeDtypeStruct((t, D_MODEL), F32),
        grid_spec=pltpu.PrefetchScalarGridSpec(
            num_scalar_prefetch=1,
            grid=(t // COMBINE_ROWS,),
            in_specs=[pl.BlockSpec(memory_space=pl.ANY),
                      pl.BlockSpec((COMBINE_ROWS, D_MODEL), lambda i, d: (i, 0)),
                      pl.BlockSpec((COMBINE_ROWS, LANES), lambda i, d: (i, 0)),
                      pl.BlockSpec((None, 6, D_MODEL), lambda i, d: (mod_row(i * COMBINE_ROWS), 0, 0))],
            out_specs=pl.BlockSpec((COMBINE_ROWS, D_MODEL), lambda i, d: (i, 0)),
            scratch_shapes=[pltpu.VMEM((2, TOP_K, COMBINE_ROWS * TOK_SUB, LANES), I32),
                            pltpu.SemaphoreType.DMA((2,))]),
        compiler_params=_cparams(("arbitrary",)),
        name="moe_combine",
    )(dest, ys, x1, tw, mods)


def _routing_tables(top_idx, rank, counts, n_items, cap):
    nch = (counts + MOE_CHUNK - 1) // MOE_CHUNK
    pad_rows = nch * MOE_CHUNK
    pad_end = jnp.cumsum(pad_rows)
    pad_start = pad_end - pad_rows
    dest = (pad_start[top_idx] + rank).reshape(-1).astype(I32)
    items_e = (nch + MOE_ITEM_CHUNKS - 1) // MOE_ITEM_CHUNKS
    item_end = jnp.cumsum(items_e)
    item_start = item_end - items_e
    total = item_end[-1]
    i = jnp.arange(n_items, dtype=I32)
    ii = jnp.minimum(i, total - 1)
    e_i = jnp.minimum(jnp.searchsorted(item_end, ii, side="right"), N_EXPERTS - 1).astype(I32)
    local = ii - item_start[e_i]
    row0 = (pad_start[e_i] + local * MOE_ITEM_ROWS).astype(I32)
    n_i = jnp.where(i < total, jnp.minimum(MOE_ITEM_CHUNKS, nch[e_i] - local * MOE_ITEM_CHUNKS), 0).astype(I32)
    fill = jnp.concatenate([pad_start + counts, pad_end[-1:], pad_rows - counts, (cap - pad_end[-1:]) // MOE_CHUNK]).astype(I32)
    return dest, fill, e_i, row0, n_i


def _rope_tables(n_lat):
    nf = ROPE // 4
    inv = ROPE_BASE ** (-jnp.arange(nf, dtype=F32) / nf)
    t = jnp.arange(n_lat)
    row = (t // GRID_W).astype(F32)
    col = (t % GRID_W).astype(F32)
    ang_r = row[:, None] * inv[None, :]
    ang_c = col[:, None] * inv[None, :]
    z = jnp.zeros((n_lat, nf), F32)
    tail = jnp.zeros((n_lat, LANES - ROPE), F32)
    cos = jnp.concatenate([jnp.cos(ang_r), jnp.cos(ang_r), jnp.cos(ang_c), jnp.cos(ang_c), tail], axis=1)
    sin_lo = jnp.concatenate([-jnp.sin(ang_r), z, -jnp.sin(ang_c), z, tail], axis=1)
    sin_hi = jnp.concatenate([z, jnp.sin(ang_r), z, jnp.sin(ang_c), tail], axis=1)
    return cos, sin_lo, sin_hi


def kernel(x_prompt, x_sample, cache_ckv, cache_krope, c, c_ctx, norm1_g, norm2_g, w_ada, b_ada, w_in,
           b_branch_gate, kv_norm_g, w_ukv, q_norm_g, k_norm_g, w_o_attn, w_pool, pool_scale, w_o_pool, w_out,
           router_w, router_b, w_exp_gate, b_exp_gate, w_exp_up, b_exp_up, w_exp_down, b_exp_down):
    assert w_in.shape[0] == 1, "single-layer trunk"
    batch, seq, _ = x_prompt.shape
    dec_batch, n_lat, _ = x_sample.shape
    past = cache_ckv.shape[2]
    n_ctx = batch * seq
    n_dec = dec_batch * n_lat

    cond = jnp.concatenate([c_ctx[None, :], c, jnp.zeros((8 - 1 - dec_batch, D_MODEL), F32)], axis=0)
    mods = _adaln(cond, w_ada[0], b_ada).reshape(8, 6, D_MODEL)

    w_cat = _w_in_layout(w_in[0].T)
    qg = jnp.pad(q_norm_g * ATTN_SCALE, ((0, 0), (0, HEAD_PAD - QK)))
    kgn = k_norm_g[:, :NOPE]
    kgr = jnp.pad(k_norm_g[:, NOPE:], ((0, 0), (0, LANES - ROPE)))
    w_ukv_b = w_ukv[0].astype(BF16)
    rope_tabs = _rope_tables(n_lat)

    ctx_row = lambda i: 0
    tm_in = 512
    lat_row_in = lambda i: 1 + (i * tm_in) // n_lat
    q_c, pool_c, gates_c, ckv_c, kr_c, krp_c = _in_proj(
        x_prompt.reshape(n_ctx, D_MODEL), mods, ctx_row, norm1_g, w_cat, qg, kv_norm_g, None, tm_in)
    q_l, pool_l, gates_l, ckv_l, _, krp_l = _in_proj(
        x_sample.reshape(n_dec, D_MODEL), mods, lat_row_in, norm1_g, w_cat, qg, kv_norm_g, rope_tabs, tm_in)

    tr = 512
    k_c, v_c = _kv_expand(ckv_c, krp_c, w_ukv_b, kgn, kgr, None, tr)
    k_l, v_l = _kv_expand(ckv_l, krp_l, w_ukv_b, kgn, kgr, rope_tabs, tr)
    cache_krp = jnp.pad(cache_krope.reshape(dec_batch * past, ROPE), ((0, 0), (0, LANES - ROPE)))
    k_p, v_p = _kv_expand(cache_ckv.reshape(dec_batch * past, KV_RANK), cache_krp, w_ukv_b, kgn, kgr, None, tr)

    attn_c = _attn_ctx(q_c, k_c, v_c, seq)
    attn_l = _attn_lat(q_l, k_l, v_l, k_p, v_p, n_lat, past, 512)

    w_pool_b = w_pool[0].astype(BF16)
    poolo_c = _pool(pool_c, w_pool_b, pool_scale, seq)
    poolo_l = _pool(pool_l, w_pool_b, pool_scale, n_lat)

    woa = w_o_attn[0].astype(BF16)
    wop = w_o_pool[0].astype(BF16)
    wout = w_out[0].astype(BF16)
    rw = jnp.pad(router_w[0], ((0, 0), (0, LANES - N_EXPERTS))).astype(BF16)
    rb = jnp.pad(router_b, ((0, 0), (0, LANES - N_EXPERTS)), constant_values=NEG_BIG)
    tm_mg = 256
    lat_row_mg = lambda i: 1 + (i * tm_mg) // n_lat
    x1_c, h2_c, tidx_c, tw_c, cnt_c = _merge(
        attn_c, poolo_c, gates_c, x_prompt.reshape(n_ctx, D_MODEL), mods, ctx_row,
        b_branch_gate, woa, wop, wout, norm2_g, rw, rb, jnp.zeros((1, LANES), F32), tm_mg)
    x1_l, h2_l, tidx_l, tw_l, cnt_all = _merge(
        attn_l, poolo_l, gates_l, x_sample.reshape(n_dec, D_MODEL), mods, lat_row_mg,
        b_branch_gate, woa, wop, wout, norm2_g, rw, rb, cnt_c, tm_mg)

    n_assign = (n_ctx + n_dec) * TOP_K
    max_chunks = (n_assign + N_EXPERTS * (MOE_CHUNK - 1)) // MOE_CHUNK
    cap = max_chunks * MOE_CHUNK
    n_items = (max_chunks + N_EXPERTS * (MOE_ITEM_CHUNKS - 1)) // MOE_ITEM_CHUNKS
    routed = jnp.concatenate([tidx_c[:, :2 * TOP_K], tidx_l[:, :2 * TOP_K]], axis=0)
    dest, fill, item_e, item_row0, item_nch = _routing_tables(
        routed[:, :TOP_K], routed[:, TOP_K:], cnt_all[0, :N_EXPERTS].astype(I32), n_items, cap)

    xs = _dispatch(dest, fill, h2_c, h2_l, cap)
    ys = _moe_experts(item_e, item_row0, item_nch, xs,
                      w_exp_gate[0], b_exp_gate[0].reshape(N_EXPERTS * N_FF_TILES, FF_TILE),
                      w_exp_up[0], b_exp_up[0].reshape(N_EXPERTS * N_FF_TILES, FF_TILE),
                      w_exp_down[0], b_exp_down[0])

    y_c = _combine(dest[:n_ctx * TOP_K], ys, x1_c, tw_c, mods, lambda r: 0)
    y_l = _combine(dest[n_ctx * TOP_K:], ys, x1_l, tw_l, mods, lambda r: 1 + r // n_lat)

    return (y_c.reshape(batch, seq, D_MODEL),
            y_l.reshape(dec_batch, n_lat, D_MODEL),
            ckv_c.reshape(batch, 1, seq, KV_RANK),
            kr_c.reshape(batch, 1, seq, ROPE))
```
